```python
import jax, jax.numpy as jnp
from jax import lax
import numpy as np

D_MODEL = 1024
BATCH = 32
SEQ = 256
DEPTH = 2
DEC_BATCH = 4
DEC_SEQ = 2048
PAST_LEN = 256

GRID_W = 64
EPS = 1e-6
CONV_CH = 512
CONV_WIDTH = 31
N_Q_HEADS = 8
N_KV_HEADS = 2
Q_PER_KV = N_Q_HEADS // N_KV_HEADS
HEAD_DIM = 128
Q_BLOCK = 128
ROPE_THETA = 10000.0
ATTN_WIDTH = N_Q_HEADS * HEAD_DIM
KV_WIDTH = N_KV_HEADS * HEAD_DIM
IN0_WIDTH = 2 * CONV_CH + ATTN_WIDTH + 2 * KV_WIDTH
OUT0_WIDTH = CONV_CH + ATTN_WIDTH
SPLIT0 = [CONV_CH, 2 * CONV_CH, 2 * CONV_CH + ATTN_WIDTH, 2 * CONV_CH + ATTN_WIDTH + KV_WIDTH]
RET_HEADS = 8
RET_DK = 128
RET_DV = 256
RET_CHUNK = 128
RET_QK_WIDTH = RET_HEADS * RET_DK
RET_V_WIDTH = RET_HEADS * RET_DV
IN1_WIDTH = 2 * RET_QK_WIDTH + 3 * RET_V_WIDTH
SPLIT1 = [RET_QK_WIDTH, 2 * RET_QK_WIDTH, 2 * RET_QK_WIDTH + RET_V_WIDTH, 2 * RET_QK_WIDTH + 2 * RET_V_WIDTH]
N_GROUPS = 4
EXPERTS_PER_GROUP = 4
N_EXPERTS = N_GROUPS * EXPERTS_PER_GROUP
EXPERT_FF = 512
TOP_K = 2

kernel_name = 'hybrid_diffusion_conv_gqa_retention_hmoe_step'


def rmsnorm(x, g):
    xf = x.astype(jnp.float32)
    y = xf * lax.rsqrt(jnp.mean(xf * xf, axis=-1, keepdims=True) + EPS)
    return (y * g.astype(jnp.float32)).astype(x.dtype)


def layernorm(x, g, b):
    xf = x.astype(jnp.float32)
    mu = jnp.mean(xf, axis=-1, keepdims=True)
    var = jnp.mean(jnp.square(xf - mu), axis=-1, keepdims=True)
    y = (xf - mu) * lax.rsqrt(var + EPS)
    return (y * g.astype(jnp.float32) + b.astype(jnp.float32)).astype(x.dtype)


def grid_rope_tables(n_tokens, dim):
    rows = n_tokens // GRID_W
    t = jnp.arange(rows * GRID_W)
    pos = jnp.stack([t // GRID_W, t % GRID_W]).astype(jnp.float32)
    n_freq = dim // 4
    freqs = ROPE_THETA ** (-jnp.arange(n_freq, dtype=jnp.float32) / n_freq)
    ang = pos[:, :, None] * freqs
    return jnp.cos(ang), jnp.sin(ang)


def apply_grid_rope(x, cos, sin):
    dim = x.shape[-1]
    half, quarter = dim // 2, dim // 4
    xf = x.astype(jnp.float32)

    def rot(xa, c, s):
        x1, x2 = xa[..., :quarter], xa[..., quarter:]
        return jnp.concatenate([x1 * c - x2 * s, x1 * s + x2 * c], axis=-1)

    out = jnp.concatenate([rot(xf[..., :half], cos[0], sin[0]), rot(xf[..., half:], cos[1], sin[1])], axis=-1)
    return out.astype(x.dtype)


def block_attention(q, k, v):
    B, H, G, T, hd = q.shape
    nb = T // Q_BLOCK
    kf, vf = k.astype(jnp.float32), v.astype(jnp.float32)
    qb = jnp.moveaxis(q.reshape(B, H, G, nb, Q_BLOCK, hd), 3, 0)
    scale = hd ** -0.5

    def attend(qi):
        s = jnp.einsum('bhgqd,bhkd->bhgqk', qi.astype(jnp.float32), kf) * scale
        p = jax.nn.softmax(s, axis=-1)
        return jnp.einsum('bhgqk,bhkd->bhgqd', p, vf)

    o = lax.map(attend, qb)
    return jnp.moveaxis(o, 0, 3).reshape(B, H, G, T, hd).astype(q.dtype)


def mix_attn_conv(h, rope, ctx_kv, in_w, conv_w, conv_b, conv_norm_g, conv_norm_b, q_norm_g, k_norm_g, out_w):
    B, T, _ = h.shape
    a_val, a_gate, q, k, v = jnp.split(h @ in_w, SPLIT0, axis=-1)
    a = a_val * jax.nn.sigmoid(a_gate)
    a = lax.conv_general_dilated(a, conv_w[:, None, :], (1,), [(CONV_WIDTH // 2, CONV_WIDTH // 2)],
                                 dimension_numbers=('NWC', 'WIO', 'NWC'), feature_group_count=CONV_CH) + conv_b
    a = jax.nn.silu(layernorm(a, conv_norm_g, conv_norm_b))
    q = rmsnorm(q.reshape(B, T, N_Q_HEADS, HEAD_DIM), q_norm_g).transpose(0, 2, 1, 3)
    k = rmsnorm(k.reshape(B, T, N_KV_HEADS, HEAD_DIM), k_norm_g).transpose(0, 2, 1, 3)
    v = v.reshape(B, T, N_KV_HEADS, HEAD_DIM).transpose(0, 2, 1, 3)
    if rope is not None:
        q = apply_grid_rope(q, *rope)
        k = apply_grid_rope(k, *rope)
    keys, vals = k, v
    if ctx_kv is not None:
        keys = jnp.concatenate([ctx_kv[0].astype(k.dtype), k], axis=2)
        vals = jnp.concatenate([ctx_kv[1].astype(v.dtype), v], axis=2)
    o = block_attention(q.reshape(B, N_KV_HEADS, Q_PER_KV, T, HEAD_DIM), keys, vals)
    o = o.reshape(B, N_Q_HEADS, T, HEAD_DIM).transpose(0, 2, 1, 3).reshape(B, T, ATTN_WIDTH)
    return jnp.concatenate([a, o], axis=-1) @ out_w, (k, v)


def retention_chunkwise(q, k, v, log_gamma, state0):
    B, H, T, dk = q.shape
    dv = v.shape[-1]
    C = RET_CHUNK
    n = T // C
    f32 = jnp.float32
    qc = jnp.moveaxis(q.astype(f32).reshape(B, H, n, C, dk), 2, 0)
    kc = jnp.moveaxis(k.astype(f32).reshape(B, H, n, C, dk), 2, 0)
    vc = jnp.moveaxis(v.astype(f32).reshape(B, H, n, C, dv), 2, 0)
    pos = jnp.arange(C, dtype=f32)
    diff = pos[:, None] - pos[None, :]
    lg = log_gamma.astype(f32)[:, None, None]
    intra = jnp.where(diff >= 0, jnp.exp(lg * jnp.maximum(diff, 0.0)), 0.0)
    q_dec = jnp.exp(lg[:, :, 0] * (pos + 1.0))[:, :, None]
    k_dec = jnp.exp(lg[:, :, 0] * (C - 1.0 - pos))[:, :, None]
    chunk_dec = jnp.exp(lg * C)

    def step(S, blk):
        qi, ki, vi = blk
        s = jnp.einsum('bhqd,bhkd->bhqk', qi, ki) * intra
        y = jnp.einsum('bhqk,bhke->bhqe', s, vi) + jnp.einsum('bhqd,bhde->bhqe', qi, S) * q_dec
        S = S * chunk_dec + jnp.einsum('bhkd,bhke->bhde', ki * k_dec, vi)
        return S, y

    S, y = lax.scan(step, state0.astype(f32), (qc, kc, vc))
    return jnp.moveaxis(y, 0, 2).reshape(B, H, T, dv), S


def head_rms(y):
    y = y * lax.rsqrt(jnp.mean(y * y, axis=-1, keepdims=True) + EPS)
    B, H, T, dv = y.shape
    return y.transpose(0, 2, 1, 3).reshape(B, T, H * dv)


def mix_retention(h, rope, states, in_w, decay_fwd, decay_bwd, out_w):
    B, T, _ = h.shape
    q, k, v, g_f, g_b = jnp.split(h @ in_w, SPLIT1, axis=-1)
    q = q.reshape(B, T, RET_HEADS, RET_DK).transpose(0, 2, 1, 3)
    k = k.reshape(B, T, RET_HEADS, RET_DK).transpose(0, 2, 1, 3)
    v = v.reshape(B, T, RET_HEADS, RET_DV).transpose(0, 2, 1, 3)
    if rope is not None:
        q = apply_grid_rope(q, *rope)
        k = apply_grid_rope(k, *rope)
    k = k * (RET_DK ** -0.5)
    log_gamma_f = -jnp.exp(decay_fwd.astype(jnp.float32))
    log_gamma_b = -jnp.exp(decay_bwd.astype(jnp.float32))
    if states is None:
        s0_f = jnp.zeros((B, RET_HEADS, RET_DK, RET_DV), jnp.float32)
        s0_b = s0_f
    else:
        s0_f, s0_b = states
    y_f, s_f = retention_chunkwise(q, k, v, log_gamma_f, s0_f)
    y_b, s_b = retention_chunkwise(jnp.flip(q, 2), jnp.flip(k, 2), jnp.flip(v, 2), log_gamma_b, s0_b)
    y_b = jnp.flip(y_b, 2)
    mixed = (jax.nn.silu(g_f.astype(jnp.float32)) * head_rms(y_f)
             + jax.nn.silu(g_b.astype(jnp.float32)) * head_rms(y_b))
    return mixed.astype(h.dtype) @ out_w, (s_f, s_b)


def hier_moe(h, grp_w, grp_b, rtr_w, rtr_b, w_gate, w_up, w_down):
    B, T, D = h.shape
    x = h.reshape(B * T, D)
    g_logits = (x @ grp_w + grp_b).astype(jnp.float32)
    g_prob = jax.nn.softmax(g_logits, axis=-1)
    g_idx = jnp.argmax(g_logits, axis=-1)
    g_w = jnp.take_along_axis(g_prob, g_idx[:, None], axis=-1)
    e_all = (jnp.einsum('nd,gde->nge', x, rtr_w) + rtr_b).astype(jnp.float32)
    e_logits = jnp.take_along_axis(e_all, g_idx[:, None, None], axis=1)[:, 0]
    top_v, top_i = lax.top_k(e_logits, TOP_K)
    top_w = jax.nn.softmax(top_v, axis=-1) * g_w
    expert_id = g_idx[:, None] * EXPERTS_PER_GROUP + top_i
    combine = jnp.sum(jax.nn.one_hot(expert_id, N_EXPERTS, dtype=jnp.float32) * top_w[..., None], axis=1)
    hidden = jax.nn.silu(jnp.einsum('nd,edf->nef', x, w_gate)) * jnp.einsum('nd,edf->nef', x, w_up)
    y = jnp.einsum('nef,efd->nd', hidden * combine[:, :, None].astype(hidden.dtype), w_down)
    return y.reshape(B, T, D)


def trunk_layer(x, cond, mod_w, mod_b, norm_mix_g, norm_ffn_g, moe_params, mixer_fn):
    mod = (jax.nn.silu(cond) @ mod_w + mod_b).reshape(cond.shape[0], 6, D_MODEL)
    shift_m, scale_m, gate_m, shift_f, scale_f, gate_f = [mod[:, i, None, :] for i in range(6)]
    h = rmsnorm(x, norm_mix_g) * (1.0 + scale_m) + shift_m
    out, ctx_tensors = mixer_fn(h)
    x = x + gate_m * out
    h = rmsnorm(x, norm_ffn_g) * (1.0 + scale_f) + shift_f
    x = x + gate_f * hier_moe(h, *moe_params)
    return x, ctx_tensors


def setup_inputs(seed: int = 0) -> dict:
    key = jax.random.key(seed)
    ks = iter(jax.random.split(key, 80))

    def nrm(shape, scale):
        return scale * jax.random.normal(next(ks), shape, jnp.float32)

    D = D_MODEL
    inp = {}
    inp['x_prompt'] = nrm((BATCH, SEQ, D), 1.0)
    inp['x_sample'] = nrm((DEC_BATCH, DEC_SEQ, D), 1.0)
    inp['cache_k0'] = nrm((DEC_BATCH, N_KV_HEADS, PAST_LEN, HEAD_DIM), 1.0)
    inp['cache_v0'] = nrm((DEC_BATCH, N_KV_HEADS, PAST_LEN, HEAD_DIM), 1.0)
    inp['state_ret_fwd1'] = nrm((DEC_BATCH, RET_HEADS, RET_DK, RET_DV), 0.1)
    inp['state_ret_bwd1'] = nrm((DEC_BATCH, RET_HEADS, RET_DK, RET_DV), 0.1)
    inp['c'] = nrm((DEC_BATCH, D), 1.0)
    inp['c_ctx'] = nrm((D,), 1.0)

    def moe(i):
        inp['moe_grp_w%d' % i] = nrm((D, N_GROUPS), D ** -0.5)
        inp['moe_grp_b%d' % i] = nrm((N_GROUPS,), 0.01)
        inp['moe_rtr_w%d' % i] = nrm((N_GROUPS, D, EXPERTS_PER_GROUP), D ** -0.5)
        inp['moe_rtr_b%d' % i] = nrm((N_GROUPS, EXPERTS_PER_GROUP), 0.01)
        inp['moe_w_gate%d' % i] = nrm((N_EXPERTS, D, EXPERT_FF), D ** -0.5)
        inp['moe_w_up%d' % i] = nrm((N_EXPERTS, D, EXPERT_FF), D ** -0.5)
        inp['moe_w_down%d' % i] = nrm((N_EXPERTS, EXPERT_FF, D), EXPERT_FF ** -0.5)

    inp['norm_mix_g0'] = 1.0 + nrm((D,), 0.05)
    inp['mod_w0'] = nrm((D, 6 * D), 0.5 * D ** -0.5)
    inp['mod_b0'] = nrm((6 * D,), 0.02)
    inp['in_w0'] = nrm((D, IN0_WIDTH), D ** -0.5)
    inp['conv_w0'] = nrm((CONV_WIDTH, CONV_CH), CONV_WIDTH ** -0.5)
    inp['conv_b0'] = nrm((CONV_CH,), 0.02)
    inp['conv_norm_g0'] = 1.0 + nrm((CONV_CH,), 0.05)
    inp['conv_norm_b0'] = nrm((CONV_CH,), 0.02)
    inp['q_norm_g0'] = 1.0 + nrm((HEAD_DIM,), 0.05)
    inp['k_norm_g0'] = 1.0 + nrm((HEAD_DIM,), 0.05)
    inp['out_w0'] = nrm((OUT0_WIDTH, D), OUT0_WIDTH ** -0.5)
    inp['norm_ffn_g0'] = 1.0 + nrm((D,), 0.05)
    moe(0)
    inp['norm_mix_g1'] = 1.0 + nrm((D,), 0.05)
    inp['mod_w1'] = nrm((D, 6 * D), 0.5 * D ** -0.5)
    inp['mod_b1'] = nrm((6 * D,), 0.02)
    inp['in_w1'] = nrm((D, IN1_WIDTH), D ** -0.5)
    base = jnp.log(-jnp.log1p(-(2.0 ** (-5.0 - jnp.arange(RET_HEADS, dtype=jnp.float32)))))
    inp['ret_decay_fwd1'] = base + nrm((RET_HEADS,), 0.05)
    inp['ret_decay_bwd1'] = base + nrm((RET_HEADS,), 0.05)
    inp['out_w1'] = nrm((RET_V_WIDTH, D), RET_V_WIDTH ** -0.5)
    inp['norm_ffn_g1'] = 1.0 + nrm((D,), 0.05)
    moe(1)
    inp['final_norm_g'] = 1.0 + nrm((D,), 0.05)
    return inp


def reference(x_prompt, x_sample, cache_k0, cache_v0, state_ret_fwd1, state_ret_bwd1, c, c_ctx,
              norm_mix_g0, mod_w0, mod_b0, in_w0, conv_w0, conv_b0, conv_norm_g0, conv_norm_b0,
              q_norm_g0, k_norm_g0, out_w0, norm_ffn_g0,
              moe_grp_w0, moe_grp_b0, moe_rtr_w0, moe_rtr_b0, moe_w_gate0, moe_w_up0, moe_w_down0,
              norm_mix_g1, mod_w1, mod_b1, in_w1, ret_decay_fwd1, ret_decay_bwd1, out_w1, norm_ffn_g1,
              moe_grp_w1, moe_grp_b1, moe_rtr_w1, moe_rtr_b1, moe_w_gate1, moe_w_up1, moe_w_down1,
              final_norm_g):
    n_lat = x_sample.shape[1]
    rope_attn = grid_rope_tables(n_lat, HEAD_DIM)
    rope_ret = grid_rope_tables(n_lat, RET_DK)
    layers = [
        dict(mod=(mod_w0, mod_b0), norms=(norm_mix_g0, norm_ffn_g0), mixer_fn=mix_attn_conv, rope=rope_attn,
             mixer=(in_w0, conv_w0, conv_b0, conv_norm_g0, conv_norm_b0, q_norm_g0, k_norm_g0, out_w0),
             moe=(moe_grp_w0, moe_grp_b0, moe_rtr_w0, moe_rtr_b0, moe_w_gate0, moe_w_up0, moe_w_down0),
             cache=(cache_k0, cache_v0)),
        dict(mod=(mod_w1, mod_b1), norms=(norm_mix_g1, norm_ffn_g1), mixer_fn=mix_retention, rope=rope_ret,
             mixer=(in_w1, ret_decay_fwd1, ret_decay_bwd1, out_w1),
             moe=(moe_grp_w1, moe_grp_b1, moe_rtr_w1, moe_rtr_b1, moe_w_gate1, moe_w_up1, moe_w_down1),
             cache=(state_ret_fwd1, state_ret_bwd1)),
    ]
    xp, xs = x_prompt, x_sample
    new_state = []
    for layer in range(DEPTH):
        p = layers[layer]
        mixer_fn, rope, mparams, cache = p['mixer_fn'], p['rope'], p['mixer'], p['cache']
        xp, ctx_tensors = trunk_layer(xp, c_ctx[None, :], p['mod'][0], p['mod'][1], p['norms'][0], p['norms'][1],
                                      p['moe'], lambda h: mixer_fn(h, None, None, *mparams))
        new_state.extend(ctx_tensors)
        xs, _ = trunk_layer(xs, c, p['mod'][0], p['mod'][1], p['norms'][0], p['norms'][1],
                            p['moe'], lambda h: mixer_fn(h, rope, cache, *mparams))
    y_prompt = rmsnorm(xp, final_norm_g)
    y_sample = rmsnorm(xs, final_norm_g)
    new_k0, new_v0, new_state_ret_fwd1, new_state_ret_bwd1 = new_state
    return (y_prompt, y_sample, new_k0, new_v0, new_state_ret_fwd1, new_state_ret_bwd1)
```

```python
import functools

import jax
import jax.numpy as jnp
from jax import lax
from jax.experimental import pallas as pl
from jax.experimental.pallas import tpu as pltpu

D_MODEL = 1024
GRID_W = 64
EPS = 1e-6
CONV_CH = 512
CONV_WIDTH = 31
CONV_HALF = CONV_WIDTH // 2
N_Q_HEADS = 8
N_KV_HEADS = 2
Q_PER_KV = N_Q_HEADS // N_KV_HEADS
HEAD_DIM = 128
ROPE_THETA = 10000.0
ATTN_WIDTH = N_Q_HEADS * HEAD_DIM
KV_WIDTH = N_KV_HEADS * HEAD_DIM
IN0_WIDTH = 2 * CONV_CH + ATTN_WIDTH + 2 * KV_WIDTH
RET_HEADS = 8
RET_DK = 128
RET_DV = 256
RET_CHUNK = 128
RET_QK_WIDTH = RET_HEADS * RET_DK
RET_V_WIDTH = RET_HEADS * RET_DV
IN1_WIDTH = 2 * RET_QK_WIDTH + 3 * RET_V_WIDTH
N_GROUPS = 4
EXPERTS_PER_GROUP = 4
N_EXPERTS = N_GROUPS * EXPERTS_PER_GROUP
EXPERT_FF = 512

LANES = 128
SUBLANES = 8
VMEM_LIMIT_BYTES = 56 * 1024 * 1024

MOD_SLOTS = 8
ROW_TILE = 512
MOE_ROW_TILE = 1024
CONV_TILE = 256
CONV_HALO = 16
CONV_ROWS = 64
ROUTE_LANES = LANES
EXPERT_LANE0 = N_GROUPS
NEG_BIG = -1e30

F32 = jnp.float32
BF16 = jnp.bfloat16


def _params(*semantics):
    return pltpu.CompilerParams(dimension_semantics=semantics, vmem_limit_bytes=VMEM_LIMIT_BYTES)


def _sigmoid(x):
    return 1.0 / (1.0 + jnp.exp(-x))


def _silu(x):
    return x * _sigmoid(x)


def _rms(x):
    return x * lax.rsqrt(jnp.mean(x * x, axis=-1, keepdims=True) + EPS)


def _rope(x, cos, sin):
    lane = lax.broadcasted_iota(jnp.int32, x.shape, 1)
    take_upper = (lane % (HEAD_DIM // 2)) < (HEAD_DIM // 4)
    partner = jnp.where(take_upper, pltpu.roll(x, HEAD_DIM - HEAD_DIM // 4, 1), pltpu.roll(x, HEAD_DIM // 4, 1))
    return x * cos + partner * sin


def _dot(a, b):
    return jnp.dot(a, b, preferred_element_type=F32)


def _dot_nt(a, b):
    return lax.dot_general(a, b, (((1,), (1,)), ((), ())), preferred_element_type=F32)


def _dot_tn(a, b):
    return lax.dot_general(a, b, (((0,), (0,)), ((), ())), preferred_element_type=F32)


def _mod_kernel(c_ref, w_ref, b_ref, o_ref):
    c = c_ref[...]
    o_ref[...] = _dot(_silu(c).astype(BF16), w_ref[...].astype(BF16)) + b_ref[...]


def _modulation(cond, mod_w, mod_b):
    n_out = mod_w.shape[1]
    col = D_MODEL
    out = pl.pallas_call(
        _mod_kernel,
        grid=(n_out // col,),
        in_specs=[
            pl.BlockSpec((MOD_SLOTS, D_MODEL), lambda j: (0, 0)),
            pl.BlockSpec((D_MODEL, col), lambda j: (0, j)),
            pl.BlockSpec((1, col), lambda j: (0, j)),
        ],
        out_specs=pl.BlockSpec((MOD_SLOTS, col), lambda j: (0, j)),
        out_shape=jax.ShapeDtypeStruct((MOD_SLOTS, n_out), F32),
        compiler_params=_params("parallel"),
        name="modulation",
    )(cond, mod_w, mod_b.reshape(1, n_out))
    return out.reshape(MOD_SLOTS, 6, D_MODEL)


def _slot_map(stream):
    if stream.first_slot == 0:
        return lambda i: 0
    return lambda i: stream.first_slot + (i * ROW_TILE) // stream.seq_len


class _Stream:
    def __init__(self, n_seq, seq_len, first_slot, rope):
        self.n_seq = n_seq
        self.seq_len = seq_len
        self.first_slot = first_slot
        self.rope = rope
        self.n_tok = n_seq * seq_len


def _modulated_norm(x, g, shift, scale):
    return _rms(x) * g * (1.0 + scale) + shift


def _inproj0_kernel(x_ref, mod_ref, g_ref, w_ref, qg_ref, kg_ref, cos_ref, sin_ref,
                    a_ref, q_ref, k_ref, v_ref, *, rope):
    h = _modulated_norm(x_ref[...], g_ref[...], mod_ref[0, 0:1, :], mod_ref[0, 1:2, :])
    p = _dot(h.astype(BF16), w_ref[...])
    a_ref[...] = p[:, :CONV_CH] * _sigmoid(p[:, CONV_CH:2 * CONV_CH])
    q0 = 2 * CONV_CH
    k0 = q0 + ATTN_WIDTH
    v0 = k0 + KV_WIDTH
    cos = cos_ref[...]
    sin = sin_ref[...]
    q_scale = HEAD_DIM ** -0.5
    for hh in range(N_Q_HEADS):
        qh = _rms(p[:, q0 + hh * HEAD_DIM:q0 + (hh + 1) * HEAD_DIM]) * qg_ref[...]
        if rope:
            qh = _rope(qh, cos, sin)
        q_ref[:, hh * HEAD_DIM:(hh + 1) * HEAD_DIM] = (qh * q_scale).astype(q_ref.dtype)
    for hh in range(N_KV_HEADS):
        kh = _rms(p[:, k0 + hh * HEAD_DIM:k0 + (hh + 1) * HEAD_DIM]) * kg_ref[...]
        if rope:
            kh = _rope(kh, cos, sin)
        k_ref[:, hh * HEAD_DIM:(hh + 1) * HEAD_DIM] = kh.astype(k_ref.dtype)
    v_ref[...] = p[:, v0:v0 + KV_WIDTH].astype(v_ref.dtype)


def _inproj0(stream, x, mod, norm_g, w_bf16, q_norm_g, k_norm_g, cos, sin, kv_dtype):
    n = stream.n_tok
    tiles_per_seq = max(stream.seq_len // ROW_TILE, 1)
    rope_map = (lambda i: (i % tiles_per_seq, 0)) if stream.rope else (lambda i: (0, 0))
    slot = _slot_map(stream)
    row = lambda i: (i, 0)
    fixed = lambda i: (0, 0)
    return pl.pallas_call(
        functools.partial(_inproj0_kernel, rope=stream.rope),
        grid=(n // ROW_TILE,),
        in_specs=[
            pl.BlockSpec((ROW_TILE, D_MODEL), row),
            pl.BlockSpec((1, 6, D_MODEL), lambda i: (slot(i), 0, 0)),
            pl.BlockSpec((1, D_MODEL), fixed),
            pl.BlockSpec((D_MODEL, IN0_WIDTH), fixed),
            pl.BlockSpec((1, HEAD_DIM), fixed),
            pl.BlockSpec((1, HEAD_DIM), fixed),
            pl.BlockSpec((ROW_TILE, HEAD_DIM), rope_map),
            pl.BlockSpec((ROW_TILE, HEAD_DIM), rope_map),
        ],
        out_specs=[
            pl.BlockSpec((ROW_TILE, CONV_CH), row),
            pl.BlockSpec((ROW_TILE, ATTN_WIDTH), row),
            pl.BlockSpec((ROW_TILE, KV_WIDTH), row),
            pl.BlockSpec((ROW_TILE, KV_WIDTH), row),
        ],
        out_shape=[
            jax.ShapeDtypeStruct((n, CONV_CH), F32),
            jax.ShapeDtypeStruct((n, ATTN_WIDTH), BF16),
            jax.ShapeDtypeStruct((n, KV_WIDTH), kv_dtype),
            jax.ShapeDtypeStruct((n, KV_WIDTH), kv_dtype),
        ],
        compiler_params=_params("parallel"),
        name="inproj0",
    )(x, mod, norm_g.reshape(1, D_MODEL), w_bf16, q_norm_g.reshape(1, HEAD_DIM), k_norm_g.reshape(1, HEAD_DIM),
      cos, sin)


def _conv_kernel(prev_ref, main_ref, next_ref, w_ref, b_ref, g_ref, beta_ref, o_ref, pad_ref, acc_ref, *,
                 tiles_per_seq):
    i = pl.program_id(0)
    has_prev = (i % tiles_per_seq) != 0
    has_next = (i % tiles_per_seq) != (tiles_per_seq - 1)
    pad_ref[0:CONV_HALO, :] = jnp.where(has_prev, prev_ref[...], 0.0)
    pad_ref[CONV_HALO:CONV_HALO + CONV_TILE, :] = main_ref[...]
    pad_ref[CONV_HALO + CONV_TILE:, :] = jnp.where(has_next, next_ref[...], 0.0)
    first = CONV_HALO - CONV_HALF
    for c0 in range(0, CONV_CH, LANES):
        for r0 in range(0, CONV_TILE, CONV_ROWS):
            acc = jnp.zeros((CONV_ROWS, LANES), F32)
            for j in range(CONV_WIDTH):
                acc = acc + w_ref[j:j + 1, c0:c0 + LANES] * pad_ref[first + r0 + j:first + r0 + j + CONV_ROWS,
                                                                    c0:c0 + LANES]
            acc_ref[r0:r0 + CONV_ROWS, c0:c0 + LANES] = acc
    a = acc_ref[...] + b_ref[...]
    mu = jnp.mean(a, axis=-1, keepdims=True)
    d = a - mu
    var = jnp.mean(d * d, axis=-1, keepdims=True)
    y = d * lax.rsqrt(var + EPS) * g_ref[...] + beta_ref[...]
    o_ref[...] = _silu(y).astype(o_ref.dtype)


def _conv_branch(stream, a, conv_w, conv_b, norm_g, norm_b):
    n = stream.n_tok
    tiles_per_seq = stream.seq_len // CONV_TILE
    halo_per_tile = CONV_TILE // CONV_HALO
    n_halo = n // CONV_HALO
    fixed = lambda i: (0, 0)
    return pl.pallas_call(
        functools.partial(_conv_kernel, tiles_per_seq=tiles_per_seq),
        grid=(n // CONV_TILE,),
        in_specs=[
            pl.BlockSpec((CONV_HALO, CONV_CH), lambda i: (jnp.maximum(i * halo_per_tile - 1, 0), 0)),
            pl.BlockSpec((CONV_TILE, CONV_CH), lambda i: (i, 0)),
            pl.BlockSpec((CONV_HALO, CONV_CH), lambda i: (jnp.minimum((i + 1) * halo_per_tile, n_halo - 1), 0)),
            pl.BlockSpec((CONV_WIDTH, CONV_CH), fixed),
            pl.BlockSpec((1, CONV_CH), fixed),
            pl.BlockSpec((1, CONV_CH), fixed),
            pl.BlockSpec((1, CONV_CH), fixed),
        ],
        out_specs=pl.BlockSpec((CONV_TILE, CONV_CH), lambda i: (i, 0)),
        out_shape=jax.ShapeDtypeStruct((n, CONV_CH), BF16),
        scratch_shapes=[
            pltpu.VMEM((CONV_TILE + 2 * CONV_HALO, CONV_CH), F32),
            pltpu.VMEM((CONV_TILE, CONV_CH), F32),
        ],
        compiler_params=_params("parallel"),
        name="conv_branch",
    )(a, a, a, conv_w, conv_b.reshape(1, CONV_CH), norm_g.reshape(1, CONV_CH), norm_b.reshape(1, CONV_CH))


def _attn_kernel(*refs, tq, has_cache):
    if has_cache:
        q_ref, k_ref, v_ref, kc_ref, vc_ref, o_ref = refs
    else:
        q_ref, k_ref, v_ref, o_ref = refs
    q = q_ref[...]
    q4 = jnp.concatenate([q[:, g * HEAD_DIM:(g + 1) * HEAD_DIM] for g in range(Q_PER_KV)], axis=0)
    s = _dot_nt(q4, k_ref[...].astype(BF16))
    m = jnp.max(s, axis=-1, keepdims=True)
    if has_cache:
        sc = _dot_nt(q4, kc_ref[0, 0].astype(BF16))
        m = jnp.maximum(m, jnp.max(sc, axis=-1, keepdims=True))
    p = jnp.exp(s - m)
    l = jnp.sum(p, axis=-1, keepdims=True)
    o = _dot(p.astype(BF16), v_ref[...].astype(BF16))
    if has_cache:
        pc = jnp.exp(sc - m)
        l = l + jnp.sum(pc, axis=-1, keepdims=True)
        o = o + _dot(pc.astype(BF16), vc_ref[0, 0].astype(BF16))
    o = o * (1.0 / l)
    for g in range(Q_PER_KV):
        o_ref[:, g * HEAD_DIM:(g + 1) * HEAD_DIM] = o[g * tq:(g + 1) * tq].astype(o_ref.dtype)


def _attention(stream, q, k, v, cache, tq):
    n = stream.n_tok
    t = stream.seq_len
    q_tiles = t // tq
    group_w = Q_PER_KV * HEAD_DIM
    in_specs = [
        pl.BlockSpec((tq, group_w), lambda b, h, i: (b * q_tiles + i, h)),
        pl.BlockSpec((t, HEAD_DIM), lambda b, h, i: (b, h)),
        pl.BlockSpec((t, HEAD_DIM), lambda b, h, i: (b, h)),
    ]
    args = [q, k, v]
    if cache is not None:
        past = cache[0].shape[2]
        in_specs += [pl.BlockSpec((1, 1, past, HEAD_DIM), lambda b, h, i: (b, h, 0, 0))] * 2
        args += list(cache)
    return pl.pallas_call(
        functools.partial(_attn_kernel, tq=tq, has_cache=cache is not None),
        grid=(stream.n_seq, N_KV_HEADS, q_tiles),
        in_specs=in_specs,
        out_specs=pl.BlockSpec((tq, group_w), lambda b, h, i: (b * q_tiles + i, h)),
        out_shape=jax.ShapeDtypeStruct((n, ATTN_WIDTH), BF16),
        compiler_params=_params("parallel", "parallel", "parallel"),
        name="attention",
    )(*args)


def _route(logits):
    lane = lax.broadcasted_iota(jnp.int32, logits.shape, 1)
    lane_f = lane.astype(F32)
    far = float(ROUTE_LANES)
    is_group = lane < N_GROUPS
    gl = jnp.where(is_group, logits, NEG_BIG)
    gmax = jnp.max(gl, axis=-1, keepdims=True)
    gsum = jnp.sum(jnp.where(is_group, jnp.exp(gl - gmax), 0.0), axis=-1, keepdims=True)
    g_w = 1.0 / gsum
    gidx = jnp.min(jnp.where(gl == gmax, lane_f, far), axis=-1, keepdims=True)
    lo = EXPERT_LANE0 + EXPERTS_PER_GROUP * gidx
    in_group = (lane_f >= lo) & (lane_f < lo + EXPERTS_PER_GROUP)
    el = jnp.where(in_group, logits, NEG_BIG)
    v1 = jnp.max(el, axis=-1, keepdims=True)
    i1 = jnp.min(jnp.where(el == v1, lane_f, far), axis=-1, keepdims=True)
    el2 = jnp.where(lane_f == i1, NEG_BIG, el)
    v2 = jnp.max(el2, axis=-1, keepdims=True)
    i2 = jnp.min(jnp.where(el2 == v2, lane_f, far), axis=-1, keepdims=True)
    e2 = jnp.exp(v2 - v1)
    w1 = g_w / (1.0 + e2)
    w2 = w1 * e2
    return jnp.where(lane_f == i1, w1, 0.0) + jnp.where(lane_f == i2, w2, 0.0)


def _outproj_kernel(*refs, widths):
    n_in = len(widths)
    in_refs = refs[:n_in]
    w_ref, x_ref, mod_ref, g_ref, rw_ref, rb_ref, x1_ref, h_ref, comb_ref = refs[n_in:]
    acc = None
    off = 0
    for r, width in zip(in_refs, widths):
        part = _dot(r[...], w_ref[off:off + width, :])
        acc = part if acc is None else acc + part
        off += width
    x1 = x_ref[...] + mod_ref[0, 2:3, :] * acc
    x1_ref[...] = x1
    h = _modulated_norm(x1, g_ref[...], mod_ref[0, 3:4, :], mod_ref[0, 4:5, :])
    h_hi = h.astype(BF16)
    h_ref[...] = h_hi
    h_lo = (h - h_hi.astype(F32)).astype(BF16)
    logits = _dot(h_hi, rw_ref[0]) + _dot(h_hi, rw_ref[1]) + _dot(h_lo, rw_ref[0]) + rb_ref[...]
    comb_ref[...] = _route(logits)


def _outproj_route(stream, inputs, w_bf16, x, mod, ffn_norm_g, route_w, route_b):
    n = stream.n_tok
    widths = tuple(a.shape[1] for a in inputs)
    slot = _slot_map(stream)
    row = lambda i: (i, 0)
    fixed = lambda i: (0, 0)
    return pl.pallas_call(
        functools.partial(_outproj_kernel, widths=widths),
        grid=(n // ROW_TILE,),
        in_specs=[pl.BlockSpec((ROW_TILE, width), row) for width in widths] + [
            pl.BlockSpec((sum(widths), D_MODEL), fixed),
            pl.BlockSpec((ROW_TILE, D_MODEL), row),
            pl.BlockSpec((1, 6, D_MODEL), lambda i: (slot(i), 0, 0)),
            pl.BlockSpec((1, D_MODEL), fixed),
            pl.BlockSpec((2, D_MODEL, ROUTE_LANES), lambda i: (0, 0, 0)),
            pl.BlockSpec((1, ROUTE_LANES), fixed),
        ],
        out_specs=[
            pl.BlockSpec((ROW_TILE, D_MODEL), row),
            pl.BlockSpec((ROW_TILE, D_MODEL), row),
            pl.BlockSpec((ROW_TILE, ROUTE_LANES), row),
        ],
        out_shape=[
            jax.ShapeDtypeStruct((n, D_MODEL), F32),
            jax.ShapeDtypeStruct((n, D_MODEL), BF16),
            jax.ShapeDtypeStruct((n, ROUTE_LANES), F32),
        ],
        compiler_params=_params("parallel"),
        name="outproj_route",
    )(*inputs, w_bf16, x, mod, ffn_norm_g.reshape(1, D_MODEL), route_w, route_b)


def _router_params(grp_w, grp_b, rtr_w, rtr_b):
    w = jnp.concatenate([grp_w, jnp.moveaxis(rtr_w, 0, 1).reshape(D_MODEL, N_EXPERTS)], axis=1)
    w = jnp.pad(w, ((0, 0), (0, ROUTE_LANES - w.shape[1])))
    hi = w.astype(BF16)
    lo = (w - hi.astype(F32)).astype(BF16)
    b = jnp.concatenate([grp_b, rtr_b.reshape(N_EXPERTS)])
    b = jnp.pad(b, (0, ROUTE_LANES - b.shape[0])).reshape(1, ROUTE_LANES)
    return jnp.stack([hi, lo]), b


def _moe_kernel(h_ref, comb_ref, wg_ref, wu_ref, wd_ref, x_ref, mod_ref, fg_ref, o_ref, acc_ref, *, final_norm):
    e = pl.program_id(1)

    @pl.when(e == 0)
    def _():
        acc_ref[...] = jnp.zeros_like(acc_ref)

    h = h_ref[...]
    comb = comb_ref[...]
    lane = lax.broadcasted_iota(jnp.int32, comb.shape, 1)
    w_e = jnp.sum(jnp.where(lane == e + EXPERT_LANE0, comb, 0.0), axis=-1, keepdims=True)
    hidden = _silu(_dot(h, wg_ref[0])) * _dot(h, wu_ref[0]) * w_e
    acc_ref[...] += _dot(hidden.astype(BF16), wd_ref[0])

    @pl.when(e == N_EXPERTS - 1)
    def _():
        x = x_ref[...] + mod_ref[0, 5:6, :] * acc_ref[...]
        if final_norm:
            x = _rms(x) * fg_ref[...]
        o_ref[...] = x


def _moe(stream, h, comb, wg, wu, wd, x, mod, final_g, final_norm):
    n = stream.n_tok
    tm = MOE_ROW_TILE
    first_slot, seq_len = stream.first_slot, stream.seq_len
    slot = (lambda i: 0) if first_slot == 0 else (lambda i: first_slot + (i * tm) // seq_len)
    row = lambda i, e: (i, 0)
    return pl.pallas_call(
        functools.partial(_moe_kernel, final_norm=final_norm),
        grid=(n // tm, N_EXPERTS),
        in_specs=[
            pl.BlockSpec((tm, D_MODEL), row),
            pl.BlockSpec((tm, ROUTE_LANES), row),
            pl.BlockSpec((1, D_MODEL, EXPERT_FF), lambda i, e: (e, 0, 0)),
            pl.BlockSpec((1, D_MODEL, EXPERT_FF), lambda i, e: (e, 0, 0)),
            pl.BlockSpec((1, EXPERT_FF, D_MODEL), lambda i, e: (e, 0, 0)),
            pl.BlockSpec((tm, D_MODEL), row),
            pl.BlockSpec((1, 6, D_MODEL), lambda i, e: (slot(i), 0, 0)),
            pl.BlockSpec((1, D_MODEL), lambda i, e: (0, 0)),
        ],
        out_specs=pl.BlockSpec((tm, D_MODEL), row),
        out_shape=jax.ShapeDtypeStruct((n, D_MODEL), F32),
        scratch_shapes=[pltpu.VMEM((tm, D_MODEL), F32)],
        compiler_params=_params("parallel", "arbitrary"),
        name="moe",
    )(h, comb, wg, wu, wd, x, mod, final_g.reshape(1, D_MODEL))


IN1_COL = 1024
Q_BLOCKS = RET_QK_WIDTH // IN1_COL
K_BLOCKS = 2 * RET_QK_WIDTH // IN1_COL
V_BLOCKS = (2 * RET_QK_WIDTH + RET_V_WIDTH) // IN1_COL


def _inproj1_kernel(x_ref, mod_ref, g_ref, w_ref, cos_ref, sin_ref, o_ref, h_ref, *, rope):
    j = pl.program_id(1)

    @pl.when(j == 0)
    def _():
        h = _modulated_norm(x_ref[...], g_ref[...], mod_ref[0, 0:1, :], mod_ref[0, 1:2, :])
        h_ref[...] = h.astype(BF16)

    p = _dot(h_ref[...], w_ref[...])

    def store_qk(scale):
        cos = cos_ref[...]
        sin = sin_ref[...]
        for hh in range(IN1_COL // RET_DK):
            ph = p[:, hh * RET_DK:(hh + 1) * RET_DK]
            if rope:
                ph = _rope(ph, cos, sin)
            o_ref[:, hh * RET_DK:(hh + 1) * RET_DK] = (ph * scale).astype(o_ref.dtype)

    @pl.when(j < Q_BLOCKS)
    def _():
        store_qk(1.0)

    @pl.when((j >= Q_BLOCKS) & (j < K_BLOCKS))
    def _():
        store_qk(RET_DK ** -0.5)

    @pl.when((j >= K_BLOCKS) & (j < V_BLOCKS))
    def _():
        o_ref[...] = p.astype(o_ref.dtype)

    @pl.when(j >= V_BLOCKS)
    def _():
        o_ref[...] = _silu(p).astype(o_ref.dtype)


def _inproj1(stream, x, mod, norm_g, w_bf16, cos, sin):
    n = stream.n_tok
    tiles_per_seq = max(stream.seq_len // ROW_TILE, 1)
    rope_map = (lambda i, j: (i % tiles_per_seq, 0)) if stream.rope else (lambda i, j: (0, 0))
    slot = _slot_map(stream)
    return pl.pallas_call(
        functools.partial(_inproj1_kernel, rope=stream.rope),
        grid=(n // ROW_TILE, IN1_WIDTH // IN1_COL),
        in_specs=[
            pl.BlockSpec((ROW_TILE, D_MODEL), lambda i, j: (i, 0)),
            pl.BlockSpec((1, 6, D_MODEL), lambda i, j: (slot(i), 0, 0)),
            pl.BlockSpec((1, D_MODEL), lambda i, j: (0, 0)),
            pl.BlockSpec((D_MODEL, IN1_COL), lambda i, j: (0, j)),
            pl.BlockSpec((ROW_TILE, RET_DK), rope_map),
            pl.BlockSpec((ROW_TILE, RET_DK), rope_map),
        ],
        out_specs=pl.BlockSpec((ROW_TILE, IN1_COL), lambda i, j: (i, j)),
        out_shape=jax.ShapeDtypeStruct((n, IN1_WIDTH), BF16),
        scratch_shapes=[pltpu.VMEM((ROW_TILE, D_MODEL), BF16)],
        compiler_params=_params("parallel", "arbitrary"),
        name="inproj1",
    )(x, mod, norm_g.reshape(1, D_MODEL), w_bf16, cos, sin)


def _retention_kernel(*refs, n_chunks, has_state):
    if has_state:
        decay_ref, q_ref, k_ref, v_ref, gf_ref, gb_ref, sf0_ref, sb0_ref, o_ref, mix_ref = refs
    else:
        decay_ref, q_ref, k_ref, v_ref, gf_ref, gb_ref, o_ref, sf_ref, sb_ref, mix_ref = refs
    c = RET_CHUNK
    hd = pl.program_id(1)
    row = lax.broadcasted_iota(jnp.int32, (c, c), 0).astype(F32)
    col = lax.broadcasted_iota(jnp.int32, (c, c), 1).astype(F32)
    pos = lax.broadcasted_iota(jnp.int32, (c, 1), 0).astype(F32)

    def chunk(i):
        return pl.ds(pl.multiple_of(i * c, c), c)

    def sweep(direction, state0):
        lg = -jnp.exp(jnp.full((1, 1), decay_ref[direction, hd], F32))
        if direction == 0:
            diff = row - col
            q_pow = pos + 1.0
            k_pow = (c - 1.0) - pos
        else:
            diff = col - row
            q_pow = c - pos
            k_pow = pos
        intra = jnp.where(diff >= 0, jnp.exp(lg * jnp.maximum(diff, 0.0)), 0.0)
        q_dec = jnp.exp(lg * q_pow)
        k_dec = jnp.exp(lg * k_pow)
        chunk_dec = jnp.exp(lg * float(c))
        g_ref = gf_ref if direction == 0 else gb_ref

        def step(t, state):
            i = t if direction == 0 else n_chunks - 1 - t
            rows = chunk(i)
            qi = q_ref[rows, :]
            ki = k_ref[rows, :]
            vi = v_ref[rows, :]
            s = _dot_nt(qi, ki) * intra
            y = _dot(s.astype(BF16), vi) + _dot(qi, state.astype(BF16)) * q_dec
            k_scaled = (ki.astype(F32) * k_dec).astype(BF16)
            state = state * chunk_dec + _dot_tn(k_scaled, vi)
            gated = g_ref[rows, :].astype(F32) * _rms(y)
            if direction == 0:
                mix_ref[rows, :] = gated
            else:
                o_ref[rows, :] = (mix_ref[rows, :] + gated).astype(o_ref.dtype)
            return state

        return lax.fori_loop(0, n_chunks, step, state0)

    if has_state:
        sf = sweep(0, sf0_ref[0, 0])
        sb = sweep(1, sb0_ref[0, 0])
    else:
        zero = jnp.zeros((RET_DK, RET_DV), F32)
        sf = sweep(0, zero)
        sb = sweep(1, zero)
        sf_ref[0, 0] = sf
        sb_ref[0, 0] = sb


def _retention(stream, qkvg, decays, states):
    n = stream.n_tok
    t = stream.seq_len
    has_state = states is not None
    k_blk = RET_QK_WIDTH // RET_DK
    v_blk = 2 * RET_QK_WIDTH // RET_DV
    gf_blk = v_blk + RET_V_WIDTH // RET_DV
    gb_blk = gf_blk + RET_V_WIDTH // RET_DV
    in_specs = [
        pl.BlockSpec(memory_space=pltpu.SMEM),
        pl.BlockSpec((t, RET_DK), lambda b, h: (b, h)),
        pl.BlockSpec((t, RET_DK), lambda b, h: (b, k_blk + h)),
        pl.BlockSpec((t, RET_DV), lambda b, h: (b, v_blk + h)),
        pl.BlockSpec((t, RET_DV), lambda b, h: (b, gf_blk + h)),
        pl.BlockSpec((t, RET_DV), lambda b, h: (b, gb_blk + h)),
    ]
    args = [decays, qkvg, qkvg, qkvg, qkvg, qkvg]
    state_spec = pl.BlockSpec((1, 1, RET_DK, RET_DV), lambda b, h: (b, h, 0, 0))
    out_specs = [pl.BlockSpec((t, RET_DV), lambda b, h: (b, h))]
    out_shape = [jax.ShapeDtypeStruct((n, RET_V_WIDTH), BF16)]
    if has_state:
        in_specs += [state_spec, state_spec]
        args += list(states)
    else:
        out_specs += [state_spec, state_spec]
        out_shape += [jax.ShapeDtypeStruct((stream.n_seq, RET_HEADS, RET_DK, RET_DV), F32)] * 2
    return pl.pallas_call(
        functools.partial(_retention_kernel, n_chunks=t // RET_CHUNK, has_state=has_state),
        grid=(stream.n_seq, RET_HEADS),
        in_specs=in_specs,
        out_specs=out_specs,
        out_shape=out_shape,
        scratch_shapes=[pltpu.VMEM((t, RET_DV), F32)],
        compiler_params=_params("parallel", "parallel"),
        name="retention",
    )(*args)


def _rope_tables(n_tokens, dim):
    t = jnp.arange(n_tokens)
    pos = jnp.stack([t // GRID_W, t % GRID_W]).astype(F32)
    n_freq = dim // 4
    freqs = ROPE_THETA ** (-jnp.arange(n_freq, dtype=F32) / n_freq)
    ang = pos[:, :, None] * freqs
    cos, sin = jnp.cos(ang), jnp.sin(ang)
    cos_t = jnp.concatenate([cos[0], cos[0], cos[1], cos[1]], axis=-1)
    sin_t = jnp.concatenate([-sin[0], sin[0], -sin[1], sin[1]], axis=-1)
    return cos_t, sin_t


def kernel(x_prompt, x_sample, cache_k0, cache_v0, state_ret_fwd1, state_ret_bwd1, c, c_ctx,
           norm_mix_g0, mod_w0, mod_b0, in_w0, conv_w0, conv_b0, conv_norm_g0, conv_norm_b0,
           q_norm_g0, k_norm_g0, out_w0, norm_ffn_g0,
           moe_grp_w0, moe_grp_b0, moe_rtr_w0, moe_rtr_b0, moe_w_gate0, moe_w_up0, moe_w_down0,
           norm_mix_g1, mod_w1, mod_b1, in_w1, ret_decay_fwd1, ret_decay_bwd1, out_w1, norm_ffn_g1,
           moe_grp_w1, moe_grp_b1, moe_rtr_w1, moe_rtr_b1, moe_w_gate1, moe_w_up1, moe_w_down1,
           final_norm_g):
    batch, seq, d = x_prompt.shape
    dec_batch, dec_seq, _ = x_sample.shape
    assert d == D_MODEL and 1 + dec_batch <= MOD_SLOTS
    ctx = _Stream(batch, seq, first_slot=0, rope=False)
    lat = _Stream(dec_batch, dec_seq, first_slot=1, rope=True)
    assert ctx.n_tok % MOE_ROW_TILE == 0 and lat.n_tok % MOE_ROW_TILE == 0
    assert seq % CONV_TILE == 0 and dec_seq % MOE_ROW_TILE == 0 and ROW_TILE % seq == 0

    cond = jnp.concatenate([c_ctx[None, :], c, jnp.zeros((MOD_SLOTS - 1 - dec_batch, d), F32)], axis=0)
    mod0 = _modulation(cond, mod_w0, mod_b0)
    mod1 = _modulation(cond, mod_w1, mod_b1)
    cos, sin = _rope_tables(dec_seq, HEAD_DIM)
    decays = jnp.stack([ret_decay_fwd1, ret_decay_bwd1]).astype(F32)

    in_w0_b = in_w0.astype(BF16)
    out_w0_b = out_w0.astype(BF16)
    in_w1_b = in_w1.astype(BF16)
    out_w1_b = out_w1.astype(BF16)
    moe0 = (moe_w_gate0.astype(BF16), moe_w_up0.astype(BF16), moe_w_down0.astype(BF16))
    moe1 = (moe_w_gate1.astype(BF16), moe_w_up1.astype(BF16), moe_w_down1.astype(BF16))
    route0 = _router_params(moe_grp_w0, moe_grp_b0, moe_rtr_w0, moe_rtr_b0)
    route1 = _router_params(moe_grp_w1, moe_grp_b1, moe_rtr_w1, moe_rtr_b1)

    def run(stream, x, cache, states):
        x = x.reshape(stream.n_tok, d)
        kv_dtype = BF16 if stream.rope else F32
        a, q, k, v = _inproj0(stream, x, mod0, norm_mix_g0, in_w0_b, q_norm_g0, k_norm_g0, cos, sin, kv_dtype)
        a = _conv_branch(stream, a, conv_w0, conv_b0, conv_norm_g0, conv_norm_b0)
        tq = min(stream.seq_len, 128 if cache is not None else 256)
        o = _attention(stream, q, k, v, cache, tq)
        x, h, comb = _outproj_route(stream, [a, o], out_w0_b, x, mod0, norm_ffn_g0, *route0)
        x = _moe(stream, h, comb, *moe0, x, mod0, final_norm_g, final_norm=False)
        qkvg = _inproj1(stream, x, mod1, norm_mix_g1, in_w1_b, cos, sin)
        ret = _retention(stream, qkvg, decays, states)
        x, h, comb = _outproj_route(stream, [ret[0]], out_w1_b, x, mod1, norm_ffn_g1, *route1)
        y = _moe(stream, h, comb, *moe1, x, mod1, final_norm_g, final_norm=True)
        return y.reshape(stream.n_seq, stream.seq_len, d), k, v, ret[1:]

    y_prompt, k_ctx, v_ctx, new_states = run(ctx, x_prompt, None, None)
    y_sample, _, _, _ = run(lat, x_sample, (cache_k0, cache_v0), (state_ret_fwd1, state_ret_bwd1))
    new_k0 = k_ctx.reshape(batch, seq, N_KV_HEADS, HEAD_DIM).transpose(0, 2, 1, 3)
    new_v0 = v_ctx.reshape(batch, seq, N_KV_HEADS, HEAD_DIM).transpose(0, 2, 1, 3)
    return (y_prompt, y_sample, new_k0, new_v0, new_states[0], new_states[1])
```

```python
import functools

import jax
import jax.numpy as jnp
from jax import lax
from jax.experimental import pallas as pl
from jax.experimental.pallas import tpu as pltpu

D_MODEL = 1024
GRID_W = 64
EPS = 1e-6
CONV_CH = 512
CONV_WIDTH = 31
CONV_HALF = CONV_WIDTH // 2
N_Q_HEADS = 8
N_KV_HEADS = 2
Q_PER_KV = N_Q_HEADS // N_KV_HEADS
HEAD_DIM = 128
ROPE_THETA = 10000.0
ATTN_WIDTH = N_Q_HEADS * HEAD_DIM
KV_WIDTH = N_KV_HEADS * HEAD_DIM
IN0_WIDTH = 2 * CONV_CH + ATTN_WIDTH + 2 * KV_WIDTH
RET_HEADS = 8
RET_DK = 128
RET_DV = 256
RET_CHUNK = 128
RET_QK_WIDTH = RET_HEADS * RET_DK
RET_V_WIDTH = RET_HEADS * RET_DV
IN1_WIDTH = 2 * RET_QK_WIDTH + 3 * RET_V_WIDTH
N_GROUPS = 4
EXPERTS_PER_GROUP = 4
N_EXPERTS = N_GROUPS * EXPERTS_PER_GROUP
EXPERT_FF = 512

LANES = 128
SUBLANES = 8
VMEM_LIMIT_BYTES = 56 * 1024 * 1024

MOD_SLOTS = 8
ROW_TILE = 512
TOP_K = 2
ROW_CHUNKS = D_MODEL // LANES
FFN_TILE = 256
DISPATCH_TILE = 256
COMBINE_TILE = 256
ROUTE_E1, ROUTE_E2, ROUTE_W1, ROUTE_W2, ROUTE_R1, ROUTE_R2 = range(6)
CONV_TILE = 256
CONV_HALO = 16
CONV_ROWS = 64
ROUTE_LANES = LANES
EXPERT_LANE0 = N_GROUPS
NEG_BIG = -1e30

F32 = jnp.float32
BF16 = jnp.bfloat16


def _params(*semantics):
    return pltpu.CompilerParams(dimension_semantics=semantics, vmem_limit_bytes=VMEM_LIMIT_BYTES)


def _sigmoid(x):
    return 1.0 / (1.0 + jnp.exp(-x))


def _silu(x):
    return x * _sigmoid(x)


def _rms(x):
    return x * lax.rsqrt(jnp.mean(x * x, axis=-1, keepdims=True) + EPS)


def _rope(x, cos, sin):
    lane = lax.broadcasted_iota(jnp.int32, x.shape, 1)
    take_upper = (lane % (HEAD_DIM // 2)) < (HEAD_DIM // 4)
    partner = jnp.where(take_upper, pltpu.roll(x, HEAD_DIM - HEAD_DIM // 4, 1), pltpu.roll(x, HEAD_DIM // 4, 1))
    return x * cos + partner * sin


def _dot(a, b):
    return jnp.dot(a, b, preferred_element_type=F32)


def _dot_nt(a, b):
    return lax.dot_general(a, b, (((1,), (1,)), ((), ())), preferred_element_type=F32)


def _dot_tn(a, b):
    return lax.dot_general(a, b, (((0,), (0,)), ((), ())), preferred_element_type=F32)


def _mod_kernel(c_ref, w_ref, b_ref, o_ref):
    c = c_ref[...]
    o_ref[...] = _dot(_silu(c).astype(BF16), w_ref[...].astype(BF16)) + b_ref[...]


def _modulation(cond, mod_w, mod_b):
    n_out = mod_w.shape[1]
    col = D_MODEL
    out = pl.pallas_call(
        _mod_kernel,
        grid=(n_out // col,),
        in_specs=[
            pl.BlockSpec((MOD_SLOTS, D_MODEL), lambda j: (0, 0)),
            pl.BlockSpec((D_MODEL, col), lambda j: (0, j)),
            pl.BlockSpec((1, col), lambda j: (0, j)),
        ],
        out_specs=pl.BlockSpec((MOD_SLOTS, col), lambda j: (0, j)),
        out_shape=jax.ShapeDtypeStruct((MOD_SLOTS, n_out), F32),
        compiler_params=_params("parallel"),
        name="modulation",
    )(cond, mod_w, mod_b.reshape(1, n_out))
    return out.reshape(MOD_SLOTS, 6, D_MODEL)


def _slot_map(stream):
    if stream.first_slot == 0:
        return lambda i: 0
    return lambda i: stream.first_slot + (i * ROW_TILE) // stream.seq_len


class _Stream:
    def __init__(self, n_seq, seq_len, first_slot, rope):
        self.n_seq = n_seq
        self.seq_len = seq_len
        self.first_slot = first_slot
        self.rope = rope
        self.n_tok = n_seq * seq_len


def _modulated_norm(x, g, shift, scale):
    return _rms(x) * g * (1.0 + scale) + shift


def _inproj0_kernel(x_ref, mod_ref, g_ref, w_ref, qg_ref, kg_ref, cos_ref, sin_ref,
                    a_ref, q_ref, k_ref, v_ref, *, rope):
    h = _modulated_norm(x_ref[...], g_ref[...], mod_ref[0, 0:1, :], mod_ref[0, 1:2, :])
    p = _dot(h.astype(BF16), w_ref[...])
    a_ref[...] = p[:, :CONV_CH] * _sigmoid(p[:, CONV_CH:2 * CONV_CH])
    q0 = 2 * CONV_CH
    k0 = q0 + ATTN_WIDTH
    v0 = k0 + KV_WIDTH
    cos = cos_ref[...]
    sin = sin_ref[...]
    q_scale = HEAD_DIM ** -0.5
    for hh in range(N_Q_HEADS):
        qh = _rms(p[:, q0 + hh * HEAD_DIM:q0 + (hh + 1) * HEAD_DIM]) * qg_ref[...]
        if rope:
            qh = _rope(qh, cos, sin)
        q_ref[:, hh * HEAD_DIM:(hh + 1) * HEAD_DIM] = (qh * q_scale).astype(q_ref.dtype)
    for hh in range(N_KV_HEADS):
        kh = _rms(p[:, k0 + hh * HEAD_DIM:k0 + (hh + 1) * HEAD_DIM]) * kg_ref[...]
        if rope:
            kh = _rope(kh, cos, sin)
        k_ref[:, hh * HEAD_DIM:(hh + 1) * HEAD_DIM] = kh.astype(k_ref.dtype)
    v_ref[...] = p[:, v0:v0 + KV_WIDTH].astype(v_ref.dtype)


def _inproj0(stream, x, mod, norm_g, w_bf16, q_norm_g, k_norm_g, cos, sin, kv_dtype):
    n = stream.n_tok
    tiles_per_seq = max(stream.seq_len // ROW_TILE, 1)
    rope_map = (lambda i: (i % tiles_per_seq, 0)) if stream.rope else (lambda i: (0, 0))
    slot = _slot_map(stream)
    row = lambda i: (i, 0)
    fixed = lambda i: (0, 0)
    return pl.pallas_call(
        functools.partial(_inproj0_kernel, rope=stream.rope),
        grid=(n // ROW_TILE,),
        in_specs=[
            pl.BlockSpec((ROW_TILE, D_MODEL), row),
            pl.BlockSpec((1, 6, D_MODEL), lambda i: (slot(i), 0, 0)),
            pl.BlockSpec((1, D_MODEL), fixed),
            pl.BlockSpec((D_MODEL, IN0_WIDTH), fixed),
            pl.BlockSpec((1, HEAD_DIM), fixed),
            pl.BlockSpec((1, HEAD_DIM), fixed),
            pl.BlockSpec((ROW_TILE, HEAD_DIM), rope_map),
            pl.BlockSpec((ROW_TILE, HEAD_DIM), rope_map),
        ],
        out_specs=[
            pl.BlockSpec((ROW_TILE, CONV_CH), row),
            pl.BlockSpec((ROW_TILE, ATTN_WIDTH), row),
            pl.BlockSpec((ROW_TILE, KV_WIDTH), row),
            pl.BlockSpec((ROW_TILE, KV_WIDTH), row),
        ],
        out_shape=[
            jax.ShapeDtypeStruct((n, CONV_CH), F32),
            jax.ShapeDtypeStruct((n, ATTN_WIDTH), BF16),
            jax.ShapeDtypeStruct((n, KV_WIDTH), kv_dtype),
            jax.ShapeDtypeStruct((n, KV_WIDTH), kv_dtype),
        ],
        compiler_params=_params("parallel"),
        name="inproj0",
    )(x, mod, norm_g.reshape(1, D_MODEL), w_bf16, q_norm_g.reshape(1, HEAD_DIM), k_norm_g.reshape(1, HEAD_DIM),
      cos, sin)


def _conv_kernel(prev_ref, main_ref, next_ref, w_ref, b_ref, g_ref, beta_ref, o_ref, pad_ref, acc_ref, *,
                 tiles_per_seq):
    i = pl.program_id(0)
    has_prev = (i % tiles_per_seq) != 0
    has_next = (i % tiles_per_seq) != (tiles_per_seq - 1)
    pad_ref[0:CONV_HALO, :] = jnp.where(has_prev, prev_ref[...], 0.0)
    pad_ref[CONV_HALO:CONV_HALO + CONV_TILE, :] = main_ref[...]
    pad_ref[CONV_HALO + CONV_TILE:, :] = jnp.where(has_next, next_ref[...], 0.0)
    first = CONV_HALO - CONV_HALF
    for c0 in range(0, CONV_CH, LANES):
        for r0 in range(0, CONV_TILE, CONV_ROWS):
            acc = jnp.zeros((CONV_ROWS, LANES), F32)
            for j in range(CONV_WIDTH):
                acc = acc + w_ref[j:j + 1, c0:c0 + LANES] * pad_ref[first + r0 + j:first + r0 + j + CONV_ROWS,
                                                                    c0:c0 + LANES]
            acc_ref[r0:r0 + CONV_ROWS, c0:c0 + LANES] = acc
    a = acc_ref[...] + b_ref[...]
    mu = jnp.mean(a, axis=-1, keepdims=True)
    d = a - mu
    var = jnp.mean(d * d, axis=-1, keepdims=True)
    y = d * lax.rsqrt(var + EPS) * g_ref[...] + beta_ref[...]
    o_ref[...] = _silu(y).astype(o_ref.dtype)


def _conv_branch(stream, a, conv_w, conv_b, norm_g, norm_b):
    n = stream.n_tok
    tiles_per_seq = stream.seq_len // CONV_TILE
    halo_per_tile = CONV_TILE // CONV_HALO
    n_halo = n // CONV_HALO
    fixed = lambda i: (0, 0)
    return pl.pallas_call(
        functools.partial(_conv_kernel, tiles_per_seq=tiles_per_seq),
        grid=(n // CONV_TILE,),
        in_specs=[
            pl.BlockSpec((CONV_HALO, CONV_CH), lambda i: (jnp.maximum(i * halo_per_tile - 1, 0), 0)),
            pl.BlockSpec((CONV_TILE, CONV_CH), lambda i: (i, 0)),
            pl.BlockSpec((CONV_HALO, CONV_CH), lambda i: (jnp.minimum((i + 1) * halo_per_tile, n_halo - 1), 0)),
            pl.BlockSpec((CONV_WIDTH, CONV_CH), fixed),
            pl.BlockSpec((1, CONV_CH), fixed),
            pl.BlockSpec((1, CONV_CH), fixed),
            pl.BlockSpec((1, CONV_CH), fixed),
        ],
        out_specs=pl.BlockSpec((CONV_TILE, CONV_CH), lambda i: (i, 0)),
        out_shape=jax.ShapeDtypeStruct((n, CONV_CH), BF16),
        scratch_shapes=[
            pltpu.VMEM((CONV_TILE + 2 * CONV_HALO, CONV_CH), F32),
            pltpu.VMEM((CONV_TILE, CONV_CH), F32),
        ],
        compiler_params=_params("parallel"),
        name="conv_branch",
    )(a, a, a, conv_w, conv_b.reshape(1, CONV_CH), norm_g.reshape(1, CONV_CH), norm_b.reshape(1, CONV_CH))


def _attn_kernel(*refs, tq, has_cache):
    if has_cache:
        q_ref, k_ref, v_ref, kc_ref, vc_ref, o_ref = refs
    else:
        q_ref, k_ref, v_ref, o_ref = refs
    q = q_ref[...]
    q4 = jnp.concatenate([q[:, g * HEAD_DIM:(g + 1) * HEAD_DIM] for g in range(Q_PER_KV)], axis=0)
    s = _dot_nt(q4, k_ref[...].astype(BF16))
    m = jnp.max(s, axis=-1, keepdims=True)
    if has_cache:
        sc = _dot_nt(q4, kc_ref[0, 0].astype(BF16))
        m = jnp.maximum(m, jnp.max(sc, axis=-1, keepdims=True))
    p = jnp.exp(s - m)
    l = jnp.sum(p, axis=-1, keepdims=True)
    o = _dot(p.astype(BF16), v_ref[...].astype(BF16))
    if has_cache:
        pc = jnp.exp(sc - m)
        l = l + jnp.sum(pc, axis=-1, keepdims=True)
        o = o + _dot(pc.astype(BF16), vc_ref[0, 0].astype(BF16))
    o = o * (1.0 / l)
    for g in range(Q_PER_KV):
        o_ref[:, g * HEAD_DIM:(g + 1) * HEAD_DIM] = o[g * tq:(g + 1) * tq].astype(o_ref.dtype)


def _attention(stream, q, k, v, cache, tq):
    n = stream.n_tok
    t = stream.seq_len
    q_tiles = t // tq
    group_w = Q_PER_KV * HEAD_DIM
    in_specs = [
        pl.BlockSpec((tq, group_w), lambda b, h, i: (b * q_tiles + i, h)),
        pl.BlockSpec((t, HEAD_DIM), lambda b, h, i: (b, h)),
        pl.BlockSpec((t, HEAD_DIM), lambda b, h, i: (b, h)),
    ]
    args = [q, k, v]
    if cache is not None:
        past = cache[0].shape[2]
        in_specs += [pl.BlockSpec((1, 1, past, HEAD_DIM), lambda b, h, i: (b, h, 0, 0))] * 2
        args += list(cache)
    return pl.pallas_call(
        functools.partial(_attn_kernel, tq=tq, has_cache=cache is not None),
        grid=(stream.n_seq, N_KV_HEADS, q_tiles),
        in_specs=in_specs,
        out_specs=pl.BlockSpec((tq, group_w), lambda b, h, i: (b * q_tiles + i, h)),
        out_shape=jax.ShapeDtypeStruct((n, ATTN_WIDTH), BF16),
        compiler_params=_params("parallel", "parallel", "parallel"),
        name="attention",
    )(*args)


def _route(logits, running):
    rows = logits.shape[0]
    lane = lax.broadcasted_iota(jnp.int32, logits.shape, 1)
    lane_f = lane.astype(F32)
    far = float(ROUTE_LANES)
    is_group = lane < N_GROUPS
    gl = jnp.where(is_group, logits, NEG_BIG)
    gmax = jnp.max(gl, axis=-1, keepdims=True)
    gsum = jnp.sum(jnp.where(is_group, jnp.exp(gl - gmax), 0.0), axis=-1, keepdims=True)
    g_w = 1.0 / gsum
    gidx = jnp.min(jnp.where(gl == gmax, lane_f, far), axis=-1, keepdims=True)
    lo = EXPERT_LANE0 + EXPERTS_PER_GROUP * gidx
    in_group = (lane_f >= lo) & (lane_f < lo + EXPERTS_PER_GROUP)
    el = jnp.where(in_group, logits, NEG_BIG)
    v1 = jnp.max(el, axis=-1, keepdims=True)
    i1 = jnp.min(jnp.where(el == v1, lane_f, far), axis=-1, keepdims=True)
    el2 = jnp.where(lane_f == i1, NEG_BIG, el)
    v2 = jnp.max(el2, axis=-1, keepdims=True)
    i2 = jnp.min(jnp.where(el2 == v2, lane_f, far), axis=-1, keepdims=True)
    e2 = jnp.exp(v2 - v1)
    w1 = g_w / (1.0 + e2)
    w2 = w1 * e2
    is1 = lane_f == i1
    is2 = lane_f == i2
    chosen = (is1 | is2).astype(BF16)
    earlier = (lax.broadcasted_iota(jnp.int32, (rows, rows), 0)
               > lax.broadcasted_iota(jnp.int32, (rows, rows), 1)).astype(BF16)
    before = running + _dot(earlier, chosen)
    r1 = jnp.sum(jnp.where(is1, before, 0.0), axis=-1, keepdims=True)
    r2 = jnp.sum(jnp.where(is2, before, 0.0), axis=-1, keepdims=True)
    record = jnp.zeros_like(logits)
    for at, val in ((ROUTE_E1, i1 - EXPERT_LANE0), (ROUTE_E2, i2 - EXPERT_LANE0), (ROUTE_W1, w1), (ROUTE_W2, w2),
                    (ROUTE_R1, r1), (ROUTE_R2, r2)):
        record = jnp.where(lane == at, val, record)
    return record, running + jnp.sum(chosen.astype(F32), axis=0, keepdims=True)


def _outproj_kernel(*refs, widths):
    n_in = len(widths)
    in_refs = refs[:n_in]
    w_ref, x_ref, mod_ref, g_ref, rw_ref, rb_ref, x1_ref, h_ref, route_ref, count_ref, run_ref = refs[n_in:]

    @pl.when(pl.program_id(0) == 0)
    def _():
        run_ref[...] = jnp.zeros_like(run_ref)

    acc = None
    off = 0
    for r, width in zip(in_refs, widths):
        part = _dot(r[...], w_ref[off:off + width, :])
        acc = part if acc is None else acc + part
        off += width
    x1 = x_ref[...] + mod_ref[0, 2:3, :] * acc
    x1_ref[...] = x1
    h = _modulated_norm(x1, g_ref[...], mod_ref[0, 3:4, :], mod_ref[0, 4:5, :])
    _store_row_major(h_ref, h)
    h_hi = h.astype(BF16)
    h_lo = (h - h_hi.astype(F32)).astype(BF16)
    logits = _dot(h_hi, rw_ref[0]) + _dot(h_hi, rw_ref[1]) + _dot(h_lo, rw_ref[0]) + rb_ref[...]
    record, running = _route(logits, run_ref[...])
    route_ref[...] = record
    run_ref[...] = running
    count_ref[...] = jnp.broadcast_to(running, count_ref.shape)


def _store_row_major(ref, x):
    rows = x.shape[0]
    for c in range(ROW_CHUNKS):
        ref[pl.ds(c, rows, stride=ROW_CHUNKS), :] = x[:, c * LANES:(c + 1) * LANES]


def _load_row_major(ref, rows):
    return jnp.concatenate([ref[pl.ds(c, rows, stride=ROW_CHUNKS), :] for c in range(ROW_CHUNKS)], axis=-1)


def _outproj_route(stream, inputs, w_bf16, x, mod, ffn_norm_g, route_w, route_b):
    n = stream.n_tok
    widths = tuple(a.shape[1] for a in inputs)
    slot = _slot_map(stream)
    row = lambda i: (i, 0)
    fixed = lambda i: (0, 0)
    return pl.pallas_call(
        functools.partial(_outproj_kernel, widths=widths),
        grid=(n // ROW_TILE,),
        in_specs=[pl.BlockSpec((ROW_TILE, width), row) for width in widths] + [
            pl.BlockSpec((sum(widths), D_MODEL), fixed),
            pl.BlockSpec((ROW_TILE, D_MODEL), row),
            pl.BlockSpec((1, 6, D_MODEL), lambda i: (slot(i), 0, 0)),
            pl.BlockSpec((1, D_MODEL), fixed),
            pl.BlockSpec((2, D_MODEL, ROUTE_LANES), lambda i: (0, 0, 0)),
            pl.BlockSpec((1, ROUTE_LANES), fixed),
        ],
        out_specs=[
            pl.BlockSpec((ROW_TILE, D_MODEL), row),
            pl.BlockSpec((ROW_TILE * ROW_CHUNKS, LANES), row),
            pl.BlockSpec((ROW_TILE, ROUTE_LANES), row),
            pl.BlockSpec((SUBLANES, ROUTE_LANES), fixed),
        ],
        out_shape=[
            jax.ShapeDtypeStruct((n, D_MODEL), F32),
            jax.ShapeDtypeStruct((n * ROW_CHUNKS, LANES), F32),
            jax.ShapeDtypeStruct((n, ROUTE_LANES), F32),
            jax.ShapeDtypeStruct((SUBLANES, ROUTE_LANES), F32),
        ],
        scratch_shapes=[pltpu.VMEM((1, ROUTE_LANES), F32)],
        compiler_params=_params("arbitrary"),
        name="outproj_route",
    )(*inputs, w_bf16, x, mod, ffn_norm_g.reshape(1, D_MODEL), route_w, route_b)


def _router_params(grp_w, grp_b, rtr_w, rtr_b):
    w = jnp.concatenate([grp_w, jnp.moveaxis(rtr_w, 0, 1).reshape(D_MODEL, N_EXPERTS)], axis=1)
    w = jnp.pad(w, ((0, 0), (0, ROUTE_LANES - w.shape[1])))
    hi = w.astype(BF16)
    lo = (w - hi.astype(F32)).astype(BF16)
    b = jnp.concatenate([grp_b, rtr_b.reshape(N_EXPERTS)])
    b = jnp.pad(b, (0, ROUTE_LANES - b.shape[0])).reshape(1, ROUTE_LANES)
    return jnp.stack([hi, lo]), b


def _sorted_tiles(n_tok):
    return (TOP_K * n_tok) // FFN_TILE + N_EXPERTS


def _dispatch_plan(route, counts, n_tok):
    count = counts[0, EXPERT_LANE0:EXPERT_LANE0 + N_EXPERTS].astype(jnp.int32)
    tiles = (count + (FFN_TILE - 1)) // FFN_TILE
    padded = tiles * FFN_TILE
    start = jnp.cumsum(padded) - padded
    n_used = jnp.sum(tiles)
    tile_end = jnp.cumsum(tiles)
    j = jnp.minimum(jnp.arange(_sorted_tiles(n_tok), dtype=jnp.int32), n_used - 1)
    tile_expert = jnp.sum(j[:, None] >= tile_end[None, :], axis=1).astype(jnp.int32)
    e1 = route[:, ROUTE_E1].astype(jnp.int32)
    e2 = route[:, ROUTE_E2].astype(jnp.int32)
    r1 = route[:, ROUTE_R1].astype(jnp.int32)
    r2 = route[:, ROUTE_R2].astype(jnp.int32)
    dest = jnp.concatenate([jnp.take(start, e1) + r1, jnp.take(start, e2) + r2])
    fill = jnp.concatenate([start + count, start + padded])
    return dest, fill, tile_expert, n_used.reshape(1)


def _token_rows(t):
    return pl.ds(pl.multiple_of(t * ROW_CHUNKS, ROW_CHUNKS), ROW_CHUNKS)


def _dispatch_kernel(dest_ref, fill_ref, h_ref, hs_ref, zero_ref, sems, *, n_tok):
    i = pl.program_id(0)
    base = i * DISPATCH_TILE

    def row_copy(t, k):
        d = dest_ref[k * n_tok + base + t]
        return pltpu.make_async_copy(h_ref.at[_token_rows(t)], hs_ref.at[_token_rows(d)], sems.at[k])

    def issue(t, carry):
        for k in range(TOP_K):
            row_copy(t, k).start()
        return carry

    lax.fori_loop(0, DISPATCH_TILE, issue, 0, unroll=8)

    def fill_copy(r):
        return pltpu.make_async_copy(zero_ref, hs_ref.at[_token_rows(r)], sems.at[TOP_K])

    @pl.when(i == 0)
    def _():
        zero_ref[...] = jnp.zeros_like(zero_ref)
        for e in range(N_EXPERTS):
            lax.fori_loop(fill_ref[e], fill_ref[N_EXPERTS + e], lambda r, c: (fill_copy(r).start(), c)[1], 0)
        for e in range(N_EXPERTS):
            lax.fori_loop(fill_ref[e], fill_ref[N_EXPERTS + e], lambda r, c: (fill_copy(r).wait(), c)[1], 0)

    for k in range(TOP_K):
        pltpu.make_async_copy(h_ref, hs_ref.at[pl.ds(0, DISPATCH_TILE * ROW_CHUNKS)], sems.at[k]).wait()


def _dispatch(n_tok, dest, fill, h_rows):
    rows = _sorted_tiles(n_tok) * FFN_TILE
    return pl.pallas_call(
        functools.partial(_dispatch_kernel, n_tok=n_tok),
        grid_spec=pltpu.PrefetchScalarGridSpec(
            num_scalar_prefetch=2,
            grid=(n_tok // DISPATCH_TILE,),
            in_specs=[pl.BlockSpec((DISPATCH_TILE * ROW_CHUNKS, LANES), lambda i, d, f: (i, 0))],
            out_specs=pl.BlockSpec(memory_space=pl.ANY),
            scratch_shapes=[pltpu.VMEM((ROW_CHUNKS, LANES), F32), pltpu.SemaphoreType.DMA((TOP_K + 1,))],
        ),
        out_shape=jax.ShapeDtypeStruct((rows * ROW_CHUNKS, LANES), F32),
        compiler_params=_params("arbitrary"),
        name="moe_dispatch",
    )(dest, fill, h_rows)


def _ffn_kernel(te_ref, nu_ref, hs_ref, wg_ref, wu_ref, wd_ref, ys_ref):
    @pl.when(pl.program_id(0) < nu_ref[0])
    def _():
        x = _load_row_major(hs_ref, FFN_TILE).astype(BF16)
        hidden = _silu(_dot(x, wg_ref[0])) * _dot(x, wu_ref[0])
        _store_row_major(ys_ref, _dot(hidden.astype(BF16), wd_ref[0]))


def _expert_ffn(n_tok, tile_expert, n_used, hs, wg, wu, wd):
    rows = lambda j, te, nu: (jnp.minimum(j, nu[0] - 1), 0)
    expert = lambda j, te, nu: (te[j], 0, 0)
    return pl.pallas_call(
        _ffn_kernel,
        grid_spec=pltpu.PrefetchScalarGridSpec(
            num_scalar_prefetch=2,
            grid=(_sorted_tiles(n_tok),),
            in_specs=[
                pl.BlockSpec((FFN_TILE * ROW_CHUNKS, LANES), rows),
                pl.BlockSpec((1, D_MODEL, EXPERT_FF), expert),
                pl.BlockSpec((1, D_MODEL, EXPERT_FF), expert),
                pl.BlockSpec((1, EXPERT_FF, D_MODEL), expert),
            ],
            out_specs=pl.BlockSpec((FFN_TILE * ROW_CHUNKS, LANES), rows),
        ),
        out_shape=jax.ShapeDtypeStruct(hs.shape, F32),
        compiler_params=_params("arbitrary"),
        name="moe_ffn",
    )(tile_expert, n_used, hs, wg, wu, wd)


def _combine_kernel(dest_ref, route_ref, x_ref, mod_ref, fg_ref, ys_ref, o_ref, y_ref, sems, *, n_tok, final_norm):
    base = pl.program_id(0) * COMBINE_TILE

    def row_copy(t, k):
        d = dest_ref[k * n_tok + base + t]
        return pltpu.make_async_copy(ys_ref.at[_token_rows(d)], y_ref.at[k, _token_rows(t)], sems.at[k])

    def issue(t, carry):
        for k in range(TOP_K):
            row_copy(t, k).start()
        return carry

    lax.fori_loop(0, COMBINE_TILE, issue, 0, unroll=8)
    route = route_ref[...]
    w1 = route[:, ROUTE_W1:ROUTE_W1 + 1]
    w2 = route[:, ROUTE_W2:ROUTE_W2 + 1]
    for k in range(TOP_K):
        pltpu.make_async_copy(ys_ref.at[pl.ds(0, COMBINE_TILE * ROW_CHUNKS)], y_ref.at[k], sems.at[k]).wait()
    y = w1 * _load_row_major(y_ref.at[0], COMBINE_TILE) + w2 * _load_row_major(y_ref.at[1], COMBINE_TILE)
    x = x_ref[...] + mod_ref[0, 5:6, :] * y
    if final_norm:
        x = _rms(x) * fg_ref[...]
    o_ref[...] = x


def _combine(stream, dest, route, x, mod, final_g, ys, final_norm):
    n = stream.n_tok
    first_slot, seq_len = stream.first_slot, stream.seq_len
    slot = (lambda i: 0) if first_slot == 0 else (lambda i: first_slot + (i * COMBINE_TILE) // seq_len)
    row = lambda i, d: (i, 0)
    return pl.pallas_call(
        functools.partial(_combine_kernel, n_tok=n, final_norm=final_norm),
        grid_spec=pltpu.PrefetchScalarGridSpec(
            num_scalar_prefetch=1,
            grid=(n // COMBINE_TILE,),
            in_specs=[
                pl.BlockSpec((COMBINE_TILE, ROUTE_LANES), row),
                pl.BlockSpec((COMBINE_TILE, D_MODEL), row),
                pl.BlockSpec((1, 6, D_MODEL), lambda i, d: (slot(i), 0, 0)),
                pl.BlockSpec((1, D_MODEL), lambda i, d: (0, 0)),
                pl.BlockSpec(memory_space=pl.ANY),
            ],
            out_specs=pl.BlockSpec((COMBINE_TILE, D_MODEL), row),
            scratch_shapes=[pltpu.VMEM((TOP_K, COMBINE_TILE * ROW_CHUNKS, LANES), F32),
                            pltpu.SemaphoreType.DMA((TOP_K,))],
        ),
        out_shape=jax.ShapeDtypeStruct((n, D_MODEL), F32),
        compiler_params=_params("arbitrary"),
        name="moe_combine",
    )(dest, route, x, mod, final_g.reshape(1, D_MODEL), ys)


def _moe(stream, h_rows, route, counts, wg, wu, wd, x, mod, final_g, final_norm):
    n = stream.n_tok
    dest, fill, tile_expert, n_used = _dispatch_plan(route, counts, n)
    hs = _dispatch(n, dest, fill, h_rows)
    ys = _expert_ffn(n, tile_expert, n_used, hs, wg, wu, wd)
    return _combine(stream, dest, route, x, mod, final_g, ys, final_norm)


IN1_COL = 1024
Q_BLOCKS = RET_QK_WIDTH // IN1_COL
K_BLOCKS = 2 * RET_QK_WIDTH // IN1_COL
V_BLOCKS = (2 * RET_QK_WIDTH + RET_V_WIDTH) // IN1_COL


def _inproj1_kernel(x_ref, mod_ref, g_ref, w_ref, cos_ref, sin_ref, o_ref, h_ref, *, rope):
    j = pl.program_id(1)

    @pl.when(j == 0)
    def _():
        h = _modulated_norm(x_ref[...], g_ref[...], mod_ref[0, 0:1, :], mod_ref[0, 1:2, :])
        h_ref[...] = h.astype(BF16)

    p = _dot(h_ref[...], w_ref[...])

    def store_qk(scale):
        cos = cos_ref[...]
        sin = sin_ref[...]
        for hh in range(IN1_COL // RET_DK):
            ph = p[:, hh * RET_DK:(hh + 1) * RET_DK]
            if rope:
                ph = _rope(ph, cos, sin)
            o_ref[:, hh * RET_DK:(hh + 1) * RET_DK] = (ph * scale).astype(o_ref.dtype)

    @pl.when(j < Q_BLOCKS)
    def _():
        store_qk(1.0)

    @pl.when((j >= Q_BLOCKS) & (j < K_BLOCKS))
    def _():
        store_qk(RET_DK ** -0.5)

    @pl.when((j >= K_BLOCKS) & (j < V_BLOCKS))
    def _():
        o_ref[...] = p.astype(o_ref.dtype)

    @pl.when(j >= V_BLOCKS)
    def _():
        o_ref[...] = _silu(p).astype(o_ref.dtype)


def _inproj1(stream, x, mod, norm_g, w_bf16, cos, sin):
    n = stream.n_tok
    tiles_per_seq = max(stream.seq_len // ROW_TILE, 1)
    rope_map = (lambda i, j: (i % tiles_per_seq, 0)) if stream.rope else (lambda i, j: (0, 0))
    slot = _slot_map(stream)
    return pl.pallas_call(
        functools.partial(_inproj1_kernel, rope=stream.rope),
        grid=(n // ROW_TILE, IN1_WIDTH // IN1_COL),
        in_specs=[
            pl.BlockSpec((ROW_TILE, D_MODEL), lambda i, j: (i, 0)),
            pl.BlockSpec((1, 6, D_MODEL), lambda i, j: (slot(i), 0, 0)),
            pl.BlockSpec((1, D_MODEL), lambda i, j: (0, 0)),
            pl.BlockSpec((D_MODEL, IN1_COL), lambda i, j: (0, j)),
            pl.BlockSpec((ROW_TILE, RET_DK), rope_map),
            pl.BlockSpec((ROW_TILE, RET_DK), rope_map),
        ],
        out_specs=pl.BlockSpec((ROW_TILE, IN1_COL), lambda i, j: (i, j)),
        out_shape=jax.ShapeDtypeStruct((n, IN1_WIDTH), BF16),
        scratch_shapes=[pltpu.VMEM((ROW_TILE, D_MODEL), BF16)],
        compiler_params=_params("parallel", "arbitrary"),
        name="inproj1",
    )(x, mod, norm_g.reshape(1, D_MODEL), w_bf16, cos, sin)


def _retention_kernel(*refs, n_chunks, has_state):
    if has_state:
        decay_ref, q_ref, k_ref, v_ref, gf_ref, gb_ref, sf0_ref, sb0_ref, o_ref, mix_ref = refs
    else:
        decay_ref, q_ref, k_ref, v_ref, gf_ref, gb_ref, o_ref, sf_ref, sb_ref, mix_ref = refs
    c = RET_CHUNK
    hd = pl.program_id(1)
    row = lax.broadcasted_iota(jnp.int32, (c, c), 0).astype(F32)
    col = lax.broadcasted_iota(jnp.int32, (c, c), 1).astype(F32)
    pos = lax.broadcasted_iota(jnp.int32, (c, 1), 0).astype(F32)

    def chunk(i):
        return pl.ds(pl.multiple_of(i * c, c), c)

    def sweep(direction, state0):
        lg = -jnp.exp(jnp.full((1, 1), decay_ref[direction, hd], F32))
        if direction == 0:
            diff = row - col
            q_pow = pos + 1.0
            k_pow = (c - 1.0) - pos
        else:
            diff = col - row
            q_pow = c - pos
            k_pow = pos
        intra = jnp.where(diff >= 0, jnp.exp(lg * jnp.maximum(diff, 0.0)), 0.0)
        q_dec = jnp.exp(lg * q_pow)
        k_dec = jnp.exp(lg * k_pow)
        chunk_dec = jnp.exp(lg * float(c))
        g_ref = gf_ref if direction == 0 else gb_ref

        def step(t, state):
            i = t if direction == 0 else n_chunks - 1 - t
            rows = chunk(i)
            qi = q_ref[rows, :]
            ki = k_ref[rows, :]
            vi = v_ref[rows, :]
            s = _dot_nt(qi, ki) * intra
            y = _dot(s.astype(BF16), vi) + _dot(qi, state.astype(BF16)) * q_dec
            k_scaled = (ki.astype(F32) * k_dec).astype(BF16)
            state = state * chunk_dec + _dot_tn(k_scaled, vi)
            gated = g_ref[rows, :].astype(F32) * _rms(y)
            if direction == 0:
                mix_ref[rows, :] = gated
            else:
                o_ref[rows, :] = (mix_ref[rows, :] + gated).astype(o_ref.dtype)
            return state

        return lax.fori_loop(0, n_chunks, step, state0)

    if has_state:
        sf = sweep(0, sf0_ref[0, 0])
        sb = sweep(1, sb0_ref[0, 0])
    else:
        zero = jnp.zeros((RET_DK, RET_DV), F32)
        sf = sweep(0, zero)
        sb = sweep(1, zero)
        sf_ref[0, 0] = sf
        sb_ref[0, 0] = sb


def _retention(stream, qkvg, decays, states):
    n = stream.n_tok
    t = stream.seq_len
    has_state = states is not None
    k_blk = RET_QK_WIDTH // RET_DK
    v_blk = 2 * RET_QK_WIDTH // RET_DV
    gf_blk = v_blk + RET_V_WIDTH // RET_DV
    gb_blk = gf_blk + RET_V_WIDTH // RET_DV
    in_specs = [
        pl.BlockSpec(memory_space=pltpu.SMEM),
        pl.BlockSpec((t, RET_DK), lambda b, h: (b, h)),
        pl.BlockSpec((t, RET_DK), lambda b, h: (b, k_blk + h)),
        pl.BlockSpec((t, RET_DV), lambda b, h: (b, v_blk + h)),
        pl.BlockSpec((t, RET_DV), lambda b, h: (b, gf_blk + h)),
        pl.BlockSpec((t, RET_DV), lambda b, h: (b, gb_blk + h)),
    ]
    args = [decays, qkvg, qkvg, qkvg, qkvg, qkvg]
    state_spec = pl.BlockSpec((1, 1, RET_DK, RET_DV), lambda b, h: (b, h, 0, 0))
    out_specs = [pl.BlockSpec((t, RET_DV), lambda b, h: (b, h))]
    out_shape = [jax.ShapeDtypeStruct((n, RET_V_WIDTH), BF16)]
    if has_state:
        in_specs += [state_spec, state_spec]
        args += list(states)
    else:
        out_specs += [state_spec, state_spec]
        out_shape += [jax.ShapeDtypeStruct((stream.n_seq, RET_HEADS, RET_DK, RET_DV), F32)] * 2
    return pl.pallas_call(
        functools.partial(_retention_kernel, n_chunks=t // RET_CHUNK, has_state=has_state),
        grid=(stream.n_seq, RET_HEADS),
        in_specs=in_specs,
        out_specs=out_specs,
        out_shape=out_shape,
        scratch_shapes=[pltpu.VMEM((t, RET_DV), F32)],
        compiler_params=_params("parallel", "parallel"),
        name="retention",
    )(*args)


def _rope_tables(n_tokens, dim):
    t = jnp.arange(n_tokens)
    pos = jnp.stack([t // GRID_W, t % GRID_W]).astype(F32)
    n_freq = dim // 4
    freqs = ROPE_THETA ** (-jnp.arange(n_freq, dtype=F32) / n_freq)
    ang = pos[:, :, None] * freqs
    cos, sin = jnp.cos(ang), jnp.sin(ang)
    cos_t = jnp.concatenate([cos[0], cos[0], cos[1], cos[1]], axis=-1)
    sin_t = jnp.concatenate([-sin[0], sin[0], -sin[1], sin[1]], axis=-1)
    return cos_t, sin_t


def kernel(x_prompt, x_sample, cache_k0, cache_v0, state_ret_fwd1, state_ret_bwd1, c, c_ctx,
           norm_mix_g0, mod_w0, mod_b0, in_w0, conv_w0, conv_b0, conv_norm_g0, conv_norm_b0,
           q_norm_g0, k_norm_g0, out_w0, norm_ffn_g0,
           moe_grp_w0, moe_grp_b0, moe_rtr_w0, moe_rtr_b0, moe_w_gate0, moe_w_up0, moe_w_down0,
           norm_mix_g1, mod_w1, mod_b1, in_w1, ret_decay_fwd1, ret_decay_bwd1, out_w1, norm_ffn_g1,
           moe_grp_w1, moe_grp_b1, moe_rtr_w1, moe_rtr_b1, moe_w_gate1, moe_w_up1, moe_w_down1,
           final_norm_g):
    batch, seq, d = x_prompt.shape
    dec_batch, dec_seq, _ = x_sample.shape
    assert d == D_MODEL and 1 + dec_batch <= MOD_SLOTS
    ctx = _Stream(batch, seq, first_slot=0, rope=False)
    lat = _Stream(dec_batch, dec_seq, first_slot=1, rope=True)
    assert ctx.n_tok % ROW_TILE == 0 and lat.n_tok % ROW_TILE == 0 and ROW_CHUNKS == SUBLANES
    assert seq % CONV_TILE == 0 and dec_seq % ROW_TILE == 0 and ROW_TILE % seq == 0

    cond = jnp.concatenate([c_ctx[None, :], c, jnp.zeros((MOD_SLOTS - 1 - dec_batch, d), F32)], axis=0)
    mod0 = _modulation(cond, mod_w0, mod_b0)
    mod1 = _modulation(cond, mod_w1, mod_b1)
    cos, sin = _rope_tables(dec_seq, HEAD_DIM)
    decays = jnp.stack([ret_decay_fwd1, ret_decay_bwd1]).astype(F32)

    in_w0_b = in_w0.astype(BF16)
    out_w0_b = out_w0.astype(BF16)
    in_w1_b = in_w1.astype(BF16)
    out_w1_b = out_w1.astype(BF16)
    moe0 = (moe_w_gate0.astype(BF16), moe_w_up0.astype(BF16), moe_w_down0.astype(BF16))
    moe1 = (moe_w_gate1.astype(BF16), moe_w_up1.astype(BF16), moe_w_down1.astype(BF16))
    route0 = _router_params(moe_grp_w0, moe_grp_b0, moe_rtr_w0, moe_rtr_b0)
    route1 = _router_params(moe_grp_w1, moe_grp_b1, moe_rtr_w1, moe_rtr_b1)

    def run(stream, x, cache, states):
        x = x.reshape(stream.n_tok, d)
        kv_dtype = BF16 if stream.rope else F32
        a, q, k, v = _inproj0(stream, x, mod0, norm_mix_g0, in_w0_b, q_norm_g0, k_norm_g0, cos, sin, kv_dtype)
        a = _conv_branch(stream, a, conv_w0, conv_b0, conv_norm_g0, conv_norm_b0)
        tq = min(stream.seq_len, 128 if cache is not None else 256)
        o = _attention(stream, q, k, v, cache, tq)
        x, h, route, counts = _outproj_route(stream, [a, o], out_w0_b, x, mod0, norm_ffn_g0, *route0)
        x = _moe(stream, h, route, counts, *moe0, x, mod0, final_norm_g, final_norm=False)
        qkvg = _inproj1(stream, x, mod1, norm_mix_g1, in_w1_b, cos, sin)
        ret = _retention(stream, qkvg, decays, states)
        x, h, route, counts = _outproj_route(stream, [ret[0]], out_w1_b, x, mod1, norm_ffn_g1, *route1)
        y = _moe(stream, h, route, counts, *moe1, x, mod1, final_norm_g, final_norm=True)
        return y.reshape(stream.n_seq, stream.seq_len, d), k, v, ret[1:]

    y_prompt, k_ctx, v_ctx, new_states = run(ctx, x_prompt, None, None)
    y_sample, _, _, _ = run(lat, x_sample, (cache_k0, cache_v0), (state_ret_fwd1, state_ret_bwd1))
    new_k0 = k_ctx.reshape(batch, seq, N_KV_HEADS, HEAD_DIM).transpose(0, 2, 1, 3)
    new_v0 = v_ctx.reshape(batch, seq, N_KV_HEADS, HEAD_DIM).transpose(0, 2, 1, 3)
    return (y_prompt, y_sample, new_k0, new_v0, new_states[0], new_states[1])
```

```python
import functools

import jax
import jax.numpy as jnp
from jax import lax
from jax.experimental import pallas as pl
from jax.experimental.pallas import tpu as pltpu

D_MODEL = 1024
GRID_W = 64
EPS = 1e-6
CONV_CH = 512
CONV_WIDTH = 31
CONV_HALF = CONV_WIDTH // 2
N_Q_HEADS = 8
N_KV_HEADS = 2
Q_PER_KV = N_Q_HEADS // N_KV_HEADS
HEAD_DIM = 128
ROPE_THETA = 10000.0
ATTN_WIDTH = N_Q_HEADS * HEAD_DIM
KV_WIDTH = N_KV_HEADS * HEAD_DIM
IN0_WIDTH = 2 * CONV_CH + ATTN_WIDTH + 2 * KV_WIDTH
RET_HEADS = 8
RET_DK = 128
RET_DV = 256
RET_CHUNK = 128
RET_QK_WIDTH = RET_HEADS * RET_DK
RET_V_WIDTH = RET_HEADS * RET_DV
IN1_WIDTH = 2 * RET_QK_WIDTH + 3 * RET_V_WIDTH
N_GROUPS = 4
EXPERTS_PER_GROUP = 4
N_EXPERTS = N_GROUPS * EXPERTS_PER_GROUP
EXPERT_FF = 512

LANES = 128
SUBLANES = 8
VMEM_LIMIT_BYTES = 56 * 1024 * 1024

MOD_SLOTS = 8
ROW_TILE = 512
TOP_K = 2
ROW_CHUNKS = D_MODEL // LANES
FFN_TILE = 256
DISPATCH_TILE = 256
COMBINE_TILE = 256
ROUTE_E1, ROUTE_E2, ROUTE_W1, ROUTE_W2, ROUTE_R1, ROUTE_R2 = range(6)
CONV_TILE = 256
CONV_HALO = 16
CONV_ROWS = 64
ROUTE_LANES = LANES
EXPERT_LANE0 = N_GROUPS
NEG_BIG = -1e30

F32 = jnp.float32
BF16 = jnp.bfloat16


def _params(*semantics):
    return pltpu.CompilerParams(dimension_semantics=semantics, vmem_limit_bytes=VMEM_LIMIT_BYTES)


def _sigmoid(x):
    return 1.0 / (1.0 + jnp.exp(-x))


def _silu(x):
    return x * _sigmoid(x)


def _rms(x):
    return x * lax.rsqrt(jnp.mean(x * x, axis=-1, keepdims=True) + EPS)


def _rope(x, cos, sin):
    lane = lax.broadcasted_iota(jnp.int32, x.shape, 1)
    take_upper = (lane % (HEAD_DIM // 2)) < (HEAD_DIM // 4)
    partner = jnp.where(take_upper, pltpu.roll(x, HEAD_DIM - HEAD_DIM // 4, 1), pltpu.roll(x, HEAD_DIM // 4, 1))
    return x * cos + partner * sin


def _dot(a, b):
    return jnp.dot(a, b, preferred_element_type=F32)


def _dot_nt(a, b):
    return lax.dot_general(a, b, (((1,), (1,)), ((), ())), preferred_element_type=F32)


def _dot_tn(a, b):
    return lax.dot_general(a, b, (((0,), (0,)), ((), ())), preferred_element_type=F32)


def _mod_kernel(c_ref, w_ref, b_ref, o_ref):
    c = c_ref[...]
    o_ref[...] = _dot(_silu(c).astype(BF16), w_ref[...].astype(BF16)) + b_ref[...]


def _modulation(cond, mod_w, mod_b):
    n_out = mod_w.shape[1]
    col = D_MODEL
    out = pl.pallas_call(
        _mod_kernel,
        grid=(n_out // col,),
        in_specs=[
            pl.BlockSpec((MOD_SLOTS, D_MODEL), lambda j: (0, 0)),
            pl.BlockSpec((D_MODEL, col), lambda j: (0, j)),
            pl.BlockSpec((1, col), lambda j: (0, j)),
        ],
        out_specs=pl.BlockSpec((MOD_SLOTS, col), lambda j: (0, j)),
        out_shape=jax.ShapeDtypeStruct((MOD_SLOTS, n_out), F32),
        compiler_params=_params("parallel"),
        name="modulation",
    )(cond, mod_w, mod_b.reshape(1, n_out))
    return out.reshape(MOD_SLOTS, 6, D_MODEL)


def _slot_map(stream):
    if stream.first_slot == 0:
        return lambda i: 0
    return lambda i: stream.first_slot + (i * ROW_TILE) // stream.seq_len


class _Stream:
    def __init__(self, n_seq, seq_len, first_slot, rope):
        self.n_seq = n_seq
        self.seq_len = seq_len
        self.first_slot = first_slot
        self.rope = rope
        self.n_tok = n_seq * seq_len


def _modulated_norm(x, g, shift, scale):
    return _rms(x) * g * (1.0 + scale) + shift


def _inproj0_kernel(x_ref, mod_ref, g_ref, w_ref, qg_ref, kg_ref, cos_ref, sin_ref,
                    a_ref, q_ref, k_ref, v_ref, *, rope):
    h = _modulated_norm(x_ref[...], g_ref[...], mod_ref[0, 0:1, :], mod_ref[0, 1:2, :])
    p = _dot(h.astype(BF16), w_ref[...])
    a_ref[...] = p[:, :CONV_CH] * _sigmoid(p[:, CONV_CH:2 * CONV_CH])
    q0 = 2 * CONV_CH
    k0 = q0 + ATTN_WIDTH
    v0 = k0 + KV_WIDTH
    cos = cos_ref[...]
    sin = sin_ref[...]
    q_scale = HEAD_DIM ** -0.5
    for hh in range(N_Q_HEADS):
        qh = _rms(p[:, q0 + hh * HEAD_DIM:q0 + (hh + 1) * HEAD_DIM]) * qg_ref[...]
        if rope:
            qh = _rope(qh, cos, sin)
        q_ref[:, hh * HEAD_DIM:(hh + 1) * HEAD_DIM] = (qh * q_scale).astype(q_ref.dtype)
    for hh in range(N_KV_HEADS):
        kh = _rms(p[:, k0 + hh * HEAD_DIM:k0 + (hh + 1) * HEAD_DIM]) * kg_ref[...]
        if rope:
            kh = _rope(kh, cos, sin)
        k_ref[:, hh * HEAD_DIM:(hh + 1) * HEAD_DIM] = kh.astype(k_ref.dtype)
    v_ref[...] = p[:, v0:v0 + KV_WIDTH].astype(v_ref.dtype)


def _inproj0(stream, x, mod, norm_g, w_bf16, q_norm_g, k_norm_g, cos, sin, kv_dtype):
    n = stream.n_tok
    tiles_per_seq = max(stream.seq_len // ROW_TILE, 1)
    rope_map = (lambda i: (i % tiles_per_seq, 0)) if stream.rope else (lambda i: (0, 0))
    slot = _slot_map(stream)
    row = lambda i: (i, 0)
    fixed = lambda i: (0, 0)
    return pl.pallas_call(
        functools.partial(_inproj0_kernel, rope=stream.rope),
        grid=(n // ROW_TILE,),
        in_specs=[
            pl.BlockSpec((ROW_TILE, D_MODEL), row),
            pl.BlockSpec((1, 6, D_MODEL), lambda i: (slot(i), 0, 0)),
            pl.BlockSpec((1, D_MODEL), fixed),
            pl.BlockSpec((D_MODEL, IN0_WIDTH), fixed),
            pl.BlockSpec((1, HEAD_DIM), fixed),
            pl.BlockSpec((1, HEAD_DIM), fixed),
            pl.BlockSpec((ROW_TILE, HEAD_DIM), rope_map),
            pl.BlockSpec((ROW_TILE, HEAD_DIM), rope_map),
        ],
        out_specs=[
            pl.BlockSpec((ROW_TILE, CONV_CH), row),
            pl.BlockSpec((ROW_TILE, ATTN_WIDTH), row),
            pl.BlockSpec((ROW_TILE, KV_WIDTH), row),
            pl.BlockSpec((ROW_TILE, KV_WIDTH), row),
        ],
        out_shape=[
            jax.ShapeDtypeStruct((n, CONV_CH), F32),
            jax.ShapeDtypeStruct((n, ATTN_WIDTH), BF16),
            jax.ShapeDtypeStruct((n, KV_WIDTH), kv_dtype),
            jax.ShapeDtypeStruct((n, KV_WIDTH), kv_dtype),
        ],
        compiler_params=_params("parallel"),
        name="inproj0",
    )(x, mod, norm_g.reshape(1, D_MODEL), w_bf16, q_norm_g.reshape(1, HEAD_DIM), k_norm_g.reshape(1, HEAD_DIM),
      cos, sin)


def _conv_kernel(prev_ref, main_ref, next_ref, w_ref, b_ref, g_ref, beta_ref, o_ref, pad_ref, acc_ref, *,
                 tiles_per_seq):
    i = pl.program_id(0)
    has_prev = (i % tiles_per_seq) != 0
    has_next = (i % tiles_per_seq) != (tiles_per_seq - 1)
    pad_ref[0:CONV_HALO, :] = jnp.where(has_prev, prev_ref[...], 0.0)
    pad_ref[CONV_HALO:CONV_HALO + CONV_TILE, :] = main_ref[...]
    pad_ref[CONV_HALO + CONV_TILE:, :] = jnp.where(has_next, next_ref[...], 0.0)
    first = CONV_HALO - CONV_HALF
    for c0 in range(0, CONV_CH, LANES):
        for r0 in range(0, CONV_TILE, CONV_ROWS):
            acc = jnp.zeros((CONV_ROWS, LANES), F32)
            for j in range(CONV_WIDTH):
                acc = acc + w_ref[j:j + 1, c0:c0 + LANES] * pad_ref[first + r0 + j:first + r0 + j + CONV_ROWS,
                                                                    c0:c0 + LANES]
            acc_ref[r0:r0 + CONV_ROWS, c0:c0 + LANES] = acc
    a = acc_ref[...] + b_ref[...]
    mu = jnp.mean(a, axis=-1, keepdims=True)
    d = a - mu
    var = jnp.mean(d * d, axis=-1, keepdims=True)
    y = d * lax.rsqrt(var + EPS) * g_ref[...] + beta_ref[...]
    o_ref[...] = _silu(y).astype(o_ref.dtype)


def _conv_branch(stream, a, conv_w, conv_b, norm_g, norm_b):
    n = stream.n_tok
    tiles_per_seq = stream.seq_len // CONV_TILE
    halo_per_tile = CONV_TILE // CONV_HALO
    n_halo = n // CONV_HALO
    fixed = lambda i: (0, 0)
    return pl.pallas_call(
        functools.partial(_conv_kernel, tiles_per_seq=tiles_per_seq),
        grid=(n // CONV_TILE,),
        in_specs=[
            pl.BlockSpec((CONV_HALO, CONV_CH), lambda i: (jnp.maximum(i * halo_per_tile - 1, 0), 0)),
            pl.BlockSpec((CONV_TILE, CONV_CH), lambda i: (i, 0)),
            pl.BlockSpec((CONV_HALO, CONV_CH), lambda i: (jnp.minimum((i + 1) * halo_per_tile, n_halo - 1), 0)),
            pl.BlockSpec((CONV_WIDTH, CONV_CH), fixed),
            pl.BlockSpec((1, CONV_CH), fixed),
            pl.BlockSpec((1, CONV_CH), fixed),
            pl.BlockSpec((1, CONV_CH), fixed),
        ],
        out_specs=pl.BlockSpec((CONV_TILE, CONV_CH), lambda i: (i, 0)),
        out_shape=jax.ShapeDtypeStruct((n, CONV_CH), BF16),
        scratch_shapes=[
            pltpu.VMEM((CONV_TILE + 2 * CONV_HALO, CONV_CH), F32),
            pltpu.VMEM((CONV_TILE, CONV_CH), F32),
        ],
        compiler_params=_params("parallel"),
        name="conv_branch",
    )(a, a, a, conv_w, conv_b.reshape(1, CONV_CH), norm_g.reshape(1, CONV_CH), norm_b.reshape(1, CONV_CH))


def _attn_kernel(*refs, tq, has_cache):
    if has_cache:
        q_ref, k_ref, v_ref, kc_ref, vc_ref, o_ref = refs
    else:
        q_ref, k_ref, v_ref, o_ref = refs
    q = q_ref[...]
    q4 = jnp.concatenate([q[:, g * HEAD_DIM:(g + 1) * HEAD_DIM] for g in range(Q_PER_KV)], axis=0)
    s = _dot_nt(q4, k_ref[...].astype(BF16))
    m = jnp.max(s, axis=-1, keepdims=True)
    if has_cache:
        sc = _dot_nt(q4, kc_ref[0, 0].astype(BF16))
        m = jnp.maximum(m, jnp.max(sc, axis=-1, keepdims=True))
    p = jnp.exp(s - m)
    l = jnp.sum(p, axis=-1, keepdims=True)
    o = _dot(p.astype(BF16), v_ref[...].astype(BF16))
    if has_cache:
        pc = jnp.exp(sc - m)
        l = l + jnp.sum(pc, axis=-1, keepdims=True)
        o = o + _dot(pc.astype(BF16), vc_ref[0, 0].astype(BF16))
    o = o * (1.0 / l)
    for g in range(Q_PER_KV):
        o_ref[:, g * HEAD_DIM:(g + 1) * HEAD_DIM] = o[g * tq:(g + 1) * tq].astype(o_ref.dtype)


def _attention(stream, q, k, v, cache, tq):
    n = stream.n_tok
    t = stream.seq_len
    q_tiles = t // tq
    group_w = Q_PER_KV * HEAD_DIM
    in_specs = [
        pl.BlockSpec((tq, group_w), lambda b, h, i: (b * q_tiles + i, h)),
        pl.BlockSpec((t, HEAD_DIM), lambda b, h, i: (b, h)),
        pl.BlockSpec((t, HEAD_DIM), lambda b, h, i: (b, h)),
    ]
    args = [q, k, v]
    if cache is not None:
        past = cache[0].shape[2]
        in_specs += [pl.BlockSpec((1, 1, past, HEAD_DIM), lambda b, h, i: (b, h, 0, 0))] * 2
        args += list(cache)
    return pl.pallas_call(
        functools.partial(_attn_kernel, tq=tq, has_cache=cache is not None),
        grid=(stream.n_seq, N_KV_HEADS, q_tiles),
        in_specs=in_specs,
        out_specs=pl.BlockSpec((tq, group_w), lambda b, h, i: (b * q_tiles + i, h)),
        out_shape=jax.ShapeDtypeStruct((n, ATTN_WIDTH), BF16),
        compiler_params=_params("parallel", "parallel", "parallel"),
        name="attention",
    )(*args)


def _route(logits, running):
    rows = logits.shape[0]
    lane = lax.broadcasted_iota(jnp.int32, logits.shape, 1)
    lane_f = lane.astype(F32)
    far = float(ROUTE_LANES)
    is_group = lane < N_GROUPS
    gl = jnp.where(is_group, logits, NEG_BIG)
    gmax = jnp.max(gl, axis=-1, keepdims=True)
    gsum = jnp.sum(jnp.where(is_group, jnp.exp(gl - gmax), 0.0), axis=-1, keepdims=True)
    g_w = 1.0 / gsum
    gidx = jnp.min(jnp.where(gl == gmax, lane_f, far), axis=-1, keepdims=True)
    lo = EXPERT_LANE0 + EXPERTS_PER_GROUP * gidx
    in_group = (lane_f >= lo) & (lane_f < lo + EXPERTS_PER_GROUP)
    el = jnp.where(in_group, logits, NEG_BIG)
    v1 = jnp.max(el, axis=-1, keepdims=True)
    i1 = jnp.min(jnp.where(el == v1, lane_f, far), axis=-1, keepdims=True)
    el2 = jnp.where(lane_f == i1, NEG_BIG, el)
    v2 = jnp.max(el2, axis=-1, keepdims=True)
    i2 = jnp.min(jnp.where(el2 == v2, lane_f, far), axis=-1, keepdims=True)
    e2 = jnp.exp(v2 - v1)
    w1 = g_w / (1.0 + e2)
    w2 = w1 * e2
    is1 = lane_f == i1
    is2 = lane_f == i2
    chosen = (is1 | is2).astype(BF16)
    earlier = (lax.broadcasted_iota(jnp.int32, (rows, rows), 0)
               > lax.broadcasted_iota(jnp.int32, (rows, rows), 1)).astype(BF16)
    before = running + _dot(earlier, chosen)
    r1 = jnp.sum(jnp.where(is1, before, 0.0), axis=-1, keepdims=True)
    r2 = jnp.sum(jnp.where(is2, before, 0.0), axis=-1, keepdims=True)
    record = jnp.zeros_like(logits)
    for at, val in ((ROUTE_E1, i1 - EXPERT_LANE0), (ROUTE_E2, i2 - EXPERT_LANE0), (ROUTE_W1, w1), (ROUTE_W2, w2),
                    (ROUTE_R1, r1), (ROUTE_R2, r2)):
        record = jnp.where(lane == at, val, record)
    return record, running + jnp.sum(chosen.astype(F32), axis=0, keepdims=True)


def _outproj_kernel(*refs, widths):
    n_in = len(widths)
    in_refs = refs[:n_in]
    w_ref, x_ref, mod_ref, g_ref, rw_ref, rb_ref, x1_ref, h_ref, route_ref, route_t_ref, count_ref, run_ref = refs[n_in:]

    @pl.when(pl.program_id(0) == 0)
    def _():
        run_ref[...] = jnp.zeros_like(run_ref)

    acc = None
    off = 0
    for r, width in zip(in_refs, widths):
        part = _dot(r[...], w_ref[off:off + width, :])
        acc = part if acc is None else acc + part
        off += width
    x1 = x_ref[...] + mod_ref[0, 2:3, :] * acc
    x1_ref[...] = x1
    h = _modulated_norm(x1, g_ref[...], mod_ref[0, 3:4, :], mod_ref[0, 4:5, :])
    _store_row_major(h_ref, h)
    h_hi = h.astype(BF16)
    h_lo = (h - h_hi.astype(F32)).astype(BF16)
    logits = _dot(h_hi, rw_ref[0]) + _dot(h_hi, rw_ref[1]) + _dot(h_lo, rw_ref[0]) + rb_ref[...]
    record, running = _route(logits, run_ref[...])
    route_ref[...] = record
    route_t_ref[...] = record.T[:SUBLANES]
    run_ref[...] = running
    count_ref[...] = jnp.broadcast_to(running, count_ref.shape)


def _store_row_major(ref, x):
    rows = x.shape[0]
    for c in range(ROW_CHUNKS):
        ref[pl.ds(c, rows, stride=ROW_CHUNKS), :] = x[:, c * LANES:(c + 1) * LANES]


def _load_row_major(ref, rows):
    return jnp.concatenate([ref[pl.ds(c, rows, stride=ROW_CHUNKS), :] for c in range(ROW_CHUNKS)], axis=-1)


def _outproj_route(stream, inputs, w_bf16, x, mod, ffn_norm_g, route_w, route_b):
    n = stream.n_tok
    widths = tuple(a.shape[1] for a in inputs)
    slot = _slot_map(stream)
    row = lambda i: (i, 0)
    fixed = lambda i: (0, 0)
    return pl.pallas_call(
        functools.partial(_outproj_kernel, widths=widths),
        grid=(n // ROW_TILE,),
        in_specs=[pl.BlockSpec((ROW_TILE, width), row) for width in widths] + [
            pl.BlockSpec((sum(widths), D_MODEL), fixed),
            pl.BlockSpec((ROW_TILE, D_MODEL), row),
            pl.BlockSpec((1, 6, D_MODEL), lambda i: (slot(i), 0, 0)),
            pl.BlockSpec((1, D_MODEL), fixed),
            pl.BlockSpec((2, D_MODEL, ROUTE_LANES), lambda i: (0, 0, 0)),
            pl.BlockSpec((1, ROUTE_LANES), fixed),
        ],
        out_specs=[
            pl.BlockSpec((ROW_TILE, D_MODEL), row),
            pl.BlockSpec((ROW_TILE * ROW_CHUNKS, LANES), row),
            pl.BlockSpec((ROW_TILE, ROUTE_LANES), row),
            pl.BlockSpec((SUBLANES, ROW_TILE), lambda i: (0, i)),
            pl.BlockSpec((SUBLANES, ROUTE_LANES), fixed),
        ],
        out_shape=[
            jax.ShapeDtypeStruct((n, D_MODEL), F32),
            jax.ShapeDtypeStruct((n * ROW_CHUNKS, LANES), F32),
            jax.ShapeDtypeStruct((n, ROUTE_LANES), F32),
            jax.ShapeDtypeStruct((SUBLANES, n), F32),
            jax.ShapeDtypeStruct((SUBLANES, ROUTE_LANES), F32),
        ],
        scratch_shapes=[pltpu.VMEM((1, ROUTE_LANES), F32)],
        compiler_params=_params("arbitrary"),
        name="outproj_route",
    )(*inputs, w_bf16, x, mod, ffn_norm_g.reshape(1, D_MODEL), route_w, route_b)


def _router_params(grp_w, grp_b, rtr_w, rtr_b):
    w = jnp.concatenate([grp_w, jnp.moveaxis(rtr_w, 0, 1).reshape(D_MODEL, N_EXPERTS)], axis=1)
    w = jnp.pad(w, ((0, 0), (0, ROUTE_LANES - w.shape[1])))
    hi = w.astype(BF16)
    lo = (w - hi.astype(F32)).astype(BF16)
    b = jnp.concatenate([grp_b, rtr_b.reshape(N_EXPERTS)])
    b = jnp.pad(b, (0, ROUTE_LANES - b.shape[0])).reshape(1, ROUTE_LANES)
    return jnp.stack([hi, lo]), b


def _sorted_tiles(n_tok):
    return (TOP_K * n_tok) // FFN_TILE + N_EXPERTS


def _dispatch_plan(route_t, counts, n_tok):
    count = counts[0, EXPERT_LANE0:EXPERT_LANE0 + N_EXPERTS].astype(jnp.int32)
    tiles = (count + (FFN_TILE - 1)) // FFN_TILE
    padded = tiles * FFN_TILE
    start = jnp.cumsum(padded) - padded
    n_used = jnp.sum(tiles)
    tile_end = jnp.cumsum(tiles)
    j = jnp.minimum(jnp.arange(_sorted_tiles(n_tok), dtype=jnp.int32), n_used - 1)
    tile_expert = jnp.sum(j[:, None] >= tile_end[None, :], axis=1).astype(jnp.int32)
    ids = route_t.astype(jnp.int32)
    dest = jnp.concatenate([jnp.take(start, ids[ROUTE_E1]) + ids[ROUTE_R1],
                            jnp.take(start, ids[ROUTE_E2]) + ids[ROUTE_R2]])
    fill = jnp.concatenate([start + count, start + padded])
    return dest, fill, tile_expert, n_used.reshape(1)


def _token_rows(t):
    return pl.ds(pl.multiple_of(t * ROW_CHUNKS, ROW_CHUNKS), ROW_CHUNKS)


def _dispatch_kernel(dest_ref, fill_ref, h_ref, hs_ref, zero_ref, sems, *, n_tok):
    i = pl.program_id(0)
    base = i * DISPATCH_TILE

    def row_copy(t, k):
        d = dest_ref[k * n_tok + base + t]
        return pltpu.make_async_copy(h_ref.at[_token_rows(t)], hs_ref.at[_token_rows(d)], sems.at[k])

    def issue(t, carry):
        for k in range(TOP_K):
            row_copy(t, k).start(priority=k)
        return carry

    lax.fori_loop(0, DISPATCH_TILE, issue, 0, unroll=8)

    def fill_copy(r):
        return pltpu.make_async_copy(zero_ref, hs_ref.at[_token_rows(r)], sems.at[TOP_K])

    @pl.when(i == 0)
    def _():
        zero_ref[...] = jnp.zeros_like(zero_ref)
        for e in range(N_EXPERTS):
            lax.fori_loop(fill_ref[e], fill_ref[N_EXPERTS + e], lambda r, c: (fill_copy(r).start(), c)[1], 0)
        for e in range(N_EXPERTS):
            lax.fori_loop(fill_ref[e], fill_ref[N_EXPERTS + e], lambda r, c: (fill_copy(r).wait(), c)[1], 0)

    for k in range(TOP_K):
        pltpu.make_async_copy(h_ref, hs_ref.at[pl.ds(0, DISPATCH_TILE * ROW_CHUNKS)], sems.at[k]).wait()


def _dispatch(n_tok, dest, fill, h_rows):
    rows = _sorted_tiles(n_tok) * FFN_TILE
    return pl.pallas_call(
        functools.partial(_dispatch_kernel, n_tok=n_tok),
        grid_spec=pltpu.PrefetchScalarGridSpec(
            num_scalar_prefetch=2,
            grid=(n_tok // DISPATCH_TILE,),
            in_specs=[pl.BlockSpec((DISPATCH_TILE * ROW_CHUNKS, LANES), lambda i, d, f: (i, 0))],
            out_specs=pl.BlockSpec(memory_space=pl.ANY),
            scratch_shapes=[pltpu.VMEM((ROW_CHUNKS, LANES), F32), pltpu.SemaphoreType.DMA((TOP_K + 1,))],
        ),
        out_shape=jax.ShapeDtypeStruct((rows * ROW_CHUNKS, LANES), F32),
        compiler_params=_params("arbitrary"),
        name="moe_dispatch",
    )(dest, fill, h_rows)


def _ffn_kernel(te_ref, nu_ref, hs_ref, wg_ref, wu_ref, wd_ref, ys_ref):
    @pl.when(pl.program_id(0) < nu_ref[0])
    def _():
        x = _load_row_major(hs_ref, FFN_TILE).astype(BF16)
        hidden = _silu(_dot(x, wg_ref[0])) * _dot(x, wu_ref[0])
        _store_row_major(ys_ref, _dot(hidden.astype(BF16), wd_ref[0]))


def _expert_ffn(n_tok, tile_expert, n_used, hs, wg, wu, wd):
    rows = lambda j, te, nu: (jnp.minimum(j, nu[0] - 1), 0)
    expert = lambda j, te, nu: (te[j], 0, 0)
    return pl.pallas_call(
        _ffn_kernel,
        grid_spec=pltpu.PrefetchScalarGridSpec(
            num_scalar_prefetch=2,
            grid=(_sorted_tiles(n_tok),),
            in_specs=[
                pl.BlockSpec((FFN_TILE * ROW_CHUNKS, LANES), rows),
                pl.BlockSpec((1, D_MODEL, EXPERT_FF), expert),
                pl.BlockSpec((1, D_MODEL, EXPERT_FF), expert),
                pl.BlockSpec((1, EXPERT_FF, D_MODEL), expert),
            ],
            out_specs=pl.BlockSpec((FFN_TILE * ROW_CHUNKS, LANES), rows),
        ),
        out_shape=jax.ShapeDtypeStruct(hs.shape, F32),
        compiler_params=_params("arbitrary"),
        name="moe_ffn",
    )(tile_expert, n_used, hs, wg, wu, wd)


def _combine_kernel(dest_ref, route_ref, x_ref, mod_ref, ng_ref, nmod_ref, ys_ref, o_ref, *rest, n_tok, last_layer):
    if last_layer:
        y_ref, sems = rest
    else:
        h_ref, y_ref, sems = rest
    i = pl.program_id(0)
    n_tiles = pl.num_programs(0)

    def row_copy(tile, t, k):
        slot = tile % 2
        d = dest_ref[k * n_tok + tile * COMBINE_TILE + t]
        return pltpu.make_async_copy(ys_ref.at[_token_rows(d)], y_ref.at[slot, k, _token_rows(t)], sems.at[slot, k])

    def start_tile(tile):
        def issue(t, carry):
            for k in range(TOP_K):
                row_copy(tile, t, k).start(priority=k)
            return carry

        lax.fori_loop(0, COMBINE_TILE, issue, 0, unroll=8)

    @pl.when(i == 0)
    def _():
        start_tile(i)

    @pl.when(i + 1 < n_tiles)
    def _():
        start_tile(i + 1)

    slot = i % 2
    route = route_ref[...]
    w1 = route[:, ROUTE_W1:ROUTE_W1 + 1]
    w2 = route[:, ROUTE_W2:ROUTE_W2 + 1]
    for k in range(TOP_K):
        pltpu.make_async_copy(ys_ref.at[pl.ds(0, COMBINE_TILE * ROW_CHUNKS)], y_ref.at[slot, k],
                              sems.at[slot, k]).wait()
    y = (w1 * _load_row_major(y_ref.at[slot, 0], COMBINE_TILE)
         + w2 * _load_row_major(y_ref.at[slot, 1], COMBINE_TILE))
    x = x_ref[...] + mod_ref[0, 5:6, :] * y
    if last_layer:
        o_ref[...] = _rms(x) * ng_ref[...]
    else:
        o_ref[...] = x
        h_ref[...] = _modulated_norm(x, ng_ref[...], nmod_ref[0, 0:1, :], nmod_ref[0, 1:2, :]).astype(h_ref.dtype)


def _combine(stream, dest, route, x, mod, norm_g, next_mod, ys, last_layer):
    n = stream.n_tok
    first_slot, seq_len = stream.first_slot, stream.seq_len
    slot = (lambda i: 0) if first_slot == 0 else (lambda i: first_slot + (i * COMBINE_TILE) // seq_len)
    row = lambda i, d: (i, 0)
    out_specs = [pl.BlockSpec((COMBINE_TILE, D_MODEL), row)]
    out_shape = [jax.ShapeDtypeStruct((n, D_MODEL), F32)]
    if not last_layer:
        out_specs.append(pl.BlockSpec((COMBINE_TILE, D_MODEL), row))
        out_shape.append(jax.ShapeDtypeStruct((n, D_MODEL), BF16))
    return pl.pallas_call(
        functools.partial(_combine_kernel, n_tok=n, last_layer=last_layer),
        grid_spec=pltpu.PrefetchScalarGridSpec(
            num_scalar_prefetch=1,
            grid=(n // COMBINE_TILE,),
            in_specs=[
                pl.BlockSpec((COMBINE_TILE, ROUTE_LANES), row),
                pl.BlockSpec((COMBINE_TILE, D_MODEL), row),
                pl.BlockSpec((1, 6, D_MODEL), lambda i, d: (slot(i), 0, 0)),
                pl.BlockSpec((1, D_MODEL), lambda i, d: (0, 0)),
                pl.BlockSpec((1, 6, D_MODEL), lambda i, d: (slot(i), 0, 0)),
                pl.BlockSpec(memory_space=pl.ANY),
            ],
            out_specs=out_specs,
            scratch_shapes=[pltpu.VMEM((2, TOP_K, COMBINE_TILE * ROW_CHUNKS, LANES), F32),
                            pltpu.SemaphoreType.DMA((2, TOP_K))],
        ),
        out_shape=out_shape,
        compiler_params=_params("arbitrary"),
        name="moe_combine",
    )(dest, route, x, mod, norm_g.reshape(1, D_MODEL), next_mod, ys)


def _moe(stream, h_rows, route, route_t, counts, wg, wu, wd, x, mod, norm_g, next_mod, last_layer):
    n = stream.n_tok
    dest, fill, tile_expert, n_used = _dispatch_plan(route_t, counts, n)
    hs = _dispatch(n, dest, fill, h_rows)
    ys = _expert_ffn(n, tile_expert, n_used, hs, wg, wu, wd)
    return _combine(stream, dest, route, x, mod, norm_g, next_mod, ys, last_layer)


IN1_COL = 1024
QK_BLOCKS = 2 * RET_QK_WIDTH // IN1_COL
Q_BLOCKS = RET_QK_WIDTH // IN1_COL
V_BLOCKS = RET_V_WIDTH // IN1_COL


def _inproj1_qk_kernel(h_ref, w_ref, cos_ref, sin_ref, o_ref, *, rope):
    p = _dot(h_ref[...], w_ref[...])
    scale = jnp.where(pl.program_id(0) < Q_BLOCKS, 1.0, RET_DK ** -0.5)
    cos = cos_ref[...]
    sin = sin_ref[...]
    for hh in range(IN1_COL // RET_DK):
        ph = p[:, hh * RET_DK:(hh + 1) * RET_DK]
        if rope:
            ph = _rope(ph, cos, sin)
        o_ref[:, hh * RET_DK:(hh + 1) * RET_DK] = (ph * scale).astype(o_ref.dtype)


def _inproj1_vg_kernel(h_ref, w_ref, o_ref):
    p = _dot(h_ref[...], w_ref[...])
    o_ref[...] = jnp.where(pl.program_id(0) < V_BLOCKS, p, _silu(p)).astype(o_ref.dtype)


def _inproj1(stream, h, w_bf16, cos, sin):
    n = stream.n_tok
    tiles_per_seq = max(stream.seq_len // ROW_TILE, 1)
    rope_map = (lambda j, i: (i % tiles_per_seq, 0)) if stream.rope else (lambda j, i: (0, 0))
    h_spec = pl.BlockSpec((ROW_TILE, D_MODEL), lambda j, i: (i, 0))
    out_spec = pl.BlockSpec((ROW_TILE, IN1_COL), lambda j, i: (i, j))
    qk = pl.pallas_call(
        functools.partial(_inproj1_qk_kernel, rope=stream.rope),
        grid=(QK_BLOCKS, n // ROW_TILE),
        in_specs=[
            h_spec,
            pl.BlockSpec((D_MODEL, IN1_COL), lambda j, i: (0, j)),
            pl.BlockSpec((ROW_TILE, RET_DK), rope_map),
            pl.BlockSpec((ROW_TILE, RET_DK), rope_map),
        ],
        out_specs=out_spec,
        out_shape=jax.ShapeDtypeStruct((n, QK_BLOCKS * IN1_COL), BF16),
        compiler_params=_params("parallel", "parallel"),
        name="inproj1_qk",
    )(h, w_bf16, cos, sin)
    vg_blocks = IN1_WIDTH // IN1_COL - QK_BLOCKS
    vg = pl.pallas_call(
        _inproj1_vg_kernel,
        grid=(vg_blocks, n // ROW_TILE),
        in_specs=[h_spec, pl.BlockSpec((D_MODEL, IN1_COL), lambda j, i: (0, QK_BLOCKS + j))],
        out_specs=out_spec,
        out_shape=jax.ShapeDtypeStruct((n, vg_blocks * IN1_COL), BF16),
        compiler_params=_params("parallel", "parallel"),
        name="inproj1_vg",
    )(h, w_bf16)
    return qk, vg


def _retention_kernel(*refs, n_chunks, has_state, heads, unroll):
    if has_state:
        decay_ref, q_ref, k_ref, v_ref, gf_ref, gb_ref, sf0_ref, sb0_ref, o_ref, inc_ref, seen_ref = refs
    else:
        decay_ref, q_ref, k_ref, v_ref, gf_ref, gb_ref, o_ref, sf_ref, sb_ref, inc_ref, seen_ref = refs
    c = RET_CHUNK
    row = lax.broadcasted_iota(jnp.int32, (c, c), 0).astype(F32)
    col = lax.broadcasted_iota(jnp.int32, (c, c), 1).astype(F32)
    pos = lax.broadcasted_iota(jnp.int32, (c, 1), 0).astype(F32)

    def chunk(i):
        return pl.ds(pl.multiple_of(i * c, c), c)

    for g in range(heads):
        hd = pl.program_id(1) * heads + g
        qk_cols = slice(g * RET_DK, (g + 1) * RET_DK)
        v_cols = slice(g * RET_DV, (g + 1) * RET_DV)
        lg = [-jnp.exp(jnp.full((1, 1), decay_ref[d, hd], F32)) for d in range(2)]
        diff = [row - col, col - row]
        intra = [jnp.where(diff[d] >= 0, jnp.exp(lg[d] * jnp.maximum(diff[d], 0.0)), 0.0) for d in range(2)]
        q_dec = [jnp.exp(lg[0] * (pos + 1.0)), jnp.exp(lg[1] * (c - pos))]
        k_dec = [jnp.exp(lg[0] * ((c - 1.0) - pos)), jnp.exp(lg[1] * pos)]
        chunk_dec = [jnp.exp(lg[d] * float(c)) for d in range(2)]

        def increments(i, carry):
            rows = chunk(i)
            ki = k_ref[rows, qk_cols].astype(F32)
            vi = v_ref[rows, v_cols]
            for d in range(2):
                inc_ref[d, i] = _dot_tn((ki * k_dec[d]).astype(BF16), vi)
            return carry

        lax.fori_loop(0, n_chunks, increments, 0, unroll=unroll)

        finals = []
        for d in range(2):
            def scan(t, state):
                i = t if d == 0 else n_chunks - 1 - t
                seen_ref[d, i] = state.astype(BF16)
                return state * chunk_dec[d] + inc_ref[d, i]

            if has_state:
                state0 = (sf0_ref if d == 0 else sb0_ref)[0, g]
            else:
                state0 = jnp.zeros((RET_DK, RET_DV), F32)
            finals.append(lax.fori_loop(0, n_chunks, scan, state0))
        if not has_state:
            sf_ref[0, g] = finals[0]
            sb_ref[0, g] = finals[1]

        def outputs(i, carry):
            rows = chunk(i)
            qi = q_ref[rows, qk_cols]
            vi = v_ref[rows, v_cols]
            s = _dot_nt(qi, k_ref[rows, qk_cols])
            mixed = None
            for d, g_ref in enumerate((gf_ref, gb_ref)):
                y = _dot((s * intra[d]).astype(BF16), vi) + _dot(qi, seen_ref[d, i]) * q_dec[d]
                gated = g_ref[rows, v_cols].astype(F32) * _rms(y)
                mixed = gated if mixed is None else mixed + gated
            o_ref[rows, v_cols] = mixed.astype(o_ref.dtype)
            return carry

        lax.fori_loop(0, n_chunks, outputs, 0, unroll=unroll)


def _retention(stream, qk, vg, decays, states, heads):
    n = stream.n_tok
    t = stream.seq_len
    n_chunks = t // RET_CHUNK
    has_state = states is not None
    qk_w = heads * RET_DK
    v_w = heads * RET_DV
    k_blk = RET_QK_WIDTH // qk_w
    gf_blk = RET_V_WIDTH // v_w
    gb_blk = 2 * gf_blk
    in_specs = [
        pl.BlockSpec(memory_space=pltpu.SMEM),
        pl.BlockSpec((t, qk_w), lambda b, h: (b, h)),
        pl.BlockSpec((t, qk_w), lambda b, h: (b, k_blk + h)),
        pl.BlockSpec((t, v_w), lambda b, h: (b, h)),
        pl.BlockSpec((t, v_w), lambda b, h: (b, gf_blk + h)),
        pl.BlockSpec((t, v_w), lambda b, h: (b, gb_blk + h)),
    ]
    args = [decays, qk, qk, vg, vg, vg]
    state_spec = pl.BlockSpec((1, heads, RET_DK, RET_DV), lambda b, h: (b, h, 0, 0))
    out_specs = [pl.BlockSpec((t, v_w), lambda b, h: (b, h))]
    out_shape = [jax.ShapeDtypeStruct((n, RET_V_WIDTH), BF16)]
    if has_state:
        in_specs += [state_spec, state_spec]
        args += list(states)
    else:
        out_specs += [state_spec, state_spec]
        out_shape += [jax.ShapeDtypeStruct((stream.n_seq, RET_HEADS, RET_DK, RET_DV), F32)] * 2
    return pl.pallas_call(
        functools.partial(_retention_kernel, n_chunks=n_chunks, has_state=has_state, heads=heads,
                          unroll=min(n_chunks, 4)),
        grid=(stream.n_seq, RET_HEADS // heads),
        in_specs=in_specs,
        out_specs=out_specs,
        out_shape=out_shape,
        scratch_shapes=[pltpu.VMEM((2, n_chunks, RET_DK, RET_DV), F32),
                        pltpu.VMEM((2, n_chunks, RET_DK, RET_DV), BF16)],
        compiler_params=_params("parallel", "parallel"),
        name="retention",
    )(*args)


def _rope_tables(n_tokens, dim):
    t = jnp.arange(n_tokens)
    pos = jnp.stack([t // GRID_W, t % GRID_W]).astype(F32)
    n_freq = dim // 4
    freqs = ROPE_THETA ** (-jnp.arange(n_freq, dtype=F32) / n_freq)
    ang = pos[:, :, None] * freqs
    cos, sin = jnp.cos(ang), jnp.sin(ang)
    cos_t = jnp.concatenate([cos[0], cos[0], cos[1], cos[1]], axis=-1)
    sin_t = jnp.concatenate([-sin[0], sin[0], -sin[1], sin[1]], axis=-1)
    return cos_t, sin_t


def kernel(x_prompt, x_sample, cache_k0, cache_v0, state_ret_fwd1, state_ret_bwd1, c, c_ctx,
           norm_mix_g0, mod_w0, mod_b0, in_w0, conv_w0, conv_b0, conv_norm_g0, conv_norm_b0,
           q_norm_g0, k_norm_g0, out_w0, norm_ffn_g0,
           moe_grp_w0, moe_grp_b0, moe_rtr_w0, moe_rtr_b0, moe_w_gate0, moe_w_up0, moe_w_down0,
           norm_mix_g1, mod_w1, mod_b1, in_w1, ret_decay_fwd1, ret_decay_bwd1, out_w1, norm_ffn_g1,
           moe_grp_w1, moe_grp_b1, moe_rtr_w1, moe_rtr_b1, moe_w_gate1, moe_w_up1, moe_w_down1,
           final_norm_g):
    batch, seq, d = x_prompt.shape
    dec_batch, dec_seq, _ = x_sample.shape
    assert d == D_MODEL and 1 + dec_batch <= MOD_SLOTS
    ctx = _Stream(batch, seq, first_slot=0, rope=False)
    lat = _Stream(dec_batch, dec_seq, first_slot=1, rope=True)
    assert ctx.n_tok % ROW_TILE == 0 and lat.n_tok % ROW_TILE == 0 and ROW_CHUNKS == SUBLANES
    assert seq % CONV_TILE == 0 and dec_seq % ROW_TILE == 0 and ROW_TILE % seq == 0

    cond = jnp.concatenate([c_ctx[None, :], c, jnp.zeros((MOD_SLOTS - 1 - dec_batch, d), F32)], axis=0)
    mod0 = _modulation(cond, mod_w0, mod_b0)
    mod1 = _modulation(cond, mod_w1, mod_b1)
    cos, sin = _rope_tables(dec_seq, HEAD_DIM)
    decays = jnp.stack([ret_decay_fwd1, ret_decay_bwd1]).astype(F32)

    in_w0_b = in_w0.astype(BF16)
    out_w0_b = out_w0.astype(BF16)
    in_w1_b = in_w1.astype(BF16)
    out_w1_b = out_w1.astype(BF16)
    moe0 = (moe_w_gate0.astype(BF16), moe_w_up0.astype(BF16), moe_w_down0.astype(BF16))
    moe1 = (moe_w_gate1.astype(BF16), moe_w_up1.astype(BF16), moe_w_down1.astype(BF16))
    route0 = _router_params(moe_grp_w0, moe_grp_b0, moe_rtr_w0, moe_rtr_b0)
    route1 = _router_params(moe_grp_w1, moe_grp_b1, moe_rtr_w1, moe_rtr_b1)

    def run(stream, x, cache, states):
        x = x.reshape(stream.n_tok, d)
        kv_dtype = BF16 if stream.rope else F32
        a, q, k, v = _inproj0(stream, x, mod0, norm_mix_g0, in_w0_b, q_norm_g0, k_norm_g0, cos, sin, kv_dtype)
        a = _conv_branch(stream, a, conv_w0, conv_b0, conv_norm_g0, conv_norm_b0)
        tq = min(stream.seq_len, 128 if cache is not None else 256)
        o = _attention(stream, q, k, v, cache, tq)
        x, *routed = _outproj_route(stream, [a, o], out_w0_b, x, mod0, norm_ffn_g0, *route0)
        x, h = _moe(stream, *routed, *moe0, x, mod0, norm_mix_g1, mod1, last_layer=False)
        qk, vg = _inproj1(stream, h, in_w1_b, cos, sin)
        ret = _retention(stream, qk, vg, decays, states, heads=2 if states is not None else 4)
        x, *routed = _outproj_route(stream, [ret[0]], out_w1_b, x, mod1, norm_ffn_g1, *route1)
        y, = _moe(stream, *routed, *moe1, x, mod1, final_norm_g, mod1, last_layer=True)
        return y.reshape(stream.n_seq, stream.seq_len, d), k, v, ret[1:]

    y_prompt, k_ctx, v_ctx, new_states = run(ctx, x_prompt, None, None)
    y_sample, _, _, _ = run(lat, x_sample, (cache_k0, cache_v0), (state_ret_fwd1, state_ret_bwd1))
    new_k0 = k_ctx.reshape(batch, seq, N_KV_HEADS, HEAD_DIM).transpose(0, 2, 1, 3)
    new_v0 = v_ctx.reshape(batch, seq, N_KV_HEADS, HEAD_DIM).transpose(0, 2, 1, 3)
    return (y_prompt, y_sample, new_k0, new_v0, new_states[0], new_states[1])
```

```python
import functools

import jax
import jax.numpy as jnp
from jax import lax
from jax.experimental import pallas as pl
from jax.experimental.pallas import tpu as pltpu

D_MODEL = 1024
GRID_W = 64
EPS = 1e-6
CONV_CH = 512
CONV_WIDTH = 31
CONV_HALF = CONV_WIDTH // 2
N_Q_HEADS = 8
N_KV_HEADS = 2
Q_PER_KV = N_Q_HEADS // N_KV_HEADS
HEAD_DIM = 128
ROPE_THETA = 10000.0
ATTN_WIDTH = N_Q_HEADS * HEAD_DIM
KV_WIDTH = N_KV_HEADS * HEAD_DIM
IN0_WIDTH = 2 * CONV_CH + ATTN_WIDTH + 2 * KV_WIDTH
RET_HEADS = 8
RET_DK = 128
RET_DV = 256
RET_CHUNK = 128
RET_QK_WIDTH = RET_HEADS * RET_DK
RET_V_WIDTH = RET_HEADS * RET_DV
IN1_WIDTH = 2 * RET_QK_WIDTH + 3 * RET_V_WIDTH
N_GROUPS = 4
EXPERTS_PER_GROUP = 4
N_EXPERTS = N_GROUPS * EXPERTS_PER_GROUP
EXPERT_FF = 512

LANES = 128
SUBLANES = 8
VMEM_LIMIT_BYTES = 56 * 1024 * 1024

MOD_SLOTS = 8
ROW_TILE = 512
TOP_K = 2
ROW_CHUNKS = D_MODEL // LANES
FFN_TILE = 256
DISPATCH_TILE = 256
COMBINE_TILE = 256
ROUTE_E1, ROUTE_E2, ROUTE_R1, ROUTE_R2, ROUTE_W1, ROUTE_W2 = range(6)
ROUTE_INT_FIELDS = 4
CONV_TILE = 256
CONV_HALO = 16
CONV_ROWS = 64
ROUTE_LANES = LANES
EXPERT_LANE0 = N_GROUPS
NEG_BIG = -1e30
LOG2_E = 1.4426950408889634
ATTN_KV_CHUNK = 512

F32 = jnp.float32
BF16 = jnp.bfloat16


def _params(*semantics):
    return pltpu.CompilerParams(dimension_semantics=semantics, vmem_limit_bytes=VMEM_LIMIT_BYTES)


def _sigmoid(x):
    return 1.0 / (1.0 + jnp.exp(-x))


def _silu(x):
    return x * _sigmoid(x)


def _rms(x):
    return x * lax.rsqrt(jnp.mean(x * x, axis=-1, keepdims=True) + EPS)


def _rope(x, cos, sin):
    lane = lax.broadcasted_iota(jnp.int32, x.shape, 1)
    take_upper = (lane % (HEAD_DIM // 2)) < (HEAD_DIM // 4)
    partner = jnp.where(take_upper, pltpu.roll(x, HEAD_DIM - HEAD_DIM // 4, 1), pltpu.roll(x, HEAD_DIM // 4, 1))
    return x * cos + partner * sin


def _dot(a, b):
    return jnp.dot(a, b, preferred_element_type=F32)


def _dot_nt(a, b):
    return lax.dot_general(a, b, (((1,), (1,)), ((), ())), preferred_element_type=F32)


def _dot_tn(a, b):
    return lax.dot_general(a, b, (((0,), (0,)), ((), ())), preferred_element_type=F32)


def _mod_kernel(c_ref, w_ref, b_ref, o_ref):
    c = c_ref[...]
    o_ref[...] = _dot(_silu(c).astype(BF16), w_ref[...].astype(BF16)) + b_ref[...]


def _modulation(cond, mod_w, mod_b):
    n_out = mod_w.shape[1]
    col = D_MODEL
    out = pl.pallas_call(
        _mod_kernel,
        grid=(n_out // col,),
        in_specs=[
            pl.BlockSpec((MOD_SLOTS, D_MODEL), lambda j: (0, 0)),
            pl.BlockSpec((D_MODEL, col), lambda j: (0, j)),
            pl.BlockSpec((1, col), lambda j: (0, j)),
        ],
        out_specs=pl.BlockSpec((MOD_SLOTS, col), lambda j: (0, j)),
        out_shape=jax.ShapeDtypeStruct((MOD_SLOTS, n_out), F32),
        compiler_params=_params("parallel"),
        name="modulation",
    )(cond, mod_w, mod_b.reshape(1, n_out))
    return out.reshape(MOD_SLOTS, 6, D_MODEL)


def _slot_map(stream):
    if stream.first_slot == 0:
        return lambda i: 0
    return lambda i: stream.first_slot + (i * ROW_TILE) // stream.seq_len


class _Stream:
    def __init__(self, n_seq, seq_len, first_slot, rope):
        self.n_seq = n_seq
        self.seq_len = seq_len
        self.first_slot = first_slot
        self.rope = rope
        self.n_tok = n_seq * seq_len


def _modulated_norm(x, g, shift, scale):
    return _rms(x) * g * (1.0 + scale) + shift


def _inproj0_kernel(x_ref, mod_ref, g_ref, w_ref, qg_ref, kg_ref, cos_ref, sin_ref,
                    a_ref, q_ref, k_ref, v_ref, *, rope):
    h = _modulated_norm(x_ref[...], g_ref[...], mod_ref[0, 0:1, :], mod_ref[0, 1:2, :])
    p = _dot(h.astype(BF16), w_ref[...])
    a_ref[...] = p[:, :CONV_CH] * _sigmoid(p[:, CONV_CH:2 * CONV_CH])
    q0 = 2 * CONV_CH
    k0 = q0 + ATTN_WIDTH
    v0 = k0 + KV_WIDTH
    cos = cos_ref[...]
    sin = sin_ref[...]
    q_scale = HEAD_DIM ** -0.5 * LOG2_E
    for hh in range(N_Q_HEADS):
        qh = _rms(p[:, q0 + hh * HEAD_DIM:q0 + (hh + 1) * HEAD_DIM]) * qg_ref[...]
        if rope:
            qh = _rope(qh, cos, sin)
        q_ref[:, hh * HEAD_DIM:(hh + 1) * HEAD_DIM] = (qh * q_scale).astype(q_ref.dtype)
    seqs, _, rows, _ = k_ref.shape
    for hh in range(N_KV_HEADS):
        kh = _rms(p[:, k0 + hh * HEAD_DIM:k0 + (hh + 1) * HEAD_DIM]) * kg_ref[...]
        if rope:
            kh = _rope(kh, cos, sin)
        vh = p[:, v0 + hh * HEAD_DIM:v0 + (hh + 1) * HEAD_DIM]
        for sq in range(seqs):
            k_ref[sq, hh] = kh[sq * rows:(sq + 1) * rows].astype(k_ref.dtype)
            v_ref[sq, hh] = vh[sq * rows:(sq + 1) * rows].astype(v_ref.dtype)


def _inproj0(stream, x, mod, norm_g, w_bf16, q_norm_g, k_norm_g, cos, sin, kv_dtype):
    n = stream.n_tok
    tiles_per_seq = max(stream.seq_len // ROW_TILE, 1)
    rope_map = (lambda i: (i % tiles_per_seq, 0)) if stream.rope else (lambda i: (0, 0))
    slot = _slot_map(stream)
    row = lambda i: (i, 0)
    fixed = lambda i: (0, 0)
    seqs_per_tile = max(ROW_TILE // stream.seq_len, 1)
    kv_spec = pl.BlockSpec((seqs_per_tile, N_KV_HEADS, ROW_TILE // seqs_per_tile, HEAD_DIM),
                           lambda i: (i // tiles_per_seq, 0, i % tiles_per_seq, 0))
    kv_shape = jax.ShapeDtypeStruct((stream.n_seq, N_KV_HEADS, stream.seq_len, HEAD_DIM), kv_dtype)
    return pl.pallas_call(
        functools.partial(_inproj0_kernel, rope=stream.rope),
        grid=(n // ROW_TILE,),
        in_specs=[
            pl.BlockSpec((ROW_TILE, D_MODEL), row),
            pl.BlockSpec((1, 6, D_MODEL), lambda i: (slot(i), 0, 0)),
            pl.BlockSpec((1, D_MODEL), fixed),
            pl.BlockSpec((D_MODEL, IN0_WIDTH), fixed),
            pl.BlockSpec((1, HEAD_DIM), fixed),
            pl.BlockSpec((1, HEAD_DIM), fixed),
            pl.BlockSpec((ROW_TILE, HEAD_DIM), rope_map),
            pl.BlockSpec((ROW_TILE, HEAD_DIM), rope_map),
        ],
        out_specs=[
            pl.BlockSpec((ROW_TILE, CONV_CH), row),
            pl.BlockSpec((ROW_TILE, ATTN_WIDTH), row),
            kv_spec,
            kv_spec,
        ],
        out_shape=[
            jax.ShapeDtypeStruct((n, CONV_CH), F32),
            jax.ShapeDtypeStruct((n, ATTN_WIDTH), BF16),
            kv_shape,
            kv_shape,
        ],
        compiler_params=_params("parallel"),
        name="inproj0",
    )(x, mod, norm_g.reshape(1, D_MODEL), w_bf16, q_norm_g.reshape(1, HEAD_DIM), k_norm_g.reshape(1, HEAD_DIM),
      cos, sin)


def _conv_kernel(prev_ref, main_ref, next_ref, w_ref, b_ref, g_ref, beta_ref, o_ref, pad_ref, acc_ref, *,
                 tiles_per_seq):
    i = pl.program_id(0)
    has_prev = (i % tiles_per_seq) != 0
    has_next = (i % tiles_per_seq) != (tiles_per_seq - 1)
    pad_ref[0:CONV_HALO, :] = jnp.where(has_prev, prev_ref[...], 0.0)
    pad_ref[CONV_HALO:CONV_HALO + CONV_TILE, :] = main_ref[...]
    pad_ref[CONV_HALO + CONV_TILE:, :] = jnp.where(has_next, next_ref[...], 0.0)
    first = CONV_HALO - CONV_HALF
    for c0 in range(0, CONV_CH, LANES):
        for r0 in range(0, CONV_TILE, CONV_ROWS):
            acc = jnp.zeros((CONV_ROWS, LANES), F32)
            for j in range(CONV_WIDTH):
                acc = acc + w_ref[j:j + 1, c0:c0 + LANES] * pad_ref[first + r0 + j:first + r0 + j + CONV_ROWS,
                                                                    c0:c0 + LANES]
            acc_ref[r0:r0 + CONV_ROWS, c0:c0 + LANES] = acc
    a = acc_ref[...] + b_ref[...]
    mu = jnp.mean(a, axis=-1, keepdims=True)
    d = a - mu
    var = jnp.mean(d * d, axis=-1, keepdims=True)
    y = d * lax.rsqrt(var + EPS) * g_ref[...] + beta_ref[...]
    o_ref[...] = _silu(y).astype(o_ref.dtype)


def _conv_branch(stream, a, conv_w, conv_b, norm_g, norm_b):
    n = stream.n_tok
    tiles_per_seq = stream.seq_len // CONV_TILE
    halo_per_tile = CONV_TILE // CONV_HALO
    n_halo = n // CONV_HALO
    fixed = lambda i: (0, 0)
    return pl.pallas_call(
        functools.partial(_conv_kernel, tiles_per_seq=tiles_per_seq),
        grid=(n // CONV_TILE,),
        in_specs=[
            pl.BlockSpec((CONV_HALO, CONV_CH), lambda i: (jnp.maximum(i * halo_per_tile - 1, 0), 0)),
            pl.BlockSpec((CONV_TILE, CONV_CH), lambda i: (i, 0)),
            pl.BlockSpec((CONV_HALO, CONV_CH), lambda i: (jnp.minimum((i + 1) * halo_per_tile, n_halo - 1), 0)),
            pl.BlockSpec((CONV_WIDTH, CONV_CH), fixed),
            pl.BlockSpec((1, CONV_CH), fixed),
            pl.BlockSpec((1, CONV_CH), fixed),
            pl.BlockSpec((1, CONV_CH), fixed),
        ],
        out_specs=pl.BlockSpec((CONV_TILE, CONV_CH), lambda i: (i, 0)),
        out_shape=jax.ShapeDtypeStruct((n, CONV_CH), BF16),
        scratch_shapes=[
            pltpu.VMEM((CONV_TILE + 2 * CONV_HALO, CONV_CH), F32),
            pltpu.VMEM((CONV_TILE, CONV_CH), F32),
        ],
        compiler_params=_params("parallel"),
        name="conv_branch",
    )(a, a, a, conv_w, conv_b.reshape(1, CONV_CH), norm_g.reshape(1, CONV_CH), norm_b.reshape(1, CONV_CH))


def _attn_kernel(*refs, tq, past):
    if past:
        q_ref, k_ref, v_ref, kc_ref, vc_ref, o_ref, ks_ref, vs_ref = refs
    else:
        q_ref, k_ref, v_ref, o_ref, ks_ref, vs_ref = refs
    n_keys = ks_ref.shape[0]

    @pl.when(pl.program_id(2) == 0)
    def _():
        lane = lax.broadcasted_iota(jnp.int32, (n_keys, HEAD_DIM), 1)
        vs_ref[:, HEAD_DIM:] = jnp.where(lane == 0, 1.0, 0.0).astype(BF16)
        if past:
            ks_ref[0:past, :] = kc_ref[0, 0].astype(BF16)
            vs_ref[0:past, 0:HEAD_DIM] = vc_ref[0, 0].astype(BF16)
        ks_ref[past:, :] = k_ref[0, 0].astype(BF16)
        vs_ref[past:, 0:HEAD_DIM] = v_ref[0, 0].astype(BF16)

    q = q_ref[...]
    q4 = jnp.concatenate([q[:, g * HEAD_DIM:(g + 1) * HEAD_DIM] for g in range(Q_PER_KV)], axis=0)
    rows = q4.shape[0]
    m = jnp.full((rows, 1), NEG_BIG, F32)
    acc = jnp.zeros((rows, 2 * HEAD_DIM), F32)
    for c0 in range(0, n_keys, ATTN_KV_CHUNK):
        c1 = min(c0 + ATTN_KV_CHUNK, n_keys)
        s = _dot_nt(q4, ks_ref[c0:c1, :])
        m_new = jnp.maximum(m, jnp.max(s, axis=-1, keepdims=True))
        p = jnp.exp2(s - m_new).astype(BF16)
        acc = jnp.exp2(m - m_new) * acc + _dot(p, vs_ref[c0:c1, :])
        m = m_new
    o = acc[:, 0:HEAD_DIM] * (1.0 / acc[:, HEAD_DIM:HEAD_DIM + 1])
    for g in range(Q_PER_KV):
        o_ref[:, g * HEAD_DIM:(g + 1) * HEAD_DIM] = o[g * tq:(g + 1) * tq].astype(o_ref.dtype)


def _attention(stream, q, k, v, cache, tq):
    n = stream.n_tok
    t = stream.seq_len
    q_tiles = t // tq
    group_w = Q_PER_KV * HEAD_DIM
    in_specs = [
        pl.BlockSpec((tq, group_w), lambda b, h, i: (b * q_tiles + i, h)),
        pl.BlockSpec((1, 1, t, HEAD_DIM), lambda b, h, i: (b, h, 0, 0)),
        pl.BlockSpec((1, 1, t, HEAD_DIM), lambda b, h, i: (b, h, 0, 0)),
    ]
    args = [q, k, v]
    past = 0
    if cache is not None:
        past = cache[0].shape[2]
        in_specs += [pl.BlockSpec((1, 1, past, HEAD_DIM), lambda b, h, i: (b, h, 0, 0))] * 2
        args += list(cache)
    return pl.pallas_call(
        functools.partial(_attn_kernel, tq=tq, past=past),
        grid=(stream.n_seq, N_KV_HEADS, q_tiles),
        in_specs=in_specs,
        out_specs=pl.BlockSpec((tq, group_w), lambda b, h, i: (b * q_tiles + i, h)),
        out_shape=jax.ShapeDtypeStruct((n, ATTN_WIDTH), BF16),
        scratch_shapes=[pltpu.VMEM((past + t, HEAD_DIM), BF16), pltpu.VMEM((past + t, 2 * HEAD_DIM), BF16)],
        compiler_params=_params("parallel", "parallel", "arbitrary"),
        name="attention",
    )(*args)


def _route(logits, running):
    rows = logits.shape[0]
    lane = lax.broadcasted_iota(jnp.int32, logits.shape, 1)
    lane_f = lane.astype(F32)
    far = float(ROUTE_LANES)
    is_group = lane < N_GROUPS
    gl = jnp.where(is_group, logits, NEG_BIG)
    gmax = jnp.max(gl, axis=-1, keepdims=True)
    gsum = jnp.sum(jnp.where(is_group, jnp.exp(gl - gmax), 0.0), axis=-1, keepdims=True)
    g_w = 1.0 / gsum
    gidx = jnp.min(jnp.where(gl == gmax, lane_f, far), axis=-1, keepdims=True)
    lo = EXPERT_LANE0 + EXPERTS_PER_GROUP * gidx
    in_group = (lane_f >= lo) & (lane_f < lo + EXPERTS_PER_GROUP)
    el = jnp.where(in_group, logits, NEG_BIG)
    v1 = jnp.max(el, axis=-1, keepdims=True)
    i1 = jnp.min(jnp.where(el == v1, lane_f, far), axis=-1, keepdims=True)
    el2 = jnp.where(lane_f == i1, NEG_BIG, el)
    v2 = jnp.max(el2, axis=-1, keepdims=True)
    i2 = jnp.min(jnp.where(el2 == v2, lane_f, far), axis=-1, keepdims=True)
    e2 = jnp.exp(v2 - v1)
    w1 = g_w / (1.0 + e2)
    w2 = w1 * e2
    is1 = lane_f == i1
    is2 = lane_f == i2
    chosen = (is1 | is2).astype(BF16)
    earlier = (lax.broadcasted_iota(jnp.int32, (rows, rows), 0)
               > lax.broadcasted_iota(jnp.int32, (rows, rows), 1)).astype(BF16)
    before = running + _dot(earlier, chosen)
    r1 = jnp.sum(jnp.where(is1, before, 0.0), axis=-1, keepdims=True)
    r2 = jnp.sum(jnp.where(is2, before, 0.0), axis=-1, keepdims=True)
    record = jnp.zeros_like(logits)
    for at, val in ((ROUTE_E1, i1 - EXPERT_LANE0), (ROUTE_E2, i2 - EXPERT_LANE0), (ROUTE_W1, w1), (ROUTE_W2, w2),
                    (ROUTE_R1, r1), (ROUTE_R2, r2)):
        record = jnp.where(lane == at, val, record)
    return record, running + jnp.sum(chosen.astype(F32), axis=0, keepdims=True)


def _outproj_kernel(*refs, widths):
    n_in = len(widths)
    in_refs = refs[:n_in]
    w_ref, x_ref, mod_ref, g_ref, rw_ref, rb_ref, x1_ref, h_ref, route_ref, route_t_ref, count_ref, run_ref = refs[n_in:]

    @pl.when(pl.program_id(0) == 0)
    def _():
        run_ref[...] = jnp.zeros_like(run_ref)

    acc = None
    off = 0
    for r, width in zip(in_refs, widths):
        part = _dot(r[...], w_ref[off:off + width, :])
        acc = part if acc is None else acc + part
        off += width
    x1 = x_ref[...] + mod_ref[0, 2:3, :] * acc
    x1_ref[...] = x1
    h = _modulated_norm(x1, g_ref[...], mod_ref[0, 3:4, :], mod_ref[0, 4:5, :])
    _store_row_major(h_ref, h)
    h_hi = h.astype(BF16)
    h_lo = (h - h_hi.astype(F32)).astype(BF16)
    both = _dot(h_hi, rw_ref[...])
    logits = (both[:, :ROUTE_LANES] + both[:, ROUTE_LANES:] + _dot(h_lo, rw_ref[:, :ROUTE_LANES]) + rb_ref[...])
    record, running = _route(logits, run_ref[...])
    route_ref[...] = record
    route_t_ref[...] = record.T[:SUBLANES]
    run_ref[...] = running
    count_ref[...] = jnp.broadcast_to(running, count_ref.shape)


def _store_row_major(ref, x):
    rows = x.shape[0]
    for c in range(ROW_CHUNKS):
        ref[pl.ds(c, rows, stride=ROW_CHUNKS), :] = x[:, c * LANES:(c + 1) * LANES]


def _load_row_major(ref, rows):
    return jnp.concatenate([ref[pl.ds(c, rows, stride=ROW_CHUNKS), :] for c in range(ROW_CHUNKS)], axis=-1)


def _outproj_route(stream, inputs, w_bf16, x, mod, ffn_norm_g, route_w, route_b):
    n = stream.n_tok
    widths = tuple(a.shape[1] for a in inputs)
    slot = _slot_map(stream)
    row = lambda i: (i, 0)
    fixed = lambda i: (0, 0)
    return pl.pallas_call(
        functools.partial(_outproj_kernel, widths=widths),
        grid=(n // ROW_TILE,),
        in_specs=[pl.BlockSpec((ROW_TILE, width), row) for width in widths] + [
            pl.BlockSpec((sum(widths), D_MODEL), fixed),
            pl.BlockSpec((ROW_TILE, D_MODEL), row),
            pl.BlockSpec((1, 6, D_MODEL), lambda i: (slot(i), 0, 0)),
            pl.BlockSpec((1, D_MODEL), fixed),
            pl.BlockSpec((D_MODEL, 2 * ROUTE_LANES), fixed),
            pl.BlockSpec((1, ROUTE_LANES), fixed),
        ],
        out_specs=[
            pl.BlockSpec((ROW_TILE, D_MODEL), row),
            pl.BlockSpec((ROW_TILE * ROW_CHUNKS, LANES), row),
            pl.BlockSpec((ROW_TILE, ROUTE_LANES), row),
            pl.BlockSpec((SUBLANES, ROW_TILE), lambda i: (0, i)),
            pl.BlockSpec((SUBLANES, ROUTE_LANES), fixed),
        ],
        out_shape=[
            jax.ShapeDtypeStruct((n, D_MODEL), F32),
            jax.ShapeDtypeStruct((n * ROW_CHUNKS, LANES), F32),
            jax.ShapeDtypeStruct((n, ROUTE_LANES), F32),
            jax.ShapeDtypeStruct((SUBLANES, n), F32),
            jax.ShapeDtypeStruct((SUBLANES, ROUTE_LANES), F32),
        ],
        scratch_shapes=[pltpu.VMEM((1, ROUTE_LANES), F32)],
        compiler_params=_params("arbitrary"),
        name="outproj_route",
    )(*inputs, w_bf16, x, mod, ffn_norm_g.reshape(1, D_MODEL), route_w, route_b)


def _router_params(grp_w, grp_b, rtr_w, rtr_b):
    w = jnp.concatenate([grp_w, jnp.moveaxis(rtr_w, 0, 1).reshape(D_MODEL, N_EXPERTS)], axis=1)
    w = jnp.pad(w, ((0, 0), (0, ROUTE_LANES - w.shape[1])))
    hi = w.astype(BF16)
    lo = (w - hi.astype(F32)).astype(BF16)
    b = jnp.concatenate([grp_b, rtr_b.reshape(N_EXPERTS)])
    b = jnp.pad(b, (0, ROUTE_LANES - b.shape[0])).reshape(1, ROUTE_LANES)
    return jnp.concatenate([hi, lo], axis=1), b


def _sorted_tiles(n_tok):
    return (TOP_K * n_tok) // FFN_TILE + N_EXPERTS


def _dispatch_plan(route_t, counts, n_tok):
    count = counts[0, EXPERT_LANE0:EXPERT_LANE0 + N_EXPERTS].astype(jnp.int32)
    tiles = (count + (FFN_TILE - 1)) // FFN_TILE
    padded = tiles * FFN_TILE
    start = jnp.cumsum(padded) - padded
    n_used = jnp.sum(tiles)
    tile_end = jnp.cumsum(tiles)
    j = jnp.minimum(jnp.arange(_sorted_tiles(n_tok), dtype=jnp.int32), n_used - 1)
    tile_expert = jnp.sum(j[:, None] >= tile_end[None, :], axis=1).astype(jnp.int32)
    where = jnp.concatenate([start, route_t[:ROUTE_INT_FIELDS].astype(jnp.int32).reshape(-1)])
    fill = jnp.concatenate([start + count, start + padded])
    return where, fill, tile_expert, n_used.reshape(1)


def _sorted_row(where_ref, n_tok, tok, k):
    expert = where_ref[N_EXPERTS + (ROUTE_E1 + k) * n_tok + tok]
    return where_ref[expert] + where_ref[N_EXPERTS + (ROUTE_R1 + k) * n_tok + tok]


def _token_rows(t):
    return pl.ds(pl.multiple_of(t * ROW_CHUNKS, ROW_CHUNKS), ROW_CHUNKS)


def _dispatch_kernel(where_ref, fill_ref, h_ref, hs_ref, zero_ref, sems, *, n_tok):
    i = pl.program_id(0)
    base = i * DISPATCH_TILE

    def row_copy(t, k):
        d = _sorted_row(where_ref, n_tok, base + t, k)
        return pltpu.make_async_copy(h_ref.at[_token_rows(t)], hs_ref.at[_token_rows(d)], sems.at[k])

    def issue(t, carry):
        for k in range(TOP_K):
            row_copy(t, k).start(priority=k)
        return carry

    lax.fori_loop(0, DISPATCH_TILE, issue, 0, unroll=8)

    def fill_copy(r):
        return pltpu.make_async_copy(zero_ref, hs_ref.at[_token_rows(r)], sems.at[TOP_K])

    @pl.when(i == 0)
    def _():
        zero_ref[...] = jnp.zeros_like(zero_ref)
        for e in range(N_EXPERTS):
            lax.fori_loop(fill_ref[e], fill_ref[N_EXPERTS + e], lambda r, c: (fill_copy(r).start(), c)[1], 0)
        for e in range(N_EXPERTS):
            lax.fori_loop(fill_ref[e], fill_ref[N_EXPERTS + e], lambda r, c: (fill_copy(r).wait(), c)[1], 0)

    for k in range(TOP_K):
        pltpu.make_async_copy(h_ref, hs_ref.at[pl.ds(0, DISPATCH_TILE * ROW_CHUNKS)], sems.at[k]).wait()


def _dispatch(n_tok, where, fill, h_rows):
    rows = _sorted_tiles(n_tok) * FFN_TILE
    return pl.pallas_call(
        functools.partial(_dispatch_kernel, n_tok=n_tok),
        grid_spec=pltpu.PrefetchScalarGridSpec(
            num_scalar_prefetch=2,
            grid=(n_tok // DISPATCH_TILE,),
            in_specs=[pl.BlockSpec((DISPATCH_TILE * ROW_CHUNKS, LANES), lambda i, d, f: (i, 0))],
            out_specs=pl.BlockSpec(memory_space=pl.ANY),
            scratch_shapes=[pltpu.VMEM((ROW_CHUNKS, LANES), F32), pltpu.SemaphoreType.DMA((TOP_K + 1,))],
        ),
        out_shape=jax.ShapeDtypeStruct((rows * ROW_CHUNKS, LANES), F32),
        compiler_params=_params("arbitrary"),
        name="moe_dispatch",
    )(where, fill, h_rows)


def _ffn_kernel(te_ref, nu_ref, hs_ref, wg_ref, wu_ref, wd_ref, ys_ref):
    @pl.when(pl.program_id(0) < nu_ref[0])
    def _():
        x = _load_row_major(hs_ref, FFN_TILE).astype(BF16)
        hidden = _silu(_dot(x, wg_ref[0])) * _dot(x, wu_ref[0])
        _store_row_major(ys_ref, _dot(hidden.astype(BF16), wd_ref[0]))


def _expert_ffn(n_tok, tile_expert, n_used, hs, wg, wu, wd):
    rows = lambda j, te, nu: (jnp.minimum(j, nu[0] - 1), 0)
    expert = lambda j, te, nu: (te[j], 0, 0)
    return pl.pallas_call(
        _ffn_kernel,
        grid_spec=pltpu.PrefetchScalarGridSpec(
            num_scalar_prefetch=2,
            grid=(_sorted_tiles(n_tok),),
            in_specs=[
                pl.BlockSpec((FFN_TILE * ROW_CHUNKS, LANES), rows),
                pl.BlockSpec((1, D_MODEL, EXPERT_FF), expert),
                pl.BlockSpec((1, D_MODEL, EXPERT_FF), expert),
                pl.BlockSpec((1, EXPERT_FF, D_MODEL), expert),
            ],
            out_specs=pl.BlockSpec((FFN_TILE * ROW_CHUNKS, LANES), rows),
        ),
        out_shape=jax.ShapeDtypeStruct(hs.shape, F32),
        compiler_params=_params("arbitrary"),
        name="moe_ffn",
    )(tile_expert, n_used, hs, wg, wu, wd)


def _combine_kernel(where_ref, route_ref, x_ref, mod_ref, ng_ref, nmod_ref, ys_ref, o_ref, *rest, n_tok, last_layer):
    if last_layer:
        y_ref, sems = rest
    else:
        h_ref, y_ref, sems = rest
    i = pl.program_id(0)
    n_tiles = pl.num_programs(0)

    def row_copy(tile, t, k):
        slot = tile % 2
        d = _sorted_row(where_ref, n_tok, tile * COMBINE_TILE + t, k)
        return pltpu.make_async_copy(ys_ref.at[_token_rows(d)], y_ref.at[slot, k, _token_rows(t)], sems.at[slot, k])

    def start_tile(tile):
        def issue(t, carry):
            for k in range(TOP_K):
                row_copy(tile, t, k).start(priority=k)
            return carry

        lax.fori_loop(0, COMBINE_TILE, issue, 0, unroll=8)

    @pl.when(i == 0)
    def _():
        start_tile(i)

    @pl.when(i + 1 < n_tiles)
    def _():
        start_tile(i + 1)

    slot = i % 2
    route = route_ref[...]
    w1 = route[:, ROUTE_W1:ROUTE_W1 + 1]
    w2 = route[:, ROUTE_W2:ROUTE_W2 + 1]
    for k in range(TOP_K):
        pltpu.make_async_copy(ys_ref.at[pl.ds(0, COMBINE_TILE * ROW_CHUNKS)], y_ref.at[slot, k],
                              sems.at[slot, k]).wait()
    y = (w1 * _load_row_major(y_ref.at[slot, 0], COMBINE_TILE)
         + w2 * _load_row_major(y_ref.at[slot, 1], COMBINE_TILE))
    x = x_ref[...] + mod_ref[0, 5:6, :] * y
    if last_layer:
        o_ref[...] = _rms(x) * ng_ref[...]
    else:
        o_ref[...] = x
        h_ref[...] = _modulated_norm(x, ng_ref[...], nmod_ref[0, 0:1, :], nmod_ref[0, 1:2, :]).astype(h_ref.dtype)


def _combine(stream, where, route, x, mod, norm_g, next_mod, ys, last_layer):
    n = stream.n_tok
    first_slot, seq_len = stream.first_slot, stream.seq_len
    slot = (lambda i: 0) if first_slot == 0 else (lambda i: first_slot + (i * COMBINE_TILE) // seq_len)
    row = lambda i, d: (i, 0)
    out_specs = [pl.BlockSpec((COMBINE_TILE, D_MODEL), row)]
    out_shape = [jax.ShapeDtypeStruct((n, D_MODEL), F32)]
    if not last_layer:
        out_specs.append(pl.BlockSpec((COMBINE_TILE, D_MODEL), row))
        out_shape.append(jax.ShapeDtypeStruct((n, D_MODEL), BF16))
    return pl.pallas_call(
        functools.partial(_combine_kernel, n_tok=n, last_layer=last_layer),
        grid_spec=pltpu.PrefetchScalarGridSpec(
            num_scalar_prefetch=1,
            grid=(n // COMBINE_TILE,),
            in_specs=[
                pl.BlockSpec((COMBINE_TILE, ROUTE_LANES), row),
                pl.BlockSpec((COMBINE_TILE, D_MODEL), row),
                pl.BlockSpec((1, 6, D_MODEL), lambda i, d: (slot(i), 0, 0)),
                pl.BlockSpec((1, D_MODEL), lambda i, d: (0, 0)),
                pl.BlockSpec((1, 6, D_MODEL), lambda i, d: (slot(i), 0, 0)),
                pl.BlockSpec(memory_space=pl.ANY),
            ],
            out_specs=out_specs,
            scratch_shapes=[pltpu.VMEM((2, TOP_K, COMBINE_TILE * ROW_CHUNKS, LANES), F32),
                            pltpu.SemaphoreType.DMA((2, TOP_K))],
        ),
        out_shape=out_shape,
        compiler_params=_params("arbitrary"),
        name="moe_combine",
    )(where, route, x, mod, norm_g.reshape(1, D_MODEL), next_mod, ys)


def _moe(stream, h_rows, route, route_t, counts, wg, wu, wd, x, mod, norm_g, next_mod, last_layer):
    n = stream.n_tok
    where, fill, tile_expert, n_used = _dispatch_plan(route_t, counts, n)
    hs = _dispatch(n, where, fill, h_rows)
    ys = _expert_ffn(n, tile_expert, n_used, hs, wg, wu, wd)
    return _combine(stream, where, route, x, mod, norm_g, next_mod, ys, last_layer)


IN1_COL = 1024
IN1_ROW_TILE = 1024
QK_BLOCKS = 2 * RET_QK_WIDTH // IN1_COL
Q_BLOCKS = RET_QK_WIDTH // IN1_COL
V_BLOCKS = RET_V_WIDTH // IN1_COL


def _inproj1_qk_kernel(h_ref, w_ref, cos_ref, sin_ref, o_ref, *, rope):
    p = _dot(h_ref[...], w_ref[...])
    scale = jnp.where(pl.program_id(0) < Q_BLOCKS, 1.0, RET_DK ** -0.5)
    cos = cos_ref[...]
    sin = sin_ref[...]
    for hh in range(IN1_COL // RET_DK):
        ph = p[:, hh * RET_DK:(hh + 1) * RET_DK]
        if rope:
            ph = _rope(ph, cos, sin)
        o_ref[:, hh * RET_DK:(hh + 1) * RET_DK] = (ph * scale).astype(o_ref.dtype)


def _inproj1_vg_kernel(h_ref, w_ref, o_ref):
    p = _dot(h_ref[...], w_ref[...])
    o_ref[...] = jnp.where(pl.program_id(0) < V_BLOCKS, p, _silu(p)).astype(o_ref.dtype)


def _inproj1(stream, h, w_bf16, cos, sin):
    n = stream.n_tok
    tm = IN1_ROW_TILE
    tiles_per_seq = max(stream.seq_len // tm, 1)
    rope_map = (lambda j, i: (i % tiles_per_seq, 0)) if stream.rope else (lambda j, i: (0, 0))
    h_spec = pl.BlockSpec((tm, D_MODEL), lambda j, i: (i, 0))
    out_spec = pl.BlockSpec((tm, IN1_COL), lambda j, i: (i, j))
    qk = pl.pallas_call(
        functools.partial(_inproj1_qk_kernel, rope=stream.rope),
        grid=(QK_BLOCKS, n // tm),
        in_specs=[
            h_spec,
            pl.BlockSpec((D_MODEL, IN1_COL), lambda j, i: (0, j)),
            pl.BlockSpec((tm, RET_DK), rope_map),
            pl.BlockSpec((tm, RET_DK), rope_map),
        ],
        out_specs=out_spec,
        out_shape=jax.ShapeDtypeStruct((n, QK_BLOCKS * IN1_COL), BF16),
        compiler_params=_params("parallel", "parallel"),
        name="inproj1_qk",
    )(h, w_bf16, cos, sin)
    vg_blocks = IN1_WIDTH // IN1_COL - QK_BLOCKS
    vg = pl.pallas_call(
        _inproj1_vg_kernel,
        grid=(vg_blocks, n // tm),
        in_specs=[h_spec, pl.BlockSpec((D_MODEL, IN1_COL), lambda j, i: (0, QK_BLOCKS + j))],
        out_specs=out_spec,
        out_shape=jax.ShapeDtypeStruct((n, vg_blocks * IN1_COL), BF16),
        compiler_params=_params("parallel", "parallel"),
        name="inproj1_vg",
    )(h, w_bf16)
    return qk, vg


def _retention_kernel(*refs, n_chunks, has_state, heads, unroll):
    if has_state:
        decay_ref, q_ref, k_ref, v_ref, gf_ref, gb_ref, sf0_ref, sb0_ref, o_ref, inc_ref, seen_ref = refs
    else:
        decay_ref, q_ref, k_ref, v_ref, gf_ref, gb_ref, o_ref, sf_ref, sb_ref, inc_ref, seen_ref = refs
    c = RET_CHUNK
    row = lax.broadcasted_iota(jnp.int32, (c, c), 0).astype(F32)
    col = lax.broadcasted_iota(jnp.int32, (c, c), 1).astype(F32)
    pos = lax.broadcasted_iota(jnp.int32, (c, 1), 0).astype(F32)

    def chunk(i):
        return pl.ds(pl.multiple_of(i * c, c), c)

    for g in range(heads):
        hd = pl.program_id(1) * heads + g
        qk_cols = slice(g * RET_DK, (g + 1) * RET_DK)
        v_cols = slice(g * RET_DV, (g + 1) * RET_DV)
        lg = [-jnp.exp(jnp.full((1, 1), decay_ref[d, hd], F32)) for d in range(2)]
        diff = [row - col, col - row]
        intra = [jnp.where(diff[d] >= 0, jnp.exp(lg[d] * jnp.maximum(diff[d], 0.0)), 0.0) for d in range(2)]
        q_dec = [jnp.exp(lg[0] * (pos + 1.0)), jnp.exp(lg[1] * (c - pos))]
        k_dec = [jnp.exp(lg[0] * ((c - 1.0) - pos)), jnp.exp(lg[1] * pos)]
        chunk_dec = [jnp.exp(lg[d] * float(c)) for d in range(2)]

        def increments(i, carry):
            rows = chunk(i)
            ki = k_ref[rows, qk_cols].astype(F32)
            vi = v_ref[rows, v_cols]
            for d in range(2):
                inc_ref[d, i] = _dot_tn((ki * k_dec[d]).astype(BF16), vi)
            return carry

        lax.fori_loop(0, n_chunks, increments, 0, unroll=unroll)

        finals = []
        for d in range(2):
            def scan(t, state):
                i = t if d == 0 else n_chunks - 1 - t
                seen_ref[d, i] = state.astype(BF16)
                return state * chunk_dec[d] + inc_ref[d, i]

            if has_state:
                state0 = (sf0_ref if d == 0 else sb0_ref)[0, g]
            else:
                state0 = jnp.zeros((RET_DK, RET_DV), F32)
            finals.append(lax.fori_loop(0, n_chunks, scan, state0))
        if not has_state:
            sf_ref[0, g] = finals[0]
            sb_ref[0, g] = finals[1]

        def outputs(i, carry):
            rows = chunk(i)
            qi = q_ref[rows, qk_cols]
            vi = v_ref[rows, v_cols]
            s = _dot_nt(qi, k_ref[rows, qk_cols])
            mixed = None
            for d, g_ref in enumerate((gf_ref, gb_ref)):
                y = _dot((s * intra[d]).astype(BF16), vi) + _dot(qi, seen_ref[d, i]) * q_dec[d]
                gated = g_ref[rows, v_cols].astype(F32) * _rms(y)
                mixed = gated if mixed is None else mixed + gated
            o_ref[rows, v_cols] = mixed.astype(o_ref.dtype)
            return carry

        lax.fori_loop(0, n_chunks, outputs, 0, unroll=unroll)


def _retention(stream, qk, vg, decays, states, heads):
    n = stream.n_tok
    t = stream.seq_len
    n_chunks = t // RET_CHUNK
    has_state = states is not None
    qk_w = heads * RET_DK
    v_w = heads * RET_DV
    k_blk = RET_QK_WIDTH // qk_w
    gf_blk = RET_V_WIDTH // v_w
    gb_blk = 2 * gf_blk
    in_specs = [
        pl.BlockSpec(memory_space=pltpu.SMEM),
        pl.BlockSpec((t, qk_w), lambda b, h: (b, h)),
        pl.BlockSpec((t, qk_w), lambda b, h: (b, k_blk + h)),
        pl.BlockSpec((t, v_w), lambda b, h: (b, h)),
        pl.BlockSpec((t, v_w), lambda b, h: (b, gf_blk + h)),
        pl.BlockSpec((t, v_w), lambda b, h: (b, gb_blk + h)),
    ]
    args = [decays, qk, qk, vg, vg, vg]
    state_spec = pl.BlockSpec((1, heads, RET_DK, RET_DV), lambda b, h: (b, h, 0, 0))
    out_specs = [pl.BlockSpec((t, v_w), lambda b, h: (b, h))]
    out_shape = [jax.ShapeDtypeStruct((n, RET_V_WIDTH), BF16)]
    if has_state:
        in_specs += [state_spec, state_spec]
        args += list(states)
    else:
        out_specs += [state_spec, state_spec]
        out_shape += [jax.ShapeDtypeStruct((stream.n_seq, RET_HEADS, RET_DK, RET_DV), F32)] * 2
    return pl.pallas_call(
        functools.partial(_retention_kernel, n_chunks=n_chunks, has_state=has_state, heads=heads,
                          unroll=min(n_chunks, 4)),
        grid=(stream.n_seq, RET_HEADS // heads),
        in_specs=in_specs,
        out_specs=out_specs,
        out_shape=out_shape,
        scratch_shapes=[pltpu.VMEM((2, n_chunks, RET_DK, RET_DV), F32),
                        pltpu.VMEM((2, n_chunks, RET_DK, RET_DV), BF16)],
        compiler_params=_params("parallel", "parallel"),
        name="retention",
    )(*args)


def _rope_tables(n_tokens, dim):
    t = jnp.arange(n_tokens)
    pos = jnp.stack([t // GRID_W, t % GRID_W]).astype(F32)
    n_freq = dim // 4
    freqs = ROPE_THETA ** (-jnp.arange(n_freq, dtype=F32) / n_freq)
    ang = pos[:, :, None] * freqs
    cos, sin = jnp.cos(ang), jnp.sin(ang)
    cos_t = jnp.concatenate([cos[0], cos[0], cos[1], cos[1]], axis=-1)
    sin_t = jnp.concatenate([-sin[0], sin[0], -sin[1], sin[1]], axis=-1)
    return cos_t, sin_t


def kernel(x_prompt, x_sample, cache_k0, cache_v0, state_ret_fwd1, state_ret_bwd1, c, c_ctx,
           norm_mix_g0, mod_w0, mod_b0, in_w0, conv_w0, conv_b0, conv_norm_g0, conv_norm_b0,
           q_norm_g0, k_norm_g0, out_w0, norm_ffn_g0,
           moe_grp_w0, moe_grp_b0, moe_rtr_w0, moe_rtr_b0, moe_w_gate0, moe_w_up0, moe_w_down0,
           norm_mix_g1, mod_w1, mod_b1, in_w1, ret_decay_fwd1, ret_decay_bwd1, out_w1, norm_ffn_g1,
           moe_grp_w1, moe_grp_b1, moe_rtr_w1, moe_rtr_b1, moe_w_gate1, moe_w_up1, moe_w_down1,
           final_norm_g):
    batch, seq, d = x_prompt.shape
    dec_batch, dec_seq, _ = x_sample.shape
    assert d == D_MODEL and 1 + dec_batch <= MOD_SLOTS
    ctx = _Stream(batch, seq, first_slot=0, rope=False)
    lat = _Stream(dec_batch, dec_seq, first_slot=1, rope=True)
    assert ctx.n_tok % ROW_TILE == 0 and lat.n_tok % ROW_TILE == 0 and ROW_CHUNKS == SUBLANES
    assert seq % CONV_TILE == 0 and dec_seq % ROW_TILE == 0 and ROW_TILE % seq == 0

    cond = jnp.concatenate([c_ctx[None, :], c, jnp.zeros((MOD_SLOTS - 1 - dec_batch, d), F32)], axis=0)
    mod0 = _modulation(cond, mod_w0, mod_b0)
    mod1 = _modulation(cond, mod_w1, mod_b1)
    cos, sin = _rope_tables(dec_seq, HEAD_DIM)
    decays = jnp.stack([ret_decay_fwd1, ret_decay_bwd1]).astype(F32)

    in_w0_b = in_w0.astype(BF16)
    out_w0_b = out_w0.astype(BF16)
    in_w1_b = in_w1.astype(BF16)
    out_w1_b = out_w1.astype(BF16)
    moe0 = (moe_w_gate0.astype(BF16), moe_w_up0.astype(BF16), moe_w_down0.astype(BF16))
    moe1 = (moe_w_gate1.astype(BF16), moe_w_up1.astype(BF16), moe_w_down1.astype(BF16))
    route0 = _router_params(moe_grp_w0, moe_grp_b0, moe_rtr_w0, moe_rtr_b0)
    route1 = _router_params(moe_grp_w1, moe_grp_b1, moe_rtr_w1, moe_rtr_b1)

    def run(stream, x, cache, states):
        x = x.reshape(stream.n_tok, d)
        kv_dtype = BF16 if stream.rope else F32
        a, q, k, v = _inproj0(stream, x, mod0, norm_mix_g0, in_w0_b, q_norm_g0, k_norm_g0, cos, sin, kv_dtype)
        a = _conv_branch(stream, a, conv_w0, conv_b0, conv_norm_g0, conv_norm_b0)
        tq = min(stream.seq_len, 128 if cache is not None else 256)
        o = _attention(stream, q, k, v, cache, tq)
        x, *routed = _outproj_route(stream, [a, o], out_w0_b, x, mod0, norm_ffn_g0, *route0)
        x, h = _moe(stream, *routed, *moe0, x, mod0, norm_mix_g1, mod1, last_layer=False)
        qk, vg = _inproj1(stream, h, in_w1_b, cos, sin)
        ret = _retention(stream, qk, vg, decays, states, heads=2 if states is not None else 4)
        x, *routed = _outproj_route(stream, [ret[0]], out_w1_b, x, mod1, norm_ffn_g1, *route1)
        y, = _moe(stream, *routed, *moe1, x, mod1, final_norm_g, mod1, last_layer=True)
        return y.reshape(stream.n_seq, stream.seq_len, d), k, v, ret[1:]

    y_prompt, k_ctx, v_ctx, new_states = run(ctx, x_prompt, None, None)
    y_sample, _, _, _ = run(lat, x_sample, (cache_k0, cache_v0), (state_ret_fwd1, state_ret_bwd1))
    return (y_prompt, y_sample, k_ctx, v_ctx, new_states[0], new_states[1])
```

```python
import functools

import jax
import jax.numpy as jnp
from jax import lax
from jax.experimental import pallas as pl
from jax.experimental.pallas import tpu as pltpu

D_MODEL = 1024
GRID_W = 64
EPS = 1e-6
CONV_CH = 512
CONV_WIDTH = 31
CONV_HALF = CONV_WIDTH // 2
N_Q_HEADS = 8
N_KV_HEADS = 2
Q_PER_KV = N_Q_HEADS // N_KV_HEADS
HEAD_DIM = 128
ROPE_THETA = 10000.0
ATTN_WIDTH = N_Q_HEADS * HEAD_DIM
KV_WIDTH = N_KV_HEADS * HEAD_DIM
IN0_WIDTH = 2 * CONV_CH + ATTN_WIDTH + 2 * KV_WIDTH
RET_HEADS = 8
RET_DK = 128
RET_DV = 256
RET_CHUNK = 128
RET_QK_WIDTH = RET_HEADS * RET_DK
RET_V_WIDTH = RET_HEADS * RET_DV
IN1_WIDTH = 2 * RET_QK_WIDTH + 3 * RET_V_WIDTH
N_GROUPS = 4
EXPERTS_PER_GROUP = 4
N_EXPERTS = N_GROUPS * EXPERTS_PER_GROUP
EXPERT_FF = 512

LANES = 128
SUBLANES = 8
VMEM_LIMIT_BYTES = 56 * 1024 * 1024

MOD_SLOTS = 8
ROW_TILE = 512
ROW_CHUNKS = D_MODEL // LANES
PAYLOAD_CHUNKS = 2 * ROW_CHUNKS
FFN_TILE = 256
ROUTE_GROUP, ROUTE_RANK = 0, 1
CONV_TILE = 256
CONV_HALO = 16
CONV_ROWS = 64
ROUTE_LANES = LANES
EXPERT_LANE0 = N_GROUPS
NEG_BIG = -1e30
LOG2_E = 1.4426950408889634
ATTN_KV_CHUNK = 512

F32 = jnp.float32
BF16 = jnp.bfloat16


def _params(*semantics):
    return pltpu.CompilerParams(dimension_semantics=semantics, vmem_limit_bytes=VMEM_LIMIT_BYTES)


def _sigmoid(x):
    return 1.0 / (1.0 + jnp.exp(-x))


def _silu(x):
    return x * _sigmoid(x)


def _rms(x):
    return x * lax.rsqrt(jnp.mean(x * x, axis=-1, keepdims=True) + EPS)


def _rope(x, cos, sin):
    lane = lax.broadcasted_iota(jnp.int32, x.shape, 1)
    take_upper = (lane % (HEAD_DIM // 2)) < (HEAD_DIM // 4)
    partner = jnp.where(take_upper, pltpu.roll(x, HEAD_DIM - HEAD_DIM // 4, 1), pltpu.roll(x, HEAD_DIM // 4, 1))
    return x * cos + partner * sin


def _dot(a, b):
    return jnp.dot(a, b, preferred_element_type=F32)


def _dot_nt(a, b):
    return lax.dot_general(a, b, (((1,), (1,)), ((), ())), preferred_element_type=F32)


def _dot_tn(a, b):
    return lax.dot_general(a, b, (((0,), (0,)), ((), ())), preferred_element_type=F32)


def _mod_kernel(c_ref, w_ref, b_ref, o_ref):
    c = c_ref[...]
    o_ref[...] = _dot(_silu(c).astype(BF16), w_ref[...].astype(BF16)) + b_ref[...]


def _modulation(cond, mod_w, mod_b):
    n_out = mod_w.shape[1]
    col = D_MODEL
    out = pl.pallas_call(
        _mod_kernel,
        grid=(n_out // col,),
        in_specs=[
            pl.BlockSpec((MOD_SLOTS, D_MODEL), lambda j: (0, 0)),
            pl.BlockSpec((D_MODEL, col), lambda j: (0, j)),
            pl.BlockSpec((1, col), lambda j: (0, j)),
        ],
        out_specs=pl.BlockSpec((MOD_SLOTS, col), lambda j: (0, j)),
        out_shape=jax.ShapeDtypeStruct((MOD_SLOTS, n_out), F32),
        compiler_params=_params("parallel"),
        name="modulation",
    )(cond, mod_w, mod_b.reshape(1, n_out))
    return out.reshape(MOD_SLOTS, 6, D_MODEL)


def _slot_map(stream):
    if stream.first_slot == 0:
        return lambda i: 0
    return lambda i: stream.first_slot + (i * ROW_TILE) // stream.seq_len


class _Stream:
    def __init__(self, n_seq, seq_len, first_slot, rope):
        self.n_seq = n_seq
        self.seq_len = seq_len
        self.first_slot = first_slot
        self.rope = rope
        self.n_tok = n_seq * seq_len


def _modulated_norm(x, g, shift, scale):
    return _rms(x) * g * (1.0 + scale) + shift


def _inproj0_kernel(x_ref, mod_ref, g_ref, w_ref, qg_ref, kg_ref, cos_ref, sin_ref,
                    a_ref, q_ref, k_ref, v_ref, *, rope):
    h = _modulated_norm(x_ref[...], g_ref[...], mod_ref[0, 0:1, :], mod_ref[0, 1:2, :])
    p = _dot(h.astype(BF16), w_ref[...])
    a_ref[...] = p[:, :CONV_CH] * _sigmoid(p[:, CONV_CH:2 * CONV_CH])
    q0 = 2 * CONV_CH
    k0 = q0 + ATTN_WIDTH
    v0 = k0 + KV_WIDTH
    cos = cos_ref[...]
    sin = sin_ref[...]
    q_scale = HEAD_DIM ** -0.5 * LOG2_E
    for hh in range(N_Q_HEADS):
        qh = _rms(p[:, q0 + hh * HEAD_DIM:q0 + (hh + 1) * HEAD_DIM]) * qg_ref[...]
        if rope:
            qh = _rope(qh, cos, sin)
        q_ref[:, hh * HEAD_DIM:(hh + 1) * HEAD_DIM] = (qh * q_scale).astype(q_ref.dtype)
    seqs, _, rows, _ = k_ref.shape
    for hh in range(N_KV_HEADS):
        kh = _rms(p[:, k0 + hh * HEAD_DIM:k0 + (hh + 1) * HEAD_DIM]) * kg_ref[...]
        if rope:
            kh = _rope(kh, cos, sin)
        vh = p[:, v0 + hh * HEAD_DIM:v0 + (hh + 1) * HEAD_DIM]
        for sq in range(seqs):
            k_ref[sq, hh] = kh[sq * rows:(sq + 1) * rows].astype(k_ref.dtype)
            v_ref[sq, hh] = vh[sq * rows:(sq + 1) * rows].astype(v_ref.dtype)


def _inproj0(stream, x, mod, norm_g, w_bf16, q_norm_g, k_norm_g, cos, sin, kv_dtype):
    n = stream.n_tok
    tiles_per_seq = max(stream.seq_len // ROW_TILE, 1)
    rope_map = (lambda i: (i % tiles_per_seq, 0)) if stream.rope else (lambda i: (0, 0))
    slot = _slot_map(stream)
    row = lambda i: (i, 0)
    fixed = lambda i: (0, 0)
    seqs_per_tile = max(ROW_TILE // stream.seq_len, 1)
    kv_spec = pl.BlockSpec((seqs_per_tile, N_KV_HEADS, ROW_TILE // seqs_per_tile, HEAD_DIM),
                           lambda i: (i // tiles_per_seq, 0, i % tiles_per_seq, 0))
    kv_shape = jax.ShapeDtypeStruct((stream.n_seq, N_KV_HEADS, stream.seq_len, HEAD_DIM), kv_dtype)
    return pl.pallas_call(
        functools.partial(_inproj0_kernel, rope=stream.rope),
        grid=(n // ROW_TILE,),
        in_specs=[
            pl.BlockSpec((ROW_TILE, D_MODEL), row),
            pl.BlockSpec((1, 6, D_MODEL), lambda i: (slot(i), 0, 0)),
            pl.BlockSpec((1, D_MODEL), fixed),
            pl.BlockSpec((D_MODEL, IN0_WIDTH), fixed),
            pl.BlockSpec((1, HEAD_DIM), fixed),
            pl.BlockSpec((1, HEAD_DIM), fixed),
            pl.BlockSpec((ROW_TILE, HEAD_DIM), rope_map),
            pl.BlockSpec((ROW_TILE, HEAD_DIM), rope_map),
        ],
        out_specs=[
            pl.BlockSpec((ROW_TILE, CONV_CH), row),
            pl.BlockSpec((ROW_TILE, ATTN_WIDTH), row),
            kv_spec,
            kv_spec,
        ],
        out_shape=[
            jax.ShapeDtypeStruct((n, CONV_CH), F32),
            jax.ShapeDtypeStruct((n, ATTN_WIDTH), BF16),
            kv_shape,
            kv_shape,
        ],
        compiler_params=_params("parallel"),
        name="inproj0",
    )(x, mod, norm_g.reshape(1, D_MODEL), w_bf16, q_norm_g.reshape(1, HEAD_DIM), k_norm_g.reshape(1, HEAD_DIM),
      cos, sin)


def _conv_kernel(prev_ref, main_ref, next_ref, w_ref, b_ref, g_ref, beta_ref, o_ref, pad_ref, acc_ref, *,
                 tiles_per_seq):
    i = pl.program_id(0)
    has_prev = (i % tiles_per_seq) != 0
    has_next = (i % tiles_per_seq) != (tiles_per_seq - 1)
    pad_ref[0:CONV_HALO, :] = jnp.where(has_prev, prev_ref[...], 0.0)
    pad_ref[CONV_HALO:CONV_HALO + CONV_TILE, :] = main_ref[...]
    pad_ref[CONV_HALO + CONV_TILE:, :] = jnp.where(has_next, next_ref[...], 0.0)
    first = CONV_HALO - CONV_HALF
    for c0 in range(0, CONV_CH, LANES):
        for r0 in range(0, CONV_TILE, CONV_ROWS):
            acc = jnp.zeros((CONV_ROWS, LANES), F32)
            for j in range(CONV_WIDTH):
                acc = acc + w_ref[j:j + 1, c0:c0 + LANES] * pad_ref[first + r0 + j:first + r0 + j + CONV_ROWS,
                                                                    c0:c0 + LANES]
            acc_ref[r0:r0 + CONV_ROWS, c0:c0 + LANES] = acc
    a = acc_ref[...] + b_ref[...]
    mu = jnp.mean(a, axis=-1, keepdims=True)
    d = a - mu
    var = jnp.mean(d * d, axis=-1, keepdims=True)
    y = d * lax.rsqrt(var + EPS) * g_ref[...] + beta_ref[...]
    o_ref[...] = _silu(y).astype(o_ref.dtype)


def _conv_branch(stream, a, conv_w, conv_b, norm_g, norm_b):
    n = stream.n_tok
    tiles_per_seq = stream.seq_len // CONV_TILE
    halo_per_tile = CONV_TILE // CONV_HALO
    n_halo = n // CONV_HALO
    fixed = lambda i: (0, 0)
    return pl.pallas_call(
        functools.partial(_conv_kernel, tiles_per_seq=tiles_per_seq),
        grid=(n // CONV_TILE,),
        in_specs=[
            pl.BlockSpec((CONV_HALO, CONV_CH), lambda i: (jnp.maximum(i * halo_per_tile - 1, 0), 0)),
            pl.BlockSpec((CONV_TILE, CONV_CH), lambda i: (i, 0)),
            pl.BlockSpec((CONV_HALO, CONV_CH), lambda i: (jnp.minimum((i + 1) * halo_per_tile, n_halo - 1), 0)),
            pl.BlockSpec((CONV_WIDTH, CONV_CH), fixed),
            pl.BlockSpec((1, CONV_CH), fixed),
            pl.BlockSpec((1, CONV_CH), fixed),
            pl.BlockSpec((1, CONV_CH), fixed),
        ],
        out_specs=pl.BlockSpec((CONV_TILE, CONV_CH), lambda i: (i, 0)),
        out_shape=jax.ShapeDtypeStruct((n, CONV_CH), BF16),
        scratch_shapes=[
            pltpu.VMEM((CONV_TILE + 2 * CONV_HALO, CONV_CH), F32),
            pltpu.VMEM((CONV_TILE, CONV_CH), F32),
        ],
        compiler_params=_params("parallel"),
        name="conv_branch",
    )(a, a, a, conv_w, conv_b.reshape(1, CONV_CH), norm_g.reshape(1, CONV_CH), norm_b.reshape(1, CONV_CH))


def _attn_kernel(*refs, tq, past):
    if past:
        q_ref, k_ref, v_ref, kc_ref, vc_ref, o_ref, ks_ref, vs_ref = refs
    else:
        q_ref, k_ref, v_ref, o_ref, ks_ref, vs_ref = refs
    n_keys = ks_ref.shape[0]

    @pl.when(pl.program_id(2) == 0)
    def _():
        lane = lax.broadcasted_iota(jnp.int32, (n_keys, HEAD_DIM), 1)
        vs_ref[:, HEAD_DIM:] = jnp.where(lane == 0, 1.0, 0.0).astype(BF16)
        if past:
            ks_ref[0:past, :] = kc_ref[0, 0].astype(BF16)
            vs_ref[0:past, 0:HEAD_DIM] = vc_ref[0, 0].astype(BF16)
        ks_ref[past:, :] = k_ref[0, 0].astype(BF16)
        vs_ref[past:, 0:HEAD_DIM] = v_ref[0, 0].astype(BF16)

    q = q_ref[...]
    q4 = jnp.concatenate([q[:, g * HEAD_DIM:(g + 1) * HEAD_DIM] for g in range(Q_PER_KV)], axis=0)
    rows = q4.shape[0]
    m = jnp.full((rows, 1), NEG_BIG, F32)
    acc = jnp.zeros((rows, 2 * HEAD_DIM), F32)
    for c0 in range(0, n_keys, ATTN_KV_CHUNK):
        c1 = min(c0 + ATTN_KV_CHUNK, n_keys)
        s = _dot_nt(q4, ks_ref[c0:c1, :])
        m_new = jnp.maximum(m, jnp.max(s, axis=-1, keepdims=True))
        p = jnp.exp2(s - m_new).astype(BF16)
        acc = jnp.exp2(m - m_new) * acc + _dot(p, vs_ref[c0:c1, :])
        m = m_new
    o = acc[:, 0:HEAD_DIM] * (1.0 / acc[:, HEAD_DIM:HEAD_DIM + 1])
    for g in range(Q_PER_KV):
        o_ref[:, g * HEAD_DIM:(g + 1) * HEAD_DIM] = o[g * tq:(g + 1) * tq].astype(o_ref.dtype)


def _attention(stream, q, k, v, cache, tq):
    n = stream.n_tok
    t = stream.seq_len
    q_tiles = t // tq
    group_w = Q_PER_KV * HEAD_DIM
    in_specs = [
        pl.BlockSpec((tq, group_w), lambda b, h, i: (b * q_tiles + i, h)),
        pl.BlockSpec((1, 1, t, HEAD_DIM), lambda b, h, i: (b, h, 0, 0)),
        pl.BlockSpec((1, 1, t, HEAD_DIM), lambda b, h, i: (b, h, 0, 0)),
    ]
    args = [q, k, v]
    past = 0
    if cache is not None:
        past = cache[0].shape[2]
        in_specs += [pl.BlockSpec((1, 1, past, HEAD_DIM), lambda b, h, i: (b, h, 0, 0))] * 2
        args += list(cache)
    return pl.pallas_call(
        functools.partial(_attn_kernel, tq=tq, past=past),
        grid=(stream.n_seq, N_KV_HEADS, q_tiles),
        in_specs=in_specs,
        out_specs=pl.BlockSpec((tq, group_w), lambda b, h, i: (b * q_tiles + i, h)),
        out_shape=jax.ShapeDtypeStruct((n, ATTN_WIDTH), BF16),
        scratch_shapes=[pltpu.VMEM((past + t, HEAD_DIM), BF16), pltpu.VMEM((past + t, 2 * HEAD_DIM), BF16)],
        compiler_params=_params("parallel", "parallel", "arbitrary"),
        name="attention",
    )(*args)


def _route(logits, running):
    rows = logits.shape[0]
    lane = lax.broadcasted_iota(jnp.int32, logits.shape, 1)
    lane_f = lane.astype(F32)
    far = float(ROUTE_LANES)
    is_group = lane < N_GROUPS
    gl = jnp.where(is_group, logits, NEG_BIG)
    gmax = jnp.max(gl, axis=-1, keepdims=True)
    gsum = jnp.sum(jnp.where(is_group, jnp.exp(gl - gmax), 0.0), axis=-1, keepdims=True)
    g_w = 1.0 / gsum
    gidx = jnp.min(jnp.where(gl == gmax, lane_f, far), axis=-1, keepdims=True)
    lo = EXPERT_LANE0 + EXPERTS_PER_GROUP * gidx
    in_group = (lane_f >= lo) & (lane_f < lo + EXPERTS_PER_GROUP)
    el = jnp.where(in_group, logits, NEG_BIG)
    v1 = jnp.max(el, axis=-1, keepdims=True)
    i1 = jnp.min(jnp.where(el == v1, lane_f, far), axis=-1, keepdims=True)
    el2 = jnp.where(lane_f == i1, NEG_BIG, el)
    v2 = jnp.max(el2, axis=-1, keepdims=True)
    i2 = jnp.min(jnp.where(el2 == v2, lane_f, far), axis=-1, keepdims=True)
    e2 = jnp.exp(v2 - v1)
    w1 = g_w / (1.0 + e2)
    w2 = w1 * e2
    in_my_group = lane_f == gidx
    chosen = in_my_group.astype(BF16)
    earlier = (lax.broadcasted_iota(jnp.int32, (rows, rows), 0)
               > lax.broadcasted_iota(jnp.int32, (rows, rows), 1)).astype(BF16)
    before = running + _dot(earlier, chosen)
    rank = jnp.sum(jnp.where(in_my_group, before, 0.0), axis=-1, keepdims=True)
    record = jnp.where(lane == ROUTE_GROUP, gidx, jnp.where(lane == ROUTE_RANK, rank, 0.0))
    weights = jnp.where(lane_f == i1 - lo, w1, 0.0) + jnp.where(lane_f == i2 - lo, w2, 0.0)
    return record, weights, running + jnp.sum(chosen.astype(F32), axis=0, keepdims=True)


def _outproj_kernel(*refs, widths):
    n_in = len(widths)
    in_refs = refs[:n_in]
    w_ref, x_ref, mod_ref, g_ref, rw_ref, rb_ref, x1_ref, pay_ref, route_t_ref, count_ref, run_ref = refs[n_in:]

    @pl.when(pl.program_id(0) == 0)
    def _():
        run_ref[...] = jnp.zeros_like(run_ref)

    acc = None
    off = 0
    for r, width in zip(in_refs, widths):
        part = _dot(r[...], w_ref[off:off + width, :])
        acc = part if acc is None else acc + part
        off += width
    x1 = x_ref[...] + mod_ref[0, 2:3, :] * acc
    x1_ref[...] = x1
    h = _modulated_norm(x1, g_ref[...], mod_ref[0, 3:4, :], mod_ref[0, 4:5, :])
    h_hi = h.astype(BF16)
    h_lo = (h - h_hi.astype(F32)).astype(BF16)
    both = _dot(h_hi, rw_ref[...])
    logits = (both[:, :ROUTE_LANES] + both[:, ROUTE_LANES:] + _dot(h_lo, rw_ref[:, :ROUTE_LANES]) + rb_ref[...])
    record, weights, running = _route(logits, run_ref[...])
    route_t_ref[...] = record.T[:SUBLANES]
    run_ref[...] = running
    count_ref[...] = jnp.broadcast_to(running, count_ref.shape)
    rows = h.shape[0]
    _store_row_major(pay_ref, h, PAYLOAD_CHUNKS)
    pay_ref[pl.ds(ROW_CHUNKS, rows, stride=PAYLOAD_CHUNKS), :] = weights
    for c in range(ROW_CHUNKS + 1, PAYLOAD_CHUNKS):
        pay_ref[pl.ds(c, rows, stride=PAYLOAD_CHUNKS), :] = jnp.zeros((rows, LANES), F32)


def _store_row_major(ref, x, pitch=ROW_CHUNKS):
    rows = x.shape[0]
    for c in range(ROW_CHUNKS):
        ref[pl.ds(c, rows, stride=pitch), :] = x[:, c * LANES:(c + 1) * LANES]


def _load_row_major(ref, rows, pitch=ROW_CHUNKS):
    return jnp.concatenate([ref[pl.ds(c, rows, stride=pitch), :] for c in range(ROW_CHUNKS)], axis=-1)


def _outproj_route(stream, inputs, w_bf16, x, mod, ffn_norm_g, route_w, route_b):
    n = stream.n_tok
    widths = tuple(a.shape[1] for a in inputs)
    slot = _slot_map(stream)
    row = lambda i: (i, 0)
    fixed = lambda i: (0, 0)
    return pl.pallas_call(
        functools.partial(_outproj_kernel, widths=widths),
        grid=(n // ROW_TILE,),
        in_specs=[pl.BlockSpec((ROW_TILE, width), row) for width in widths] + [
            pl.BlockSpec((sum(widths), D_MODEL), fixed),
            pl.BlockSpec((ROW_TILE, D_MODEL), row),
            pl.BlockSpec((1, 6, D_MODEL), lambda i: (slot(i), 0, 0)),
            pl.BlockSpec((1, D_MODEL), fixed),
            pl.BlockSpec((D_MODEL, 2 * ROUTE_LANES), fixed),
            pl.BlockSpec((1, ROUTE_LANES), fixed),
        ],
        out_specs=[
            pl.BlockSpec((ROW_TILE, D_MODEL), row),
            pl.BlockSpec((ROW_TILE * PAYLOAD_CHUNKS, LANES), row),
            pl.BlockSpec((SUBLANES, ROW_TILE), lambda i: (0, i)),
            pl.BlockSpec((SUBLANES, ROUTE_LANES), fixed),
        ],
        out_shape=[
            jax.ShapeDtypeStruct((n, D_MODEL), F32),
            jax.ShapeDtypeStruct((n * PAYLOAD_CHUNKS, LANES), F32),
            jax.ShapeDtypeStruct((SUBLANES, n), F32),
            jax.ShapeDtypeStruct((SUBLANES, ROUTE_LANES), F32),
        ],
        scratch_shapes=[pltpu.VMEM((1, ROUTE_LANES), F32)],
        compiler_params=_params("arbitrary"),
        name="outproj_route",
    )(*inputs, w_bf16, x, mod, ffn_norm_g.reshape(1, D_MODEL), route_w, route_b)


def _router_params(grp_w, grp_b, rtr_w, rtr_b):
    w = jnp.concatenate([grp_w, jnp.moveaxis(rtr_w, 0, 1).reshape(D_MODEL, N_EXPERTS)], axis=1)
    w = jnp.pad(w, ((0, 0), (0, ROUTE_LANES - w.shape[1])))
    hi = w.astype(BF16)
    lo = (w - hi.astype(F32)).astype(BF16)
    b = jnp.concatenate([grp_b, rtr_b.reshape(N_EXPERTS)])
    b = jnp.pad(b, (0, ROUTE_LANES - b.shape[0])).reshape(1, ROUTE_LANES)
    return jnp.concatenate([hi, lo], axis=1), b


def _sorted_tiles(n_tok):
    return n_tok // FFN_TILE + N_GROUPS


def _dispatch_plan(route_t, counts, n_tok):
    n_tiles = _sorted_tiles(n_tok)
    count = counts[0, :N_GROUPS].astype(jnp.int32)
    tiles = (count + (FFN_TILE - 1)) // FFN_TILE
    start = (jnp.cumsum(tiles) - tiles) * FFN_TILE
    tile_end = jnp.cumsum(tiles)
    j = jnp.arange(n_tiles, dtype=jnp.int32)
    tile_group = jnp.minimum(jnp.sum(j[:, None] >= tile_end[None, :], axis=1), N_GROUPS - 1).astype(jnp.int32)
    group_end = jnp.take(start + count, tile_group)
    valid = jnp.where(j < tile_end[-1], jnp.clip(group_end - j * FFN_TILE, 0, FFN_TILE), 0).astype(jnp.int32)
    ids = route_t[:2].astype(jnp.int32)
    row = jnp.take(start, ids[ROUTE_GROUP]) + ids[ROUTE_RANK]
    src = jnp.zeros((n_tiles * FFN_TILE,), jnp.int32).at[row].set(jnp.arange(n_tok, dtype=jnp.int32))
    return tile_group, valid, src


def _token_rows(t, pitch=ROW_CHUNKS):
    return pl.ds(pl.multiple_of(t * pitch, pitch), pitch)


def _ffn_kernel(tg_ref, valid_ref, src_ref, pay_ref, wg_ref, wu_ref, wd_ref, y_ref, x_buf, y_buf, g_sems, s_sems):
    j = pl.program_id(0)
    n_tiles = pl.num_programs(0)

    def gather_copy(tile, r):
        tok = src_ref[tile * FFN_TILE + r]
        return pltpu.make_async_copy(pay_ref.at[_token_rows(tok, PAYLOAD_CHUNKS)],
                                     x_buf.at[tile % 2, _token_rows(r, PAYLOAD_CHUNKS)], g_sems.at[tile % 2])

    def scatter_copy(tile, r):
        tok = src_ref[tile * FFN_TILE + r]
        return pltpu.make_async_copy(y_buf.at[tile % 2, _token_rows(r)], y_ref.at[_token_rows(tok)],
                                     s_sems.at[tile % 2])

    def start_rows(copy, tile, priority):
        def issue(r, carry):
            copy(tile, r).start(priority=priority)
            return carry

        @pl.when(valid_ref[tile] == FFN_TILE)
        def _():
            lax.fori_loop(0, FFN_TILE, issue, 0, unroll=8)

        @pl.when(valid_ref[tile] < FFN_TILE)
        def _():
            lax.fori_loop(0, valid_ref[tile], issue, 0)

    def wait_rows(src, dst, sem, tile, pitch):
        n = pl.multiple_of(valid_ref[tile] * pitch, pitch)
        pltpu.make_async_copy(src.at[pl.ds(0, n)], dst.at[pl.ds(0, n)], sem).wait()

    @pl.when(j == 0)
    def _():
        x_buf[...] = jnp.zeros_like(x_buf)
        start_rows(gather_copy, j, 0)

    @pl.when(j + 1 < n_tiles)
    def _():
        @pl.when(valid_ref[j + 1] > 0)
        def _():
            start_rows(gather_copy, j + 1, 0)

    @pl.when(valid_ref[j] > 0)
    def _():
        slot = j % 2
        wait_rows(pay_ref, x_buf.at[slot], g_sems.at[slot], j, PAYLOAD_CHUNKS)

        @pl.when(j >= 2)
        def _():
            wait_rows(y_buf.at[slot], y_ref, s_sems.at[slot], j - 2, ROW_CHUNKS)

        x = _load_row_major(x_buf.at[slot], FFN_TILE, PAYLOAD_CHUNKS).astype(BF16)
        w = x_buf[slot, pl.ds(ROW_CHUNKS, FFN_TILE, stride=PAYLOAD_CHUNKS), :]
        y = None
        for e in range(EXPERTS_PER_GROUP):
            hidden = _silu(_dot(x, wg_ref[e])) * _dot(x, wu_ref[e]) * w[:, e:e + 1]
            part = _dot(hidden.astype(BF16), wd_ref[e])
            y = part if y is None else y + part
        _store_row_major(y_buf.at[slot], y)
        start_rows(scatter_copy, j, 1)

        last = (j + 1 == n_tiles) | (valid_ref[jnp.minimum(j + 1, n_tiles - 1)] == 0)

        @pl.when(last)
        def _():
            @pl.when(j >= 1)
            def _():
                wait_rows(y_buf.at[1 - slot], y_ref, s_sems.at[1 - slot], j - 1, ROW_CHUNKS)

            wait_rows(y_buf.at[slot], y_ref, s_sems.at[slot], j, ROW_CHUNKS)


def _expert_ffn(n_tok, tile_group, valid, src, payload, wg, wu, wd):
    group = lambda j, tg, va, sr: (tg[j], 0, 0)
    tile_rows = FFN_TILE * ROW_CHUNKS
    return pl.pallas_call(
        _ffn_kernel,
        grid_spec=pltpu.PrefetchScalarGridSpec(
            num_scalar_prefetch=3,
            grid=(_sorted_tiles(n_tok),),
            in_specs=[
                pl.BlockSpec(memory_space=pl.ANY),
                pl.BlockSpec((EXPERTS_PER_GROUP, D_MODEL, EXPERT_FF), group),
                pl.BlockSpec((EXPERTS_PER_GROUP, D_MODEL, EXPERT_FF), group),
                pl.BlockSpec((EXPERTS_PER_GROUP, EXPERT_FF, D_MODEL), group),
            ],
            out_specs=pl.BlockSpec(memory_space=pl.ANY),
            scratch_shapes=[pltpu.VMEM((2, FFN_TILE * PAYLOAD_CHUNKS, LANES), F32),
                            pltpu.VMEM((2, tile_rows, LANES), F32),
                            pltpu.SemaphoreType.DMA((2,)), pltpu.SemaphoreType.DMA((2,))],
        ),
        out_shape=jax.ShapeDtypeStruct((n_tok * ROW_CHUNKS, LANES), F32),
        compiler_params=_params("arbitrary"),
        name="moe_ffn",
    )(tile_group, valid, src, payload, wg, wu, wd)


def _combine_kernel(y_ref, x_ref, mod_ref, ng_ref, nmod_ref, o_ref, *rest, last_layer):
    x = x_ref[...] + mod_ref[0, 5:6, :] * _load_row_major(y_ref, x_ref.shape[0])
    if last_layer:
        o_ref[...] = _rms(x) * ng_ref[...]
    else:
        h_ref, = rest
        o_ref[...] = x
        h_ref[...] = _modulated_norm(x, ng_ref[...], nmod_ref[0, 0:1, :], nmod_ref[0, 1:2, :]).astype(h_ref.dtype)


def _combine(stream, y_rows, x, mod, norm_g, next_mod, last_layer):
    n = stream.n_tok
    slot = _slot_map(stream)
    row = lambda i: (i, 0)
    fixed = lambda i: (0, 0)
    out_specs = [pl.BlockSpec((ROW_TILE, D_MODEL), row)]
    out_shape = [jax.ShapeDtypeStruct((n, D_MODEL), F32)]
    if not last_layer:
        out_specs.append(pl.BlockSpec((ROW_TILE, D_MODEL), row))
        out_shape.append(jax.ShapeDtypeStruct((n, D_MODEL), BF16))
    return pl.pallas_call(
        functools.partial(_combine_kernel, last_layer=last_layer),
        grid=(n // ROW_TILE,),
        in_specs=[
            pl.BlockSpec((ROW_TILE * ROW_CHUNKS, LANES), row),
            pl.BlockSpec((ROW_TILE, D_MODEL), row),
            pl.BlockSpec((1, 6, D_MODEL), lambda i: (slot(i), 0, 0)),
            pl.BlockSpec((1, D_MODEL), fixed),
            pl.BlockSpec((1, 6, D_MODEL), lambda i: (slot(i), 0, 0)),
        ],
        out_specs=out_specs,
        out_shape=out_shape,
        compiler_params=_params("parallel"),
        name="moe_combine",
    )(y_rows, x, mod, norm_g.reshape(1, D_MODEL), next_mod)


def _moe(stream, payload, route_t, counts, wg, wu, wd, x, mod, norm_g, next_mod, last_layer):
    n = stream.n_tok
    tile_group, valid, src = _dispatch_plan(route_t, counts, n)
    y_rows = _expert_ffn(n, tile_group, valid, src, payload, wg, wu, wd)
    return _combine(stream, y_rows, x, mod, norm_g, next_mod, last_layer)


IN1_COL = 1024
IN1_ROW_TILE = 1024
QK_BLOCKS = 2 * RET_QK_WIDTH // IN1_COL
Q_BLOCKS = RET_QK_WIDTH // IN1_COL
V_BLOCKS = RET_V_WIDTH // IN1_COL


def _inproj1_qk_kernel(h_ref, w_ref, cos_ref, sin_ref, o_ref, *, rope):
    p = _dot(h_ref[...], w_ref[...])
    scale = jnp.where(pl.program_id(0) < Q_BLOCKS, 1.0, RET_DK ** -0.5)
    cos = cos_ref[...]
    sin = sin_ref[...]
    for hh in range(IN1_COL // RET_DK):
        ph = p[:, hh * RET_DK:(hh + 1) * RET_DK]
        if rope:
            ph = _rope(ph, cos, sin)
        o_ref[:, hh * RET_DK:(hh + 1) * RET_DK] = (ph * scale).astype(o_ref.dtype)


def _inproj1_vg_kernel(h_ref, w_ref, o_ref):
    p = _dot(h_ref[...], w_ref[...])
    o_ref[...] = jnp.where(pl.program_id(0) < V_BLOCKS, p, _silu(p)).astype(o_ref.dtype)


def _inproj1(stream, h, w_bf16, cos, sin):
    n = stream.n_tok
    tm = IN1_ROW_TILE
    tiles_per_seq = max(stream.seq_len // tm, 1)
    rope_map = (lambda j, i: (i % tiles_per_seq, 0)) if stream.rope else (lambda j, i: (0, 0))
    h_spec = pl.BlockSpec((tm, D_MODEL), lambda j, i: (i, 0))
    out_spec = pl.BlockSpec((tm, IN1_COL), lambda j, i: (i, j))
    qk = pl.pallas_call(
        functools.partial(_inproj1_qk_kernel, rope=stream.rope),
        grid=(QK_BLOCKS, n // tm),
        in_specs=[
            h_spec,
            pl.BlockSpec((D_MODEL, IN1_COL), lambda j, i: (0, j)),
            pl.BlockSpec((tm, RET_DK), rope_map),
            pl.BlockSpec((tm, RET_DK), rope_map),
        ],
        out_specs=out_spec,
        out_shape=jax.ShapeDtypeStruct((n, QK_BLOCKS * IN1_COL), BF16),
        compiler_params=_params("parallel", "parallel"),
        name="inproj1_qk",
    )(h, w_bf16, cos, sin)
    vg_blocks = IN1_WIDTH // IN1_COL - QK_BLOCKS
    vg = pl.pallas_call(
        _inproj1_vg_kernel,
        grid=(vg_blocks, n // tm),
        in_specs=[h_spec, pl.BlockSpec((D_MODEL, IN1_COL), lambda j, i: (0, QK_BLOCKS + j))],
        out_specs=out_spec,
        out_shape=jax.ShapeDtypeStruct((n, vg_blocks * IN1_COL), BF16),
        compiler_params=_params("parallel", "parallel"),
        name="inproj1_vg",
    )(h, w_bf16)
    return qk, vg


def _retention_kernel(*refs, n_chunks, has_state, heads, unroll):
    if has_state:
        decay_ref, q_ref, k_ref, v_ref, gf_ref, gb_ref, sf0_ref, sb0_ref, o_ref, inc_ref, seen_ref = refs
    else:
        decay_ref, q_ref, k_ref, v_ref, gf_ref, gb_ref, o_ref, sf_ref, sb_ref, inc_ref, seen_ref = refs
    c = RET_CHUNK
    row = lax.broadcasted_iota(jnp.int32, (c, c), 0).astype(F32)
    col = lax.broadcasted_iota(jnp.int32, (c, c), 1).astype(F32)
    pos = lax.broadcasted_iota(jnp.int32, (c, 1), 0).astype(F32)

    def chunk(i):
        return pl.ds(pl.multiple_of(i * c, c), c)

    for g in range(heads):
        hd = pl.program_id(1) * heads + g
        qk_cols = slice(g * RET_DK, (g + 1) * RET_DK)
        v_cols = slice(g * RET_DV, (g + 1) * RET_DV)
        lg = [-jnp.exp(jnp.full((1, 1), decay_ref[d, hd], F32)) for d in range(2)]
        diff = [row - col, col - row]
        intra = [jnp.where(diff[d] >= 0, jnp.exp(lg[d] * jnp.maximum(diff[d], 0.0)), 0.0) for d in range(2)]
        q_dec = [jnp.exp(lg[0] * (pos + 1.0)), jnp.exp(lg[1] * (c - pos))]
        k_dec = [jnp.exp(lg[0] * ((c - 1.0) - pos)), jnp.exp(lg[1] * pos)]
        chunk_dec = [jnp.exp(lg[d] * float(c)) for d in range(2)]

        def increments(i, carry):
            rows = chunk(i)
            ki = k_ref[rows, qk_cols].astype(F32)
            vi = v_ref[rows, v_cols]
            for d in range(2):
                inc_ref[d, i] = _dot_tn((ki * k_dec[d]).astype(BF16), vi)
            return carry

        lax.fori_loop(0, n_chunks, increments, 0, unroll=unroll)

        finals = []
        for d in range(2):
            def scan(t, state):
                i = t if d == 0 else n_chunks - 1 - t
                seen_ref[d, i] = state.astype(BF16)
                return state * chunk_dec[d] + inc_ref[d, i]

            if has_state:
                state0 = (sf0_ref if d == 0 else sb0_ref)[0, g]
            else:
                state0 = jnp.zeros((RET_DK, RET_DV), F32)
            finals.append(lax.fori_loop(0, n_chunks, scan, state0))
        if not has_state:
            sf_ref[0, g] = finals[0]
            sb_ref[0, g] = finals[1]

        def outputs(i, carry):
            rows = chunk(i)
            qi = q_ref[rows, qk_cols]
            vi = v_ref[rows, v_cols]
            s = _dot_nt(qi, k_ref[rows, qk_cols])
            mixed = None
            for d, g_ref in enumerate((gf_ref, gb_ref)):
                y = _dot((s * intra[d]).astype(BF16), vi) + _dot(qi, seen_ref[d, i]) * q_dec[d]
                gated = g_ref[rows, v_cols].astype(F32) * _rms(y)
                mixed = gated if mixed is None else mixed + gated
            o_ref[rows, v_cols] = mixed.astype(o_ref.dtype)
            return carry

        lax.fori_loop(0, n_chunks, outputs, 0, unroll=unroll)


def _retention(stream, qk, vg, decays, states, heads):
    n = stream.n_tok
    t = stream.seq_len
    n_chunks = t // RET_CHUNK
    has_state = states is not None
    qk_w = heads * RET_DK
    v_w = heads * RET_DV
    k_blk = RET_QK_WIDTH // qk_w
    gf_blk = RET_V_WIDTH // v_w
    gb_blk = 2 * gf_blk
    in_specs = [
        pl.BlockSpec(memory_space=pltpu.SMEM),
        pl.BlockSpec((t, qk_w), lambda b, h: (b, h)),
        pl.BlockSpec((t, qk_w), lambda b, h: (b, k_blk + h)),
        pl.BlockSpec((t, v_w), lambda b, h: (b, h)),
        pl.BlockSpec((t, v_w), lambda b, h: (b, gf_blk + h)),
        pl.BlockSpec((t, v_w), lambda b, h: (b, gb_blk + h)),
    ]
    args = [decays, qk, qk, vg, vg, vg]
    state_spec = pl.BlockSpec((1, heads, RET_DK, RET_DV), lambda b, h: (b, h, 0, 0))
    out_specs = [pl.BlockSpec((t, v_w), lambda b, h: (b, h))]
    out_shape = [jax.ShapeDtypeStruct((n, RET_V_WIDTH), BF16)]
    if has_state:
        in_specs += [state_spec, state_spec]
        args += list(states)
    else:
        out_specs += [state_spec, state_spec]
        out_shape += [jax.ShapeDtypeStruct((stream.n_seq, RET_HEADS, RET_DK, RET_DV), F32)] * 2
    return pl.pallas_call(
        functools.partial(_retention_kernel, n_chunks=n_chunks, has_state=has_state, heads=heads,
                          unroll=min(n_chunks, 4)),
        grid=(stream.n_seq, RET_HEADS // heads),
        in_specs=in_specs,
        out_specs=out_specs,
        out_shape=out_shape,
        scratch_shapes=[pltpu.VMEM((2, n_chunks, RET_DK, RET_DV), F32),
                        pltpu.VMEM((2, n_chunks, RET_DK, RET_DV), BF16)],
        compiler_params=_params("parallel", "parallel"),
        name="retention",
    )(*args)


def _rope_tables(n_tokens, dim):
    t = jnp.arange(n_tokens)
    pos = jnp.stack([t // GRID_W, t % GRID_W]).astype(F32)
    n_freq = dim // 4
    freqs = ROPE_THETA ** (-jnp.arange(n_freq, dtype=F32) / n_freq)
    ang = pos[:, :, None] * freqs
    cos, sin = jnp.cos(ang), jnp.sin(ang)
    cos_t = jnp.concatenate([cos[0], cos[0], cos[1], cos[1]], axis=-1)
    sin_t = jnp.concatenate([-sin[0], sin[0], -sin[1], sin[1]], axis=-1)
    return cos_t, sin_t


def kernel(x_prompt, x_sample, cache_k0, cache_v0, state_ret_fwd1, state_ret_bwd1, c, c_ctx,
           norm_mix_g0, mod_w0, mod_b0, in_w0, conv_w0, conv_b0, conv_norm_g0, conv_norm_b0,
           q_norm_g0, k_norm_g0, out_w0, norm_ffn_g0,
           moe_grp_w0, moe_grp_b0, moe_rtr_w0, moe_rtr_b0, moe_w_gate0, moe_w_up0, moe_w_down0,
           norm_mix_g1, mod_w1, mod_b1, in_w1, ret_decay_fwd1, ret_decay_bwd1, out_w1, norm_ffn_g1,
           moe_grp_w1, moe_grp_b1, moe_rtr_w1, moe_rtr_b1, moe_w_gate1, moe_w_up1, moe_w_down1,
           final_norm_g):
    batch, seq, d = x_prompt.shape
    dec_batch, dec_seq, _ = x_sample.shape
    assert d == D_MODEL and 1 + dec_batch <= MOD_SLOTS
    ctx = _Stream(batch, seq, first_slot=0, rope=False)
    lat = _Stream(dec_batch, dec_seq, first_slot=1, rope=True)
    assert ctx.n_tok % ROW_TILE == 0 and lat.n_tok % ROW_TILE == 0 and ROW_CHUNKS == SUBLANES
    assert seq % CONV_TILE == 0 and dec_seq % ROW_TILE == 0 and ROW_TILE % seq == 0

    cond = jnp.concatenate([c_ctx[None, :], c, jnp.zeros((MOD_SLOTS - 1 - dec_batch, d), F32)], axis=0)
    mod0 = _modulation(cond, mod_w0, mod_b0)
    mod1 = _modulation(cond, mod_w1, mod_b1)
    cos, sin = _rope_tables(dec_seq, HEAD_DIM)
    decays = jnp.stack([ret_decay_fwd1, ret_decay_bwd1]).astype(F32)

    in_w0_b = in_w0.astype(BF16)
    out_w0_b = out_w0.astype(BF16)
    in_w1_b = in_w1.astype(BF16)
    out_w1_b = out_w1.astype(BF16)
    moe0 = (moe_w_gate0.astype(BF16), moe_w_up0.astype(BF16), moe_w_down0.astype(BF16))
    moe1 = (moe_w_gate1.astype(BF16), moe_w_up1.astype(BF16), moe_w_down1.astype(BF16))
    route0 = _router_params(moe_grp_w0, moe_grp_b0, moe_rtr_w0, moe_rtr_b0)
    route1 = _router_params(moe_grp_w1, moe_grp_b1, moe_rtr_w1, moe_rtr_b1)

    def run(stream, x, cache, states):
        x = x.reshape(stream.n_tok, d)
        kv_dtype = BF16 if stream.rope else F32
        a, q, k, v = _inproj0(stream, x, mod0, norm_mix_g0, in_w0_b, q_norm_g0, k_norm_g0, cos, sin, kv_dtype)
        a = _conv_branch(stream, a, conv_w0, conv_b0, conv_norm_g0, conv_norm_b0)
        tq = min(stream.seq_len, 128 if cache is not None else 256)
        o = _attention(stream, q, k, v, cache, tq)
        x, *routed = _outproj_route(stream, [a, o], out_w0_b, x, mod0, norm_ffn_g0, *route0)
        x, h = _moe(stream, *routed, *moe0, x, mod0, norm_mix_g1, mod1, last_layer=False)
        qk, vg = _inproj1(stream, h, in_w1_b, cos, sin)
        ret = _retention(stream, qk, vg, decays, states, heads=2 if states is not None else 4)
        x, *routed = _outproj_route(stream, [ret[0]], out_w1_b, x, mod1, norm_ffn_g1, *route1)
        y, = _moe(stream, *routed, *moe1, x, mod1, final_norm_g, mod1, last_layer=True)
        return y.reshape(stream.n_seq, stream.seq_len, d), k, v, ret[1:]

    y_prompt, k_ctx, v_ctx, new_states = run(ctx, x_prompt, None, None)
    y_sample, _, _, _ = run(lat, x_sample, (cache_k0, cache_v0), (state_ret_fwd1, state_ret_bwd1))
    return (y_prompt, y_sample, k_ctx, v_ctx, new_states[0], new_states[1])
```

```python
import functools

import jax
import jax.numpy as jnp
from jax import lax
from jax.experimental import pallas as pl
from jax.experimental.pallas import tpu as pltpu

D_MODEL = 1024
GRID_W = 64
EPS = 1e-6
CONV_CH = 512
CONV_WIDTH = 31
CONV_HALF = CONV_WIDTH // 2
N_Q_HEADS = 8
N_KV_HEADS = 2
Q_PER_KV = N_Q_HEADS // N_KV_HEADS
HEAD_DIM = 128
ROPE_THETA = 10000.0
ATTN_WIDTH = N_Q_HEADS * HEAD_DIM
KV_WIDTH = N_KV_HEADS * HEAD_DIM
IN0_WIDTH = 2 * CONV_CH + ATTN_WIDTH + 2 * KV_WIDTH
RET_HEADS = 8
RET_DK = 128
RET_DV = 256
RET_CHUNK = 128
RET_QK_WIDTH = RET_HEADS * RET_DK
RET_V_WIDTH = RET_HEADS * RET_DV
IN1_WIDTH = 2 * RET_QK_WIDTH + 3 * RET_V_WIDTH
N_GROUPS = 4
EXPERTS_PER_GROUP = 4
N_EXPERTS = N_GROUPS * EXPERTS_PER_GROUP
EXPERT_FF = 512

LANES = 128
SUBLANES = 8
VMEM_LIMIT_BYTES = 56 * 1024 * 1024

MOD_SLOTS = 8
ROW_TILE = 512
ROW_CHUNKS = D_MODEL // LANES
PAYLOAD_CHUNKS = 2 * ROW_CHUNKS
FFN_TILE = 256
ROUTE_GROUP, ROUTE_RANK = 0, 1
CONV_TILE = 256
CONV_HALO = 16
CONV_ROWS = 64
ROUTE_LANES = LANES
EXPERT_LANE0 = N_GROUPS
NEG_BIG = -1e30
LOG2_E = 1.4426950408889634
ATTN_KV_CHUNK = 512

F32 = jnp.float32
BF16 = jnp.bfloat16


def _params(*semantics):
    return pltpu.CompilerParams(dimension_semantics=semantics, vmem_limit_bytes=VMEM_LIMIT_BYTES)


def _sigmoid(x):
    return 1.0 / (1.0 + jnp.exp(-x))


def _silu(x):
    return x * _sigmoid(x)


def _rms(x):
    return x * lax.rsqrt(jnp.mean(x * x, axis=-1, keepdims=True) + EPS)


def _rope(x, cos, sin):
    lane = lax.broadcasted_iota(jnp.int32, x.shape, 1)
    take_upper = (lane % (HEAD_DIM // 2)) < (HEAD_DIM // 4)
    partner = jnp.where(take_upper, pltpu.roll(x, HEAD_DIM - HEAD_DIM // 4, 1), pltpu.roll(x, HEAD_DIM // 4, 1))
    return x * cos + partner * sin


def _dot(a, b):
    return jnp.dot(a, b, preferred_element_type=F32)


def _dot_nt(a, b):
    return lax.dot_general(a, b, (((1,), (1,)), ((), ())), preferred_element_type=F32)


def _dot_tn(a, b):
    return lax.dot_general(a, b, (((0,), (0,)), ((), ())), preferred_element_type=F32)


def _mod_kernel(c_ref, w_ref, b_ref, o_ref):
    c = c_ref[...]
    o_ref[...] = _dot(_silu(c).astype(BF16), w_ref[...].astype(BF16)) + b_ref[...]


def _modulation(cond, mod_w, mod_b):
    n_out = mod_w.shape[1]
    col = D_MODEL
    out = pl.pallas_call(
        _mod_kernel,
        grid=(n_out // col,),
        in_specs=[
            pl.BlockSpec((MOD_SLOTS, D_MODEL), lambda j: (0, 0)),
            pl.BlockSpec((D_MODEL, col), lambda j: (0, j)),
            pl.BlockSpec((1, col), lambda j: (0, j)),
        ],
        out_specs=pl.BlockSpec((MOD_SLOTS, col), lambda j: (0, j)),
        out_shape=jax.ShapeDtypeStruct((MOD_SLOTS, n_out), F32),
        compiler_params=_params("parallel"),
        name="modulation",
    )(cond, mod_w, mod_b.reshape(1, n_out))
    return out.reshape(MOD_SLOTS, 6, D_MODEL)


def _slot_map(stream):
    if stream.first_slot == 0:
        return lambda i: 0
    return lambda i: stream.first_slot + (i * ROW_TILE) // stream.seq_len


class _Stream:
    def __init__(self, n_seq, seq_len, first_slot, rope):
        self.n_seq = n_seq
        self.seq_len = seq_len
        self.first_slot = first_slot
        self.rope = rope
        self.n_tok = n_seq * seq_len


def _modulated_norm(x, g, shift, scale):
    return _rms(x) * g * (1.0 + scale) + shift


def _inproj0_kernel(x_ref, mod_ref, g_ref, w_ref, qg_ref, kg_ref, cos_ref, sin_ref,
                    a_ref, q_ref, k_ref, v_ref, *, rope):
    h = _modulated_norm(x_ref[...], g_ref[...], mod_ref[0, 0:1, :], mod_ref[0, 1:2, :])
    p = _dot(h.astype(BF16), w_ref[...])
    a_ref[...] = p[:, :CONV_CH] * _sigmoid(p[:, CONV_CH:2 * CONV_CH])
    q0 = 2 * CONV_CH
    k0 = q0 + ATTN_WIDTH
    v0 = k0 + KV_WIDTH
    cos = cos_ref[...]
    sin = sin_ref[...]
    q_scale = HEAD_DIM ** -0.5 * LOG2_E
    for hh in range(N_Q_HEADS):
        qh = _rms(p[:, q0 + hh * HEAD_DIM:q0 + (hh + 1) * HEAD_DIM]) * qg_ref[...]
        if rope:
            qh = _rope(qh, cos, sin)
        q_ref[:, hh * HEAD_DIM:(hh + 1) * HEAD_DIM] = (qh * q_scale).astype(q_ref.dtype)
    seqs, _, rows, _ = k_ref.shape
    for hh in range(N_KV_HEADS):
        kh = _rms(p[:, k0 + hh * HEAD_DIM:k0 + (hh + 1) * HEAD_DIM]) * kg_ref[...]
        if rope:
            kh = _rope(kh, cos, sin)
        vh = p[:, v0 + hh * HEAD_DIM:v0 + (hh + 1) * HEAD_DIM]
        for sq in range(seqs):
            k_ref[sq, hh] = kh[sq * rows:(sq + 1) * rows].astype(k_ref.dtype)
            v_ref[sq, hh] = vh[sq * rows:(sq + 1) * rows].astype(v_ref.dtype)


def _inproj0(stream, x, mod, norm_g, w_bf16, q_norm_g, k_norm_g, cos, sin, kv_dtype):
    n = stream.n_tok
    tiles_per_seq = max(stream.seq_len // ROW_TILE, 1)
    rope_map = (lambda i: (i % tiles_per_seq, 0)) if stream.rope else (lambda i: (0, 0))
    slot = _slot_map(stream)
    row = lambda i: (i, 0)
    fixed = lambda i: (0, 0)
    seqs_per_tile = max(ROW_TILE // stream.seq_len, 1)
    kv_spec = pl.BlockSpec((seqs_per_tile, N_KV_HEADS, ROW_TILE // seqs_per_tile, HEAD_DIM),
                           lambda i: (i // tiles_per_seq, 0, i % tiles_per_seq, 0))
    kv_shape = jax.ShapeDtypeStruct((stream.n_seq, N_KV_HEADS, stream.seq_len, HEAD_DIM), kv_dtype)
    return pl.pallas_call(
        functools.partial(_inproj0_kernel, rope=stream.rope),
        grid=(n // ROW_TILE,),
        in_specs=[
            pl.BlockSpec((ROW_TILE, D_MODEL), row),
            pl.BlockSpec((1, 6, D_MODEL), lambda i: (slot(i), 0, 0)),
            pl.BlockSpec((1, D_MODEL), fixed),
            pl.BlockSpec((D_MODEL, IN0_WIDTH), fixed),
            pl.BlockSpec((1, HEAD_DIM), fixed),
            pl.BlockSpec((1, HEAD_DIM), fixed),
            pl.BlockSpec((ROW_TILE, HEAD_DIM), rope_map),
            pl.BlockSpec((ROW_TILE, HEAD_DIM), rope_map),
        ],
        out_specs=[
            pl.BlockSpec((ROW_TILE, CONV_CH), row),
            pl.BlockSpec((ROW_TILE, ATTN_WIDTH), row),
            kv_spec,
            kv_spec,
        ],
        out_shape=[
            jax.ShapeDtypeStruct((n, CONV_CH), F32),
            jax.ShapeDtypeStruct((n, ATTN_WIDTH), BF16),
            kv_shape,
            kv_shape,
        ],
        compiler_params=_params("parallel"),
        name="inproj0",
    )(x, mod, norm_g.reshape(1, D_MODEL), w_bf16, q_norm_g.reshape(1, HEAD_DIM), k_norm_g.reshape(1, HEAD_DIM),
      cos, sin)


def _conv_kernel(prev_ref, main_ref, next_ref, w_ref, b_ref, g_ref, beta_ref, o_ref, pad_ref, acc_ref, *,
                 tiles_per_seq):
    i = pl.program_id(0)
    has_prev = (i % tiles_per_seq) != 0
    has_next = (i % tiles_per_seq) != (tiles_per_seq - 1)
    pad_ref[0:CONV_HALO, :] = jnp.where(has_prev, prev_ref[...], 0.0)
    pad_ref[CONV_HALO:CONV_HALO + CONV_TILE, :] = main_ref[...]
    pad_ref[CONV_HALO + CONV_TILE:, :] = jnp.where(has_next, next_ref[...], 0.0)
    first = CONV_HALO - CONV_HALF
    window = CONV_ROWS + SUBLANES
    for c0 in range(0, CONV_CH, LANES):
        for r0 in range(0, CONV_TILE, CONV_ROWS):
            acc = jnp.zeros((CONV_ROWS, LANES), F32)
            for phase in range(SUBLANES):
                partial = jnp.zeros((window, LANES), F32)
                for j in range(phase, CONV_WIDTH, SUBLANES):
                    base = r0 + j - phase
                    partial = partial + w_ref[j:j + 1, c0:c0 + LANES] * pad_ref[base:base + window, c0:c0 + LANES]
                acc = acc + partial[first + phase:first + phase + CONV_ROWS]
            acc_ref[r0:r0 + CONV_ROWS, c0:c0 + LANES] = acc
    a = acc_ref[...] + b_ref[...]
    mu = jnp.mean(a, axis=-1, keepdims=True)
    d = a - mu
    var = jnp.mean(d * d, axis=-1, keepdims=True)
    y = d * lax.rsqrt(var + EPS) * g_ref[...] + beta_ref[...]
    o_ref[...] = _silu(y).astype(o_ref.dtype)


def _conv_branch(stream, a, conv_w, conv_b, norm_g, norm_b):
    n = stream.n_tok
    tiles_per_seq = stream.seq_len // CONV_TILE
    halo_per_tile = CONV_TILE // CONV_HALO
    n_halo = n // CONV_HALO
    fixed = lambda i: (0, 0)
    return pl.pallas_call(
        functools.partial(_conv_kernel, tiles_per_seq=tiles_per_seq),
        grid=(n // CONV_TILE,),
        in_specs=[
            pl.BlockSpec((CONV_HALO, CONV_CH), lambda i: (jnp.maximum(i * halo_per_tile - 1, 0), 0)),
            pl.BlockSpec((CONV_TILE, CONV_CH), lambda i: (i, 0)),
            pl.BlockSpec((CONV_HALO, CONV_CH), lambda i: (jnp.minimum((i + 1) * halo_per_tile, n_halo - 1), 0)),
            pl.BlockSpec((CONV_WIDTH, CONV_CH), fixed),
            pl.BlockSpec((1, CONV_CH), fixed),
            pl.BlockSpec((1, CONV_CH), fixed),
            pl.BlockSpec((1, CONV_CH), fixed),
        ],
        out_specs=pl.BlockSpec((CONV_TILE, CONV_CH), lambda i: (i, 0)),
        out_shape=jax.ShapeDtypeStruct((n, CONV_CH), BF16),
        scratch_shapes=[
            pltpu.VMEM((CONV_TILE + 2 * CONV_HALO, CONV_CH), F32),
            pltpu.VMEM((CONV_TILE, CONV_CH), F32),
        ],
        compiler_params=_params("parallel"),
        name="conv_branch",
    )(a, a, a, conv_w, conv_b.reshape(1, CONV_CH), norm_g.reshape(1, CONV_CH), norm_b.reshape(1, CONV_CH))


def _attn_kernel(*refs, tq, past):
    if past:
        q_ref, k_ref, v_ref, kc_ref, vc_ref, o_ref, ks_ref, vs_ref = refs
    else:
        q_ref, k_ref, v_ref, o_ref, ks_ref, vs_ref = refs
    kv_heads, n_keys, _ = ks_ref.shape
    group_w = Q_PER_KV * HEAD_DIM

    @pl.when(pl.program_id(2) == 0)
    def _():
        lane = lax.broadcasted_iota(jnp.int32, (n_keys, HEAD_DIM), 1)
        for hk in range(kv_heads):
            vs_ref[hk, :, HEAD_DIM:] = jnp.where(lane == 0, 1.0, 0.0).astype(BF16)
            if past:
                ks_ref[hk, 0:past, :] = kc_ref[0, hk].astype(BF16)
                vs_ref[hk, 0:past, 0:HEAD_DIM] = vc_ref[0, hk].astype(BF16)
            ks_ref[hk, past:, :] = k_ref[0, hk].astype(BF16)
            vs_ref[hk, past:, 0:HEAD_DIM] = v_ref[0, hk].astype(BF16)

    for hk in range(kv_heads):
        q = q_ref[:, hk * group_w:(hk + 1) * group_w]
        q4 = jnp.concatenate([q[:, g * HEAD_DIM:(g + 1) * HEAD_DIM] for g in range(Q_PER_KV)], axis=0)
        rows = q4.shape[0]
        m = jnp.full((rows, 1), NEG_BIG, F32)
        acc = jnp.zeros((rows, 2 * HEAD_DIM), F32)
        for c0 in range(0, n_keys, ATTN_KV_CHUNK):
            c1 = min(c0 + ATTN_KV_CHUNK, n_keys)
            s = _dot_nt(q4, ks_ref[hk, c0:c1, :])
            m_new = jnp.maximum(m, jnp.max(s, axis=-1, keepdims=True))
            p = jnp.exp2(s - m_new).astype(BF16)
            acc = jnp.exp2(m - m_new) * acc + _dot(p, vs_ref[hk, c0:c1, :])
            m = m_new
        o = acc[:, 0:HEAD_DIM] * (1.0 / acc[:, HEAD_DIM:HEAD_DIM + 1])
        for g in range(Q_PER_KV):
            col = hk * group_w + g * HEAD_DIM
            o_ref[:, col:col + HEAD_DIM] = o[g * tq:(g + 1) * tq].astype(o_ref.dtype)


def _attention(stream, q, k, v, cache, tq, kv_heads):
    n = stream.n_tok
    t = stream.seq_len
    q_tiles = t // tq
    width = kv_heads * Q_PER_KV * HEAD_DIM
    in_specs = [
        pl.BlockSpec((tq, width), lambda b, h, i: (b * q_tiles + i, h)),
        pl.BlockSpec((1, kv_heads, t, HEAD_DIM), lambda b, h, i: (b, h, 0, 0)),
        pl.BlockSpec((1, kv_heads, t, HEAD_DIM), lambda b, h, i: (b, h, 0, 0)),
    ]
    args = [q, k, v]
    past = 0
    if cache is not None:
        past = cache[0].shape[2]
        in_specs += [pl.BlockSpec((1, kv_heads, past, HEAD_DIM), lambda b, h, i: (b, h, 0, 0))] * 2
        args += list(cache)
    return pl.pallas_call(
        functools.partial(_attn_kernel, tq=tq, past=past),
        grid=(stream.n_seq, N_KV_HEADS // kv_heads, q_tiles),
        in_specs=in_specs,
        out_specs=pl.BlockSpec((tq, width), lambda b, h, i: (b * q_tiles + i, h)),
        out_shape=jax.ShapeDtypeStruct((n, ATTN_WIDTH), BF16),
        scratch_shapes=[pltpu.VMEM((kv_heads, past + t, HEAD_DIM), BF16),
                        pltpu.VMEM((kv_heads, past + t, 2 * HEAD_DIM), BF16)],
        compiler_params=_params("parallel", "parallel", "arbitrary"),
        name="attention",
    )(*args)


def _route(logits, running):
    rows = logits.shape[0]
    lane = lax.broadcasted_iota(jnp.int32, logits.shape, 1)
    lane_f = lane.astype(F32)
    far = float(ROUTE_LANES)
    is_group = lane < N_GROUPS
    gl = jnp.where(is_group, logits, NEG_BIG)
    gmax = jnp.max(gl, axis=-1, keepdims=True)
    gsum = jnp.sum(jnp.where(is_group, jnp.exp(gl - gmax), 0.0), axis=-1, keepdims=True)
    g_w = 1.0 / gsum
    gidx = jnp.min(jnp.where(gl == gmax, lane_f, far), axis=-1, keepdims=True)
    lo = EXPERT_LANE0 + EXPERTS_PER_GROUP * gidx
    in_group = (lane_f >= lo) & (lane_f < lo + EXPERTS_PER_GROUP)
    el = jnp.where(in_group, logits, NEG_BIG)
    v1 = jnp.max(el, axis=-1, keepdims=True)
    i1 = jnp.min(jnp.where(el == v1, lane_f, far), axis=-1, keepdims=True)
    el2 = jnp.where(lane_f == i1, NEG_BIG, el)
    v2 = jnp.max(el2, axis=-1, keepdims=True)
    i2 = jnp.min(jnp.where(el2 == v2, lane_f, far), axis=-1, keepdims=True)
    e2 = jnp.exp(v2 - v1)
    w1 = g_w / (1.0 + e2)
    w2 = w1 * e2
    in_my_group = lane_f == gidx
    chosen = in_my_group.astype(BF16)
    earlier = (lax.broadcasted_iota(jnp.int32, (rows, rows), 0)
               > lax.broadcasted_iota(jnp.int32, (rows, rows), 1)).astype(BF16)
    before = running + _dot(earlier, chosen)
    rank = jnp.sum(jnp.where(in_my_group, before, 0.0), axis=-1, keepdims=True)
    record = jnp.where(lane == ROUTE_GROUP, gidx, jnp.where(lane == ROUTE_RANK, rank, 0.0))
    weights = jnp.where(lane_f == i1 - lo, w1, 0.0) + jnp.where(lane_f == i2 - lo, w2, 0.0)
    return record, weights, running + jnp.sum(chosen.astype(F32), axis=0, keepdims=True)


def _outproj_kernel(*refs, widths):
    n_in = len(widths)
    in_refs = refs[:n_in]
    w_ref, x_ref, mod_ref, g_ref, rw_ref, rb_ref, x1_ref, pay_ref, route_t_ref, count_ref, run_ref = refs[n_in:]

    @pl.when(pl.program_id(0) == 0)
    def _():
        run_ref[...] = jnp.zeros_like(run_ref)

    acc = None
    off = 0
    for r, width in zip(in_refs, widths):
        part = _dot(r[...], w_ref[off:off + width, :])
        acc = part if acc is None else acc + part
        off += width
    x1 = x_ref[...] + mod_ref[0, 2:3, :] * acc
    x1_ref[...] = x1
    h = _modulated_norm(x1, g_ref[...], mod_ref[0, 3:4, :], mod_ref[0, 4:5, :])
    h_hi = h.astype(BF16)
    h_lo = (h - h_hi.astype(F32)).astype(BF16)
    both = _dot(h_hi, rw_ref[...])
    logits = (both[:, :ROUTE_LANES] + both[:, ROUTE_LANES:] + _dot(h_lo, rw_ref[:, :ROUTE_LANES]) + rb_ref[...])
    record, weights, running = _route(logits, run_ref[...])
    route_t_ref[...] = record.T[:SUBLANES]
    run_ref[...] = running
    count_ref[...] = jnp.broadcast_to(running, count_ref.shape)
    rows = h.shape[0]
    _store_row_major(pay_ref, h, PAYLOAD_CHUNKS)
    pay_ref[pl.ds(ROW_CHUNKS, rows, stride=PAYLOAD_CHUNKS), :] = weights
    for c in range(ROW_CHUNKS + 1, PAYLOAD_CHUNKS):
        pay_ref[pl.ds(c, rows, stride=PAYLOAD_CHUNKS), :] = jnp.zeros((rows, LANES), F32)


def _store_row_major(ref, x, pitch=ROW_CHUNKS):
    rows = x.shape[0]
    for c in range(ROW_CHUNKS):
        ref[pl.ds(c, rows, stride=pitch), :] = x[:, c * LANES:(c + 1) * LANES]


def _load_row_major(ref, rows, pitch=ROW_CHUNKS):
    return jnp.concatenate([ref[pl.ds(c, rows, stride=pitch), :] for c in range(ROW_CHUNKS)], axis=-1)


def _outproj_route(stream, inputs, w_bf16, x, mod, ffn_norm_g, route_w, route_b):
    n = stream.n_tok
    widths = tuple(a.shape[1] for a in inputs)
    slot = _slot_map(stream)
    row = lambda i: (i, 0)
    fixed = lambda i: (0, 0)
    return pl.pallas_call(
        functools.partial(_outproj_kernel, widths=widths),
        grid=(n // ROW_TILE,),
        in_specs=[pl.BlockSpec((ROW_TILE, width), row) for width in widths] + [
            pl.BlockSpec((sum(widths), D_MODEL), fixed),
            pl.BlockSpec((ROW_TILE, D_MODEL), row),
            pl.BlockSpec((1, 6, D_MODEL), lambda i: (slot(i), 0, 0)),
            pl.BlockSpec((1, D_MODEL), fixed),
            pl.BlockSpec((D_MODEL, 2 * ROUTE_LANES), fixed),
            pl.BlockSpec((1, ROUTE_LANES), fixed),
        ],
        out_specs=[
            pl.BlockSpec((ROW_TILE, D_MODEL), row),
            pl.BlockSpec((ROW_TILE * PAYLOAD_CHUNKS, LANES), row),
            pl.BlockSpec((SUBLANES, ROW_TILE), lambda i: (0, i)),
            pl.BlockSpec((SUBLANES, ROUTE_LANES), fixed),
        ],
        out_shape=[
            jax.ShapeDtypeStruct((n, D_MODEL), F32),
            jax.ShapeDtypeStruct((n * PAYLOAD_CHUNKS, LANES), F32),
            jax.ShapeDtypeStruct((SUBLANES, n), F32),
            jax.ShapeDtypeStruct((SUBLANES, ROUTE_LANES), F32),
        ],
        scratch_shapes=[pltpu.VMEM((1, ROUTE_LANES), F32)],
        compiler_params=_params("arbitrary"),
        name="outproj_route",
    )(*inputs, w_bf16, x, mod, ffn_norm_g.reshape(1, D_MODEL), route_w, route_b)


def _router_params(grp_w, grp_b, rtr_w, rtr_b):
    w = jnp.concatenate([grp_w, jnp.moveaxis(rtr_w, 0, 1).reshape(D_MODEL, N_EXPERTS)], axis=1)
    w = jnp.pad(w, ((0, 0), (0, ROUTE_LANES - w.shape[1])))
    hi = w.astype(BF16)
    lo = (w - hi.astype(F32)).astype(BF16)
    b = jnp.concatenate([grp_b, rtr_b.reshape(N_EXPERTS)])
    b = jnp.pad(b, (0, ROUTE_LANES - b.shape[0])).reshape(1, ROUTE_LANES)
    return jnp.concatenate([hi, lo], axis=1), b


def _sorted_tiles(n_tok):
    return n_tok // FFN_TILE + N_GROUPS


def _dispatch_plan(route_t, counts, n_tok):
    n_tiles = _sorted_tiles(n_tok)
    count = counts[0, :N_GROUPS].astype(jnp.int32)
    tiles = (count + (FFN_TILE - 1)) // FFN_TILE
    start = (jnp.cumsum(tiles) - tiles) * FFN_TILE
    tile_end = jnp.cumsum(tiles)
    j = jnp.arange(n_tiles, dtype=jnp.int32)
    tile_group = jnp.minimum(jnp.sum(j[:, None] >= tile_end[None, :], axis=1), N_GROUPS - 1).astype(jnp.int32)
    group_end = jnp.take(start + count, tile_group)
    valid = jnp.where(j < tile_end[-1], jnp.clip(group_end - j * FFN_TILE, 0, FFN_TILE), 0).astype(jnp.int32)
    ids = route_t[:2].astype(jnp.int32)
    row = jnp.take(start, ids[ROUTE_GROUP]) + ids[ROUTE_RANK]
    src = jnp.zeros((n_tiles * FFN_TILE,), jnp.int32).at[row].set(jnp.arange(n_tok, dtype=jnp.int32))
    return tile_group, valid, src


def _token_rows(t, pitch=ROW_CHUNKS):
    return pl.ds(pl.multiple_of(t * pitch, pitch), pitch)


def _ffn_kernel(tg_ref, valid_ref, src_ref, pay_ref, wg_ref, wu_ref, wd_ref, y_ref, x_buf, y_buf, g_sems, s_sems):
    j = pl.program_id(0)
    n_tiles = pl.num_programs(0)

    def gather_copy(tile, r):
        tok = src_ref[tile * FFN_TILE + r]
        return pltpu.make_async_copy(pay_ref.at[_token_rows(tok, PAYLOAD_CHUNKS)],
                                     x_buf.at[tile % 2, _token_rows(r, PAYLOAD_CHUNKS)], g_sems.at[tile % 2])

    def scatter_copy(tile, r):
        tok = src_ref[tile * FFN_TILE + r]
        return pltpu.make_async_copy(y_buf.at[tile % 2, _token_rows(r)], y_ref.at[_token_rows(tok)],
                                     s_sems.at[tile % 2])

    def start_rows(copy, tile):
        def issue_pair(r2, carry):
            for k in range(2):
                copy(tile, 2 * r2 + k).start(priority=k)
            return carry

        def issue(r, carry):
            copy(tile, r).start()
            return carry

        @pl.when(valid_ref[tile] == FFN_TILE)
        def _():
            lax.fori_loop(0, FFN_TILE // 2, issue_pair, 0, unroll=4)

        @pl.when(valid_ref[tile] < FFN_TILE)
        def _():
            lax.fori_loop(0, valid_ref[tile], issue, 0)

    def wait_rows(src, dst, sem, tile, pitch):
        n = pl.multiple_of(valid_ref[tile] * pitch, pitch)
        pltpu.make_async_copy(src.at[pl.ds(0, n)], dst.at[pl.ds(0, n)], sem).wait()

    @pl.when(j == 0)
    def _():
        x_buf[...] = jnp.zeros_like(x_buf)
        start_rows(gather_copy, j)

    @pl.when(j + 1 < n_tiles)
    def _():
        @pl.when(valid_ref[j + 1] > 0)
        def _():
            start_rows(gather_copy, j + 1)

    @pl.when(valid_ref[j] > 0)
    def _():
        slot = j % 2
        wait_rows(pay_ref, x_buf.at[slot], g_sems.at[slot], j, PAYLOAD_CHUNKS)

        @pl.when(j >= 2)
        def _():
            wait_rows(y_buf.at[slot], y_ref, s_sems.at[slot], j - 2, ROW_CHUNKS)

        x = _load_row_major(x_buf.at[slot], FFN_TILE, PAYLOAD_CHUNKS).astype(BF16)
        w = x_buf[slot, pl.ds(ROW_CHUNKS, FFN_TILE, stride=PAYLOAD_CHUNKS), :]
        y = None
        for e in range(EXPERTS_PER_GROUP):
            hidden = _silu(_dot(x, wg_ref[e])) * _dot(x, wu_ref[e]) * w[:, e:e + 1]
            part = _dot(hidden.astype(BF16), wd_ref[e])
            y = part if y is None else y + part
        _store_row_major(y_buf.at[slot], y)
        start_rows(scatter_copy, j)

        last = (j + 1 == n_tiles) | (valid_ref[jnp.minimum(j + 1, n_tiles - 1)] == 0)

        @pl.when(last)
        def _():
            @pl.when(j >= 1)
            def _():
                wait_rows(y_buf.at[1 - slot], y_ref, s_sems.at[1 - slot], j - 1, ROW_CHUNKS)

            wait_rows(y_buf.at[slot], y_ref, s_sems.at[slot], j, ROW_CHUNKS)


def _expert_ffn(n_tok, tile_group, valid, src, payload, wg, wu, wd):
    group = lambda j, tg, va, sr: (tg[j], 0, 0)
    tile_rows = FFN_TILE * ROW_CHUNKS
    return pl.pallas_call(
        _ffn_kernel,
        grid_spec=pltpu.PrefetchScalarGridSpec(
            num_scalar_prefetch=3,
            grid=(_sorted_tiles(n_tok),),
            in_specs=[
                pl.BlockSpec(memory_space=pl.ANY),
                pl.BlockSpec((EXPERTS_PER_GROUP, D_MODEL, EXPERT_FF), group),
                pl.BlockSpec((EXPERTS_PER_GROUP, D_MODEL, EXPERT_FF), group),
                pl.BlockSpec((EXPERTS_PER_GROUP, EXPERT_FF, D_MODEL), group),
            ],
            out_specs=pl.BlockSpec(memory_space=pl.ANY),
            scratch_shapes=[pltpu.VMEM((2, FFN_TILE * PAYLOAD_CHUNKS, LANES), F32),
                            pltpu.VMEM((2, tile_rows, LANES), F32),
                            pltpu.SemaphoreType.DMA((2,)), pltpu.SemaphoreType.DMA((2,))],
        ),
        out_shape=jax.ShapeDtypeStruct((n_tok * ROW_CHUNKS, LANES), F32),
        compiler_params=_params("arbitrary"),
        name="moe_ffn",
    )(tile_group, valid, src, payload, wg, wu, wd)


def _combine_kernel(y_ref, x_ref, mod_ref, ng_ref, nmod_ref, o_ref, *rest, last_layer):
    x = x_ref[...] + mod_ref[0, 5:6, :] * _load_row_major(y_ref, x_ref.shape[0])
    if last_layer:
        o_ref[...] = _rms(x) * ng_ref[...]
    else:
        h_ref, = rest
        o_ref[...] = x
        h_ref[...] = _modulated_norm(x, ng_ref[...], nmod_ref[0, 0:1, :], nmod_ref[0, 1:2, :]).astype(h_ref.dtype)


def _combine(stream, y_rows, x, mod, norm_g, next_mod, last_layer):
    n = stream.n_tok
    slot = _slot_map(stream)
    row = lambda i: (i, 0)
    fixed = lambda i: (0, 0)
    out_specs = [pl.BlockSpec((ROW_TILE, D_MODEL), row)]
    out_shape = [jax.ShapeDtypeStruct((n, D_MODEL), F32)]
    if not last_layer:
        out_specs.append(pl.BlockSpec((ROW_TILE, D_MODEL), row))
        out_shape.append(jax.ShapeDtypeStruct((n, D_MODEL), BF16))
    return pl.pallas_call(
        functools.partial(_combine_kernel, last_layer=last_layer),
        grid=(n // ROW_TILE,),
        in_specs=[
            pl.BlockSpec((ROW_TILE * ROW_CHUNKS, LANES), row),
            pl.BlockSpec((ROW_TILE, D_MODEL), row),
            pl.BlockSpec((1, 6, D_MODEL), lambda i: (slot(i), 0, 0)),
            pl.BlockSpec((1, D_MODEL), fixed),
            pl.BlockSpec((1, 6, D_MODEL), lambda i: (slot(i), 0, 0)),
        ],
        out_specs=out_specs,
        out_shape=out_shape,
        compiler_params=_params("parallel"),
        name="moe_combine",
    )(y_rows, x, mod, norm_g.reshape(1, D_MODEL), next_mod)


def _moe(stream, payload, route_t, counts, wg, wu, wd, x, mod, norm_g, next_mod, last_layer):
    n = stream.n_tok
    tile_group, valid, src = _dispatch_plan(route_t, counts, n)
    y_rows = _expert_ffn(n, tile_group, valid, src, payload, wg, wu, wd)
    return _combine(stream, y_rows, x, mod, norm_g, next_mod, last_layer)


IN1_COL = 1024
IN1_ROW_TILE = 1024
QK_BLOCKS = 2 * RET_QK_WIDTH // IN1_COL
Q_BLOCKS = RET_QK_WIDTH // IN1_COL
V_BLOCKS = RET_V_WIDTH // IN1_COL


def _inproj1_qk_kernel(h_ref, w_ref, cos_ref, sin_ref, o_ref, *, rope):
    p = _dot(h_ref[...], w_ref[...])
    scale = jnp.where(pl.program_id(0) < Q_BLOCKS, 1.0, RET_DK ** -0.5)
    cos = cos_ref[...]
    sin = sin_ref[...]
    for hh in range(IN1_COL // RET_DK):
        ph = p[:, hh * RET_DK:(hh + 1) * RET_DK]
        if rope:
            ph = _rope(ph, cos, sin)
        o_ref[:, hh * RET_DK:(hh + 1) * RET_DK] = (ph * scale).astype(o_ref.dtype)


def _inproj1_vg_kernel(h_ref, w_ref, o_ref):
    p = _dot(h_ref[...], w_ref[...])
    o_ref[...] = jnp.where(pl.program_id(0) < V_BLOCKS, p, _silu(p)).astype(o_ref.dtype)


def _inproj1(stream, h, w_bf16, cos, sin):
    n = stream.n_tok
    tm = IN1_ROW_TILE
    tiles_per_seq = max(stream.seq_len // tm, 1)
    rope_map = (lambda j, i: (i % tiles_per_seq, 0)) if stream.rope else (lambda j, i: (0, 0))
    h_spec = pl.BlockSpec((tm, D_MODEL), lambda j, i: (i, 0))
    out_spec = pl.BlockSpec((tm, IN1_COL), lambda j, i: (i, j))
    qk = pl.pallas_call(
        functools.partial(_inproj1_qk_kernel, rope=stream.rope),
        grid=(QK_BLOCKS, n // tm),
        in_specs=[
            h_spec,
            pl.BlockSpec((D_MODEL, IN1_COL), lambda j, i: (0, j)),
            pl.BlockSpec((tm, RET_DK), rope_map),
            pl.BlockSpec((tm, RET_DK), rope_map),
        ],
        out_specs=out_spec,
        out_shape=jax.ShapeDtypeStruct((n, QK_BLOCKS * IN1_COL), BF16),
        compiler_params=_params("parallel", "parallel"),
        name="inproj1_qk",
    )(h, w_bf16, cos, sin)
    vg_blocks = IN1_WIDTH // IN1_COL - QK_BLOCKS
    vg = pl.pallas_call(
        _inproj1_vg_kernel,
        grid=(vg_blocks, n // tm),
        in_specs=[h_spec, pl.BlockSpec((D_MODEL, IN1_COL), lambda j, i: (0, QK_BLOCKS + j))],
        out_specs=out_spec,
        out_shape=jax.ShapeDtypeStruct((n, vg_blocks * IN1_COL), BF16),
        compiler_params=_params("parallel", "parallel"),
        name="inproj1_vg",
    )(h, w_bf16)
    return qk, vg


def _retention_kernel(*refs, n_chunks, has_state, heads, unroll):
    if has_state:
        decay_ref, q_ref, k_ref, v_ref, gf_ref, gb_ref, sf0_ref, sb0_ref, o_ref, inc_ref, seen_ref = refs
    else:
        decay_ref, q_ref, k_ref, v_ref, gf_ref, gb_ref, o_ref, sf_ref, sb_ref, inc_ref, seen_ref = refs
    c = RET_CHUNK
    row = lax.broadcasted_iota(jnp.int32, (c, c), 0).astype(F32)
    col = lax.broadcasted_iota(jnp.int32, (c, c), 1).astype(F32)
    pos = lax.broadcasted_iota(jnp.int32, (c, 1), 0).astype(F32)

    def chunk(i):
        return pl.ds(pl.multiple_of(i * c, c), c)

    diff = [row - col, col - row]
    qk_cols = [slice(g * RET_DK, (g + 1) * RET_DK) for g in range(heads)]
    v_cols = [slice(g * RET_DV, (g + 1) * RET_DV) for g in range(heads)]
    intra, q_dec, k_dec, chunk_dec = [], [], [], []
    for g in range(heads):
        hd = pl.program_id(1) * heads + g
        lg = [-jnp.exp(jnp.full((1, 1), decay_ref[d, hd], F32)) for d in range(2)]
        intra.append([jnp.where(diff[d] >= 0, jnp.exp(lg[d] * jnp.maximum(diff[d], 0.0)), 0.0) for d in range(2)])
        q_dec.append([jnp.exp(lg[0] * (pos + 1.0)), jnp.exp(lg[1] * (c - pos))])
        k_dec.append([jnp.exp(lg[0] * ((c - 1.0) - pos)), jnp.exp(lg[1] * pos)])
        chunk_dec.append([jnp.exp(lg[d] * float(c)) for d in range(2)])

    def increments(i, carry):
        rows = chunk(i)
        for g in range(heads):
            ki = k_ref[rows, qk_cols[g]].astype(F32)
            vi = v_ref[rows, v_cols[g]]
            for d in range(2):
                inc_ref[g, d, i] = _dot_tn((ki * k_dec[g][d]).astype(BF16), vi)
        return carry

    lax.fori_loop(0, n_chunks, increments, 0, unroll=unroll)

    for g in range(heads):
        finals = []
        for d in range(2):
            def scan(t, state):
                i = t if d == 0 else n_chunks - 1 - t
                seen_ref[g, d, i] = state.astype(BF16)
                return state * chunk_dec[g][d] + inc_ref[g, d, i]

            if has_state:
                state0 = (sf0_ref if d == 0 else sb0_ref)[0, g]
            else:
                state0 = jnp.zeros((RET_DK, RET_DV), F32)
            finals.append(lax.fori_loop(0, n_chunks, scan, state0))
        if not has_state:
            sf_ref[0, g] = finals[0]
            sb_ref[0, g] = finals[1]

    def outputs(i, carry):
        rows = chunk(i)
        for g in range(heads):
            qi = q_ref[rows, qk_cols[g]]
            vi = v_ref[rows, v_cols[g]]
            s = _dot_nt(qi, k_ref[rows, qk_cols[g]])
            mixed = None
            for d, g_ref in enumerate((gf_ref, gb_ref)):
                y = _dot((s * intra[g][d]).astype(BF16), vi) + _dot(qi, seen_ref[g, d, i]) * q_dec[g][d]
                gated = g_ref[rows, v_cols[g]].astype(F32) * _rms(y)
                mixed = gated if mixed is None else mixed + gated
            o_ref[rows, v_cols[g]] = mixed.astype(o_ref.dtype)
        return carry

    lax.fori_loop(0, n_chunks, outputs, 0, unroll=unroll)


def _retention(stream, qk, vg, decays, states, heads):
    n = stream.n_tok
    t = stream.seq_len
    n_chunks = t // RET_CHUNK
    has_state = states is not None
    qk_w = heads * RET_DK
    v_w = heads * RET_DV
    k_blk = RET_QK_WIDTH // qk_w
    gf_blk = RET_V_WIDTH // v_w
    gb_blk = 2 * gf_blk
    in_specs = [
        pl.BlockSpec(memory_space=pltpu.SMEM),
        pl.BlockSpec((t, qk_w), lambda b, h: (b, h)),
        pl.BlockSpec((t, qk_w), lambda b, h: (b, k_blk + h)),
        pl.BlockSpec((t, v_w), lambda b, h: (b, h)),
        pl.BlockSpec((t, v_w), lambda b, h: (b, gf_blk + h)),
        pl.BlockSpec((t, v_w), lambda b, h: (b, gb_blk + h)),
    ]
    args = [decays, qk, qk, vg, vg, vg]
    state_spec = pl.BlockSpec((1, heads, RET_DK, RET_DV), lambda b, h: (b, h, 0, 0))
    out_specs = [pl.BlockSpec((t, v_w), lambda b, h: (b, h))]
    out_shape = [jax.ShapeDtypeStruct((n, RET_V_WIDTH), BF16)]
    if has_state:
        in_specs += [state_spec, state_spec]
        args += list(states)
    else:
        out_specs += [state_spec, state_spec]
        out_shape += [jax.ShapeDtypeStruct((stream.n_seq, RET_HEADS, RET_DK, RET_DV), F32)] * 2
    return pl.pallas_call(
        functools.partial(_retention_kernel, n_chunks=n_chunks, has_state=has_state, heads=heads,
                          unroll=min(n_chunks, 2)),
        grid=(stream.n_seq, RET_HEADS // heads),
        in_specs=in_specs,
        out_specs=out_specs,
        out_shape=out_shape,
        scratch_shapes=[pltpu.VMEM((heads, 2, n_chunks, RET_DK, RET_DV), F32),
                        pltpu.VMEM((heads, 2, n_chunks, RET_DK, RET_DV), BF16)],
        compiler_params=_params("parallel", "parallel"),
        name="retention",
    )(*args)


def _rope_tables(n_tokens, dim):
    t = jnp.arange(n_tokens)
    pos = jnp.stack([t // GRID_W, t % GRID_W]).astype(F32)
    n_freq = dim // 4
    freqs = ROPE_THETA ** (-jnp.arange(n_freq, dtype=F32) / n_freq)
    ang = pos[:, :, None] * freqs
    cos, sin = jnp.cos(ang), jnp.sin(ang)
    cos_t = jnp.concatenate([cos[0], cos[0], cos[1], cos[1]], axis=-1)
    sin_t = jnp.concatenate([-sin[0], sin[0], -sin[1], sin[1]], axis=-1)
    return cos_t, sin_t


def kernel(x_prompt, x_sample, cache_k0, cache_v0, state_ret_fwd1, state_ret_bwd1, c, c_ctx,
           norm_mix_g0, mod_w0, mod_b0, in_w0, conv_w0, conv_b0, conv_norm_g0, conv_norm_b0,
           q_norm_g0, k_norm_g0, out_w0, norm_ffn_g0,
           moe_grp_w0, moe_grp_b0, moe_rtr_w0, moe_rtr_b0, moe_w_gate0, moe_w_up0, moe_w_down0,
           norm_mix_g1, mod_w1, mod_b1, in_w1, ret_decay_fwd1, ret_decay_bwd1, out_w1, norm_ffn_g1,
           moe_grp_w1, moe_grp_b1, moe_rtr_w1, moe_rtr_b1, moe_w_gate1, moe_w_up1, moe_w_down1,
           final_norm_g):
    batch, seq, d = x_prompt.shape
    dec_batch, dec_seq, _ = x_sample.shape
    assert d == D_MODEL and 1 + dec_batch <= MOD_SLOTS
    ctx = _Stream(batch, seq, first_slot=0, rope=False)
    lat = _Stream(dec_batch, dec_seq, first_slot=1, rope=True)
    assert ctx.n_tok % ROW_TILE == 0 and lat.n_tok % ROW_TILE == 0 and ROW_CHUNKS == SUBLANES
    assert seq % CONV_TILE == 0 and dec_seq % ROW_TILE == 0 and ROW_TILE % seq == 0

    cond = jnp.concatenate([c_ctx[None, :], c, jnp.zeros((MOD_SLOTS - 1 - dec_batch, d), F32)], axis=0)
    mod0 = _modulation(cond, mod_w0, mod_b0)
    mod1 = _modulation(cond, mod_w1, mod_b1)
    cos, sin = _rope_tables(dec_seq, HEAD_DIM)
    decays = jnp.stack([ret_decay_fwd1, ret_decay_bwd1]).astype(F32)

    in_w0_b = in_w0.astype(BF16)
    out_w0_b = out_w0.astype(BF16)
    in_w1_b = in_w1.astype(BF16)
    out_w1_b = out_w1.astype(BF16)
    moe0 = (moe_w_gate0.astype(BF16), moe_w_up0.astype(BF16), moe_w_down0.astype(BF16))
    moe1 = (moe_w_gate1.astype(BF16), moe_w_up1.astype(BF16), moe_w_down1.astype(BF16))
    route0 = _router_params(moe_grp_w0, moe_grp_b0, moe_rtr_w0, moe_rtr_b0)
    route1 = _router_params(moe_grp_w1, moe_grp_b1, moe_rtr_w1, moe_rtr_b1)

    def run(stream, x, cache, states):
        x = x.reshape(stream.n_tok, d)
        kv_dtype = BF16 if stream.rope else F32
        a, q, k, v = _inproj0(stream, x, mod0, norm_mix_g0, in_w0_b, q_norm_g0, k_norm_g0, cos, sin, kv_dtype)
        a = _conv_branch(stream, a, conv_w0, conv_b0, conv_norm_g0, conv_norm_b0)
        tq = min(stream.seq_len, 128 if cache is not None else 256)
        o = _attention(stream, q, k, v, cache, tq, kv_heads=1 if cache is not None else N_KV_HEADS)
        x, *routed = _outproj_route(stream, [a, o], out_w0_b, x, mod0, norm_ffn_g0, *route0)
        x, h = _moe(stream, *routed, *moe0, x, mod0, norm_mix_g1, mod1, last_layer=False)
        qk, vg = _inproj1(stream, h, in_w1_b, cos, sin)
        ret = _retention(stream, qk, vg, decays, states, heads=2 if states is not None else 4)
        x, *routed = _outproj_route(stream, [ret[0]], out_w1_b, x, mod1, norm_ffn_g1, *route1)
        y, = _moe(stream, *routed, *moe1, x, mod1, final_norm_g, mod1, last_layer=True)
        return y.reshape(stream.n_seq, stream.seq_len, d), k, v, ret[1:]

    y_prompt, k_ctx, v_ctx, new_states = run(ctx, x_prompt, None, None)
    y_sample, _, _, _ = run(lat, x_sample, (cache_k0, cache_v0), (state_ret_fwd1, state_ret_bwd1))
    return (y_prompt, y_sample, k_ctx, v_ctx, new_states[0], new_states[1])
```

```python
import functools

import jax
import jax.numpy as jnp
from jax import lax
from jax.experimental import pallas as pl
from jax.experimental.pallas import tpu as pltpu

D_MODEL = 1024
GRID_W = 64
EPS = 1e-6
CONV_CH = 512
CONV_WIDTH = 31
CONV_HALF = CONV_WIDTH // 2
N_Q_HEADS = 8
N_KV_HEADS = 2
Q_PER_KV = N_Q_HEADS // N_KV_HEADS
HEAD_DIM = 128
ROPE_THETA = 10000.0
ATTN_WIDTH = N_Q_HEADS * HEAD_DIM
KV_WIDTH = N_KV_HEADS * HEAD_DIM
IN0_WIDTH = 2 * CONV_CH + ATTN_WIDTH + 2 * KV_WIDTH
RET_HEADS = 8
RET_DK = 128
RET_DV = 256
RET_CHUNK = 128
RET_QK_WIDTH = RET_HEADS * RET_DK
RET_V_WIDTH = RET_HEADS * RET_DV
IN1_WIDTH = 2 * RET_QK_WIDTH + 3 * RET_V_WIDTH
N_GROUPS = 4
EXPERTS_PER_GROUP = 4
N_EXPERTS = N_GROUPS * EXPERTS_PER_GROUP
EXPERT_FF = 512

LANES = 128
SUBLANES = 8
VMEM_LIMIT_BYTES = 56 * 1024 * 1024

MOD_SLOTS = 8
ROW_TILE = 512
ROW_CHUNKS = D_MODEL // LANES
PAYLOAD_CHUNKS = 2 * ROW_CHUNKS
FFN_TILE = 256
ROUTE_GROUP, ROUTE_RANK = 0, 1
CONV_TILE = 256
CONV_HALO = 16
CONV_ROWS = 64
ROUTE_LANES = LANES
EXPERT_LANE0 = N_GROUPS
NEG_BIG = -1e30
LOG2_E = 1.4426950408889634
ATTN_KV_CHUNK = 512

F32 = jnp.float32
BF16 = jnp.bfloat16


def _params(*semantics):
    return pltpu.CompilerParams(dimension_semantics=semantics, vmem_limit_bytes=VMEM_LIMIT_BYTES)


def _sigmoid(x):
    return 1.0 / (1.0 + jnp.exp(-x))


def _silu(x):
    return x * _sigmoid(x)


def _rms(x):
    return x * lax.rsqrt(jnp.mean(x * x, axis=-1, keepdims=True) + EPS)


def _rope(x, cos, sin):
    lane = lax.broadcasted_iota(jnp.int32, x.shape, 1)
    take_upper = (lane % (HEAD_DIM // 2)) < (HEAD_DIM // 4)
    partner = jnp.where(take_upper, pltpu.roll(x, HEAD_DIM - HEAD_DIM // 4, 1), pltpu.roll(x, HEAD_DIM // 4, 1))
    return x * cos + partner * sin


def _dot(a, b):
    return jnp.dot(a, b, preferred_element_type=F32)


def _dot_nt(a, b):
    return lax.dot_general(a, b, (((1,), (1,)), ((), ())), preferred_element_type=F32)


def _dot_tn(a, b):
    return lax.dot_general(a, b, (((0,), (0,)), ((), ())), preferred_element_type=F32)


def _mod_kernel(c_ref, w_ref, b_ref, o_ref):
    c = c_ref[...]
    o_ref[...] = _dot(_silu(c).astype(BF16), w_ref[...].astype(BF16)) + b_ref[...]


def _modulation(cond, mod_w, mod_b):
    n_out = mod_w.shape[1]
    col = D_MODEL
    out = pl.pallas_call(
        _mod_kernel,
        grid=(n_out // col,),
        in_specs=[
            pl.BlockSpec((MOD_SLOTS, D_MODEL), lambda j: (0, 0)),
            pl.BlockSpec((D_MODEL, col), lambda j: (0, j)),
            pl.BlockSpec((1, col), lambda j: (0, j)),
        ],
        out_specs=pl.BlockSpec((MOD_SLOTS, col), lambda j: (0, j)),
        out_shape=jax.ShapeDtypeStruct((MOD_SLOTS, n_out), F32),
        compiler_params=_params("parallel"),
        name="modulation",
    )(cond, mod_w, mod_b.reshape(1, n_out))
    return out.reshape(MOD_SLOTS, 6, D_MODEL)


def _slot_map(stream):
    if stream.first_slot == 0:
        return lambda i: 0
    return lambda i: stream.first_slot + (i * ROW_TILE) // stream.seq_len


class _Stream:
    def __init__(self, n_seq, seq_len, first_slot, rope):
        self.n_seq = n_seq
        self.seq_len = seq_len
        self.first_slot = first_slot
        self.rope = rope
        self.n_tok = n_seq * seq_len


def _modulated_norm(x, g, shift, scale):
    return _rms(x) * g * (1.0 + scale) + shift


def _inproj0_kernel(x_ref, mod_ref, g_ref, w_ref, qg_ref, kg_ref, cos_ref, sin_ref,
                    a_ref, q_ref, k_ref, v_ref, *, rope):
    h = _modulated_norm(x_ref[...], g_ref[...], mod_ref[0, 0:1, :], mod_ref[0, 1:2, :])
    p = _dot(h.astype(BF16), w_ref[...])
    a_ref[...] = p[:, :CONV_CH] * _sigmoid(p[:, CONV_CH:2 * CONV_CH])
    q0 = 2 * CONV_CH
    k0 = q0 + ATTN_WIDTH
    v0 = k0 + KV_WIDTH
    cos = cos_ref[...]
    sin = sin_ref[...]
    q_scale = HEAD_DIM ** -0.5 * LOG2_E
    for hh in range(N_Q_HEADS):
        qh = _rms(p[:, q0 + hh * HEAD_DIM:q0 + (hh + 1) * HEAD_DIM]) * qg_ref[...]
        if rope:
            qh = _rope(qh, cos, sin)
        q_ref[:, hh * HEAD_DIM:(hh + 1) * HEAD_DIM] = (qh * q_scale).astype(q_ref.dtype)
    seqs, _, rows, _ = k_ref.shape
    for hh in range(N_KV_HEADS):
        kh = _rms(p[:, k0 + hh * HEAD_DIM:k0 + (hh + 1) * HEAD_DIM]) * kg_ref[...]
        if rope:
            kh = _rope(kh, cos, sin)
        vh = p[:, v0 + hh * HEAD_DIM:v0 + (hh + 1) * HEAD_DIM]
        for sq in range(seqs):
            k_ref[sq, hh] = kh[sq * rows:(sq + 1) * rows].astype(k_ref.dtype)
            v_ref[sq, hh] = vh[sq * rows:(sq + 1) * rows].astype(v_ref.dtype)


def _inproj0(stream, x, mod, norm_g, w_bf16, q_norm_g, k_norm_g, cos, sin, kv_dtype):
    n = stream.n_tok
    tiles_per_seq = max(stream.seq_len // ROW_TILE, 1)
    rope_map = (lambda i: (i % tiles_per_seq, 0)) if stream.rope else (lambda i: (0, 0))
    slot = _slot_map(stream)
    row = lambda i: (i, 0)
    fixed = lambda i: (0, 0)
    seqs_per_tile = max(ROW_TILE // stream.seq_len, 1)
    kv_spec = pl.BlockSpec((seqs_per_tile, N_KV_HEADS, ROW_TILE // seqs_per_tile, HEAD_DIM),
                           lambda i: (i // tiles_per_seq, 0, i % tiles_per_seq, 0))
    kv_shape = jax.ShapeDtypeStruct((stream.n_seq, N_KV_HEADS, stream.seq_len, HEAD_DIM), kv_dtype)
    return pl.pallas_call(
        functools.partial(_inproj0_kernel, rope=stream.rope),
        grid=(n // ROW_TILE,),
        in_specs=[
            pl.BlockSpec((ROW_TILE, D_MODEL), row),
            pl.BlockSpec((1, 6, D_MODEL), lambda i: (slot(i), 0, 0)),
            pl.BlockSpec((1, D_MODEL), fixed),
            pl.BlockSpec((D_MODEL, IN0_WIDTH), fixed),
            pl.BlockSpec((1, HEAD_DIM), fixed),
            pl.BlockSpec((1, HEAD_DIM), fixed),
            pl.BlockSpec((ROW_TILE, HEAD_DIM), rope_map),
            pl.BlockSpec((ROW_TILE, HEAD_DIM), rope_map),
        ],
        out_specs=[
            pl.BlockSpec((ROW_TILE, CONV_CH), row),
            pl.BlockSpec((ROW_TILE, ATTN_WIDTH), row),
            kv_spec,
            kv_spec,
        ],
        out_shape=[
            jax.ShapeDtypeStruct((n, CONV_CH), F32),
            jax.ShapeDtypeStruct((n, ATTN_WIDTH), BF16),
            kv_shape,
            kv_shape,
        ],
        compiler_params=_params("parallel"),
        name="inproj0",
    )(x, mod, norm_g.reshape(1, D_MODEL), w_bf16, q_norm_g.reshape(1, HEAD_DIM), k_norm_g.reshape(1, HEAD_DIM),
      cos, sin)


def _conv_kernel(prev_ref, main_ref, next_ref, w_ref, b_ref, g_ref, beta_ref, o_ref, pad_ref, acc_ref, *,
                 tiles_per_seq):
    i = pl.program_id(0)
    has_prev = (i % tiles_per_seq) != 0
    has_next = (i % tiles_per_seq) != (tiles_per_seq - 1)
    pad_ref[0:CONV_HALO, :] = jnp.where(has_prev, prev_ref[...], 0.0)
    pad_ref[CONV_HALO:CONV_HALO + CONV_TILE, :] = main_ref[...]
    pad_ref[CONV_HALO + CONV_TILE:, :] = jnp.where(has_next, next_ref[...], 0.0)
    first = CONV_HALO - CONV_HALF
    window = CONV_ROWS + SUBLANES
    for c0 in range(0, CONV_CH, LANES):
        for r0 in range(0, CONV_TILE, CONV_ROWS):
            acc = jnp.zeros((CONV_ROWS, LANES), F32)
            for phase in range(SUBLANES):
                partial = jnp.zeros((window, LANES), F32)
                for j in range(phase, CONV_WIDTH, SUBLANES):
                    base = r0 + j - phase
                    partial = partial + w_ref[j:j + 1, c0:c0 + LANES] * pad_ref[base:base + window, c0:c0 + LANES]
                acc = acc + partial[first + phase:first + phase + CONV_ROWS]
            acc_ref[r0:r0 + CONV_ROWS, c0:c0 + LANES] = acc
    a = acc_ref[...] + b_ref[...]
    mu = jnp.mean(a, axis=-1, keepdims=True)
    d = a - mu
    var = jnp.mean(d * d, axis=-1, keepdims=True)
    y = d * lax.rsqrt(var + EPS) * g_ref[...] + beta_ref[...]
    o_ref[...] = _silu(y).astype(o_ref.dtype)


def _conv_branch(stream, a, conv_w, conv_b, norm_g, norm_b):
    n = stream.n_tok
    tiles_per_seq = stream.seq_len // CONV_TILE
    halo_per_tile = CONV_TILE // CONV_HALO
    n_halo = n // CONV_HALO
    fixed = lambda i: (0, 0)
    return pl.pallas_call(
        functools.partial(_conv_kernel, tiles_per_seq=tiles_per_seq),
        grid=(n // CONV_TILE,),
        in_specs=[
            pl.BlockSpec((CONV_HALO, CONV_CH), lambda i: (jnp.maximum(i * halo_per_tile - 1, 0), 0)),
            pl.BlockSpec((CONV_TILE, CONV_CH), lambda i: (i, 0)),
            pl.BlockSpec((CONV_HALO, CONV_CH), lambda i: (jnp.minimum((i + 1) * halo_per_tile, n_halo - 1), 0)),
            pl.BlockSpec((CONV_WIDTH, CONV_CH), fixed),
            pl.BlockSpec((1, CONV_CH), fixed),
            pl.BlockSpec((1, CONV_CH), fixed),
            pl.BlockSpec((1, CONV_CH), fixed),
        ],
        out_specs=pl.BlockSpec((CONV_TILE, CONV_CH), lambda i: (i, 0)),
        out_shape=jax.ShapeDtypeStruct((n, CONV_CH), BF16),
        scratch_shapes=[
            pltpu.VMEM((CONV_TILE + 2 * CONV_HALO, CONV_CH), F32),
            pltpu.VMEM((CONV_TILE, CONV_CH), F32),
        ],
        compiler_params=_params("parallel"),
        name="conv_branch",
    )(a, a, a, conv_w, conv_b.reshape(1, CONV_CH), norm_g.reshape(1, CONV_CH), norm_b.reshape(1, CONV_CH))


def _attn_kernel(*refs, tq, past):
    if past:
        q_ref, k_ref, v_ref, kc_ref, vc_ref, o_ref, ks_ref, vs_ref = refs
    else:
        q_ref, k_ref, v_ref, o_ref, ks_ref, vs_ref = refs
    kv_heads, n_keys, _ = ks_ref.shape
    group_w = Q_PER_KV * HEAD_DIM

    @pl.when(pl.program_id(2) == 0)
    def _():
        lane = lax.broadcasted_iota(jnp.int32, (n_keys, HEAD_DIM), 1)
        for hk in range(kv_heads):
            vs_ref[hk, :, HEAD_DIM:] = jnp.where(lane == 0, 1.0, 0.0).astype(BF16)
            if past:
                ks_ref[hk, 0:past, :] = kc_ref[0, hk].astype(BF16)
                vs_ref[hk, 0:past, 0:HEAD_DIM] = vc_ref[0, hk].astype(BF16)
            ks_ref[hk, past:, :] = k_ref[0, hk].astype(BF16)
            vs_ref[hk, past:, 0:HEAD_DIM] = v_ref[0, hk].astype(BF16)

    for hk in range(kv_heads):
        q = q_ref[:, hk * group_w:(hk + 1) * group_w]
        q4 = jnp.concatenate([q[:, g * HEAD_DIM:(g + 1) * HEAD_DIM] for g in range(Q_PER_KV)], axis=0)
        rows = q4.shape[0]
        m = jnp.full((rows, 1), NEG_BIG, F32)
        acc = jnp.zeros((rows, 2 * HEAD_DIM), F32)
        for c0 in range(0, n_keys, ATTN_KV_CHUNK):
            c1 = min(c0 + ATTN_KV_CHUNK, n_keys)
            s = _dot_nt(q4, ks_ref[hk, c0:c1, :])
            m_new = jnp.maximum(m, jnp.max(s, axis=-1, keepdims=True))
            p = jnp.exp2(s - m_new).astype(BF16)
            acc = jnp.exp2(m - m_new) * acc + _dot(p, vs_ref[hk, c0:c1, :])
            m = m_new
        o = acc[:, 0:HEAD_DIM] * (1.0 / acc[:, HEAD_DIM:HEAD_DIM + 1])
        for g in range(Q_PER_KV):
            col = hk * group_w + g * HEAD_DIM
            o_ref[:, col:col + HEAD_DIM] = o[g * tq:(g + 1) * tq].astype(o_ref.dtype)


def _attention(stream, q, k, v, cache, tq, kv_heads):
    n = stream.n_tok
    t = stream.seq_len
    q_tiles = t // tq
    width = kv_heads * Q_PER_KV * HEAD_DIM
    in_specs = [
        pl.BlockSpec((tq, width), lambda b, h, i: (b * q_tiles + i, h)),
        pl.BlockSpec((1, kv_heads, t, HEAD_DIM), lambda b, h, i: (b, h, 0, 0)),
        pl.BlockSpec((1, kv_heads, t, HEAD_DIM), lambda b, h, i: (b, h, 0, 0)),
    ]
    args = [q, k, v]
    past = 0
    if cache is not None:
        past = cache[0].shape[2]
        in_specs += [pl.BlockSpec((1, kv_heads, past, HEAD_DIM), lambda b, h, i: (b, h, 0, 0))] * 2
        args += list(cache)
    return pl.pallas_call(
        functools.partial(_attn_kernel, tq=tq, past=past),
        grid=(stream.n_seq, N_KV_HEADS // kv_heads, q_tiles),
        in_specs=in_specs,
        out_specs=pl.BlockSpec((tq, width), lambda b, h, i: (b * q_tiles + i, h)),
        out_shape=jax.ShapeDtypeStruct((n, ATTN_WIDTH), BF16),
        scratch_shapes=[pltpu.VMEM((kv_heads, past + t, HEAD_DIM), BF16),
                        pltpu.VMEM((kv_heads, past + t, 2 * HEAD_DIM), BF16)],
        compiler_params=_params("parallel", "parallel", "arbitrary"),
        name="attention",
    )(*args)


def _route(logits, running):
    rows = logits.shape[0]
    lane = lax.broadcasted_iota(jnp.int32, logits.shape, 1)
    lane_f = lane.astype(F32)
    far = float(ROUTE_LANES)
    is_group = lane < N_GROUPS
    gl = jnp.where(is_group, logits, NEG_BIG)
    gmax = jnp.max(gl, axis=-1, keepdims=True)
    gsum = jnp.sum(jnp.where(is_group, jnp.exp(gl - gmax), 0.0), axis=-1, keepdims=True)
    g_w = 1.0 / gsum
    gidx = jnp.min(jnp.where(gl == gmax, lane_f, far), axis=-1, keepdims=True)
    lo = EXPERT_LANE0 + EXPERTS_PER_GROUP * gidx
    in_group = (lane_f >= lo) & (lane_f < lo + EXPERTS_PER_GROUP)
    el = jnp.where(in_group, logits, NEG_BIG)
    v1 = jnp.max(el, axis=-1, keepdims=True)
    i1 = jnp.min(jnp.where(el == v1, lane_f, far), axis=-1, keepdims=True)
    el2 = jnp.where(lane_f == i1, NEG_BIG, el)
    v2 = jnp.max(el2, axis=-1, keepdims=True)
    i2 = jnp.min(jnp.where(el2 == v2, lane_f, far), axis=-1, keepdims=True)
    e2 = jnp.exp(v2 - v1)
    w1 = g_w / (1.0 + e2)
    w2 = w1 * e2
    in_my_group = lane_f == gidx
    chosen = in_my_group.astype(BF16)
    earlier = (lax.broadcasted_iota(jnp.int32, (rows, rows), 0)
               > lax.broadcasted_iota(jnp.int32, (rows, rows), 1)).astype(BF16)
    before = running + _dot(earlier, chosen)
    rank = jnp.sum(jnp.where(in_my_group, before, 0.0), axis=-1, keepdims=True)
    record = jnp.where(lane == ROUTE_GROUP, gidx, jnp.where(lane == ROUTE_RANK, rank, 0.0))
    weights = jnp.where(lane_f == i1 - lo, w1, 0.0) + jnp.where(lane_f == i2 - lo, w2, 0.0)
    return record, weights, running + jnp.sum(chosen.astype(F32), axis=0, keepdims=True)


def _outproj_kernel(*refs, widths):
    n_in = len(widths)
    in_refs = refs[:n_in]
    w_ref, x_ref, mod_ref, g_ref, rw_ref, rb_ref, x1_ref, pay_ref, route_t_ref, count_ref, run_ref = refs[n_in:]

    @pl.when(pl.program_id(0) == 0)
    def _():
        run_ref[...] = jnp.zeros_like(run_ref)

    acc = None
    off = 0
    for r, width in zip(in_refs, widths):
        part = _dot(r[...], w_ref[off:off + width, :])
        acc = part if acc is None else acc + part
        off += width
    x1 = x_ref[...] + mod_ref[0, 2:3, :] * acc
    x1_ref[...] = x1
    h = _modulated_norm(x1, g_ref[...], mod_ref[0, 3:4, :], mod_ref[0, 4:5, :])
    h_hi = h.astype(BF16)
    h_lo = (h - h_hi.astype(F32)).astype(BF16)
    both = _dot(h_hi, rw_ref[...])
    logits = (both[:, :ROUTE_LANES] + both[:, ROUTE_LANES:] + _dot(h_lo, rw_ref[:, :ROUTE_LANES]) + rb_ref[...])
    record, weights, running = _route(logits, run_ref[...])
    route_t_ref[...] = record.T[:SUBLANES]
    run_ref[...] = running
    count_ref[...] = jnp.broadcast_to(running, count_ref.shape)
    rows = h.shape[0]
    _store_row_major(pay_ref, h, PAYLOAD_CHUNKS)
    pay_ref[pl.ds(ROW_CHUNKS, rows, stride=PAYLOAD_CHUNKS), :] = weights
    for c in range(ROW_CHUNKS + 1, PAYLOAD_CHUNKS):
        pay_ref[pl.ds(c, rows, stride=PAYLOAD_CHUNKS), :] = jnp.zeros((rows, LANES), F32)


def _store_row_major(ref, x, pitch=ROW_CHUNKS):
    rows = x.shape[0]
    for c in range(ROW_CHUNKS):
        ref[pl.ds(c, rows, stride=pitch), :] = x[:, c * LANES:(c + 1) * LANES]


def _load_row_major(ref, rows, pitch=ROW_CHUNKS):
    return jnp.concatenate([ref[pl.ds(c, rows, stride=pitch), :] for c in range(ROW_CHUNKS)], axis=-1)


def _outproj_route(stream, inputs, w_bf16, x, mod, ffn_norm_g, route_w, route_b):
    n = stream.n_tok
    widths = tuple(a.shape[1] for a in inputs)
    slot = _slot_map(stream)
    row = lambda i: (i, 0)
    fixed = lambda i: (0, 0)
    return pl.pallas_call(
        functools.partial(_outproj_kernel, widths=widths),
        grid=(n // ROW_TILE,),
        in_specs=[pl.BlockSpec((ROW_TILE, width), row) for width in widths] + [
            pl.BlockSpec((sum(widths), D_MODEL), fixed),
            pl.BlockSpec((ROW_TILE, D_MODEL), row),
            pl.BlockSpec((1, 6, D_MODEL), lambda i: (slot(i), 0, 0)),
            pl.BlockSpec((1, D_MODEL), fixed),
            pl.BlockSpec((D_MODEL, 2 * ROUTE_LANES), fixed),
            pl.BlockSpec((1, ROUTE_LANES), fixed),
        ],
        out_specs=[
            pl.BlockSpec((ROW_TILE, D_MODEL), row),
            pl.BlockSpec((ROW_TILE * PAYLOAD_CHUNKS, LANES), row),
            pl.BlockSpec((SUBLANES, ROW_TILE), lambda i: (0, i)),
            pl.BlockSpec((SUBLANES, ROUTE_LANES), fixed),
        ],
        out_shape=[
            jax.ShapeDtypeStruct((n, D_MODEL), F32),
            jax.ShapeDtypeStruct((n * PAYLOAD_CHUNKS, LANES), F32),
            jax.ShapeDtypeStruct((SUBLANES, n), F32),
            jax.ShapeDtypeStruct((SUBLANES, ROUTE_LANES), F32),
        ],
        scratch_shapes=[pltpu.VMEM((1, ROUTE_LANES), F32)],
        compiler_params=_params("arbitrary"),
        name="outproj_route",
    )(*inputs, w_bf16, x, mod, ffn_norm_g.reshape(1, D_MODEL), route_w, route_b)


def _router_params(grp_w, grp_b, rtr_w, rtr_b):
    w = jnp.concatenate([grp_w, jnp.moveaxis(rtr_w, 0, 1).reshape(D_MODEL, N_EXPERTS)], axis=1)
    w = jnp.pad(w, ((0, 0), (0, ROUTE_LANES - w.shape[1])))
    hi = w.astype(BF16)
    lo = (w - hi.astype(F32)).astype(BF16)
    b = jnp.concatenate([grp_b, rtr_b.reshape(N_EXPERTS)])
    b = jnp.pad(b, (0, ROUTE_LANES - b.shape[0])).reshape(1, ROUTE_LANES)
    return jnp.concatenate([hi, lo], axis=1), b


def _sorted_tiles(n_tok):
    return n_tok // FFN_TILE + N_GROUPS


def _dispatch_plan(route_t, counts, n_tok):
    n_tiles = _sorted_tiles(n_tok)
    count = counts[0, :N_GROUPS].astype(jnp.int32)
    tiles = (count + (FFN_TILE - 1)) // FFN_TILE
    start = (jnp.cumsum(tiles) - tiles) * FFN_TILE
    tile_end = jnp.cumsum(tiles)
    j = jnp.arange(n_tiles, dtype=jnp.int32)
    tile_group = jnp.minimum(jnp.sum(j[:, None] >= tile_end[None, :], axis=1), N_GROUPS - 1).astype(jnp.int32)
    group_end = jnp.take(start + count, tile_group)
    valid = jnp.where(j < tile_end[-1], jnp.clip(group_end - j * FFN_TILE, 0, FFN_TILE), 0).astype(jnp.int32)
    ids = route_t[:2].astype(jnp.int32)
    row = jnp.take(start, ids[ROUTE_GROUP]) + ids[ROUTE_RANK]
    n_rows = n_tiles * FFN_TILE
    src = jnp.zeros((n_rows,), jnp.int32).at[row].set(jnp.arange(n_tok, dtype=jnp.int32))
    p = jnp.arange(n_rows, dtype=jnp.int32)
    real = (p % FFN_TILE) < jnp.repeat(valid, FFN_TILE)
    spare = n_tok + ((p // FFN_TILE) % 2) * FFN_TILE + p % FFN_TILE
    dst = jnp.where(real, src, spare)
    lead = n_tok + FFN_TILE + jnp.arange(FFN_TILE, dtype=jnp.int32)
    return tile_group, tile_end[-1:].astype(jnp.int32), src, jnp.concatenate([lead, dst])


def _token_rows(t, pitch=ROW_CHUNKS):
    return pl.ds(pl.multiple_of(t * pitch, pitch), pitch)


def _ffn_kernel(tg_ref, used_ref, src_ref, dst_ref, pay_ref, wg_ref, wu_ref, wd_ref, y_ref,
                x_buf, y_buf, xb_ref, acc_ref, g_sems, s_sems, *, n_tok):
    j = pl.program_id(0)
    n_tiles = pl.num_programs(0)
    n_used = used_ref[0]
    slot = j % 2
    other = 1 - slot
    rows_per_expert = FFN_TILE // EXPERTS_PER_GROUP

    def gather_copy(tile, to_slot, r):
        tok = src_ref[tile * FFN_TILE + r]
        return pltpu.make_async_copy(pay_ref.at[_token_rows(tok, PAYLOAD_CHUNKS)],
                                     x_buf.at[to_slot, _token_rows(r, PAYLOAD_CHUNKS)], g_sems.at[to_slot])

    def scatter_copy(step, from_slot, r):
        tok = dst_ref[step * FFN_TILE + r]
        return pltpu.make_async_copy(y_buf.at[from_slot, _token_rows(r)], y_ref.at[_token_rows(tok)],
                                     s_sems.at[from_slot])

    def start_rows(copy, lo, n):
        for k in range(n):
            copy(lo + k).start(priority=k % 2)

    def start_tile(copy):
        def body(i, carry):
            start_rows(copy, i * SUBLANES, SUBLANES)
            return carry

        lax.fori_loop(0, FFN_TILE // SUBLANES, body, 0)

    def wait_gather(at_slot):
        pltpu.make_async_copy(pay_ref.at[pl.ds(0, FFN_TILE * PAYLOAD_CHUNKS)], x_buf.at[at_slot],
                              g_sems.at[at_slot]).wait()

    def wait_scatter(at_slot):
        pltpu.make_async_copy(y_buf.at[at_slot], y_ref.at[pl.ds(0, FFN_TILE * ROW_CHUNKS)],
                              s_sems.at[at_slot]).wait()

    @pl.when(j == 0)
    def _():
        y_buf[...] = jnp.zeros_like(y_buf)
        for s in range(2):
            spare = pl.ds((n_tok + s * FFN_TILE) * ROW_CHUNKS, FFN_TILE * ROW_CHUNKS)
            pltpu.make_async_copy(y_buf.at[s], y_ref.at[spare], s_sems.at[s]).start()
        for s in range(2):
            wait_scatter(s)
        start_tile(lambda r: gather_copy(0, 0, r))

    @pl.when(j < n_used)
    def _():
        wait_gather(slot)
        xb_ref[...] = _load_row_major(x_buf.at[slot], FFN_TILE, PAYLOAD_CHUNKS).astype(BF16)
        w = x_buf[slot, pl.ds(ROW_CHUNKS, FFN_TILE, stride=PAYLOAD_CHUNKS), :]
        lane = lax.broadcasted_iota(jnp.int32, w.shape, 1)
        acc_ref[...] = jnp.zeros_like(acc_ref)
        nxt = jnp.minimum(j + 1, n_tiles - 1)

        def expert(e, carry):
            lo = e * rows_per_expert
            start_rows(lambda r: gather_copy(nxt, other, r), lo, rows_per_expert)
            start_rows(lambda r: scatter_copy(j, other, r), lo, rows_per_expert)
            x = xb_ref[...]
            w_e = jnp.sum(jnp.where(lane == e, w, 0.0), axis=-1, keepdims=True)
            hidden = _silu(_dot(x, wg_ref[e])) * _dot(x, wu_ref[e]) * w_e
            acc_ref[...] += _dot(hidden.astype(BF16), wd_ref[e])
            return carry

        lax.fori_loop(0, EXPERTS_PER_GROUP, expert, 0)

        @pl.when(j >= 1)
        def _():
            wait_scatter(slot)

        _store_row_major(y_buf.at[slot], acc_ref[...])

        @pl.when(j == n_tiles - 1)
        def _():
            wait_gather(other)
            start_tile(lambda r: scatter_copy(j + 1, slot, r))
            wait_scatter(other)
            wait_scatter(slot)

    @pl.when(j == n_used)
    def _():
        wait_gather(slot)
        start_tile(lambda r: scatter_copy(j, other, r))
        wait_scatter(slot)
        wait_scatter(other)


def _expert_ffn(n_tok, tile_group, n_used, src, dst, payload, wg, wu, wd):
    group = lambda j, tg, nu, sr, ds: (tg[j], 0, 0)
    tile_rows = FFN_TILE * ROW_CHUNKS
    return pl.pallas_call(
        functools.partial(_ffn_kernel, n_tok=n_tok),
        grid_spec=pltpu.PrefetchScalarGridSpec(
            num_scalar_prefetch=4,
            grid=(_sorted_tiles(n_tok),),
            in_specs=[
                pl.BlockSpec(memory_space=pl.ANY),
                pl.BlockSpec((EXPERTS_PER_GROUP, D_MODEL, EXPERT_FF), group),
                pl.BlockSpec((EXPERTS_PER_GROUP, D_MODEL, EXPERT_FF), group),
                pl.BlockSpec((EXPERTS_PER_GROUP, EXPERT_FF, D_MODEL), group),
            ],
            out_specs=pl.BlockSpec(memory_space=pl.ANY),
            scratch_shapes=[pltpu.VMEM((2, FFN_TILE * PAYLOAD_CHUNKS, LANES), F32),
                            pltpu.VMEM((2, tile_rows, LANES), F32),
                            pltpu.VMEM((FFN_TILE, D_MODEL), BF16),
                            pltpu.VMEM((FFN_TILE, D_MODEL), F32),
                            pltpu.SemaphoreType.DMA((2,)), pltpu.SemaphoreType.DMA((2,))],
        ),
        out_shape=jax.ShapeDtypeStruct(((n_tok + 2 * FFN_TILE) * ROW_CHUNKS, LANES), F32),
        compiler_params=_params("arbitrary"),
        name="moe_ffn",
    )(tile_group, n_used, src, dst, payload, wg, wu, wd)


def _combine_kernel(y_ref, x_ref, mod_ref, ng_ref, nmod_ref, o_ref, *rest, last_layer):
    x = x_ref[...] + mod_ref[0, 5:6, :] * _load_row_major(y_ref, x_ref.shape[0])
    if last_layer:
        o_ref[...] = _rms(x) * ng_ref[...]
    else:
        h_ref, = rest
        o_ref[...] = x
        h_ref[...] = _modulated_norm(x, ng_ref[...], nmod_ref[0, 0:1, :], nmod_ref[0, 1:2, :]).astype(h_ref.dtype)


def _combine(stream, y_rows, x, mod, norm_g, next_mod, last_layer):
    n = stream.n_tok
    slot = _slot_map(stream)
    row = lambda i: (i, 0)
    fixed = lambda i: (0, 0)
    out_specs = [pl.BlockSpec((ROW_TILE, D_MODEL), row)]
    out_shape = [jax.ShapeDtypeStruct((n, D_MODEL), F32)]
    if not last_layer:
        out_specs.append(pl.BlockSpec((ROW_TILE, D_MODEL), row))
        out_shape.append(jax.ShapeDtypeStruct((n, D_MODEL), BF16))
    return pl.pallas_call(
        functools.partial(_combine_kernel, last_layer=last_layer),
        grid=(n // ROW_TILE,),
        in_specs=[
            pl.BlockSpec((ROW_TILE * ROW_CHUNKS, LANES), row),
            pl.BlockSpec((ROW_TILE, D_MODEL), row),
            pl.BlockSpec((1, 6, D_MODEL), lambda i: (slot(i), 0, 0)),
            pl.BlockSpec((1, D_MODEL), fixed),
            pl.BlockSpec((1, 6, D_MODEL), lambda i: (slot(i), 0, 0)),
        ],
        out_specs=out_specs,
        out_shape=out_shape,
        compiler_params=_params("parallel"),
        name="moe_combine",
    )(y_rows, x, mod, norm_g.reshape(1, D_MODEL), next_mod)


def _moe(stream, payload, route_t, counts, wg, wu, wd, x, mod, norm_g, next_mod, last_layer):
    n = stream.n_tok
    tile_group, n_used, src, dst = _dispatch_plan(route_t, counts, n)
    y_rows = _expert_ffn(n, tile_group, n_used, src, dst, payload, wg, wu, wd)
    return _combine(stream, y_rows, x, mod, norm_g, next_mod, last_layer)


IN1_COL = 1024
IN1_ROW_TILE = 1024
QK_BLOCKS = 2 * RET_QK_WIDTH // IN1_COL
Q_BLOCKS = RET_QK_WIDTH // IN1_COL
V_BLOCKS = RET_V_WIDTH // IN1_COL


def _inproj1_qk_kernel(h_ref, w_ref, cos_ref, sin_ref, o_ref, *, rope):
    p = _dot(h_ref[...], w_ref[...])
    scale = jnp.where(pl.program_id(0) < Q_BLOCKS, 1.0, RET_DK ** -0.5)
    cos = cos_ref[...]
    sin = sin_ref[...]
    for hh in range(IN1_COL // RET_DK):
        ph = p[:, hh * RET_DK:(hh + 1) * RET_DK]
        if rope:
            ph = _rope(ph, cos, sin)
        o_ref[:, hh * RET_DK:(hh + 1) * RET_DK] = (ph * scale).astype(o_ref.dtype)


def _inproj1_vg_kernel(h_ref, w_ref, o_ref):
    p = _dot(h_ref[...], w_ref[...])
    o_ref[...] = jnp.where(pl.program_id(0) < V_BLOCKS, p, _silu(p)).astype(o_ref.dtype)


def _inproj1(stream, h, w_bf16, cos, sin):
    n = stream.n_tok
    tm = IN1_ROW_TILE
    tiles_per_seq = max(stream.seq_len // tm, 1)
    rope_map = (lambda j, i: (i % tiles_per_seq, 0)) if stream.rope else (lambda j, i: (0, 0))
    h_spec = pl.BlockSpec((tm, D_MODEL), lambda j, i: (i, 0))
    out_spec = pl.BlockSpec((tm, IN1_COL), lambda j, i: (i, j))
    qk = pl.pallas_call(
        functools.partial(_inproj1_qk_kernel, rope=stream.rope),
        grid=(QK_BLOCKS, n // tm),
        in_specs=[
            h_spec,
            pl.BlockSpec((D_MODEL, IN1_COL), lambda j, i: (0, j)),
            pl.BlockSpec((tm, RET_DK), rope_map),
            pl.BlockSpec((tm, RET_DK), rope_map),
        ],
        out_specs=out_spec,
        out_shape=jax.ShapeDtypeStruct((n, QK_BLOCKS * IN1_COL), BF16),
        compiler_params=_params("parallel", "parallel"),
        name="inproj1_qk",
    )(h, w_bf16, cos, sin)
    vg_blocks = IN1_WIDTH // IN1_COL - QK_BLOCKS
    vg = pl.pallas_call(
        _inproj1_vg_kernel,
        grid=(vg_blocks, n // tm),
        in_specs=[h_spec, pl.BlockSpec((D_MODEL, IN1_COL), lambda j, i: (0, QK_BLOCKS + j))],
        out_specs=out_spec,
        out_shape=jax.ShapeDtypeStruct((n, vg_blocks * IN1_COL), BF16),
        compiler_params=_params("parallel", "parallel"),
        name="inproj1_vg",
    )(h, w_bf16)
    return qk, vg


def _retention_kernel(*refs, n_chunks, has_state, heads, unroll):
    if has_state:
        decay_ref, q_ref, k_ref, v_ref, gf_ref, gb_ref, sf0_ref, sb0_ref, o_ref, inc_ref, seen_ref = refs
    else:
        decay_ref, q_ref, k_ref, v_ref, gf_ref, gb_ref, o_ref, sf_ref, sb_ref, inc_ref, seen_ref = refs
    c = RET_CHUNK
    row = lax.broadcasted_iota(jnp.int32, (c, c), 0).astype(F32)
    col = lax.broadcasted_iota(jnp.int32, (c, c), 1).astype(F32)
    pos = lax.broadcasted_iota(jnp.int32, (c, 1), 0).astype(F32)

    def chunk(i):
        return pl.ds(pl.multiple_of(i * c, c), c)

    diff = [row - col, col - row]
    qk_cols = [slice(g * RET_DK, (g + 1) * RET_DK) for g in range(heads)]
    v_cols = [slice(g * RET_DV, (g + 1) * RET_DV) for g in range(heads)]
    intra, q_dec, k_dec, chunk_dec = [], [], [], []
    for g in range(heads):
        hd = pl.program_id(1) * heads + g
        lg = [-jnp.exp(jnp.full((1, 1), decay_ref[d, hd], F32)) for d in range(2)]
        intra.append([jnp.where(diff[d] >= 0, jnp.exp(lg[d] * jnp.maximum(diff[d], 0.0)), 0.0) for d in range(2)])
        q_dec.append([jnp.exp(lg[0] * (pos + 1.0)), jnp.exp(lg[1] * (c - pos))])
        k_dec.append([jnp.exp(lg[0] * ((c - 1.0) - pos)), jnp.exp(lg[1] * pos)])
        chunk_dec.append([jnp.exp(lg[d] * float(c)) for d in range(2)])

    def increments(i, carry):
        rows = chunk(i)
        for g in range(heads):
            ki = k_ref[rows, qk_cols[g]].astype(F32)
            vi = v_ref[rows, v_cols[g]]
            for d in range(2):
                inc_ref[g, d, i] = _dot_tn((ki * k_dec[g][d]).astype(BF16), vi)
        return carry

    lax.fori_loop(0, n_chunks, increments, 0, unroll=unroll)

    for g in range(heads):
        finals = []
        for d in range(2):
            def scan(t, state):
                i = t if d == 0 else n_chunks - 1 - t
                seen_ref[g, d, i] = state.astype(BF16)
                return state * chunk_dec[g][d] + inc_ref[g, d, i]

            if has_state:
                state0 = (sf0_ref if d == 0 else sb0_ref)[0, g]
            else:
                state0 = jnp.zeros((RET_DK, RET_DV), F32)
            finals.append(lax.fori_loop(0, n_chunks, scan, state0))
        if not has_state:
            sf_ref[0, g] = finals[0]
            sb_ref[0, g] = finals[1]

    def outputs(i, carry):
        rows = chunk(i)
        for g in range(heads):
            qi = q_ref[rows, qk_cols[g]]
            vi = v_ref[rows, v_cols[g]]
            s = _dot_nt(qi, k_ref[rows, qk_cols[g]])
            mixed = None
            for d, g_ref in enumerate((gf_ref, gb_ref)):
                y = _dot((s * intra[g][d]).astype(BF16), vi) + _dot(qi, seen_ref[g, d, i]) * q_dec[g][d]
                gated = g_ref[rows, v_cols[g]].astype(F32) * _rms(y)
                mixed = gated if mixed is None else mixed + gated
            o_ref[rows, v_cols[g]] = mixed.astype(o_ref.dtype)
        return carry

    lax.fori_loop(0, n_chunks, outputs, 0, unroll=unroll)


def _retention(stream, qk, vg, decays, states, heads):
    n = stream.n_tok
    t = stream.seq_len
    n_chunks = t // RET_CHUNK
    has_state = states is not None
    qk_w = heads * RET_DK
    v_w = heads * RET_DV
    k_blk = RET_QK_WIDTH // qk_w
    gf_blk = RET_V_WIDTH // v_w
    gb_blk = 2 * gf_blk
    in_specs = [
        pl.BlockSpec(memory_space=pltpu.SMEM),
        pl.BlockSpec((t, qk_w), lambda b, h: (b, h)),
        pl.BlockSpec((t, qk_w), lambda b, h: (b, k_blk + h)),
        pl.BlockSpec((t, v_w), lambda b, h: (b, h)),
        pl.BlockSpec((t, v_w), lambda b, h: (b, gf_blk + h)),
        pl.BlockSpec((t, v_w), lambda b, h: (b, gb_blk + h)),
    ]
    args = [decays, qk, qk, vg, vg, vg]
    state_spec = pl.BlockSpec((1, heads, RET_DK, RET_DV), lambda b, h: (b, h, 0, 0))
    out_specs = [pl.BlockSpec((t, v_w), lambda b, h: (b, h))]
    out_shape = [jax.ShapeDtypeStruct((n, RET_V_WIDTH), BF16)]
    if has_state:
        in_specs += [state_spec, state_spec]
        args += list(states)
    else:
        out_specs += [state_spec, state_spec]
        out_shape += [jax.ShapeDtypeStruct((stream.n_seq, RET_HEADS, RET_DK, RET_DV), F32)] * 2
    return pl.pallas_call(
        functools.partial(_retention_kernel, n_chunks=n_chunks, has_state=has_state, heads=heads,
                          unroll=min(n_chunks, 2)),
        grid=(stream.n_seq, RET_HEADS // heads),
        in_specs=in_specs,
        out_specs=out_specs,
        out_shape=out_shape,
        scratch_shapes=[pltpu.VMEM((heads, 2, n_chunks, RET_DK, RET_DV), F32),
                        pltpu.VMEM((heads, 2, n_chunks, RET_DK, RET_DV), BF16)],
        compiler_params=_params("parallel", "parallel"),
        name="retention",
    )(*args)


def _rope_tables(n_tokens, dim):
    t = jnp.arange(n_tokens)
    pos = jnp.stack([t // GRID_W, t % GRID_W]).astype(F32)
    n_freq = dim // 4
    freqs = ROPE_THETA ** (-jnp.arange(n_freq, dtype=F32) / n_freq)
    ang = pos[:, :, None] * freqs
    cos, sin = jnp.cos(ang), jnp.sin(ang)
    cos_t = jnp.concatenate([cos[0], cos[0], cos[1], cos[1]], axis=-1)
    sin_t = jnp.concatenate([-sin[0], sin[0], -sin[1], sin[1]], axis=-1)
    return cos_t, sin_t


def kernel(x_prompt, x_sample, cache_k0, cache_v0, state_ret_fwd1, state_ret_bwd1, c, c_ctx,
           norm_mix_g0, mod_w0, mod_b0, in_w0, conv_w0, conv_b0, conv_norm_g0, conv_norm_b0,
           q_norm_g0, k_norm_g0, out_w0, norm_ffn_g0,
           moe_grp_w0, moe_grp_b0, moe_rtr_w0, moe_rtr_b0, moe_w_gate0, moe_w_up0, moe_w_down0,
           norm_mix_g1, mod_w1, mod_b1, in_w1, ret_decay_fwd1, ret_decay_bwd1, out_w1, norm_ffn_g1,
           moe_grp_w1, moe_grp_b1, moe_rtr_w1, moe_rtr_b1, moe_w_gate1, moe_w_up1, moe_w_down1,
           final_norm_g):
    batch, seq, d = x_prompt.shape
    dec_batch, dec_seq, _ = x_sample.shape
    assert d == D_MODEL and 1 + dec_batch <= MOD_SLOTS
    ctx = _Stream(batch, seq, first_slot=0, rope=False)
    lat = _Stream(dec_batch, dec_seq, first_slot=1, rope=True)
    assert ctx.n_tok % ROW_TILE == 0 and lat.n_tok % ROW_TILE == 0 and ROW_CHUNKS == SUBLANES
    assert seq % CONV_TILE == 0 and dec_seq % ROW_TILE == 0 and ROW_TILE % seq == 0

    cond = jnp.concatenate([c_ctx[None, :], c, jnp.zeros((MOD_SLOTS - 1 - dec_batch, d), F32)], axis=0)
    mod0 = _modulation(cond, mod_w0, mod_b0)
    mod1 = _modulation(cond, mod_w1, mod_b1)
    cos, sin = _rope_tables(dec_seq, HEAD_DIM)
    decays = jnp.stack([ret_decay_fwd1, ret_decay_bwd1]).astype(F32)

    in_w0_b = in_w0.astype(BF16)
    out_w0_b = out_w0.astype(BF16)
    in_w1_b = in_w1.astype(BF16)
    out_w1_b = out_w1.astype(BF16)
    moe0 = (moe_w_gate0.astype(BF16), moe_w_up0.astype(BF16), moe_w_down0.astype(BF16))
    moe1 = (moe_w_gate1.astype(BF16), moe_w_up1.astype(BF16), moe_w_down1.astype(BF16))
    route0 = _router_params(moe_grp_w0, moe_grp_b0, moe_rtr_w0, moe_rtr_b0)
    route1 = _router_params(moe_grp_w1, moe_grp_b1, moe_rtr_w1, moe_rtr_b1)

    def run(stream, x, cache, states):
        x = x.reshape(stream.n_tok, d)
        kv_dtype = BF16 if stream.rope else F32
        a, q, k, v = _inproj0(stream, x, mod0, norm_mix_g0, in_w0_b, q_norm_g0, k_norm_g0, cos, sin, kv_dtype)
        a = _conv_branch(stream, a, conv_w0, conv_b0, conv_norm_g0, conv_norm_b0)
        tq = min(stream.seq_len, 128 if cache is not None else 256)
        o = _attention(stream, q, k, v, cache, tq, kv_heads=1 if cache is not None else N_KV_HEADS)
        x, *routed = _outproj_route(stream, [a, o], out_w0_b, x, mod0, norm_ffn_g0, *route0)
        x, h = _moe(stream, *routed, *moe0, x, mod0, norm_mix_g1, mod1, last_layer=False)
        qk, vg = _inproj1(stream, h, in_w1_b, cos, sin)
        ret = _retention(stream, qk, vg, decays, states, heads=2 if states is not None else 4)
        x, *routed = _outproj_route(stream, [ret[0]], out_w1_b, x, mod1, norm_ffn_g1, *route1)
        y, = _moe(stream, *routed, *moe1, x, mod1, final_norm_g, mod1, last_layer=True)
        return y.reshape(stream.n_seq, stream.seq_len, d), k, v, ret[1:]

    y_prompt, k_ctx, v_ctx, new_states = run(ctx, x_prompt, None, None)
    y_sample, _, _, _ = run(lat, x_sample, (cache_k0, cache_v0), (state_ret_fwd1, state_ret_bwd1))
    return (y_prompt, y_sample, k_ctx, v_ctx, new_states[0], new_states[1])
```

```python
import functools

import jax
import jax.numpy as jnp
from jax import lax
from jax.experimental import pallas as pl
from jax.experimental.pallas import tpu as pltpu

D_MODEL = 1024
GRID_W = 64
EPS = 1e-6
CONV_CH = 512
CONV_WIDTH = 31
CONV_HALF = CONV_WIDTH // 2
N_Q_HEADS = 8
N_KV_HEADS = 2
Q_PER_KV = N_Q_HEADS // N_KV_HEADS
HEAD_DIM = 128
ROPE_THETA = 10000.0
ATTN_WIDTH = N_Q_HEADS * HEAD_DIM
KV_WIDTH = N_KV_HEADS * HEAD_DIM
IN0_WIDTH = 2 * CONV_CH + ATTN_WIDTH + 2 * KV_WIDTH
RET_HEADS = 8
RET_DK = 128
RET_DV = 256
RET_CHUNK = 128
RET_QK_WIDTH = RET_HEADS * RET_DK
RET_V_WIDTH = RET_HEADS * RET_DV
IN1_WIDTH = 2 * RET_QK_WIDTH + 3 * RET_V_WIDTH
N_GROUPS = 4
EXPERTS_PER_GROUP = 4
N_EXPERTS = N_GROUPS * EXPERTS_PER_GROUP
EXPERT_FF = 512

LANES = 128
SUBLANES = 8
VMEM_LIMIT_BYTES = 56 * 1024 * 1024

MOD_SLOTS = 8
ROW_TILE = 512
ROW_CHUNKS = D_MODEL // LANES
PAYLOAD_CHUNKS = 2 * ROW_CHUNKS
FFN_TILE = 256
DISPATCH_TILE = 256
COMBINE_TILE = 256
ROUTE_GROUP, ROUTE_RANK = 0, 1
CONV_TILE = 256
CONV_HALO = 16
CONV_ROWS = 64
ROUTE_LANES = LANES
EXPERT_LANE0 = N_GROUPS
NEG_BIG = -1e30
LOG2_E = 1.4426950408889634
ATTN_KV_CHUNK = 512

F32 = jnp.float32
BF16 = jnp.bfloat16


def _params(*semantics):
    return pltpu.CompilerParams(dimension_semantics=semantics, vmem_limit_bytes=VMEM_LIMIT_BYTES)


def _sigmoid(x):
    return 1.0 / (1.0 + jnp.exp(-x))


def _silu(x):
    return x * _sigmoid(x)


def _rms(x):
    return x * lax.rsqrt(jnp.mean(x * x, axis=-1, keepdims=True) + EPS)


def _rope(x, cos, sin):
    lane = lax.broadcasted_iota(jnp.int32, x.shape, 1)
    take_upper = (lane % (HEAD_DIM // 2)) < (HEAD_DIM // 4)
    partner = jnp.where(take_upper, pltpu.roll(x, HEAD_DIM - HEAD_DIM // 4, 1), pltpu.roll(x, HEAD_DIM // 4, 1))
    return x * cos + partner * sin


def _dot(a, b):
    return jnp.dot(a, b, preferred_element_type=F32)


def _dot_nt(a, b):
    return lax.dot_general(a, b, (((1,), (1,)), ((), ())), preferred_element_type=F32)


def _dot_tn(a, b):
    return lax.dot_general(a, b, (((0,), (0,)), ((), ())), preferred_element_type=F32)


def _mod_kernel(c_ref, w_ref, b_ref, o_ref):
    c = c_ref[...]
    o_ref[...] = _dot(_silu(c).astype(BF16), w_ref[...].astype(BF16)) + b_ref[...]


def _modulation(cond, mod_w, mod_b):
    n_out = mod_w.shape[1]
    col = D_MODEL
    out = pl.pallas_call(
        _mod_kernel,
        grid=(n_out // col,),
        in_specs=[
            pl.BlockSpec((MOD_SLOTS, D_MODEL), lambda j: (0, 0)),
            pl.BlockSpec((D_MODEL, col), lambda j: (0, j)),
            pl.BlockSpec((1, col), lambda j: (0, j)),
        ],
        out_specs=pl.BlockSpec((MOD_SLOTS, col), lambda j: (0, j)),
        out_shape=jax.ShapeDtypeStruct((MOD_SLOTS, n_out), F32),
        compiler_params=_params("parallel"),
        name="modulation",
    )(cond, mod_w, mod_b.reshape(1, n_out))
    return out.reshape(MOD_SLOTS, 6, D_MODEL)


def _slot_map(stream):
    if stream.first_slot == 0:
        return lambda i: 0
    return lambda i: stream.first_slot + (i * ROW_TILE) // stream.seq_len


class _Stream:
    def __init__(self, n_seq, seq_len, first_slot, rope):
        self.n_seq = n_seq
        self.seq_len = seq_len
        self.first_slot = first_slot
        self.rope = rope
        self.n_tok = n_seq * seq_len


def _modulated_norm(x, g, shift, scale):
    return _rms(x) * g * (1.0 + scale) + shift


def _inproj0_kernel(x_ref, mod_ref, g_ref, w_ref, qg_ref, kg_ref, cos_ref, sin_ref,
                    a_ref, q_ref, k_ref, v_ref, *, rope):
    h = _modulated_norm(x_ref[...], g_ref[...], mod_ref[0, 0:1, :], mod_ref[0, 1:2, :])
    p = _dot(h.astype(BF16), w_ref[...])
    a_ref[...] = p[:, :CONV_CH] * _sigmoid(p[:, CONV_CH:2 * CONV_CH])
    q0 = 2 * CONV_CH
    k0 = q0 + ATTN_WIDTH
    v0 = k0 + KV_WIDTH
    cos = cos_ref[...]
    sin = sin_ref[...]
    q_scale = HEAD_DIM ** -0.5 * LOG2_E
    for hh in range(N_Q_HEADS):
        qh = _rms(p[:, q0 + hh * HEAD_DIM:q0 + (hh + 1) * HEAD_DIM]) * qg_ref[...]
        if rope:
            qh = _rope(qh, cos, sin)
        q_ref[:, hh * HEAD_DIM:(hh + 1) * HEAD_DIM] = (qh * q_scale).astype(q_ref.dtype)
    seqs, _, rows, _ = k_ref.shape
    for hh in range(N_KV_HEADS):
        kh = _rms(p[:, k0 + hh * HEAD_DIM:k0 + (hh + 1) * HEAD_DIM]) * kg_ref[...]
        if rope:
            kh = _rope(kh, cos, sin)
        vh = p[:, v0 + hh * HEAD_DIM:v0 + (hh + 1) * HEAD_DIM]
        for sq in range(seqs):
            k_ref[sq, hh] = kh[sq * rows:(sq + 1) * rows].astype(k_ref.dtype)
            v_ref[sq, hh] = vh[sq * rows:(sq + 1) * rows].astype(v_ref.dtype)


def _inproj0(stream, x, mod, norm_g, w_bf16, q_norm_g, k_norm_g, cos, sin, kv_dtype):
    n = stream.n_tok
    tiles_per_seq = max(stream.seq_len // ROW_TILE, 1)
    rope_map = (lambda i: (i % tiles_per_seq, 0)) if stream.rope else (lambda i: (0, 0))
    slot = _slot_map(stream)
    row = lambda i: (i, 0)
    fixed = lambda i: (0, 0)
    seqs_per_tile = max(ROW_TILE // stream.seq_len, 1)
    kv_spec = pl.BlockSpec((seqs_per_tile, N_KV_HEADS, ROW_TILE // seqs_per_tile, HEAD_DIM),
                           lambda i: (i // tiles_per_seq, 0, i % tiles_per_seq, 0))
    kv_shape = jax.ShapeDtypeStruct((stream.n_seq, N_KV_HEADS, stream.seq_len, HEAD_DIM), kv_dtype)
    return pl.pallas_call(
        functools.partial(_inproj0_kernel, rope=stream.rope),
        grid=(n // ROW_TILE,),
        in_specs=[
            pl.BlockSpec((ROW_TILE, D_MODEL), row),
            pl.BlockSpec((1, 6, D_MODEL), lambda i: (slot(i), 0, 0)),
            pl.BlockSpec((1, D_MODEL), fixed),
            pl.BlockSpec((D_MODEL, IN0_WIDTH), fixed),
            pl.BlockSpec((1, HEAD_DIM), fixed),
            pl.BlockSpec((1, HEAD_DIM), fixed),
            pl.BlockSpec((ROW_TILE, HEAD_DIM), rope_map),
            pl.BlockSpec((ROW_TILE, HEAD_DIM), rope_map),
        ],
        out_specs=[
            pl.BlockSpec((ROW_TILE, CONV_CH), row),
            pl.BlockSpec((ROW_TILE, ATTN_WIDTH), row),
            kv_spec,
            kv_spec,
        ],
        out_shape=[
            jax.ShapeDtypeStruct((n, CONV_CH), F32),
            jax.ShapeDtypeStruct((n, ATTN_WIDTH), BF16),
            kv_shape,
            kv_shape,
        ],
        compiler_params=_params("parallel"),
        name="inproj0",
    )(x, mod, norm_g.reshape(1, D_MODEL), w_bf16, q_norm_g.reshape(1, HEAD_DIM), k_norm_g.reshape(1, HEAD_DIM),
      cos, sin)


def _conv_kernel(prev_ref, main_ref, next_ref, w_ref, b_ref, g_ref, beta_ref, o_ref, pad_ref, acc_ref, *,
                 tiles_per_seq):
    i = pl.program_id(0)
    has_prev = (i % tiles_per_seq) != 0
    has_next = (i % tiles_per_seq) != (tiles_per_seq - 1)
    pad_ref[0:CONV_HALO, :] = jnp.where(has_prev, prev_ref[...], 0.0)
    pad_ref[CONV_HALO:CONV_HALO + CONV_TILE, :] = main_ref[...]
    pad_ref[CONV_HALO + CONV_TILE:, :] = jnp.where(has_next, next_ref[...], 0.0)
    first = CONV_HALO - CONV_HALF
    window = CONV_ROWS + SUBLANES
    for c0 in range(0, CONV_CH, LANES):
        for r0 in range(0, CONV_TILE, CONV_ROWS):
            acc = jnp.zeros((CONV_ROWS, LANES), F32)
            for phase in range(SUBLANES):
                partial = jnp.zeros((window, LANES), F32)
                for j in range(phase, CONV_WIDTH, SUBLANES):
                    base = r0 + j - phase
                    partial = partial + w_ref[j:j + 1, c0:c0 + LANES] * pad_ref[base:base + window, c0:c0 + LANES]
                acc = acc + partial[first + phase:first + phase + CONV_ROWS]
            acc_ref[r0:r0 + CONV_ROWS, c0:c0 + LANES] = acc
    a = acc_ref[...] + b_ref[...]
    mu = jnp.mean(a, axis=-1, keepdims=True)
    d = a - mu
    var = jnp.mean(d * d, axis=-1, keepdims=True)
    y = d * lax.rsqrt(var + EPS) * g_ref[...] + beta_ref[...]
    o_ref[...] = _silu(y).astype(o_ref.dtype)


def _conv_branch(stream, a, conv_w, conv_b, norm_g, norm_b):
    n = stream.n_tok
    tiles_per_seq = stream.seq_len // CONV_TILE
    halo_per_tile = CONV_TILE // CONV_HALO
    n_halo = n // CONV_HALO
    fixed = lambda i: (0, 0)
    return pl.pallas_call(
        functools.partial(_conv_kernel, tiles_per_seq=tiles_per_seq),
        grid=(n // CONV_TILE,),
        in_specs=[
            pl.BlockSpec((CONV_HALO, CONV_CH), lambda i: (jnp.maximum(i * halo_per_tile - 1, 0), 0)),
            pl.BlockSpec((CONV_TILE, CONV_CH), lambda i: (i, 0)),
            pl.BlockSpec((CONV_HALO, CONV_CH), lambda i: (jnp.minimum((i + 1) * halo_per_tile, n_halo - 1), 0)),
            pl.BlockSpec((CONV_WIDTH, CONV_CH), fixed),
            pl.BlockSpec((1, CONV_CH), fixed),
            pl.BlockSpec((1, CONV_CH), fixed),
            pl.BlockSpec((1, CONV_CH), fixed),
        ],
        out_specs=pl.BlockSpec((CONV_TILE, CONV_CH), lambda i: (i, 0)),
        out_shape=jax.ShapeDtypeStruct((n, CONV_CH), BF16),
        scratch_shapes=[
            pltpu.VMEM((CONV_TILE + 2 * CONV_HALO, CONV_CH), F32),
            pltpu.VMEM((CONV_TILE, CONV_CH), F32),
        ],
        compiler_params=_params("parallel"),
        name="conv_branch",
    )(a, a, a, conv_w, conv_b.reshape(1, CONV_CH), norm_g.reshape(1, CONV_CH), norm_b.reshape(1, CONV_CH))


def _attn_kernel(*refs, tq, past):
    if past:
        q_ref, k_ref, v_ref, kc_ref, vc_ref, o_ref, ks_ref, vs_ref = refs
    else:
        q_ref, k_ref, v_ref, o_ref, ks_ref, vs_ref = refs
    kv_heads, n_keys, _ = ks_ref.shape
    group_w = Q_PER_KV * HEAD_DIM

    @pl.when(pl.program_id(2) == 0)
    def _():
        lane = lax.broadcasted_iota(jnp.int32, (n_keys, HEAD_DIM), 1)
        for hk in range(kv_heads):
            vs_ref[hk, :, HEAD_DIM:] = jnp.where(lane == 0, 1.0, 0.0).astype(BF16)
            if past:
                ks_ref[hk, 0:past, :] = kc_ref[0, hk].astype(BF16)
                vs_ref[hk, 0:past, 0:HEAD_DIM] = vc_ref[0, hk].astype(BF16)
            ks_ref[hk, past:, :] = k_ref[0, hk].astype(BF16)
            vs_ref[hk, past:, 0:HEAD_DIM] = v_ref[0, hk].astype(BF16)

    for hk in range(kv_heads):
        q = q_ref[:, hk * group_w:(hk + 1) * group_w]
        q4 = jnp.concatenate([q[:, g * HEAD_DIM:(g + 1) * HEAD_DIM] for g in range(Q_PER_KV)], axis=0)
        rows = q4.shape[0]
        m = jnp.full((rows, 1), NEG_BIG, F32)
        acc = jnp.zeros((rows, 2 * HEAD_DIM), F32)
        for c0 in range(0, n_keys, ATTN_KV_CHUNK):
            c1 = min(c0 + ATTN_KV_CHUNK, n_keys)
            s = _dot_nt(q4, ks_ref[hk, c0:c1, :])
            m_new = jnp.maximum(m, jnp.max(s, axis=-1, keepdims=True))
            p = jnp.exp2(s - m_new).astype(BF16)
            acc = jnp.exp2(m - m_new) * acc + _dot(p, vs_ref[hk, c0:c1, :])
            m = m_new
        o = acc[:, 0:HEAD_DIM] * (1.0 / acc[:, HEAD_DIM:HEAD_DIM + 1])
        for g in range(Q_PER_KV):
            col = hk * group_w + g * HEAD_DIM
            o_ref[:, col:col + HEAD_DIM] = o[g * tq:(g + 1) * tq].astype(o_ref.dtype)


def _attention(stream, q, k, v, cache, tq, kv_heads):
    n = stream.n_tok
    t = stream.seq_len
    q_tiles = t // tq
    width = kv_heads * Q_PER_KV * HEAD_DIM
    in_specs = [
        pl.BlockSpec((tq, width), lambda b, h, i: (b * q_tiles + i, h)),
        pl.BlockSpec((1, kv_heads, t, HEAD_DIM), lambda b, h, i: (b, h, 0, 0)),
        pl.BlockSpec((1, kv_heads, t, HEAD_DIM), lambda b, h, i: (b, h, 0, 0)),
    ]
    args = [q, k, v]
    past = 0
    if cache is not None:
        past = cache[0].shape[2]
        in_specs += [pl.BlockSpec((1, kv_heads, past, HEAD_DIM), lambda b, h, i: (b, h, 0, 0))] * 2
        args += list(cache)
    return pl.pallas_call(
        functools.partial(_attn_kernel, tq=tq, past=past),
        grid=(stream.n_seq, N_KV_HEADS // kv_heads, q_tiles),
        in_specs=in_specs,
        out_specs=pl.BlockSpec((tq, width), lambda b, h, i: (b * q_tiles + i, h)),
        out_shape=jax.ShapeDtypeStruct((n, ATTN_WIDTH), BF16),
        scratch_shapes=[pltpu.VMEM((kv_heads, past + t, HEAD_DIM), BF16),
                        pltpu.VMEM((kv_heads, past + t, 2 * HEAD_DIM), BF16)],
        compiler_params=_params("parallel", "parallel", "arbitrary"),
        name="attention",
    )(*args)


def _route(logits, running):
    rows = logits.shape[0]
    lane = lax.broadcasted_iota(jnp.int32, logits.shape, 1)
    lane_f = lane.astype(F32)
    far = float(ROUTE_LANES)
    is_group = lane < N_GROUPS
    gl = jnp.where(is_group, logits, NEG_BIG)
    gmax = jnp.max(gl, axis=-1, keepdims=True)
    gsum = jnp.sum(jnp.where(is_group, jnp.exp(gl - gmax), 0.0), axis=-1, keepdims=True)
    g_w = 1.0 / gsum
    gidx = jnp.min(jnp.where(gl == gmax, lane_f, far), axis=-1, keepdims=True)
    lo = EXPERT_LANE0 + EXPERTS_PER_GROUP * gidx
    in_group = (lane_f >= lo) & (lane_f < lo + EXPERTS_PER_GROUP)
    el = jnp.where(in_group, logits, NEG_BIG)
    v1 = jnp.max(el, axis=-1, keepdims=True)
    i1 = jnp.min(jnp.where(el == v1, lane_f, far), axis=-1, keepdims=True)
    el2 = jnp.where(lane_f == i1, NEG_BIG, el)
    v2 = jnp.max(el2, axis=-1, keepdims=True)
    i2 = jnp.min(jnp.where(el2 == v2, lane_f, far), axis=-1, keepdims=True)
    e2 = jnp.exp(v2 - v1)
    w1 = g_w / (1.0 + e2)
    w2 = w1 * e2
    in_my_group = lane_f == gidx
    chosen = in_my_group.astype(BF16)
    earlier = (lax.broadcasted_iota(jnp.int32, (rows, rows), 0)
               > lax.broadcasted_iota(jnp.int32, (rows, rows), 1)).astype(BF16)
    before = running + _dot(earlier, chosen)
    rank = jnp.sum(jnp.where(in_my_group, before, 0.0), axis=-1, keepdims=True)
    record = jnp.where(lane == ROUTE_GROUP, gidx, jnp.where(lane == ROUTE_RANK, rank, 0.0))
    weights = jnp.where(lane_f == i1 - lo, w1, 0.0) + jnp.where(lane_f == i2 - lo, w2, 0.0)
    return record, weights, running + jnp.sum(chosen.astype(F32), axis=0, keepdims=True)


def _outproj_kernel(*refs, widths):
    n_in = len(widths)
    in_refs = refs[:n_in]
    w_ref, x_ref, mod_ref, g_ref, rw_ref, rb_ref, x1_ref, pay_ref, route_t_ref, count_ref, run_ref = refs[n_in:]

    @pl.when(pl.program_id(0) == 0)
    def _():
        run_ref[...] = jnp.zeros_like(run_ref)

    acc = None
    off = 0
    for r, width in zip(in_refs, widths):
        part = _dot(r[...], w_ref[off:off + width, :])
        acc = part if acc is None else acc + part
        off += width
    x1 = x_ref[...] + mod_ref[0, 2:3, :] * acc
    x1_ref[...] = x1
    h = _modulated_norm(x1, g_ref[...], mod_ref[0, 3:4, :], mod_ref[0, 4:5, :])
    h_hi = h.astype(BF16)
    h_lo = (h - h_hi.astype(F32)).astype(BF16)
    both = _dot(h_hi, rw_ref[...])
    logits = (both[:, :ROUTE_LANES] + both[:, ROUTE_LANES:] + _dot(h_lo, rw_ref[:, :ROUTE_LANES]) + rb_ref[...])
    record, weights, running = _route(logits, run_ref[...])
    route_t_ref[...] = record.T[:SUBLANES]
    run_ref[...] = running
    count_ref[...] = jnp.broadcast_to(running, count_ref.shape)
    rows = h.shape[0]
    _store_row_major(pay_ref, h, PAYLOAD_CHUNKS)
    pay_ref[pl.ds(ROW_CHUNKS, rows, stride=PAYLOAD_CHUNKS), :] = weights
    for c in range(ROW_CHUNKS + 1, PAYLOAD_CHUNKS):
        pay_ref[pl.ds(c, rows, stride=PAYLOAD_CHUNKS), :] = jnp.zeros((rows, LANES), F32)


def _store_row_major(ref, x, pitch=ROW_CHUNKS):
    rows = x.shape[0]
    for c in range(ROW_CHUNKS):
        ref[pl.ds(c, rows, stride=pitch), :] = x[:, c * LANES:(c + 1) * LANES]


def _load_row_major(ref, rows, pitch=ROW_CHUNKS):
    return jnp.concatenate([ref[pl.ds(c, rows, stride=pitch), :] for c in range(ROW_CHUNKS)], axis=-1)


def _outproj_route(stream, inputs, w_bf16, x, mod, ffn_norm_g, route_w, route_b):
    n = stream.n_tok
    widths = tuple(a.shape[1] for a in inputs)
    slot = _slot_map(stream)
    row = lambda i: (i, 0)
    fixed = lambda i: (0, 0)
    return pl.pallas_call(
        functools.partial(_outproj_kernel, widths=widths),
        grid=(n // ROW_TILE,),
        in_specs=[pl.BlockSpec((ROW_TILE, width), row) for width in widths] + [
            pl.BlockSpec((sum(widths), D_MODEL), fixed),
            pl.BlockSpec((ROW_TILE, D_MODEL), row),
            pl.BlockSpec((1, 6, D_MODEL), lambda i: (slot(i), 0, 0)),
            pl.BlockSpec((1, D_MODEL), fixed),
            pl.BlockSpec((D_MODEL, 2 * ROUTE_LANES), fixed),
            pl.BlockSpec((1, ROUTE_LANES), fixed),
        ],
        out_specs=[
            pl.BlockSpec((ROW_TILE, D_MODEL), row),
            pl.BlockSpec((ROW_TILE * PAYLOAD_CHUNKS, LANES), row),
            pl.BlockSpec((SUBLANES, ROW_TILE), lambda i: (0, i)),
            pl.BlockSpec((SUBLANES, ROUTE_LANES), fixed),
        ],
        out_shape=[
            jax.ShapeDtypeStruct((n, D_MODEL), F32),
            jax.ShapeDtypeStruct((n * PAYLOAD_CHUNKS, LANES), F32),
            jax.ShapeDtypeStruct((SUBLANES, n), F32),
            jax.ShapeDtypeStruct((SUBLANES, ROUTE_LANES), F32),
        ],
        scratch_shapes=[pltpu.VMEM((1, ROUTE_LANES), F32)],
        compiler_params=_params("arbitrary"),
        name="outproj_route",
    )(*inputs, w_bf16, x, mod, ffn_norm_g.reshape(1, D_MODEL), route_w, route_b)


def _router_params(grp_w, grp_b, rtr_w, rtr_b):
    w = jnp.concatenate([grp_w, jnp.moveaxis(rtr_w, 0, 1).reshape(D_MODEL, N_EXPERTS)], axis=1)
    w = jnp.pad(w, ((0, 0), (0, ROUTE_LANES - w.shape[1])))
    hi = w.astype(BF16)
    lo = (w - hi.astype(F32)).astype(BF16)
    b = jnp.concatenate([grp_b, rtr_b.reshape(N_EXPERTS)])
    b = jnp.pad(b, (0, ROUTE_LANES - b.shape[0])).reshape(1, ROUTE_LANES)
    return jnp.concatenate([hi, lo], axis=1), b


def _sorted_tiles(n_tok):
    return n_tok // FFN_TILE + N_GROUPS


def _dispatch_plan(route_t, counts, n_tok):
    n_tiles = _sorted_tiles(n_tok)
    count = counts[0, :N_GROUPS].astype(jnp.int32)
    tiles = (count + (FFN_TILE - 1)) // FFN_TILE
    start = (jnp.cumsum(tiles) - tiles) * FFN_TILE
    tile_end = jnp.cumsum(tiles)
    j = jnp.arange(n_tiles, dtype=jnp.int32)
    tile_group = jnp.minimum(jnp.sum(j[:, None] >= tile_end[None, :], axis=1), N_GROUPS - 1).astype(jnp.int32)
    where = jnp.concatenate([start, route_t[:2].astype(jnp.int32).reshape(-1)])
    fill = jnp.concatenate([start + count, start + tiles * FFN_TILE])
    return where, fill, tile_group, tile_end[-1:].astype(jnp.int32)


def _sorted_row(where_ref, n_tok, tok):
    return where_ref[where_ref[N_GROUPS + ROUTE_GROUP * n_tok + tok]] + where_ref[N_GROUPS + ROUTE_RANK * n_tok + tok]


def _token_rows(t, pitch=ROW_CHUNKS):
    return pl.ds(pl.multiple_of(t * pitch, pitch), pitch)


def _dispatch_kernel(where_ref, fill_ref, pay_ref, ps_ref, zero_ref, sems, *, n_tok):
    i = pl.program_id(0)
    base = i * DISPATCH_TILE

    def row_copy(t, k):
        d = _sorted_row(where_ref, n_tok, base + t)
        return pltpu.make_async_copy(pay_ref.at[_token_rows(t, PAYLOAD_CHUNKS)],
                                     ps_ref.at[_token_rows(d, PAYLOAD_CHUNKS)], sems.at[k])

    def issue(t2, carry):
        for k in range(2):
            row_copy(2 * t2 + k, k).start(priority=k)
        return carry

    lax.fori_loop(0, DISPATCH_TILE // 2, issue, 0, unroll=4)

    def fill_copy(r):
        return pltpu.make_async_copy(zero_ref.at[pl.ds(0, PAYLOAD_CHUNKS)],
                                     ps_ref.at[_token_rows(r, PAYLOAD_CHUNKS)], sems.at[2])

    def tile_fill_copy(tile):
        rows = FFN_TILE * PAYLOAD_CHUNKS
        return pltpu.make_async_copy(zero_ref, ps_ref.at[pl.ds(pl.multiple_of(tile * rows, rows), rows)], sems.at[2])

    @pl.when(i == 0)
    def _():
        zero_ref[...] = jnp.zeros_like(zero_ref)
        n_tiles = ps_ref.shape[0] // (FFN_TILE * PAYLOAD_CHUNKS)
        first_unused = fill_ref[2 * N_GROUPS - 1] // FFN_TILE
        for g in range(N_GROUPS):
            lax.fori_loop(fill_ref[g], fill_ref[N_GROUPS + g], lambda r, c: (fill_copy(r).start(), c)[1], 0)
        lax.fori_loop(first_unused, n_tiles, lambda t, c: (tile_fill_copy(t).start(), c)[1], 0)
        for g in range(N_GROUPS):
            lax.fori_loop(fill_ref[g], fill_ref[N_GROUPS + g], lambda r, c: (fill_copy(r).wait(), c)[1], 0)
        lax.fori_loop(first_unused, n_tiles, lambda t, c: (tile_fill_copy(t).wait(), c)[1], 0)

    half = DISPATCH_TILE * PAYLOAD_CHUNKS // 2
    for k in range(2):
        pltpu.make_async_copy(pay_ref.at[pl.ds(0, half)], ps_ref.at[pl.ds(0, half)], sems.at[k]).wait()


def _dispatch(n_tok, where, fill, payload):
    rows = _sorted_tiles(n_tok) * FFN_TILE
    return pl.pallas_call(
        functools.partial(_dispatch_kernel, n_tok=n_tok),
        grid_spec=pltpu.PrefetchScalarGridSpec(
            num_scalar_prefetch=2,
            grid=(n_tok // DISPATCH_TILE,),
            in_specs=[pl.BlockSpec((DISPATCH_TILE * PAYLOAD_CHUNKS, LANES), lambda i, w, f: (i, 0))],
            out_specs=pl.BlockSpec(memory_space=pl.ANY),
            scratch_shapes=[pltpu.VMEM((FFN_TILE * PAYLOAD_CHUNKS, LANES), F32), pltpu.SemaphoreType.DMA((3,))],
        ),
        out_shape=jax.ShapeDtypeStruct((rows * PAYLOAD_CHUNKS, LANES), F32),
        compiler_params=_params("arbitrary"),
        name="moe_dispatch",
    )(where, fill, payload)


def _ffn_kernel(tg_ref, used_ref, ps_ref, wg_ref, wu_ref, wd_ref, ys_ref):
    @pl.when(pl.program_id(0) >= used_ref[0])
    def _():
        ys_ref[...] = jnp.zeros_like(ys_ref)

    @pl.when(pl.program_id(0) < used_ref[0])
    def _():
        x = _load_row_major(ps_ref, FFN_TILE, PAYLOAD_CHUNKS).astype(BF16)
        w = ps_ref[pl.ds(ROW_CHUNKS, FFN_TILE, stride=PAYLOAD_CHUNKS), :]
        y = None
        for e in range(EXPERTS_PER_GROUP):
            hidden = _silu(_dot(x, wg_ref[e])) * _dot(x, wu_ref[e]) * w[:, e:e + 1]
            part = _dot(hidden.astype(BF16), wd_ref[e])
            y = part if y is None else y + part
        _store_row_major(ys_ref, y)


def _expert_ffn(n_tok, tile_group, n_used, ps, wg, wu, wd):
    rows_in = lambda j, tg, nu: (jnp.minimum(j, nu[0] - 1), 0)
    rows_out = lambda j, tg, nu: (j, 0)
    group = lambda j, tg, nu: (tg[j], 0, 0)
    n_tiles = _sorted_tiles(n_tok)
    return pl.pallas_call(
        _ffn_kernel,
        grid_spec=pltpu.PrefetchScalarGridSpec(
            num_scalar_prefetch=2,
            grid=(n_tiles,),
            in_specs=[
                pl.BlockSpec((FFN_TILE * PAYLOAD_CHUNKS, LANES), rows_in),
                pl.BlockSpec((EXPERTS_PER_GROUP, D_MODEL, EXPERT_FF), group),
                pl.BlockSpec((EXPERTS_PER_GROUP, D_MODEL, EXPERT_FF), group),
                pl.BlockSpec((EXPERTS_PER_GROUP, EXPERT_FF, D_MODEL), group),
            ],
            out_specs=pl.BlockSpec((FFN_TILE * ROW_CHUNKS, LANES), rows_out),
        ),
        out_shape=jax.ShapeDtypeStruct((n_tiles * FFN_TILE * ROW_CHUNKS, LANES), F32),
        compiler_params=_params("arbitrary"),
        name="moe_ffn",
    )(tile_group, n_used, ps, wg, wu, wd)


def _combine_kernel(where_ref, x_ref, mod_ref, ng_ref, nmod_ref, ys_ref, o_ref, *rest, n_tok, last_layer):
    if last_layer:
        y_buf, sems = rest
    else:
        h_ref, y_buf, sems = rest
    i = pl.program_id(0)
    n_tiles = pl.num_programs(0)

    def row_copy(tile, t, k):
        slot = tile % 2
        d = _sorted_row(where_ref, n_tok, tile * COMBINE_TILE + t)
        return pltpu.make_async_copy(ys_ref.at[_token_rows(d)], y_buf.at[slot, _token_rows(t)], sems.at[slot, k])

    def start_tile(tile):
        def issue(t2, carry):
            for k in range(2):
                row_copy(tile, 2 * t2 + k, k).start(priority=k)
            return carry

        lax.fori_loop(0, COMBINE_TILE // 2, issue, 0, unroll=4)

    @pl.when(i == 0)
    def _():
        start_tile(i)

    @pl.when(i + 1 < n_tiles)
    def _():
        start_tile(i + 1)

    slot = i % 2
    half = COMBINE_TILE * ROW_CHUNKS // 2
    for k in range(2):
        pltpu.make_async_copy(ys_ref.at[pl.ds(0, half)], y_buf.at[slot, pl.ds(0, half)], sems.at[slot, k]).wait()
    x = x_ref[...] + mod_ref[0, 5:6, :] * _load_row_major(y_buf.at[slot], COMBINE_TILE)
    if last_layer:
        o_ref[...] = _rms(x) * ng_ref[...]
    else:
        o_ref[...] = x
        h_ref[...] = _modulated_norm(x, ng_ref[...], nmod_ref[0, 0:1, :], nmod_ref[0, 1:2, :]).astype(h_ref.dtype)


def _combine(stream, where, ys, x, mod, norm_g, next_mod, last_layer):
    n = stream.n_tok
    first_slot, seq_len = stream.first_slot, stream.seq_len
    slot = (lambda i: 0) if first_slot == 0 else (lambda i: first_slot + (i * COMBINE_TILE) // seq_len)
    row = lambda i, w: (i, 0)
    out_specs = [pl.BlockSpec((COMBINE_TILE, D_MODEL), row)]
    out_shape = [jax.ShapeDtypeStruct((n, D_MODEL), F32)]
    if not last_layer:
        out_specs.append(pl.BlockSpec((COMBINE_TILE, D_MODEL), row))
        out_shape.append(jax.ShapeDtypeStruct((n, D_MODEL), BF16))
    return pl.pallas_call(
        functools.partial(_combine_kernel, n_tok=n, last_layer=last_layer),
        grid_spec=pltpu.PrefetchScalarGridSpec(
            num_scalar_prefetch=1,
            grid=(n // COMBINE_TILE,),
            in_specs=[
                pl.BlockSpec((COMBINE_TILE, D_MODEL), row),
                pl.BlockSpec((1, 6, D_MODEL), lambda i, w: (slot(i), 0, 0)),
                pl.BlockSpec((1, D_MODEL), lambda i, w: (0, 0)),
                pl.BlockSpec((1, 6, D_MODEL), lambda i, w: (slot(i), 0, 0)),
                pl.BlockSpec(memory_space=pl.ANY),
            ],
            out_specs=out_specs,
            scratch_shapes=[pltpu.VMEM((2, COMBINE_TILE * ROW_CHUNKS, LANES), F32),
                            pltpu.SemaphoreType.DMA((2, 2))],
        ),
        out_shape=out_shape,
        compiler_params=_params("arbitrary"),
        name="moe_combine",
    )(where, x, mod, norm_g.reshape(1, D_MODEL), next_mod, ys)


def _moe(stream, payload, route_t, counts, wg, wu, wd, x, mod, norm_g, next_mod, last_layer):
    n = stream.n_tok
    where, fill, tile_group, n_used = _dispatch_plan(route_t, counts, n)
    ps = _dispatch(n, where, fill, payload)
    ys = _expert_ffn(n, tile_group, n_used, ps, wg, wu, wd)
    return _combine(stream, where, ys, x, mod, norm_g, next_mod, last_layer)


IN1_COL = 1024
IN1_ROW_TILE = 1024
QK_BLOCKS = 2 * RET_QK_WIDTH // IN1_COL
Q_BLOCKS = RET_QK_WIDTH // IN1_COL
V_BLOCKS = RET_V_WIDTH // IN1_COL


def _inproj1_qk_kernel(h_ref, w_ref, cos_ref, sin_ref, o_ref, *, rope):
    p = _dot(h_ref[...], w_ref[...])
    scale = jnp.where(pl.program_id(0) < Q_BLOCKS, 1.0, RET_DK ** -0.5)
    cos = cos_ref[...]
    sin = sin_ref[...]
    for hh in range(IN1_COL // RET_DK):
        ph = p[:, hh * RET_DK:(hh + 1) * RET_DK]
        if rope:
            ph = _rope(ph, cos, sin)
        o_ref[:, hh * RET_DK:(hh + 1) * RET_DK] = (ph * scale).astype(o_ref.dtype)


def _inproj1_vg_kernel(h_ref, w_ref, o_ref):
    p = _dot(h_ref[...], w_ref[...])
    o_ref[...] = jnp.where(pl.program_id(0) < V_BLOCKS, p, _silu(p)).astype(o_ref.dtype)


def _inproj1(stream, h, w_bf16, cos, sin):
    n = stream.n_tok
    tm = IN1_ROW_TILE
    tiles_per_seq = max(stream.seq_len // tm, 1)
    rope_map = (lambda j, i: (i % tiles_per_seq, 0)) if stream.rope else (lambda j, i: (0, 0))
    h_spec = pl.BlockSpec((tm, D_MODEL), lambda j, i: (i, 0))
    out_spec = pl.BlockSpec((tm, IN1_COL), lambda j, i: (i, j))
    qk = pl.pallas_call(
        functools.partial(_inproj1_qk_kernel, rope=stream.rope),
        grid=(QK_BLOCKS, n // tm),
        in_specs=[
            h_spec,
            pl.BlockSpec((D_MODEL, IN1_COL), lambda j, i: (0, j)),
            pl.BlockSpec((tm, RET_DK), rope_map),
            pl.BlockSpec((tm, RET_DK), rope_map),
        ],
        out_specs=out_spec,
        out_shape=jax.ShapeDtypeStruct((n, QK_BLOCKS * IN1_COL), BF16),
        compiler_params=_params("parallel", "parallel"),
        name="inproj1_qk",
    )(h, w_bf16, cos, sin)
    vg_blocks = IN1_WIDTH // IN1_COL - QK_BLOCKS
    vg = pl.pallas_call(
        _inproj1_vg_kernel,
        grid=(vg_blocks, n // tm),
        in_specs=[h_spec, pl.BlockSpec((D_MODEL, IN1_COL), lambda j, i: (0, QK_BLOCKS + j))],
        out_specs=out_spec,
        out_shape=jax.ShapeDtypeStruct((n, vg_blocks * IN1_COL), BF16),
        compiler_params=_params("parallel", "parallel"),
        name="inproj1_vg",
    )(h, w_bf16)
    return qk, vg


def _retention_kernel(*refs, n_chunks, has_state, heads, unroll):
    if has_state:
        decay_ref, q_ref, k_ref, v_ref, gf_ref, gb_ref, sf0_ref, sb0_ref, o_ref, inc_ref, seen_ref = refs
    else:
        decay_ref, q_ref, k_ref, v_ref, gf_ref, gb_ref, o_ref, sf_ref, sb_ref, inc_ref, seen_ref = refs
    c = RET_CHUNK
    row = lax.broadcasted_iota(jnp.int32, (c, c), 0).astype(F32)
    col = lax.broadcasted_iota(jnp.int32, (c, c), 1).astype(F32)
    pos = lax.broadcasted_iota(jnp.int32, (c, 1), 0).astype(F32)

    def chunk(i):
        return pl.ds(pl.multiple_of(i * c, c), c)

    diff = [row - col, col - row]
    qk_cols = [slice(g * RET_DK, (g + 1) * RET_DK) for g in range(heads)]
    v_cols = [slice(g * RET_DV, (g + 1) * RET_DV) for g in range(heads)]
    intra, q_dec, k_dec, chunk_dec = [], [], [], []
    for g in range(heads):
        hd = pl.program_id(1) * heads + g
        lg = [-jnp.exp(jnp.full((1, 1), decay_ref[d, hd], F32)) for d in range(2)]
        intra.append([jnp.where(diff[d] >= 0, jnp.exp(lg[d] * jnp.maximum(diff[d], 0.0)), 0.0) for d in range(2)])
        q_dec.append([jnp.exp(lg[0] * (pos + 1.0)), jnp.exp(lg[1] * (c - pos))])
        k_dec.append([jnp.exp(lg[0] * ((c - 1.0) - pos)), jnp.exp(lg[1] * pos)])
        chunk_dec.append([jnp.exp(lg[d] * float(c)) for d in range(2)])

    def increments(i, carry):
        rows = chunk(i)
        for g in range(heads):
            ki = k_ref[rows, qk_cols[g]].astype(F32)
            vi = v_ref[rows, v_cols[g]]
            for d in range(2):
                inc_ref[g, d, i] = _dot_tn((ki * k_dec[g][d]).astype(BF16), vi)
        return carry

    lax.fori_loop(0, n_chunks, increments, 0, unroll=unroll)

    for g in range(heads):
        finals = []
        for d in range(2):
            def scan(t, state):
                i = t if d == 0 else n_chunks - 1 - t
                seen_ref[g, d, i] = state.astype(BF16)
                return state * chunk_dec[g][d] + inc_ref[g, d, i]

            if has_state:
                state0 = (sf0_ref if d == 0 else sb0_ref)[0, g]
            else:
                state0 = jnp.zeros((RET_DK, RET_DV), F32)
            finals.append(lax.fori_loop(0, n_chunks, scan, state0))
        if not has_state:
            sf_ref[0, g] = finals[0]
            sb_ref[0, g] = finals[1]

    def outputs(i, carry):
        rows = chunk(i)
        for g in range(heads):
            qi = q_ref[rows, qk_cols[g]]
            vi = v_ref[rows, v_cols[g]]
            s = _dot_nt(qi, k_ref[rows, qk_cols[g]])
            mixed = None
            for d, g_ref in enumerate((gf_ref, gb_ref)):
                y = _dot((s * intra[g][d]).astype(BF16), vi) + _dot(qi, seen_ref[g, d, i]) * q_dec[g][d]
                gated = g_ref[rows, v_cols[g]].astype(F32) * _rms(y)
                mixed = gated if mixed is None else mixed + gated
            o_ref[rows, v_cols[g]] = mixed.astype(o_ref.dtype)
        return carry

    lax.fori_loop(0, n_chunks, outputs, 0, unroll=unroll)


def _retention(stream, qk, vg, decays, states, heads):
    n = stream.n_tok
    t = stream.seq_len
    n_chunks = t // RET_CHUNK
    has_state = states is not None
    qk_w = heads * RET_DK
    v_w = heads * RET_DV
    k_blk = RET_QK_WIDTH // qk_w
    gf_blk = RET_V_WIDTH // v_w
    gb_blk = 2 * gf_blk
    in_specs = [
        pl.BlockSpec(memory_space=pltpu.SMEM),
        pl.BlockSpec((t, qk_w), lambda b, h: (b, h)),
        pl.BlockSpec((t, qk_w), lambda b, h: (b, k_blk + h)),
        pl.BlockSpec((t, v_w), lambda b, h: (b, h)),
        pl.BlockSpec((t, v_w), lambda b, h: (b, gf_blk + h)),
        pl.BlockSpec((t, v_w), lambda b, h: (b, gb_blk + h)),
    ]
    args = [decays, qk, qk, vg, vg, vg]
    state_spec = pl.BlockSpec((1, heads, RET_DK, RET_DV), lambda b, h: (b, h, 0, 0))
    out_specs = [pl.BlockSpec((t, v_w), lambda b, h: (b, h))]
    out_shape = [jax.ShapeDtypeStruct((n, RET_V_WIDTH), BF16)]
    if has_state:
        in_specs += [state_spec, state_spec]
        args += list(states)
    else:
        out_specs += [state_spec, state_spec]
        out_shape += [jax.ShapeDtypeStruct((stream.n_seq, RET_HEADS, RET_DK, RET_DV), F32)] * 2
    return pl.pallas_call(
        functools.partial(_retention_kernel, n_chunks=n_chunks, has_state=has_state, heads=heads,
                          unroll=min(n_chunks, 2)),
        grid=(stream.n_seq, RET_HEADS // heads),
        in_specs=in_specs,
        out_specs=out_specs,
        out_shape=out_shape,
        scratch_shapes=[pltpu.VMEM((heads, 2, n_chunks, RET_DK, RET_DV), F32),
                        pltpu.VMEM((heads, 2, n_chunks, RET_DK, RET_DV), BF16)],
        compiler_params=_params("parallel", "parallel"),
        name="retention",
    )(*args)


def _rope_tables(n_tokens, dim):
    t = jnp.arange(n_tokens)
    pos = jnp.stack([t // GRID_W, t % GRID_W]).astype(F32)
    n_freq = dim // 4
    freqs = ROPE_THETA ** (-jnp.arange(n_freq, dtype=F32) / n_freq)
    ang = pos[:, :, None] * freqs
    cos, sin = jnp.cos(ang), jnp.sin(ang)
    cos_t = jnp.concatenate([cos[0], cos[0], cos[1], cos[1]], axis=-1)
    sin_t = jnp.concatenate([-sin[0], sin[0], -sin[1], sin[1]], axis=-1)
    return cos_t, sin_t


def kernel(x_prompt, x_sample, cache_k0, cache_v0, state_ret_fwd1, state_ret_bwd1, c, c_ctx,
           norm_mix_g0, mod_w0, mod_b0, in_w0, conv_w0, conv_b0, conv_norm_g0, conv_norm_b0,
           q_norm_g0, k_norm_g0, out_w0, norm_ffn_g0,
           moe_grp_w0, moe_grp_b0, moe_rtr_w0, moe_rtr_b0, moe_w_gate0, moe_w_up0, moe_w_down0,
           norm_mix_g1, mod_w1, mod_b1, in_w1, ret_decay_fwd1, ret_decay_bwd1, out_w1, norm_ffn_g1,
           moe_grp_w1, moe_grp_b1, moe_rtr_w1, moe_rtr_b1, moe_w_gate1, moe_w_up1, moe_w_down1,
           final_norm_g):
    batch, seq, d = x_prompt.shape
    dec_batch, dec_seq, _ = x_sample.shape
    assert d == D_MODEL and 1 + dec_batch <= MOD_SLOTS
    ctx = _Stream(batch, seq, first_slot=0, rope=False)
    lat = _Stream(dec_batch, dec_seq, first_slot=1, rope=True)
    assert ctx.n_tok % ROW_TILE == 0 and lat.n_tok % ROW_TILE == 0 and ROW_CHUNKS == SUBLANES
    assert seq % CONV_TILE == 0 and dec_seq % ROW_TILE == 0 and ROW_TILE % seq == 0

    cond = jnp.concatenate([c_ctx[None, :], c, jnp.zeros((MOD_SLOTS - 1 - dec_batch, d), F32)], axis=0)
    mod0 = _modulation(cond, mod_w0, mod_b0)
    mod1 = _modulation(cond, mod_w1, mod_b1)
    cos, sin = _rope_tables(dec_seq, HEAD_DIM)
    decays = jnp.stack([ret_decay_fwd1, ret_decay_bwd1]).astype(F32)

    in_w0_b = in_w0.astype(BF16)
    out_w0_b = out_w0.astype(BF16)
    in_w1_b = in_w1.astype(BF16)
    out_w1_b = out_w1.astype(BF16)
    moe0 = (moe_w_gate0.astype(BF16), moe_w_up0.astype(BF16), moe_w_down0.astype(BF16))
    moe1 = (moe_w_gate1.astype(BF16), moe_w_up1.astype(BF16), moe_w_down1.astype(BF16))
    route0 = _router_params(moe_grp_w0, moe_grp_b0, moe_rtr_w0, moe_rtr_b0)
    route1 = _router_params(moe_grp_w1, moe_grp_b1, moe_rtr_w1, moe_rtr_b1)

    def run(stream, x, cache, states):
        x = x.reshape(stream.n_tok, d)
        kv_dtype = BF16 if stream.rope else F32
        a, q, k, v = _inproj0(stream, x, mod0, norm_mix_g0, in_w0_b, q_norm_g0, k_norm_g0, cos, sin, kv_dtype)
        a = _conv_branch(stream, a, conv_w0, conv_b0, conv_norm_g0, conv_norm_b0)
        tq = min(stream.seq_len, 128 if cache is not None else 256)
        o = _attention(stream, q, k, v, cache, tq, kv_heads=1 if cache is not None else N_KV_HEADS)
        x, *routed = _outproj_route(stream, [a, o], out_w0_b, x, mod0, norm_ffn_g0, *route0)
        x, h = _moe(stream, *routed, *moe0, x, mod0, norm_mix_g1, mod1, last_layer=False)
        qk, vg = _inproj1(stream, h, in_w1_b, cos, sin)
        ret = _retention(stream, qk, vg, decays, states, heads=2 if states is not None else 4)
        x, *routed = _outproj_route(stream, [ret[0]], out_w1_b, x, mod1, norm_ffn_g1, *route1)
        y, = _moe(stream, *routed, *moe1, x, mod1, final_norm_g, mod1, last_layer=True)
        return y.reshape(stream.n_seq, stream.seq_len, d), k, v, ret[1:]

    y_prompt, k_ctx, v_ctx, new_states = run(ctx, x_prompt, None, None)
    y_sample, _, _, _ = run(lat, x_sample, (cache_k0, cache_v0), (state_ret_fwd1, state_ret_bwd1))
    return (y_prompt, y_sample, k_ctx, v_ctx, new_states[0], new_states[1])
```

```python
import functools

import jax
import jax.numpy as jnp
from jax import lax
from jax.experimental import pallas as pl
from jax.experimental.pallas import tpu as pltpu

D_MODEL = 1024
GRID_W = 64
EPS = 1e-6
CONV_CH = 512
CONV_WIDTH = 31
CONV_HALF = CONV_WIDTH // 2
N_Q_HEADS = 8
N_KV_HEADS = 2
Q_PER_KV = N_Q_HEADS // N_KV_HEADS
HEAD_DIM = 128
ROPE_THETA = 10000.0
ATTN_WIDTH = N_Q_HEADS * HEAD_DIM
KV_WIDTH = N_KV_HEADS * HEAD_DIM
IN0_WIDTH = 2 * CONV_CH + ATTN_WIDTH + 2 * KV_WIDTH
RET_HEADS = 8
RET_DK = 128
RET_DV = 256
RET_CHUNK = 128
RET_QK_WIDTH = RET_HEADS * RET_DK
RET_V_WIDTH = RET_HEADS * RET_DV
IN1_WIDTH = 2 * RET_QK_WIDTH + 3 * RET_V_WIDTH
N_GROUPS = 4
EXPERTS_PER_GROUP = 4
N_EXPERTS = N_GROUPS * EXPERTS_PER_GROUP
EXPERT_FF = 512

LANES = 128
SUBLANES = 8
VMEM_LIMIT_BYTES = 56 * 1024 * 1024

MOD_SLOTS = 8
ROW_TILE = 512
ROW_CHUNKS = D_MODEL // LANES
PAYLOAD_CHUNKS = 2 * ROW_CHUNKS
FFN_TILE = 256
COMBINE_TILE = 256
ROUTE_GROUP, ROUTE_RANK = 0, 1
CONV_TILE = 256
CONV_HALO = 16
CONV_ROWS = 64
ROUTE_LANES = LANES
EXPERT_LANE0 = N_GROUPS
NEG_BIG = -1e30
LOG2_E = 1.4426950408889634
ATTN_KV_CHUNK = 512

F32 = jnp.float32
BF16 = jnp.bfloat16


def _params(*semantics):
    return pltpu.CompilerParams(dimension_semantics=semantics, vmem_limit_bytes=VMEM_LIMIT_BYTES)


def _sigmoid(x):
    return 1.0 / (1.0 + jnp.exp(-x))


def _silu(x):
    return x * _sigmoid(x)


def _rms(x):
    return x * lax.rsqrt(jnp.mean(x * x, axis=-1, keepdims=True) + EPS)


def _rope(x, cos, sin):
    lane = lax.broadcasted_iota(jnp.int32, x.shape, 1)
    take_upper = (lane % (HEAD_DIM // 2)) < (HEAD_DIM // 4)
    partner = jnp.where(take_upper, pltpu.roll(x, HEAD_DIM - HEAD_DIM // 4, 1), pltpu.roll(x, HEAD_DIM // 4, 1))
    return x * cos + partner * sin


def _dot(a, b):
    return jnp.dot(a, b, preferred_element_type=F32)


def _dot_nt(a, b):
    return lax.dot_general(a, b, (((1,), (1,)), ((), ())), preferred_element_type=F32)


def _dot_tn(a, b):
    return lax.dot_general(a, b, (((0,), (0,)), ((), ())), preferred_element_type=F32)


def _mod_kernel(c_ref, w_ref, b_ref, o_ref):
    c = c_ref[...]
    o_ref[...] = _dot(_silu(c).astype(BF16), w_ref[...].astype(BF16)) + b_ref[...]


def _modulation(cond, mod_w, mod_b):
    n_out = mod_w.shape[1]
    col = D_MODEL
    out = pl.pallas_call(
        _mod_kernel,
        grid=(n_out // col,),
        in_specs=[
            pl.BlockSpec((MOD_SLOTS, D_MODEL), lambda j: (0, 0)),
            pl.BlockSpec((D_MODEL, col), lambda j: (0, j)),
            pl.BlockSpec((1, col), lambda j: (0, j)),
        ],
        out_specs=pl.BlockSpec((MOD_SLOTS, col), lambda j: (0, j)),
        out_shape=jax.ShapeDtypeStruct((MOD_SLOTS, n_out), F32),
        compiler_params=_params("parallel"),
        name="modulation",
    )(cond, mod_w, mod_b.reshape(1, n_out))
    return out.reshape(MOD_SLOTS, 6, D_MODEL)


def _slot_map(stream):
    if stream.first_slot == 0:
        return lambda i: 0
    return lambda i: stream.first_slot + (i * ROW_TILE) // stream.seq_len


class _Stream:
    def __init__(self, n_seq, seq_len, first_slot, rope):
        self.n_seq = n_seq
        self.seq_len = seq_len
        self.first_slot = first_slot
        self.rope = rope
        self.n_tok = n_seq * seq_len


def _modulated_norm(x, g, shift, scale):
    return _rms(x) * g * (1.0 + scale) + shift


def _inproj0_kernel(x_ref, mod_ref, g_ref, w_ref, qg_ref, kg_ref, cos_ref, sin_ref,
                    a_ref, q_ref, k_ref, v_ref, *, rope):
    h = _modulated_norm(x_ref[...], g_ref[...], mod_ref[0, 0:1, :], mod_ref[0, 1:2, :])
    p = _dot(h.astype(BF16), w_ref[...])
    a_ref[...] = p[:, :CONV_CH] * _sigmoid(p[:, CONV_CH:2 * CONV_CH])
    q0 = 2 * CONV_CH
    k0 = q0 + ATTN_WIDTH
    v0 = k0 + KV_WIDTH
    cos = cos_ref[...]
    sin = sin_ref[...]
    q_scale = HEAD_DIM ** -0.5 * LOG2_E
    for hh in range(N_Q_HEADS):
        qh = _rms(p[:, q0 + hh * HEAD_DIM:q0 + (hh + 1) * HEAD_DIM]) * qg_ref[...]
        if rope:
            qh = _rope(qh, cos, sin)
        q_ref[:, hh * HEAD_DIM:(hh + 1) * HEAD_DIM] = (qh * q_scale).astype(q_ref.dtype)
    seqs, _, rows, _ = k_ref.shape
    for hh in range(N_KV_HEADS):
        kh = _rms(p[:, k0 + hh * HEAD_DIM:k0 + (hh + 1) * HEAD_DIM]) * kg_ref[...]
        if rope:
            kh = _rope(kh, cos, sin)
        vh = p[:, v0 + hh * HEAD_DIM:v0 + (hh + 1) * HEAD_DIM]
        for sq in range(seqs):
            k_ref[sq, hh] = kh[sq * rows:(sq + 1) * rows].astype(k_ref.dtype)
            v_ref[sq, hh] = vh[sq * rows:(sq + 1) * rows].astype(v_ref.dtype)


def _inproj0(stream, x, mod, norm_g, w_bf16, q_norm_g, k_norm_g, cos, sin, kv_dtype):
    n = stream.n_tok
    tiles_per_seq = max(stream.seq_len // ROW_TILE, 1)
    rope_map = (lambda i: (i % tiles_per_seq, 0)) if stream.rope else (lambda i: (0, 0))
    slot = _slot_map(stream)
    row = lambda i: (i, 0)
    fixed = lambda i: (0, 0)
    seqs_per_tile = max(ROW_TILE // stream.seq_len, 1)
    kv_spec = pl.BlockSpec((seqs_per_tile, N_KV_HEADS, ROW_TILE // seqs_per_tile, HEAD_DIM),
                           lambda i: (i // tiles_per_seq, 0, i % tiles_per_seq, 0))
    kv_shape = jax.ShapeDtypeStruct((stream.n_seq, N_KV_HEADS, stream.seq_len, HEAD_DIM), kv_dtype)
    return pl.pallas_call(
        functools.partial(_inproj0_kernel, rope=stream.rope),
        grid=(n // ROW_TILE,),
        in_specs=[
            pl.BlockSpec((ROW_TILE, D_MODEL), row),
            pl.BlockSpec((1, 6, D_MODEL), lambda i: (slot(i), 0, 0)),
            pl.BlockSpec((1, D_MODEL), fixed),
            pl.BlockSpec((D_MODEL, IN0_WIDTH), fixed),
            pl.BlockSpec((1, HEAD_DIM), fixed),
            pl.BlockSpec((1, HEAD_DIM), fixed),
            pl.BlockSpec((ROW_TILE, HEAD_DIM), rope_map),
            pl.BlockSpec((ROW_TILE, HEAD_DIM), rope_map),
        ],
        out_specs=[
            pl.BlockSpec((ROW_TILE, CONV_CH), row),
            pl.BlockSpec((ROW_TILE, ATTN_WIDTH), row),
            kv_spec,
            kv_spec,
        ],
        out_shape=[
            jax.ShapeDtypeStruct((n, CONV_CH), F32),
            jax.ShapeDtypeStruct((n, ATTN_WIDTH), BF16),
            kv_shape,
            kv_shape,
        ],
        compiler_params=_params("parallel"),
        name="inproj0",
    )(x, mod, norm_g.reshape(1, D_MODEL), w_bf16, q_norm_g.reshape(1, HEAD_DIM), k_norm_g.reshape(1, HEAD_DIM),
      cos, sin)


def _conv_kernel(prev_ref, main_ref, next_ref, w_ref, b_ref, g_ref, beta_ref, o_ref, pad_ref, acc_ref, *,
                 tiles_per_seq):
    i = pl.program_id(0)
    has_prev = (i % tiles_per_seq) != 0
    has_next = (i % tiles_per_seq) != (tiles_per_seq - 1)
    pad_ref[0:CONV_HALO, :] = jnp.where(has_prev, prev_ref[...], 0.0)
    pad_ref[CONV_HALO:CONV_HALO + CONV_TILE, :] = main_ref[...]
    pad_ref[CONV_HALO + CONV_TILE:, :] = jnp.where(has_next, next_ref[...], 0.0)
    first = CONV_HALO - CONV_HALF
    window = CONV_ROWS + SUBLANES
    for c0 in range(0, CONV_CH, LANES):
        for r0 in range(0, CONV_TILE, CONV_ROWS):
            acc = jnp.zeros((CONV_ROWS, LANES), F32)
            for phase in range(SUBLANES):
                partial = jnp.zeros((window, LANES), F32)
                for j in range(phase, CONV_WIDTH, SUBLANES):
                    base = r0 + j - phase
                    partial = partial + w_ref[j:j + 1, c0:c0 + LANES] * pad_ref[base:base + window, c0:c0 + LANES]
                acc = acc + partial[first + phase:first + phase + CONV_ROWS]
            acc_ref[r0:r0 + CONV_ROWS, c0:c0 + LANES] = acc
    a = acc_ref[...] + b_ref[...]
    mu = jnp.mean(a, axis=-1, keepdims=True)
    d = a - mu
    var = jnp.mean(d * d, axis=-1, keepdims=True)
    y = d * lax.rsqrt(var + EPS) * g_ref[...] + beta_ref[...]
    o_ref[...] = _silu(y).astype(o_ref.dtype)


def _conv_branch(stream, a, conv_w, conv_b, norm_g, norm_b):
    n = stream.n_tok
    tiles_per_seq = stream.seq_len // CONV_TILE
    halo_per_tile = CONV_TILE // CONV_HALO
    n_halo = n // CONV_HALO
    fixed = lambda i: (0, 0)
    return pl.pallas_call(
        functools.partial(_conv_kernel, tiles_per_seq=tiles_per_seq),
        grid=(n // CONV_TILE,),
        in_specs=[
            pl.BlockSpec((CONV_HALO, CONV_CH), lambda i: (jnp.maximum(i * halo_per_tile - 1, 0), 0)),
            pl.BlockSpec((CONV_TILE, CONV_CH), lambda i: (i, 0)),
            pl.BlockSpec((CONV_HALO, CONV_CH), lambda i: (jnp.minimum((i + 1) * halo_per_tile, n_halo - 1), 0)),
            pl.BlockSpec((CONV_WIDTH, CONV_CH), fixed),
            pl.BlockSpec((1, CONV_CH), fixed),
            pl.BlockSpec((1, CONV_CH), fixed),
            pl.BlockSpec((1, CONV_CH), fixed),
        ],
        out_specs=pl.BlockSpec((CONV_TILE, CONV_CH), lambda i: (i, 0)),
        out_shape=jax.ShapeDtypeStruct((n, CONV_CH), BF16),
        scratch_shapes=[
            pltpu.VMEM((CONV_TILE + 2 * CONV_HALO, CONV_CH), F32),
            pltpu.VMEM((CONV_TILE, CONV_CH), F32),
        ],
        compiler_params=_params("parallel"),
        name="conv_branch",
    )(a, a, a, conv_w, conv_b.reshape(1, CONV_CH), norm_g.reshape(1, CONV_CH), norm_b.reshape(1, CONV_CH))


def _attn_kernel(*refs, tq, past):
    if past:
        q_ref, k_ref, v_ref, kc_ref, vc_ref, o_ref, ks_ref, vs_ref = refs
    else:
        q_ref, k_ref, v_ref, o_ref, ks_ref, vs_ref = refs
    kv_heads, n_keys, _ = ks_ref.shape
    group_w = Q_PER_KV * HEAD_DIM

    @pl.when(pl.program_id(2) == 0)
    def _():
        lane = lax.broadcasted_iota(jnp.int32, (n_keys, HEAD_DIM), 1)
        for hk in range(kv_heads):
            vs_ref[hk, :, HEAD_DIM:] = jnp.where(lane == 0, 1.0, 0.0).astype(BF16)
            if past:
                ks_ref[hk, 0:past, :] = kc_ref[0, hk].astype(BF16)
                vs_ref[hk, 0:past, 0:HEAD_DIM] = vc_ref[0, hk].astype(BF16)
            ks_ref[hk, past:, :] = k_ref[0, hk].astype(BF16)
            vs_ref[hk, past:, 0:HEAD_DIM] = v_ref[0, hk].astype(BF16)

    for hk in range(kv_heads):
        q = q_ref[:, hk * group_w:(hk + 1) * group_w]
        q4 = jnp.concatenate([q[:, g * HEAD_DIM:(g + 1) * HEAD_DIM] for g in range(Q_PER_KV)], axis=0)
        rows = q4.shape[0]
        m = jnp.full((rows, 1), NEG_BIG, F32)
        acc = jnp.zeros((rows, 2 * HEAD_DIM), F32)
        for c0 in range(0, n_keys, ATTN_KV_CHUNK):
            c1 = min(c0 + ATTN_KV_CHUNK, n_keys)
            s = _dot_nt(q4, ks_ref[hk, c0:c1, :])
            m_new = jnp.maximum(m, jnp.max(s, axis=-1, keepdims=True))
            p = jnp.exp2(s - m_new).astype(BF16)
            acc = jnp.exp2(m - m_new) * acc + _dot(p, vs_ref[hk, c0:c1, :])
            m = m_new
        o = acc[:, 0:HEAD_DIM] * (1.0 / acc[:, HEAD_DIM:HEAD_DIM + 1])
        for g in range(Q_PER_KV):
            col = hk * group_w + g * HEAD_DIM
            o_ref[:, col:col + HEAD_DIM] = o[g * tq:(g + 1) * tq].astype(o_ref.dtype)


def _attention(stream, q, k, v, cache, tq, kv_heads):
    n = stream.n_tok
    t = stream.seq_len
    q_tiles = t // tq
    width = kv_heads * Q_PER_KV * HEAD_DIM
    in_specs = [
        pl.BlockSpec((tq, width), lambda b, h, i: (b * q_tiles + i, h)),
        pl.BlockSpec((1, kv_heads, t, HEAD_DIM), lambda b, h, i: (b, h, 0, 0)),
        pl.BlockSpec((1, kv_heads, t, HEAD_DIM), lambda b, h, i: (b, h, 0, 0)),
    ]
    args = [q, k, v]
    past = 0
    if cache is not None:
        past = cache[0].shape[2]
        in_specs += [pl.BlockSpec((1, kv_heads, past, HEAD_DIM), lambda b, h, i: (b, h, 0, 0))] * 2
        args += list(cache)
    return pl.pallas_call(
        functools.partial(_attn_kernel, tq=tq, past=past),
        grid=(stream.n_seq, N_KV_HEADS // kv_heads, q_tiles),
        in_specs=in_specs,
        out_specs=pl.BlockSpec((tq, width), lambda b, h, i: (b * q_tiles + i, h)),
        out_shape=jax.ShapeDtypeStruct((n, ATTN_WIDTH), BF16),
        scratch_shapes=[pltpu.VMEM((kv_heads, past + t, HEAD_DIM), BF16),
                        pltpu.VMEM((kv_heads, past + t, 2 * HEAD_DIM), BF16)],
        compiler_params=_params("parallel", "parallel", "arbitrary"),
        name="attention",
    )(*args)


def _route(logits, running):
    rows = logits.shape[0]
    lane = lax.broadcasted_iota(jnp.int32, logits.shape, 1)
    lane_f = lane.astype(F32)
    far = float(ROUTE_LANES)
    is_group = lane < N_GROUPS
    gl = jnp.where(is_group, logits, NEG_BIG)
    gmax = jnp.max(gl, axis=-1, keepdims=True)
    gsum = jnp.sum(jnp.where(is_group, jnp.exp(gl - gmax), 0.0), axis=-1, keepdims=True)
    g_w = 1.0 / gsum
    gidx = jnp.min(jnp.where(gl == gmax, lane_f, far), axis=-1, keepdims=True)
    lo = EXPERT_LANE0 + EXPERTS_PER_GROUP * gidx
    in_group = (lane_f >= lo) & (lane_f < lo + EXPERTS_PER_GROUP)
    el = jnp.where(in_group, logits, NEG_BIG)
    v1 = jnp.max(el, axis=-1, keepdims=True)
    i1 = jnp.min(jnp.where(el == v1, lane_f, far), axis=-1, keepdims=True)
    el2 = jnp.where(lane_f == i1, NEG_BIG, el)
    v2 = jnp.max(el2, axis=-1, keepdims=True)
    i2 = jnp.min(jnp.where(el2 == v2, lane_f, far), axis=-1, keepdims=True)
    e2 = jnp.exp(v2 - v1)
    w1 = g_w / (1.0 + e2)
    w2 = w1 * e2
    in_my_group = lane_f == gidx
    chosen = in_my_group.astype(BF16)
    earlier = (lax.broadcasted_iota(jnp.int32, (rows, rows), 0)
               > lax.broadcasted_iota(jnp.int32, (rows, rows), 1)).astype(BF16)
    before = running + _dot(earlier, chosen)
    rank = jnp.sum(jnp.where(in_my_group, before, 0.0), axis=-1, keepdims=True)
    record = jnp.where(lane == ROUTE_GROUP, gidx, jnp.where(lane == ROUTE_RANK, rank, 0.0))
    weights = jnp.where(lane_f == i1 - lo, w1, 0.0) + jnp.where(lane_f == i2 - lo, w2, 0.0)
    return record, weights, running + jnp.sum(chosen.astype(F32), axis=0, keepdims=True)


def _outproj_kernel(*refs, widths):
    n_in = len(widths)
    in_refs = refs[:n_in]
    w_ref, x_ref, mod_ref, g_ref, rw_ref, rb_ref, x1_ref, pay_ref, route_t_ref, count_ref, run_ref = refs[n_in:]

    @pl.when(pl.program_id(0) == 0)
    def _():
        run_ref[...] = jnp.zeros_like(run_ref)

    acc = None
    off = 0
    for r, width in zip(in_refs, widths):
        part = _dot(r[...], w_ref[off:off + width, :])
        acc = part if acc is None else acc + part
        off += width
    x1 = x_ref[...] + mod_ref[0, 2:3, :] * acc
    x1_ref[...] = x1
    h = _modulated_norm(x1, g_ref[...], mod_ref[0, 3:4, :], mod_ref[0, 4:5, :])
    h_hi = h.astype(BF16)
    h_lo = (h - h_hi.astype(F32)).astype(BF16)
    both = _dot(h_hi, rw_ref[...])
    logits = (both[:, :ROUTE_LANES] + both[:, ROUTE_LANES:] + _dot(h_lo, rw_ref[:, :ROUTE_LANES]) + rb_ref[...])
    record, weights, running = _route(logits, run_ref[...])
    route_t_ref[...] = record.T[:SUBLANES]
    run_ref[...] = running
    count_ref[...] = jnp.broadcast_to(running, count_ref.shape)
    rows = h.shape[0]
    _store_row_major(pay_ref, h, PAYLOAD_CHUNKS)
    pay_ref[pl.ds(ROW_CHUNKS, rows, stride=PAYLOAD_CHUNKS), :] = weights
    for c in range(ROW_CHUNKS + 1, PAYLOAD_CHUNKS):
        pay_ref[pl.ds(c, rows, stride=PAYLOAD_CHUNKS), :] = jnp.zeros((rows, LANES), F32)


def _store_row_major(ref, x, pitch=ROW_CHUNKS):
    rows = x.shape[0]
    for c in range(ROW_CHUNKS):
        ref[pl.ds(c, rows, stride=pitch), :] = x[:, c * LANES:(c + 1) * LANES]


def _load_row_major(ref, rows, pitch=ROW_CHUNKS):
    return jnp.concatenate([ref[pl.ds(c, rows, stride=pitch), :] for c in range(ROW_CHUNKS)], axis=-1)


def _outproj_route(stream, inputs, w_bf16, x, mod, ffn_norm_g, route_w, route_b):
    n = stream.n_tok
    widths = tuple(a.shape[1] for a in inputs)
    slot = _slot_map(stream)
    row = lambda i: (i, 0)
    fixed = lambda i: (0, 0)
    return pl.pallas_call(
        functools.partial(_outproj_kernel, widths=widths),
        grid=(n // ROW_TILE,),
        in_specs=[pl.BlockSpec((ROW_TILE, width), row) for width in widths] + [
            pl.BlockSpec((sum(widths), D_MODEL), fixed),
            pl.BlockSpec((ROW_TILE, D_MODEL), row),
            pl.BlockSpec((1, 6, D_MODEL), lambda i: (slot(i), 0, 0)),
            pl.BlockSpec((1, D_MODEL), fixed),
            pl.BlockSpec((D_MODEL, 2 * ROUTE_LANES), fixed),
            pl.BlockSpec((1, ROUTE_LANES), fixed),
        ],
        out_specs=[
            pl.BlockSpec((ROW_TILE, D_MODEL), row),
            pl.BlockSpec((ROW_TILE * PAYLOAD_CHUNKS, LANES), row),
            pl.BlockSpec((SUBLANES, ROW_TILE), lambda i: (0, i)),
            pl.BlockSpec((SUBLANES, ROUTE_LANES), fixed),
        ],
        out_shape=[
            jax.ShapeDtypeStruct((n, D_MODEL), F32),
            jax.ShapeDtypeStruct((n * PAYLOAD_CHUNKS, LANES), F32),
            jax.ShapeDtypeStruct((SUBLANES, n), F32),
            jax.ShapeDtypeStruct((SUBLANES, ROUTE_LANES), F32),
        ],
        scratch_shapes=[pltpu.VMEM((1, ROUTE_LANES), F32)],
        compiler_params=_params("arbitrary"),
        name="outproj_route",
    )(*inputs, w_bf16, x, mod, ffn_norm_g.reshape(1, D_MODEL), route_w, route_b)


def _router_params(grp_w, grp_b, rtr_w, rtr_b):
    w = jnp.concatenate([grp_w, jnp.moveaxis(rtr_w, 0, 1).reshape(D_MODEL, N_EXPERTS)], axis=1)
    w = jnp.pad(w, ((0, 0), (0, ROUTE_LANES - w.shape[1])))
    hi = w.astype(BF16)
    lo = (w - hi.astype(F32)).astype(BF16)
    b = jnp.concatenate([grp_b, rtr_b.reshape(N_EXPERTS)])
    b = jnp.pad(b, (0, ROUTE_LANES - b.shape[0])).reshape(1, ROUTE_LANES)
    return jnp.concatenate([hi, lo], axis=1), b


def _sorted_tiles(n_tok):
    return n_tok // FFN_TILE + N_GROUPS


def _dispatch_plan(route_t, counts, n_tok):
    n_tiles = _sorted_tiles(n_tok)
    count = counts[0, :N_GROUPS].astype(jnp.int32)
    tiles = (count + (FFN_TILE - 1)) // FFN_TILE
    start = (jnp.cumsum(tiles) - tiles) * FFN_TILE
    tile_end = jnp.cumsum(tiles)
    j = jnp.arange(n_tiles, dtype=jnp.int32)
    tile_group = jnp.minimum(jnp.sum(j[:, None] >= tile_end[None, :], axis=1), N_GROUPS - 1).astype(jnp.int32)
    group_end = jnp.take(start + count, tile_group)
    valid = jnp.where(j < tile_end[-1], jnp.clip(group_end - j * FFN_TILE, 0, FFN_TILE), 0).astype(jnp.int32)
    ids = route_t[:2].astype(jnp.int32)
    where = jnp.concatenate([start, ids.reshape(-1)])
    row = jnp.take(start, ids[ROUTE_GROUP]) + ids[ROUTE_RANK]
    src = jnp.zeros((n_tiles * FFN_TILE,), jnp.int32).at[row].set(jnp.arange(n_tok, dtype=jnp.int32))
    return where, tile_group, valid, src


def _sorted_row(where_ref, n_tok, tok):
    return where_ref[where_ref[N_GROUPS + ROUTE_GROUP * n_tok + tok]] + where_ref[N_GROUPS + ROUTE_RANK * n_tok + tok]


def _token_rows(t, pitch=ROW_CHUNKS):
    return pl.ds(pl.multiple_of(t * pitch, pitch), pitch)


def _ffn_kernel(tg_ref, valid_ref, src_ref, pay_ref, wg_ref, wu_ref, wd_ref, ys_ref, x_buf, sems):
    j = pl.program_id(0)
    n_tiles = pl.num_programs(0)

    def gather_copy(tile, r):
        tok = src_ref[tile * FFN_TILE + r]
        return pltpu.make_async_copy(pay_ref.at[_token_rows(tok, PAYLOAD_CHUNKS)],
                                     x_buf.at[tile % 2, _token_rows(r, PAYLOAD_CHUNKS)], sems.at[tile % 2])

    def start_gather(tile):
        def issue_pair(r2, carry):
            for k in range(2):
                gather_copy(tile, 2 * r2 + k).start(priority=k)
            return carry

        def issue(r, carry):
            gather_copy(tile, r).start()
            return carry

        @pl.when(valid_ref[tile] == FFN_TILE)
        def _():
            lax.fori_loop(0, FFN_TILE // 2, issue_pair, 0, unroll=4)

        @pl.when(valid_ref[tile] < FFN_TILE)
        def _():
            lax.fori_loop(0, valid_ref[tile], issue, 0)

    @pl.when(j == 0)
    def _():
        x_buf[...] = jnp.zeros_like(x_buf)
        start_gather(j)

    @pl.when(j + 1 < n_tiles)
    def _():
        @pl.when(valid_ref[j + 1] > 0)
        def _():
            start_gather(j + 1)

    @pl.when(valid_ref[j] == 0)
    def _():
        ys_ref[...] = jnp.zeros_like(ys_ref)

    @pl.when(valid_ref[j] > 0)
    def _():
        slot = j % 2
        n = pl.multiple_of(valid_ref[j] * PAYLOAD_CHUNKS, PAYLOAD_CHUNKS)
        pltpu.make_async_copy(pay_ref.at[pl.ds(0, n)], x_buf.at[slot, pl.ds(0, n)], sems.at[slot]).wait()
        x = _load_row_major(x_buf.at[slot], FFN_TILE, PAYLOAD_CHUNKS).astype(BF16)
        w = x_buf[slot, pl.ds(ROW_CHUNKS, FFN_TILE, stride=PAYLOAD_CHUNKS), :]
        y = None
        for e in range(EXPERTS_PER_GROUP):
            hidden = _silu(_dot(x, wg_ref[e])) * _dot(x, wu_ref[e]) * w[:, e:e + 1]
            part = _dot(hidden.astype(BF16), wd_ref[e])
            y = part if y is None else y + part
        _store_row_major(ys_ref, y)


def _expert_ffn(n_tok, tile_group, valid, src, payload, wg, wu, wd):
    group = lambda j, tg, va, sr: (tg[j], 0, 0)
    n_tiles = _sorted_tiles(n_tok)
    return pl.pallas_call(
        _ffn_kernel,
        grid_spec=pltpu.PrefetchScalarGridSpec(
            num_scalar_prefetch=3,
            grid=(n_tiles,),
            in_specs=[
                pl.BlockSpec(memory_space=pl.ANY),
                pl.BlockSpec((EXPERTS_PER_GROUP, D_MODEL, EXPERT_FF), group),
                pl.BlockSpec((EXPERTS_PER_GROUP, D_MODEL, EXPERT_FF), group),
                pl.BlockSpec((EXPERTS_PER_GROUP, EXPERT_FF, D_MODEL), group),
            ],
            out_specs=pl.BlockSpec((FFN_TILE * ROW_CHUNKS, LANES), lambda j, tg, va, sr: (j, 0)),
            scratch_shapes=[pltpu.VMEM((2, FFN_TILE * PAYLOAD_CHUNKS, LANES), F32), pltpu.SemaphoreType.DMA((2,))],
        ),
        out_shape=jax.ShapeDtypeStruct((n_tiles * FFN_TILE * ROW_CHUNKS, LANES), F32),
        compiler_params=_params("arbitrary"),
        name="moe_ffn",
    )(tile_group, valid, src, payload, wg, wu, wd)


def _combine_kernel(where_ref, x_ref, mod_ref, ng_ref, nmod_ref, ys_ref, o_ref, *rest, n_tok, last_layer):
    if last_layer:
        y_buf, sems = rest
    else:
        h_ref, y_buf, sems = rest
    i = pl.program_id(0)
    n_tiles = pl.num_programs(0)

    def row_copy(tile, t, k):
        slot = tile % 2
        d = _sorted_row(where_ref, n_tok, tile * COMBINE_TILE + t)
        return pltpu.make_async_copy(ys_ref.at[_token_rows(d)], y_buf.at[slot, _token_rows(t)], sems.at[slot, k])

    def start_tile(tile):
        def issue(t2, carry):
            for k in range(2):
                row_copy(tile, 2 * t2 + k, k).start(priority=k)
            return carry

        lax.fori_loop(0, COMBINE_TILE // 2, issue, 0, unroll=4)

    @pl.when(i == 0)
    def _():
        start_tile(i)

    @pl.when(i + 1 < n_tiles)
    def _():
        start_tile(i + 1)

    slot = i % 2
    half = COMBINE_TILE * ROW_CHUNKS // 2
    for k in range(2):
        pltpu.make_async_copy(ys_ref.at[pl.ds(0, half)], y_buf.at[slot, pl.ds(0, half)], sems.at[slot, k]).wait()
    x = x_ref[...] + mod_ref[0, 5:6, :] * _load_row_major(y_buf.at[slot], COMBINE_TILE)
    if last_layer:
        o_ref[...] = _rms(x) * ng_ref[...]
    else:
        o_ref[...] = x
        h_ref[...] = _modulated_norm(x, ng_ref[...], nmod_ref[0, 0:1, :], nmod_ref[0, 1:2, :]).astype(h_ref.dtype)


def _combine(stream, where, ys, x, mod, norm_g, next_mod, last_layer):
    n = stream.n_tok
    first_slot, seq_len = stream.first_slot, stream.seq_len
    slot = (lambda i: 0) if first_slot == 0 else (lambda i: first_slot + (i * COMBINE_TILE) // seq_len)
    row = lambda i, w: (i, 0)
    out_specs = [pl.BlockSpec((COMBINE_TILE, D_MODEL), row)]
    out_shape = [jax.ShapeDtypeStruct((n, D_MODEL), F32)]
    if not last_layer:
        out_specs.append(pl.BlockSpec((COMBINE_TILE, D_MODEL), row))
        out_shape.append(jax.ShapeDtypeStruct((n, D_MODEL), BF16))
    return pl.pallas_call(
        functools.partial(_combine_kernel, n_tok=n, last_layer=last_layer),
        grid_spec=pltpu.PrefetchScalarGridSpec(
            num_scalar_prefetch=1,
            grid=(n // COMBINE_TILE,),
            in_specs=[
                pl.BlockSpec((COMBINE_TILE, D_MODEL), row),
                pl.BlockSpec((1, 6, D_MODEL), lambda i, w: (slot(i), 0, 0)),
                pl.BlockSpec((1, D_MODEL), lambda i, w: (0, 0)),
                pl.BlockSpec((1, 6, D_MODEL), lambda i, w: (slot(i), 0, 0)),
                pl.BlockSpec(memory_space=pl.ANY),
            ],
            out_specs=out_specs,
            scratch_shapes=[pltpu.VMEM((2, COMBINE_TILE * ROW_CHUNKS, LANES), F32),
                            pltpu.SemaphoreType.DMA((2, 2))],
        ),
        out_shape=out_shape,
        compiler_params=_params("arbitrary"),
        name="moe_combine",
    )(where, x, mod, norm_g.reshape(1, D_MODEL), next_mod, ys)


def _moe(stream, payload, route_t, counts, wg, wu, wd, x, mod, norm_g, next_mod, last_layer):
    n = stream.n_tok
    where, tile_group, valid, src = _dispatch_plan(route_t, counts, n)
    ys = _expert_ffn(n, tile_group, valid, src, payload, wg, wu, wd)
    return _combine(stream, where, ys, x, mod, norm_g, next_mod, last_layer)


IN1_COL = 1024
IN1_ROW_TILE = 1024
QK_BLOCKS = 2 * RET_QK_WIDTH // IN1_COL
Q_BLOCKS = RET_QK_WIDTH // IN1_COL
V_BLOCKS = RET_V_WIDTH // IN1_COL


def _inproj1_qk_kernel(h_ref, w_ref, cos_ref, sin_ref, o_ref, *, rope):
    p = _dot(h_ref[...], w_ref[...])
    scale = jnp.where(pl.program_id(0) < Q_BLOCKS, 1.0, RET_DK ** -0.5)
    cos = cos_ref[...]
    sin = sin_ref[...]
    for hh in range(IN1_COL // RET_DK):
        ph = p[:, hh * RET_DK:(hh + 1) * RET_DK]
        if rope:
            ph = _rope(ph, cos, sin)
        o_ref[:, hh * RET_DK:(hh + 1) * RET_DK] = (ph * scale).astype(o_ref.dtype)


def _inproj1_vg_kernel(h_ref, w_ref, o_ref):
    p = _dot(h_ref[...], w_ref[...])
    o_ref[...] = jnp.where(pl.program_id(0) < V_BLOCKS, p, _silu(p)).astype(o_ref.dtype)


def _inproj1(stream, h, w_bf16, cos, sin):
    n = stream.n_tok
    tm = IN1_ROW_TILE
    tiles_per_seq = max(stream.seq_len // tm, 1)
    rope_map = (lambda j, i: (i % tiles_per_seq, 0)) if stream.rope else (lambda j, i: (0, 0))
    h_spec = pl.BlockSpec((tm, D_MODEL), lambda j, i: (i, 0))
    out_spec = pl.BlockSpec((tm, IN1_COL), lambda j, i: (i, j))
    qk = pl.pallas_call(
        functools.partial(_inproj1_qk_kernel, rope=stream.rope),
        grid=(QK_BLOCKS, n // tm),
        in_specs=[
            h_spec,
            pl.BlockSpec((D_MODEL, IN1_COL), lambda j, i: (0, j)),
            pl.BlockSpec((tm, RET_DK), rope_map),
            pl.BlockSpec((tm, RET_DK), rope_map),
        ],
        out_specs=out_spec,
        out_shape=jax.ShapeDtypeStruct((n, QK_BLOCKS * IN1_COL), BF16),
        compiler_params=_params("parallel", "parallel"),
        name="inproj1_qk",
    )(h, w_bf16, cos, sin)
    vg_blocks = IN1_WIDTH // IN1_COL - QK_BLOCKS
    vg = pl.pallas_call(
        _inproj1_vg_kernel,
        grid=(vg_blocks, n // tm),
        in_specs=[h_spec, pl.BlockSpec((D_MODEL, IN1_COL), lambda j, i: (0, QK_BLOCKS + j))],
        out_specs=out_spec,
        out_shape=jax.ShapeDtypeStruct((n, vg_blocks * IN1_COL), BF16),
        compiler_params=_params("parallel", "parallel"),
        name="inproj1_vg",
    )(h, w_bf16)
    return qk, vg


def _retention_kernel(*refs, n_chunks, has_state, heads, unroll):
    if has_state:
        decay_ref, q_ref, k_ref, v_ref, gf_ref, gb_ref, sf0_ref, sb0_ref, o_ref, inc_ref, seen_ref = refs
    else:
        decay_ref, q_ref, k_ref, v_ref, gf_ref, gb_ref, o_ref, sf_ref, sb_ref, inc_ref, seen_ref = refs
    c = RET_CHUNK
    row = lax.broadcasted_iota(jnp.int32, (c, c), 0).astype(F32)
    col = lax.broadcasted_iota(jnp.int32, (c, c), 1).astype(F32)
    pos = lax.broadcasted_iota(jnp.int32, (c, 1), 0).astype(F32)

    def chunk(i):
        return pl.ds(pl.multiple_of(i * c, c), c)

    diff = [row - col, col - row]
    qk_cols = [slice(g * RET_DK, (g + 1) * RET_DK) for g in range(heads)]
    v_cols = [slice(g * RET_DV, (g + 1) * RET_DV) for g in range(heads)]
    intra, q_dec, k_dec, chunk_dec = [], [], [], []
    for g in range(heads):
        hd = pl.program_id(1) * heads + g
        lg = [-jnp.exp(jnp.full((1, 1), decay_ref[d, hd], F32)) for d in range(2)]
        intra.append([jnp.where(diff[d] >= 0, jnp.exp(lg[d] * jnp.maximum(diff[d], 0.0)), 0.0) for d in range(2)])
        q_dec.append([jnp.exp(lg[0] * (pos + 1.0)), jnp.exp(lg[1] * (c - pos))])
        k_dec.append([jnp.exp(lg[0] * ((c - 1.0) - pos)), jnp.exp(lg[1] * pos)])
        chunk_dec.append([jnp.exp(lg[d] * float(c)) for d in range(2)])

    def increments(i, carry):
        rows = chunk(i)
        for g in range(heads):
            ki = k_ref[rows, qk_cols[g]].astype(F32)
            vi = v_ref[rows, v_cols[g]]
            for d in range(2):
                inc_ref[g, d, i] = _dot_tn((ki * k_dec[g][d]).astype(BF16), vi)
        return carry

    lax.fori_loop(0, n_chunks, increments, 0, unroll=unroll)

    for g in range(heads):
        finals = []
        for d in range(2):
            def scan(t, state):
                i = t if d == 0 else n_chunks - 1 - t
                seen_ref[g, d, i] = state.astype(BF16)
                return state * chunk_dec[g][d] + inc_ref[g, d, i]

            if has_state:
                state0 = (sf0_ref if d == 0 else sb0_ref)[0, g]
            else:
                state0 = jnp.zeros((RET_DK, RET_DV), F32)
            finals.append(lax.fori_loop(0, n_chunks, scan, state0))
        if not has_state:
            sf_ref[0, g] = finals[0]
            sb_ref[0, g] = finals[1]

    def outputs(i, carry):
        rows = chunk(i)
        for g in range(heads):
            qi = q_ref[rows, qk_cols[g]]
            vi = v_ref[rows, v_cols[g]]
            s = _dot_nt(qi, k_ref[rows, qk_cols[g]])
            mixed = None
            for d, g_ref in enumerate((gf_ref, gb_ref)):
                y = _dot((s * intra[g][d]).astype(BF16), vi) + _dot(qi, seen_ref[g, d, i]) * q_dec[g][d]
                gated = g_ref[rows, v_cols[g]].astype(F32) * _rms(y)
                mixed = gated if mixed is None else mixed + gated
            o_ref[rows, v_cols[g]] = mixed.astype(o_ref.dtype)
        return carry

    lax.fori_loop(0, n_chunks, outputs, 0, unroll=unroll)


def _retention(stream, qk, vg, decays, states, heads):
    n = stream.n_tok
    t = stream.seq_len
    n_chunks = t // RET_CHUNK
    has_state = states is not None
    qk_w = heads * RET_DK
    v_w = heads * RET_DV
    k_blk = RET_QK_WIDTH // qk_w
    gf_blk = RET_V_WIDTH // v_w
    gb_blk = 2 * gf_blk
    in_specs = [
        pl.BlockSpec(memory_space=pltpu.SMEM),
        pl.BlockSpec((t, qk_w), lambda b, h: (b, h)),
        pl.BlockSpec((t, qk_w), lambda b, h: (b, k_blk + h)),
        pl.BlockSpec((t, v_w), lambda b, h: (b, h)),
        pl.BlockSpec((t, v_w), lambda b, h: (b, gf_blk + h)),
        pl.BlockSpec((t, v_w), lambda b, h: (b, gb_blk + h)),
    ]
    args = [decays, qk, qk, vg, vg, vg]
    state_spec = pl.BlockSpec((1, heads, RET_DK, RET_DV), lambda b, h: (b, h, 0, 0))
    out_specs = [pl.BlockSpec((t, v_w), lambda b, h: (b, h))]
    out_shape = [jax.ShapeDtypeStruct((n, RET_V_WIDTH), BF16)]
    if has_state:
        in_specs += [state_spec, state_spec]
        args += list(states)
    else:
        out_specs += [state_spec, state_spec]
        out_shape += [jax.ShapeDtypeStruct((stream.n_seq, RET_HEADS, RET_DK, RET_DV), F32)] * 2
    return pl.pallas_call(
        functools.partial(_retention_kernel, n_chunks=n_chunks, has_state=has_state, heads=heads,
                          unroll=min(n_chunks, 2)),
        grid=(stream.n_seq, RET_HEADS // heads),
        in_specs=in_specs,
        out_specs=out_specs,
        out_shape=out_shape,
        scratch_shapes=[pltpu.VMEM((heads, 2, n_chunks, RET_DK, RET_DV), F32),
                        pltpu.VMEM((heads, 2, n_chunks, RET_DK, RET_DV), BF16)],
        compiler_params=_params("parallel", "parallel"),
        name="retention",
    )(*args)


def _rope_tables(n_tokens, dim):
    t = jnp.arange(n_tokens)
    pos = jnp.stack([t // GRID_W, t % GRID_W]).astype(F32)
    n_freq = dim // 4
    freqs = ROPE_THETA ** (-jnp.arange(n_freq, dtype=F32) / n_freq)
    ang = pos[:, :, None] * freqs
    cos, sin = jnp.cos(ang), jnp.sin(ang)
    cos_t = jnp.concatenate([cos[0], cos[0], cos[1], cos[1]], axis=-1)
    sin_t = jnp.concatenate([-sin[0], sin[0], -sin[1], sin[1]], axis=-1)
    return cos_t, sin_t


def kernel(x_prompt, x_sample, cache_k0, cache_v0, state_ret_fwd1, state_ret_bwd1, c, c_ctx,
           norm_mix_g0, mod_w0, mod_b0, in_w0, conv_w0, conv_b0, conv_norm_g0, conv_norm_b0,
           q_norm_g0, k_norm_g0, out_w0, norm_ffn_g0,
           moe_grp_w0, moe_grp_b0, moe_rtr_w0, moe_rtr_b0, moe_w_gate0, moe_w_up0, moe_w_down0,
           norm_mix_g1, mod_w1, mod_b1, in_w1, ret_decay_fwd1, ret_decay_bwd1, out_w1, norm_ffn_g1,
           moe_grp_w1, moe_grp_b1, moe_rtr_w1, moe_rtr_b1, moe_w_gate1, moe_w_up1, moe_w_down1,
           final_norm_g):
    batch, seq, d = x_prompt.shape
    dec_batch, dec_seq, _ = x_sample.shape
    assert d == D_MODEL and 1 + dec_batch <= MOD_SLOTS
    ctx = _Stream(batch, seq, first_slot=0, rope=False)
    lat = _Stream(dec_batch, dec_seq, first_slot=1, rope=True)
    assert ctx.n_tok % ROW_TILE == 0 and lat.n_tok % ROW_TILE == 0 and ROW_CHUNKS == SUBLANES
    assert seq % CONV_TILE == 0 and dec_seq % ROW_TILE == 0 and ROW_TILE % seq == 0

    cond = jnp.concatenate([c_ctx[None, :], c, jnp.zeros((MOD_SLOTS - 1 - dec_batch, d), F32)], axis=0)
    mod0 = _modulation(cond, mod_w0, mod_b0)
    mod1 = _modulation(cond, mod_w1, mod_b1)
    cos, sin = _rope_tables(dec_seq, HEAD_DIM)
    decays = jnp.stack([ret_decay_fwd1, ret_decay_bwd1]).astype(F32)

    in_w0_b = in_w0.astype(BF16)
    out_w0_b = out_w0.astype(BF16)
    in_w1_b = in_w1.astype(BF16)
    out_w1_b = out_w1.astype(BF16)
    moe0 = (moe_w_gate0.astype(BF16), moe_w_up0.astype(BF16), moe_w_down0.astype(BF16))
    moe1 = (moe_w_gate1.astype(BF16), moe_w_up1.astype(BF16), moe_w_down1.astype(BF16))
    route0 = _router_params(moe_grp_w0, moe_grp_b0, moe_rtr_w0, moe_rtr_b0)
    route1 = _router_params(moe_grp_w1, moe_grp_b1, moe_rtr_w1, moe_rtr_b1)

    def run(stream, x, cache, states):
        x = x.reshape(stream.n_tok, d)
        kv_dtype = BF16 if stream.rope else F32
        a, q, k, v = _inproj0(stream, x, mod0, norm_mix_g0, in_w0_b, q_norm_g0, k_norm_g0, cos, sin, kv_dtype)
        a = _conv_branch(stream, a, conv_w0, conv_b0, conv_norm_g0, conv_norm_b0)
        tq = min(stream.seq_len, 256)
        o = _attention(stream, q, k, v, cache, tq, kv_heads=1 if cache is not None else N_KV_HEADS)
        x, *routed = _outproj_route(stream, [a, o], out_w0_b, x, mod0, norm_ffn_g0, *route0)
        x, h = _moe(stream, *routed, *moe0, x, mod0, norm_mix_g1, mod1, last_layer=False)
        qk, vg = _inproj1(stream, h, in_w1_b, cos, sin)
        ret = _retention(stream, qk, vg, decays, states, heads=2 if states is not None else 4)
        x, *routed = _outproj_route(stream, [ret[0]], out_w1_b, x, mod1, norm_ffn_g1, *route1)
        y, = _moe(stream, *routed, *moe1, x, mod1, final_norm_g, mod1, last_layer=True)
        return y.reshape(stream.n_seq, stream.seq_len, d), k, v, ret[1:]

    y_prompt, k_ctx, v_ctx, new_states = run(ctx, x_prompt, None, None)
    y_sample, _, _, _ = run(lat, x_sample, (cache_k0, cache_v0), (state_ret_fwd1, state_ret_bwd1))
    return (y_prompt, y_sample, k_ctx, v_ctx, new_states[0], new_states[1])
```

```python
import functools

import jax
import jax.numpy as jnp
from jax import lax
from jax.experimental import pallas as pl
from jax.experimental.pallas import tpu as pltpu

D_MODEL = 1024
GRID_W = 64
EPS = 1e-6
CONV_CH = 512
CONV_WIDTH = 31
CONV_HALF = CONV_WIDTH // 2
N_Q_HEADS = 8
N_KV_HEADS = 2
Q_PER_KV = N_Q_HEADS // N_KV_HEADS
HEAD_DIM = 128
ROPE_THETA = 10000.0
ATTN_WIDTH = N_Q_HEADS * HEAD_DIM
KV_WIDTH = N_KV_HEADS * HEAD_DIM
IN0_WIDTH = 2 * CONV_CH + ATTN_WIDTH + 2 * KV_WIDTH
RET_HEADS = 8
RET_DK = 128
RET_DV = 256
RET_CHUNK = 128
RET_QK_WIDTH = RET_HEADS * RET_DK
RET_V_WIDTH = RET_HEADS * RET_DV
IN1_WIDTH = 2 * RET_QK_WIDTH + 3 * RET_V_WIDTH
N_GROUPS = 4
EXPERTS_PER_GROUP = 4
N_EXPERTS = N_GROUPS * EXPERTS_PER_GROUP
EXPERT_FF = 512

LANES = 128
SUBLANES = 8
VMEM_LIMIT_BYTES = 56 * 1024 * 1024

MOD_SLOTS = 8
ROW_TILE = 512
ROW_CHUNKS = D_MODEL // LANES
PAYLOAD_CHUNKS = 2 * ROW_CHUNKS
FFN_TILE = 256
COMBINE_TILE = 256
ROUTE_GROUP, ROUTE_RANK = 0, 1
CONV_TILE = 256
CONV_HALO = 16
CONV_ROWS = 64
ROUTE_LANES = LANES
EXPERT_LANE0 = N_GROUPS
NEG_BIG = -1e30
LOG2_E = 1.4426950408889634
ATTN_KV_CHUNK = 512

F32 = jnp.float32
BF16 = jnp.bfloat16


def _params(*semantics):
    return pltpu.CompilerParams(dimension_semantics=semantics, vmem_limit_bytes=VMEM_LIMIT_BYTES)


def _sigmoid(x):
    return 1.0 / (1.0 + jnp.exp(-x))


def _silu(x):
    return x * _sigmoid(x)


def _rms(x):
    return x * lax.rsqrt(jnp.mean(x * x, axis=-1, keepdims=True) + EPS)


def _rope(x, cos, sin):
    lane = lax.broadcasted_iota(jnp.int32, x.shape, 1)
    take_upper = (lane % (HEAD_DIM // 2)) < (HEAD_DIM // 4)
    partner = jnp.where(take_upper, pltpu.roll(x, HEAD_DIM - HEAD_DIM // 4, 1), pltpu.roll(x, HEAD_DIM // 4, 1))
    return x * cos + partner * sin


def _dot(a, b):
    return jnp.dot(a, b, preferred_element_type=F32)


def _dot_nt(a, b):
    return lax.dot_general(a, b, (((1,), (1,)), ((), ())), preferred_element_type=F32)


def _dot_tn(a, b):
    return lax.dot_general(a, b, (((0,), (0,)), ((), ())), preferred_element_type=F32)


def _mod_kernel(c_ref, w_ref, b_ref, o_ref):
    c = c_ref[...]
    o_ref[...] = _dot(_silu(c).astype(BF16), w_ref[...].astype(BF16)) + b_ref[...]


def _modulation(cond, mod_w, mod_b):
    n_out = mod_w.shape[1]
    col = D_MODEL
    out = pl.pallas_call(
        _mod_kernel,
        grid=(n_out // col,),
        in_specs=[
            pl.BlockSpec((MOD_SLOTS, D_MODEL), lambda j: (0, 0)),
            pl.BlockSpec((D_MODEL, col), lambda j: (0, j)),
            pl.BlockSpec((1, col), lambda j: (0, j)),
        ],
        out_specs=pl.BlockSpec((MOD_SLOTS, col), lambda j: (0, j)),
        out_shape=jax.ShapeDtypeStruct((MOD_SLOTS, n_out), F32),
        compiler_params=_params("parallel"),
        name="modulation",
    )(cond, mod_w, mod_b.reshape(1, n_out))
    return out.reshape(MOD_SLOTS, 6, D_MODEL)


def _slot_map(stream):
    if stream.first_slot == 0:
        return lambda i: 0
    return lambda i: stream.first_slot + (i * ROW_TILE) // stream.seq_len


class _Stream:
    def __init__(self, n_seq, seq_len, first_slot, rope):
        self.n_seq = n_seq
        self.seq_len = seq_len
        self.first_slot = first_slot
        self.rope = rope
        self.n_tok = n_seq * seq_len


def _modulated_norm(x, g, shift, scale):
    return _rms(x) * g * (1.0 + scale) + shift


def _inproj0_kernel(x_ref, mod_ref, g_ref, w_ref, qg_ref, kg_ref, cos_ref, sin_ref,
                    a_ref, q_ref, k_ref, v_ref, *, rope):
    h = _modulated_norm(x_ref[...], g_ref[...], mod_ref[0, 0:1, :], mod_ref[0, 1:2, :])
    p = _dot(h.astype(BF16), w_ref[...])
    a_ref[...] = p[:, :CONV_CH] * _sigmoid(p[:, CONV_CH:2 * CONV_CH])
    q0 = 2 * CONV_CH
    k0 = q0 + ATTN_WIDTH
    v0 = k0 + KV_WIDTH
    cos = cos_ref[...]
    sin = sin_ref[...]
    q_scale = HEAD_DIM ** -0.5 * LOG2_E
    for hh in range(N_Q_HEADS):
        qh = _rms(p[:, q0 + hh * HEAD_DIM:q0 + (hh + 1) * HEAD_DIM]) * qg_ref[...]
        if rope:
            qh = _rope(qh, cos, sin)
        q_ref[:, hh * HEAD_DIM:(hh + 1) * HEAD_DIM] = (qh * q_scale).astype(q_ref.dtype)
    seqs, _, rows, _ = k_ref.shape
    for hh in range(N_KV_HEADS):
        kh = _rms(p[:, k0 + hh * HEAD_DIM:k0 + (hh + 1) * HEAD_DIM]) * kg_ref[...]
        if rope:
            kh = _rope(kh, cos, sin)
        vh = p[:, v0 + hh * HEAD_DIM:v0 + (hh + 1) * HEAD_DIM]
        for sq in range(seqs):
            k_ref[sq, hh] = kh[sq * rows:(sq + 1) * rows].astype(k_ref.dtype)
            v_ref[sq, hh] = vh[sq * rows:(sq + 1) * rows].astype(v_ref.dtype)


def _inproj0(stream, x, mod, norm_g, w_bf16, q_norm_g, k_norm_g, cos, sin, kv_dtype):
    n = stream.n_tok
    tiles_per_seq = max(stream.seq_len // ROW_TILE, 1)
    rope_map = (lambda i: (i % tiles_per_seq, 0)) if stream.rope else (lambda i: (0, 0))
    slot = _slot_map(stream)
    row = lambda i: (i, 0)
    fixed = lambda i: (0, 0)
    seqs_per_tile = max(ROW_TILE // stream.seq_len, 1)
    kv_spec = pl.BlockSpec((seqs_per_tile, N_KV_HEADS, ROW_TILE // seqs_per_tile, HEAD_DIM),
                           lambda i: (i // tiles_per_seq, 0, i % tiles_per_seq, 0))
    kv_shape = jax.ShapeDtypeStruct((stream.n_seq, N_KV_HEADS, stream.seq_len, HEAD_DIM), kv_dtype)
    return pl.pallas_call(
        functools.partial(_inproj0_kernel, rope=stream.rope),
        grid=(n // ROW_TILE,),
        in_specs=[
            pl.BlockSpec((ROW_TILE, D_MODEL), row),
            pl.BlockSpec((1, 6, D_MODEL), lambda i: (slot(i), 0, 0)),
            pl.BlockSpec((1, D_MODEL), fixed),
            pl.BlockSpec((D_MODEL, IN0_WIDTH), fixed),
            pl.BlockSpec((1, HEAD_DIM), fixed),
            pl.BlockSpec((1, HEAD_DIM), fixed),
            pl.BlockSpec((ROW_TILE, HEAD_DIM), rope_map),
            pl.BlockSpec((ROW_TILE, HEAD_DIM), rope_map),
        ],
        out_specs=[
            pl.BlockSpec((ROW_TILE, CONV_CH), row),
            pl.BlockSpec((ROW_TILE, ATTN_WIDTH), row),
            kv_spec,
            kv_spec,
        ],
        out_shape=[
            jax.ShapeDtypeStruct((n, CONV_CH), F32),
            jax.ShapeDtypeStruct((n, ATTN_WIDTH), BF16),
            kv_shape,
            kv_shape,
        ],
        compiler_params=_params("parallel"),
        name="inproj0",
    )(x, mod, norm_g.reshape(1, D_MODEL), w_bf16, q_norm_g.reshape(1, HEAD_DIM), k_norm_g.reshape(1, HEAD_DIM),
      cos, sin)


def _conv_kernel(prev_ref, main_ref, next_ref, w_ref, b_ref, g_ref, beta_ref, o_ref, pad_ref, acc_ref, *,
                 tiles_per_seq):
    i = pl.program_id(0)
    has_prev = (i % tiles_per_seq) != 0
    has_next = (i % tiles_per_seq) != (tiles_per_seq - 1)
    pad_ref[0:CONV_HALO, :] = jnp.where(has_prev, prev_ref[...], 0.0)
    pad_ref[CONV_HALO:CONV_HALO + CONV_TILE, :] = main_ref[...]
    pad_ref[CONV_HALO + CONV_TILE:, :] = jnp.where(has_next, next_ref[...], 0.0)
    first = CONV_HALO - CONV_HALF
    window = CONV_ROWS + SUBLANES
    for c0 in range(0, CONV_CH, LANES):
        for r0 in range(0, CONV_TILE, CONV_ROWS):
            acc = jnp.zeros((CONV_ROWS, LANES), F32)
            for phase in range(SUBLANES):
                partial = jnp.zeros((window, LANES), F32)
                for j in range(phase, CONV_WIDTH, SUBLANES):
                    base = r0 + j - phase
                    partial = partial + w_ref[j:j + 1, c0:c0 + LANES] * pad_ref[base:base + window, c0:c0 + LANES]
                acc = acc + partial[first + phase:first + phase + CONV_ROWS]
            acc_ref[r0:r0 + CONV_ROWS, c0:c0 + LANES] = acc
    a = acc_ref[...] + b_ref[...]
    mu = jnp.mean(a, axis=-1, keepdims=True)
    d = a - mu
    var = jnp.mean(d * d, axis=-1, keepdims=True)
    y = d * lax.rsqrt(var + EPS) * g_ref[...] + beta_ref[...]
    o_ref[...] = _silu(y).astype(o_ref.dtype)


def _conv_branch(stream, a, conv_w, conv_b, norm_g, norm_b):
    n = stream.n_tok
    tiles_per_seq = stream.seq_len // CONV_TILE
    halo_per_tile = CONV_TILE // CONV_HALO
    n_halo = n // CONV_HALO
    fixed = lambda i: (0, 0)
    return pl.pallas_call(
        functools.partial(_conv_kernel, tiles_per_seq=tiles_per_seq),
        grid=(n // CONV_TILE,),
        in_specs=[
            pl.BlockSpec((CONV_HALO, CONV_CH), lambda i: (jnp.maximum(i * halo_per_tile - 1, 0), 0)),
            pl.BlockSpec((CONV_TILE, CONV_CH), lambda i: (i, 0)),
            pl.BlockSpec((CONV_HALO, CONV_CH), lambda i: (jnp.minimum((i + 1) * halo_per_tile, n_halo - 1), 0)),
            pl.BlockSpec((CONV_WIDTH, CONV_CH), fixed),
            pl.BlockSpec((1, CONV_CH), fixed),
            pl.BlockSpec((1, CONV_CH), fixed),
            pl.BlockSpec((1, CONV_CH), fixed),
        ],
        out_specs=pl.BlockSpec((CONV_TILE, CONV_CH), lambda i: (i, 0)),
        out_shape=jax.ShapeDtypeStruct((n, CONV_CH), BF16),
        scratch_shapes=[
            pltpu.VMEM((CONV_TILE + 2 * CONV_HALO, CONV_CH), F32),
            pltpu.VMEM((CONV_TILE, CONV_CH), F32),
        ],
        compiler_params=_params("parallel"),
        name="conv_branch",
    )(a, a, a, conv_w, conv_b.reshape(1, CONV_CH), norm_g.reshape(1, CONV_CH), norm_b.reshape(1, CONV_CH))


def _attn_kernel(*refs, tq, past):
    if past:
        q_ref, k_ref, v_ref, kc_ref, vc_ref, o_ref, ks_ref, vs_ref = refs
    else:
        q_ref, k_ref, v_ref, o_ref, ks_ref, vs_ref = refs
    kv_heads, n_keys, _ = ks_ref.shape
    group_w = Q_PER_KV * HEAD_DIM

    @pl.when(pl.program_id(2) == 0)
    def _():
        lane = lax.broadcasted_iota(jnp.int32, (n_keys, HEAD_DIM), 1)
        for hk in range(kv_heads):
            vs_ref[hk, :, HEAD_DIM:] = jnp.where(lane == 0, 1.0, 0.0).astype(BF16)
            if past:
                ks_ref[hk, 0:past, :] = kc_ref[0, hk].astype(BF16)
                vs_ref[hk, 0:past, 0:HEAD_DIM] = vc_ref[0, hk].astype(BF16)
            ks_ref[hk, past:, :] = k_ref[0, hk].astype(BF16)
            vs_ref[hk, past:, 0:HEAD_DIM] = v_ref[0, hk].astype(BF16)

    for hk in range(kv_heads):
        q = q_ref[:, hk * group_w:(hk + 1) * group_w]
        q4 = jnp.concatenate([q[:, g * HEAD_DIM:(g + 1) * HEAD_DIM] for g in range(Q_PER_KV)], axis=0)
        rows = q4.shape[0]
        m = jnp.full((rows, 1), NEG_BIG, F32)
        acc = jnp.zeros((rows, 2 * HEAD_DIM), F32)
        for c0 in range(0, n_keys, ATTN_KV_CHUNK):
            c1 = min(c0 + ATTN_KV_CHUNK, n_keys)
            s = _dot_nt(q4, ks_ref[hk, c0:c1, :])
            m_new = jnp.maximum(m, jnp.max(s, axis=-1, keepdims=True))
            p = jnp.exp2(s - m_new).astype(BF16)
            acc = jnp.exp2(m - m_new) * acc + _dot(p, vs_ref[hk, c0:c1, :])
            m = m_new
        o = acc[:, 0:HEAD_DIM] * (1.0 / acc[:, HEAD_DIM:HEAD_DIM + 1])
        for g in range(Q_PER_KV):
            col = hk * group_w + g * HEAD_DIM
            o_ref[:, col:col + HEAD_DIM] = o[g * tq:(g + 1) * tq].astype(o_ref.dtype)


def _attention(stream, q, k, v, cache, tq, kv_heads):
    n = stream.n_tok
    t = stream.seq_len
    q_tiles = t // tq
    width = kv_heads * Q_PER_KV * HEAD_DIM
    in_specs = [
        pl.BlockSpec((tq, width), lambda b, h, i: (b * q_tiles + i, h)),
        pl.BlockSpec((1, kv_heads, t, HEAD_DIM), lambda b, h, i: (b, h, 0, 0)),
        pl.BlockSpec((1, kv_heads, t, HEAD_DIM), lambda b, h, i: (b, h, 0, 0)),
    ]
    args = [q, k, v]
    past = 0
    if cache is not None:
        past = cache[0].shape[2]
        in_specs += [pl.BlockSpec((1, kv_heads, past, HEAD_DIM), lambda b, h, i: (b, h, 0, 0))] * 2
        args += list(cache)
    return pl.pallas_call(
        functools.partial(_attn_kernel, tq=tq, past=past),
        grid=(stream.n_seq, N_KV_HEADS // kv_heads, q_tiles),
        in_specs=in_specs,
        out_specs=pl.BlockSpec((tq, width), lambda b, h, i: (b * q_tiles + i, h)),
        out_shape=jax.ShapeDtypeStruct((n, ATTN_WIDTH), BF16),
        scratch_shapes=[pltpu.VMEM((kv_heads, past + t, HEAD_DIM), BF16),
                        pltpu.VMEM((kv_heads, past + t, 2 * HEAD_DIM), BF16)],
        compiler_params=_params("parallel", "parallel", "arbitrary"),
        name="attention",
    )(*args)


def _route(logits, running):
    rows = logits.shape[0]
    lane = lax.broadcasted_iota(jnp.int32, logits.shape, 1)
    lane_f = lane.astype(F32)
    far = float(ROUTE_LANES)
    is_group = lane < N_GROUPS
    gl = jnp.where(is_group, logits, NEG_BIG)
    gmax = jnp.max(gl, axis=-1, keepdims=True)
    gsum = jnp.sum(jnp.where(is_group, jnp.exp(gl - gmax), 0.0), axis=-1, keepdims=True)
    g_w = 1.0 / gsum
    gidx = jnp.min(jnp.where(gl == gmax, lane_f, far), axis=-1, keepdims=True)
    lo = EXPERT_LANE0 + EXPERTS_PER_GROUP * gidx
    in_group = (lane_f >= lo) & (lane_f < lo + EXPERTS_PER_GROUP)
    el = jnp.where(in_group, logits, NEG_BIG)
    v1 = jnp.max(el, axis=-1, keepdims=True)
    i1 = jnp.min(jnp.where(el == v1, lane_f, far), axis=-1, keepdims=True)
    el2 = jnp.where(lane_f == i1, NEG_BIG, el)
    v2 = jnp.max(el2, axis=-1, keepdims=True)
    i2 = jnp.min(jnp.where(el2 == v2, lane_f, far), axis=-1, keepdims=True)
    e2 = jnp.exp(v2 - v1)
    w1 = g_w / (1.0 + e2)
    w2 = w1 * e2
    in_my_group = lane_f == gidx
    chosen = in_my_group.astype(BF16)
    earlier = (lax.broadcasted_iota(jnp.int32, (rows, rows), 0)
               > lax.broadcasted_iota(jnp.int32, (rows, rows), 1)).astype(BF16)
    before = running + _dot(earlier, chosen)
    rank = jnp.sum(jnp.where(in_my_group, before, 0.0), axis=-1, keepdims=True)
    record = jnp.where(lane == ROUTE_GROUP, gidx, jnp.where(lane == ROUTE_RANK, rank, 0.0))
    weights = jnp.where(lane_f == i1 - lo, w1, 0.0) + jnp.where(lane_f == i2 - lo, w2, 0.0)
    return record, weights, running + jnp.sum(chosen.astype(F32), axis=0, keepdims=True)


def _outproj_kernel(*refs, widths):
    n_in = len(widths)
    in_refs = refs[:n_in]
    w_ref, x_ref, mod_ref, g_ref, rw_ref, rb_ref, x1_ref, pay_ref, route_t_ref, count_ref, run_ref = refs[n_in:]

    @pl.when(pl.program_id(0) == 0)
    def _():
        run_ref[...] = jnp.zeros_like(run_ref)

    acc = None
    off = 0
    for r, width in zip(in_refs, widths):
        part = _dot(r[...], w_ref[off:off + width, :])
        acc = part if acc is None else acc + part
        off += width
    x1 = x_ref[...] + mod_ref[0, 2:3, :] * acc
    x1_ref[...] = x1
    h = _modulated_norm(x1, g_ref[...], mod_ref[0, 3:4, :], mod_ref[0, 4:5, :])
    h_hi = h.astype(BF16)
    h_lo = (h - h_hi.astype(F32)).astype(BF16)
    both = _dot(h_hi, rw_ref[...])
    logits = (both[:, :ROUTE_LANES] + both[:, ROUTE_LANES:] + _dot(h_lo, rw_ref[:, :ROUTE_LANES]) + rb_ref[...])
    record, weights, running = _route(logits, run_ref[...])
    route_t_ref[...] = record.T[:SUBLANES]
    run_ref[...] = running
    count_ref[...] = jnp.broadcast_to(running, count_ref.shape)
    rows = h.shape[0]
    _store_row_major(pay_ref, h, PAYLOAD_CHUNKS)
    pay_ref[pl.ds(ROW_CHUNKS, rows, stride=PAYLOAD_CHUNKS), :] = weights
    for c in range(ROW_CHUNKS + 1, PAYLOAD_CHUNKS):
        pay_ref[pl.ds(c, rows, stride=PAYLOAD_CHUNKS), :] = jnp.zeros((rows, LANES), F32)


def _store_row_major(ref, x, pitch=ROW_CHUNKS):
    rows = x.shape[0]
    for c in range(ROW_CHUNKS):
        ref[pl.ds(c, rows, stride=pitch), :] = x[:, c * LANES:(c + 1) * LANES]


def _load_row_major(ref, rows, pitch=ROW_CHUNKS):
    return jnp.concatenate([ref[pl.ds(c, rows, stride=pitch), :] for c in range(ROW_CHUNKS)], axis=-1)


def _outproj_route(stream, inputs, w_bf16, x, mod, ffn_norm_g, route_w, route_b):
    n = stream.n_tok
    widths = tuple(a.shape[1] for a in inputs)
    slot = _slot_map(stream)
    row = lambda i: (i, 0)
    fixed = lambda i: (0, 0)
    return pl.pallas_call(
        functools.partial(_outproj_kernel, widths=widths),
        grid=(n // ROW_TILE,),
        in_specs=[pl.BlockSpec((ROW_TILE, width), row) for width in widths] + [
            pl.BlockSpec((sum(widths), D_MODEL), fixed),
            pl.BlockSpec((ROW_TILE, D_MODEL), row),
            pl.BlockSpec((1, 6, D_MODEL), lambda i: (slot(i), 0, 0)),
            pl.BlockSpec((1, D_MODEL), fixed),
            pl.BlockSpec((D_MODEL, 2 * ROUTE_LANES), fixed),
            pl.BlockSpec((1, ROUTE_LANES), fixed),
        ],
        out_specs=[
            pl.BlockSpec((ROW_TILE, D_MODEL), row),
            pl.BlockSpec((ROW_TILE * PAYLOAD_CHUNKS, LANES), row),
            pl.BlockSpec((SUBLANES, ROW_TILE), lambda i: (0, i)),
            pl.BlockSpec((SUBLANES, ROUTE_LANES), fixed),
        ],
        out_shape=[
            jax.ShapeDtypeStruct((n, D_MODEL), F32),
            jax.ShapeDtypeStruct((n * PAYLOAD_CHUNKS, LANES), F32),
            jax.ShapeDtypeStruct((SUBLANES, n), F32),
            jax.ShapeDtypeStruct((SUBLANES, ROUTE_LANES), F32),
        ],
        scratch_shapes=[pltpu.VMEM((1, ROUTE_LANES), F32)],
        compiler_params=_params("arbitrary"),
        name="outproj_route",
    )(*inputs, w_bf16, x, mod, ffn_norm_g.reshape(1, D_MODEL), route_w, route_b)


def _router_params(grp_w, grp_b, rtr_w, rtr_b):
    w = jnp.concatenate([grp_w, jnp.moveaxis(rtr_w, 0, 1).reshape(D_MODEL, N_EXPERTS)], axis=1)
    w = jnp.pad(w, ((0, 0), (0, ROUTE_LANES - w.shape[1])))
    hi = w.astype(BF16)
    lo = (w - hi.astype(F32)).astype(BF16)
    b = jnp.concatenate([grp_b, rtr_b.reshape(N_EXPERTS)])
    b = jnp.pad(b, (0, ROUTE_LANES - b.shape[0])).reshape(1, ROUTE_LANES)
    return jnp.concatenate([hi, lo], axis=1), b


def _sorted_tiles(n_tok):
    return n_tok // FFN_TILE + N_GROUPS


def _dispatch_plan(route_t, counts, n_tok):
    n_tiles = _sorted_tiles(n_tok)
    count = counts[0, :N_GROUPS].astype(jnp.int32)
    tiles = (count + (FFN_TILE - 1)) // FFN_TILE
    start = (jnp.cumsum(tiles) - tiles) * FFN_TILE
    tile_end = jnp.cumsum(tiles)
    j = jnp.arange(n_tiles, dtype=jnp.int32)
    tile_group = jnp.minimum(jnp.sum(j[:, None] >= tile_end[None, :], axis=1), N_GROUPS - 1).astype(jnp.int32)
    group_end = jnp.take(start + count, tile_group)
    valid = jnp.where(j < tile_end[-1], jnp.clip(group_end - j * FFN_TILE, 0, FFN_TILE), 0).astype(jnp.int32)
    where = jnp.concatenate([start, route_t[:2].astype(jnp.int32).reshape(-1)])
    return where, tile_group, valid


def _sorted_row(where_ref, n_tok, tok):
    return where_ref[where_ref[N_GROUPS + ROUTE_GROUP * n_tok + tok]] + where_ref[N_GROUPS + ROUTE_RANK * n_tok + tok]


def _token_rows(t, pitch=ROW_CHUNKS):
    return pl.ds(pl.multiple_of(t * pitch, pitch), pitch)


def _ffn_kernel(tg_ref, valid_ref, where_ref, pay_ref, wg_ref, wu_ref, wd_ref, ys_ref, x_buf, src_ref, sems, *, n_tok):
    j = pl.program_id(0)
    n_tiles = pl.num_programs(0)

    @pl.when(j == 0)
    def _():
        def clear(r, carry):
            src_ref[r] = 0
            return carry

        def place(t, carry):
            src_ref[_sorted_row(where_ref, n_tok, t)] = t
            return carry

        lax.fori_loop(0, src_ref.shape[0], clear, 0, unroll=8)
        lax.fori_loop(0, n_tok, place, 0, unroll=8)

    def gather_copy(tile, r):
        tok = src_ref[tile * FFN_TILE + r]
        return pltpu.make_async_copy(pay_ref.at[_token_rows(tok, PAYLOAD_CHUNKS)],
                                     x_buf.at[tile % 2, _token_rows(r, PAYLOAD_CHUNKS)], sems.at[tile % 2])

    def start_gather(tile):
        def issue_pair(r2, carry):
            for k in range(2):
                gather_copy(tile, 2 * r2 + k).start(priority=k)
            return carry

        def issue(r, carry):
            gather_copy(tile, r).start()
            return carry

        @pl.when(valid_ref[tile] == FFN_TILE)
        def _():
            lax.fori_loop(0, FFN_TILE // 2, issue_pair, 0, unroll=4)

        @pl.when(valid_ref[tile] < FFN_TILE)
        def _():
            lax.fori_loop(0, valid_ref[tile], issue, 0)

    @pl.when(j == 0)
    def _():
        x_buf[...] = jnp.zeros_like(x_buf)
        start_gather(j)

    @pl.when(j + 1 < n_tiles)
    def _():
        @pl.when(valid_ref[j + 1] > 0)
        def _():
            start_gather(j + 1)

    @pl.when(valid_ref[j] == 0)
    def _():
        ys_ref[...] = jnp.zeros_like(ys_ref)

    @pl.when(valid_ref[j] > 0)
    def _():
        slot = j % 2
        n = pl.multiple_of(valid_ref[j] * PAYLOAD_CHUNKS, PAYLOAD_CHUNKS)
        pltpu.make_async_copy(pay_ref.at[pl.ds(0, n)], x_buf.at[slot, pl.ds(0, n)], sems.at[slot]).wait()
        x = _load_row_major(x_buf.at[slot], FFN_TILE, PAYLOAD_CHUNKS).astype(BF16)
        w = x_buf[slot, pl.ds(ROW_CHUNKS, FFN_TILE, stride=PAYLOAD_CHUNKS), :]
        y = None
        for e in range(EXPERTS_PER_GROUP):
            hidden = _silu(_dot(x, wg_ref[e])) * _dot(x, wu_ref[e]) * w[:, e:e + 1]
            part = _dot(hidden.astype(BF16), wd_ref[e])
            y = part if y is None else y + part
        _store_row_major(ys_ref, y)


def _expert_ffn(n_tok, tile_group, valid, where, payload, wg, wu, wd):
    group = lambda j, tg, va, sr: (tg[j], 0, 0)
    n_tiles = _sorted_tiles(n_tok)
    return pl.pallas_call(
        functools.partial(_ffn_kernel, n_tok=n_tok),
        grid_spec=pltpu.PrefetchScalarGridSpec(
            num_scalar_prefetch=3,
            grid=(n_tiles,),
            in_specs=[
                pl.BlockSpec(memory_space=pl.ANY),
                pl.BlockSpec((EXPERTS_PER_GROUP, D_MODEL, EXPERT_FF), group),
                pl.BlockSpec((EXPERTS_PER_GROUP, D_MODEL, EXPERT_FF), group),
                pl.BlockSpec((EXPERTS_PER_GROUP, EXPERT_FF, D_MODEL), group),
            ],
            out_specs=pl.BlockSpec((FFN_TILE * ROW_CHUNKS, LANES), lambda j, tg, va, sr: (j, 0)),
            scratch_shapes=[pltpu.VMEM((2, FFN_TILE * PAYLOAD_CHUNKS, LANES), F32),
                            pltpu.SMEM((n_tiles * FFN_TILE,), jnp.int32), pltpu.SemaphoreType.DMA((2,))],
        ),
        out_shape=jax.ShapeDtypeStruct((n_tiles * FFN_TILE * ROW_CHUNKS, LANES), F32),
        compiler_params=_params("arbitrary"),
        name="moe_ffn",
    )(tile_group, valid, where, payload, wg, wu, wd)


def _combine_kernel(where_ref, x_ref, mod_ref, ng_ref, nmod_ref, ys_ref, o_ref, *rest, n_tok, last_layer):
    if last_layer:
        y_buf, sems = rest
    else:
        h_ref, y_buf, sems = rest
    i = pl.program_id(0)
    n_tiles = pl.num_programs(0)

    def row_copy(tile, t, k):
        slot = tile % 2
        d = _sorted_row(where_ref, n_tok, tile * COMBINE_TILE + t)
        return pltpu.make_async_copy(ys_ref.at[_token_rows(d)], y_buf.at[slot, _token_rows(t)], sems.at[slot, k])

    def start_tile(tile):
        def issue(t2, carry):
            for k in range(2):
                row_copy(tile, 2 * t2 + k, k).start(priority=k)
            return carry

        lax.fori_loop(0, COMBINE_TILE // 2, issue, 0, unroll=4)

    @pl.when(i == 0)
    def _():
        start_tile(i)

    @pl.when(i + 1 < n_tiles)
    def _():
        start_tile(i + 1)

    slot = i % 2
    half = COMBINE_TILE * ROW_CHUNKS // 2
    for k in range(2):
        pltpu.make_async_copy(ys_ref.at[pl.ds(0, half)], y_buf.at[slot, pl.ds(0, half)], sems.at[slot, k]).wait()
    x = x_ref[...] + mod_ref[0, 5:6, :] * _load_row_major(y_buf.at[slot], COMBINE_TILE)
    if last_layer:
        o_ref[...] = _rms(x) * ng_ref[...]
    else:
        o_ref[...] = x
        h_ref[...] = _modulated_norm(x, ng_ref[...], nmod_ref[0, 0:1, :], nmod_ref[0, 1:2, :]).astype(h_ref.dtype)


def _combine(stream, where, ys, x, mod, norm_g, next_mod, last_layer):
    n = stream.n_tok
    first_slot, seq_len = stream.first_slot, stream.seq_len
    slot = (lambda i: 0) if first_slot == 0 else (lambda i: first_slot + (i * COMBINE_TILE) // seq_len)
    row = lambda i, w: (i, 0)
    out_specs = [pl.BlockSpec((COMBINE_TILE, D_MODEL), row)]
    out_shape = [jax.ShapeDtypeStruct((n, D_MODEL), F32)]
    if not last_layer:
        out_specs.append(pl.BlockSpec((COMBINE_TILE, D_MODEL), row))
        out_shape.append(jax.ShapeDtypeStruct((n, D_MODEL), BF16))
    return pl.pallas_call(
        functools.partial(_combine_kernel, n_tok=n, last_layer=last_layer),
        grid_spec=pltpu.PrefetchScalarGridSpec(
            num_scalar_prefetch=1,
            grid=(n // COMBINE_TILE,),
            in_specs=[
                pl.BlockSpec((COMBINE_TILE, D_MODEL), row),
                pl.BlockSpec((1, 6, D_MODEL), lambda i, w: (slot(i), 0, 0)),
                pl.BlockSpec((1, D_MODEL), lambda i, w: (0, 0)),
                pl.BlockSpec((1, 6, D_MODEL), lambda i, w: (slot(i), 0, 0)),
                pl.BlockSpec(memory_space=pl.ANY),
            ],
            out_specs=out_specs,
            scratch_shapes=[pltpu.VMEM((2, COMBINE_TILE * ROW_CHUNKS, LANES), F32),
                            pltpu.SemaphoreType.DMA((2, 2))],
        ),
        out_shape=out_shape,
        compiler_params=_params("arbitrary"),
        name="moe_combine",
    )(where, x, mod, norm_g.reshape(1, D_MODEL), next_mod, ys)


def _moe(stream, payload, route_t, counts, wg, wu, wd, x, mod, norm_g, next_mod, last_layer):
    n = stream.n_tok
    where, tile_group, valid = _dispatch_plan(route_t, counts, n)
    ys = _expert_ffn(n, tile_group, valid, where, payload, wg, wu, wd)
    return _combine(stream, where, ys, x, mod, norm_g, next_mod, last_layer)


IN1_COL = 1024
IN1_ROW_TILE = 1024
QK_BLOCKS = 2 * RET_QK_WIDTH // IN1_COL
Q_BLOCKS = RET_QK_WIDTH // IN1_COL
V_BLOCKS = RET_V_WIDTH // IN1_COL


def _inproj1_qk_kernel(h_ref, w_ref, cos_ref, sin_ref, o_ref, *, rope):
    p = _dot(h_ref[...], w_ref[...])
    scale = jnp.where(pl.program_id(0) < Q_BLOCKS, 1.0, RET_DK ** -0.5)
    cos = cos_ref[...]
    sin = sin_ref[...]
    for hh in range(IN1_COL // RET_DK):
        ph = p[:, hh * RET_DK:(hh + 1) * RET_DK]
        if rope:
            ph = _rope(ph, cos, sin)
        o_ref[:, hh * RET_DK:(hh + 1) * RET_DK] = (ph * scale).astype(o_ref.dtype)


def _inproj1_vg_kernel(h_ref, w_ref, o_ref):
    p = _dot(h_ref[...], w_ref[...])
    o_ref[...] = jnp.where(pl.program_id(0) < V_BLOCKS, p, _silu(p)).astype(o_ref.dtype)


def _inproj1(stream, h, w_bf16, cos, sin):
    n = stream.n_tok
    tm = IN1_ROW_TILE
    tiles_per_seq = max(stream.seq_len // tm, 1)
    rope_map = (lambda j, i: (i % tiles_per_seq, 0)) if stream.rope else (lambda j, i: (0, 0))
    h_spec = pl.BlockSpec((tm, D_MODEL), lambda j, i: (i, 0))
    out_spec = pl.BlockSpec((tm, IN1_COL), lambda j, i: (i, j))
    qk = pl.pallas_call(
        functools.partial(_inproj1_qk_kernel, rope=stream.rope),
        grid=(QK_BLOCKS, n // tm),
        in_specs=[
            h_spec,
            pl.BlockSpec((D_MODEL, IN1_COL), lambda j, i: (0, j)),
            pl.BlockSpec((tm, RET_DK), rope_map),
            pl.BlockSpec((tm, RET_DK), rope_map),
        ],
        out_specs=out_spec,
        out_shape=jax.ShapeDtypeStruct((n, QK_BLOCKS * IN1_COL), BF16),
        compiler_params=_params("parallel", "parallel"),
        name="inproj1_qk",
    )(h, w_bf16, cos, sin)
    vg_blocks = IN1_WIDTH // IN1_COL - QK_BLOCKS
    vg = pl.pallas_call(
        _inproj1_vg_kernel,
        grid=(vg_blocks, n // tm),
        in_specs=[h_spec, pl.BlockSpec((D_MODEL, IN1_COL), lambda j, i: (0, QK_BLOCKS + j))],
        out_specs=out_spec,
        out_shape=jax.ShapeDtypeStruct((n, vg_blocks * IN1_COL), BF16),
        compiler_params=_params("parallel", "parallel"),
        name="inproj1_vg",
    )(h, w_bf16)
    return qk, vg


def _retention_kernel(*refs, n_chunks, has_state, heads, unroll):
    if has_state:
        decay_ref, q_ref, k_ref, v_ref, gf_ref, gb_ref, sf0_ref, sb0_ref, o_ref, inc_ref, seen_ref = refs
    else:
        decay_ref, q_ref, k_ref, v_ref, gf_ref, gb_ref, o_ref, sf_ref, sb_ref, inc_ref, seen_ref = refs
    c = RET_CHUNK
    row = lax.broadcasted_iota(jnp.int32, (c, c), 0).astype(F32)
    col = lax.broadcasted_iota(jnp.int32, (c, c), 1).astype(F32)
    pos = lax.broadcasted_iota(jnp.int32, (c, 1), 0).astype(F32)

    def chunk(i):
        return pl.ds(pl.multiple_of(i * c, c), c)

    diff = [row - col, col - row]
    qk_cols = [slice(g * RET_DK, (g + 1) * RET_DK) for g in range(heads)]
    v_cols = [slice(g * RET_DV, (g + 1) * RET_DV) for g in range(heads)]
    intra, q_dec, k_dec, chunk_dec = [], [], [], []
    for g in range(heads):
        hd = pl.program_id(1) * heads + g
        lg = [-jnp.exp(jnp.full((1, 1), decay_ref[d, hd], F32)) for d in range(2)]
        intra.append([jnp.where(diff[d] >= 0, jnp.exp(lg[d] * jnp.maximum(diff[d], 0.0)), 0.0) for d in range(2)])
        q_dec.append([jnp.exp(lg[0] * (pos + 1.0)), jnp.exp(lg[1] * (c - pos))])
        k_dec.append([jnp.exp(lg[0] * ((c - 1.0) - pos)), jnp.exp(lg[1] * pos)])
        chunk_dec.append([jnp.exp(lg[d] * float(c)) for d in range(2)])

    def increments(i, carry):
        rows = chunk(i)
        for g in range(heads):
            ki = k_ref[rows, qk_cols[g]].astype(F32)
            vi = v_ref[rows, v_cols[g]]
            for d in range(2):
                inc_ref[g, d, i] = _dot_tn((ki * k_dec[g][d]).astype(BF16), vi)
        return carry

    lax.fori_loop(0, n_chunks, increments, 0, unroll=unroll)

    for g in range(heads):
        finals = []
        for d in range(2):
            def scan(t, state):
                i = t if d == 0 else n_chunks - 1 - t
                seen_ref[g, d, i] = state.astype(BF16)
                return state * chunk_dec[g][d] + inc_ref[g, d, i]

            if has_state:
                state0 = (sf0_ref if d == 0 else sb0_ref)[0, g]
            else:
                state0 = jnp.zeros((RET_DK, RET_DV), F32)
            finals.append(lax.fori_loop(0, n_chunks, scan, state0))
        if not has_state:
            sf_ref[0, g] = finals[0]
            sb_ref[0, g] = finals[1]

    def outputs(i, carry):
        rows = chunk(i)
        for g in range(heads):
            qi = q_ref[rows, qk_cols[g]]
            vi = v_ref[rows, v_cols[g]]
            s = _dot_nt(qi, k_ref[rows, qk_cols[g]])
            mixed = None
            for d, g_ref in enumerate((gf_ref, gb_ref)):
                y = _dot((s * intra[g][d]).astype(BF16), vi) + _dot(qi, seen_ref[g, d, i]) * q_dec[g][d]
                gated = g_ref[rows, v_cols[g]].astype(F32) * _rms(y)
                mixed = gated if mixed is None else mixed + gated
            o_ref[rows, v_cols[g]] = mixed.astype(o_ref.dtype)
        return carry

    lax.fori_loop(0, n_chunks, outputs, 0, unroll=unroll)


def _retention(stream, qk, vg, decays, states, heads):
    n = stream.n_tok
    t = stream.seq_len
    n_chunks = t // RET_CHUNK
    has_state = states is not None
    qk_w = heads * RET_DK
    v_w = heads * RET_DV
    k_blk = RET_QK_WIDTH // qk_w
    gf_blk = RET_V_WIDTH // v_w
    gb_blk = 2 * gf_blk
    in_specs = [
        pl.BlockSpec(memory_space=pltpu.SMEM),
        pl.BlockSpec((t, qk_w), lambda b, h: (b, h)),
        pl.BlockSpec((t, qk_w), lambda b, h: (b, k_blk + h)),
        pl.BlockSpec((t, v_w), lambda b, h: (b, h)),
        pl.BlockSpec((t, v_w), lambda b, h: (b, gf_blk + h)),
        pl.BlockSpec((t, v_w), lambda b, h: (b, gb_blk + h)),
    ]
    args = [decays, qk, qk, vg, vg, vg]
    state_spec = pl.BlockSpec((1, heads, RET_DK, RET_DV), lambda b, h: (b, h, 0, 0))
    out_specs = [pl.BlockSpec((t, v_w), lambda b, h: (b, h))]
    out_shape = [jax.ShapeDtypeStruct((n, RET_V_WIDTH), BF16)]
    if has_state:
        in_specs += [state_spec, state_spec]
        args += list(states)
    else:
        out_specs += [state_spec, state_spec]
        out_shape += [jax.ShapeDtypeStruct((stream.n_seq, RET_HEADS, RET_DK, RET_DV), F32)] * 2
    return pl.pallas_call(
        functools.partial(_retention_kernel, n_chunks=n_chunks, has_state=has_state, heads=heads,
                          unroll=min(n_chunks, 2)),
        grid=(stream.n_seq, RET_HEADS // heads),
        in_specs=in_specs,
        out_specs=out_specs,
        out_shape=out_shape,
        scratch_shapes=[pltpu.VMEM((heads, 2, n_chunks, RET_DK, RET_DV), F32),
                        pltpu.VMEM((heads, 2, n_chunks, RET_DK, RET_DV), BF16)],
        compiler_params=_params("parallel", "parallel"),
        name="retention",
    )(*args)


def _rope_tables(n_tokens, dim):
    t = jnp.arange(n_tokens)
    pos = jnp.stack([t // GRID_W, t % GRID_W]).astype(F32)
    n_freq = dim // 4
    freqs = ROPE_THETA ** (-jnp.arange(n_freq, dtype=F32) / n_freq)
    ang = pos[:, :, None] * freqs
    cos, sin = jnp.cos(ang), jnp.sin(ang)
    cos_t = jnp.concatenate([cos[0], cos[0], cos[1], cos[1]], axis=-1)
    sin_t = jnp.concatenate([-sin[0], sin[0], -sin[1], sin[1]], axis=-1)
    return cos_t, sin_t


def kernel(x_prompt, x_sample, cache_k0, cache_v0, state_ret_fwd1, state_ret_bwd1, c, c_ctx,
           norm_mix_g0, mod_w0, mod_b0, in_w0, conv_w0, conv_b0, conv_norm_g0, conv_norm_b0,
           q_norm_g0, k_norm_g0, out_w0, norm_ffn_g0,
           moe_grp_w0, moe_grp_b0, moe_rtr_w0, moe_rtr_b0, moe_w_gate0, moe_w_up0, moe_w_down0,
           norm_mix_g1, mod_w1, mod_b1, in_w1, ret_decay_fwd1, ret_decay_bwd1, out_w1, norm_ffn_g1,
           moe_grp_w1, moe_grp_b1, moe_rtr_w1, moe_rtr_b1, moe_w_gate1, moe_w_up1, moe_w_down1,
           final_norm_g):
    batch, seq, d = x_prompt.shape
    dec_batch, dec_seq, _ = x_sample.shape
    assert d == D_MODEL and 1 + dec_batch <= MOD_SLOTS
    ctx = _Stream(batch, seq, first_slot=0, rope=False)
    lat = _Stream(dec_batch, dec_seq, first_slot=1, rope=True)
    assert ctx.n_tok % ROW_TILE == 0 and lat.n_tok % ROW_TILE == 0 and ROW_CHUNKS == SUBLANES
    assert seq % CONV_TILE == 0 and dec_seq % ROW_TILE == 0 and ROW_TILE % seq == 0

    cond = jnp.concatenate([c_ctx[None, :], c, jnp.zeros((MOD_SLOTS - 1 - dec_batch, d), F32)], axis=0)
    mod0 = _modulation(cond, mod_w0, mod_b0)
    mod1 = _modulation(cond, mod_w1, mod_b1)
    cos, sin = _rope_tables(dec_seq, HEAD_DIM)
    decays = jnp.stack([ret_decay_fwd1, ret_decay_bwd1]).astype(F32)

    in_w0_b = in_w0.astype(BF16)
    out_w0_b = out_w0.astype(BF16)
    in_w1_b = in_w1.astype(BF16)
    out_w1_b = out_w1.astype(BF16)
    moe0 = (moe_w_gate0.astype(BF16), moe_w_up0.astype(BF16), moe_w_down0.astype(BF16))
    moe1 = (moe_w_gate1.astype(BF16), moe_w_up1.astype(BF16), moe_w_down1.astype(BF16))
    route0 = _router_params(moe_grp_w0, moe_grp_b0, moe_rtr_w0, moe_rtr_b0)
    route1 = _router_params(moe_grp_w1, moe_grp_b1, moe_rtr_w1, moe_rtr_b1)

    def run(stream, x, cache, states):
        x = x.reshape(stream.n_tok, d)
        kv_dtype = BF16 if stream.rope else F32
        a, q, k, v = _inproj0(stream, x, mod0, norm_mix_g0, in_w0_b, q_norm_g0, k_norm_g0, cos, sin, kv_dtype)
        a = _conv_branch(stream, a, conv_w0, conv_b0, conv_norm_g0, conv_norm_b0)
        tq = min(stream.seq_len, 256)
        o = _attention(stream, q, k, v, cache, tq, kv_heads=1 if cache is not None else N_KV_HEADS)
        x, *routed = _outproj_route(stream, [a, o], out_w0_b, x, mod0, norm_ffn_g0, *route0)
        x, h = _moe(stream, *routed, *moe0, x, mod0, norm_mix_g1, mod1, last_layer=False)
        qk, vg = _inproj1(stream, h, in_w1_b, cos, sin)
        ret = _retention(stream, qk, vg, decays, states, heads=2 if states is not None else RET_HEADS)
        x, *routed = _outproj_route(stream, [ret[0]], out_w1_b, x, mod1, norm_ffn_g1, *route1)
        y, = _moe(stream, *routed, *moe1, x, mod1, final_norm_g, mod1, last_layer=True)
        return y.reshape(stream.n_seq, stream.seq_len, d), k, v, ret[1:]

    y_prompt, k_ctx, v_ctx, new_states = run(ctx, x_prompt, None, None)
    y_sample, _, _, _ = run(lat, x_sample, (cache_k0, cache_v0), (state_ret_fwd1, state_ret_bwd1))
    return (y_prompt, y_sample, k_ctx, v_ctx, new_states[0], new_states[1])
```

```python
import functools

import jax
import jax.numpy as jnp
from jax import lax
from jax.experimental import pallas as pl
from jax.experimental.pallas import tpu as pltpu

D_MODEL = 1024
GRID_W = 64
EPS = 1e-6
CONV_CH = 512
CONV_WIDTH = 31
CONV_HALF = CONV_WIDTH // 2
N_Q_HEADS = 8
N_KV_HEADS = 2
Q_PER_KV = N_Q_HEADS // N_KV_HEADS
HEAD_DIM = 128
ROPE_THETA = 10000.0
ATTN_WIDTH = N_Q_HEADS * HEAD_DIM
KV_WIDTH = N_KV_HEADS * HEAD_DIM
IN0_WIDTH = 2 * CONV_CH + ATTN_WIDTH + 2 * KV_WIDTH
RET_HEADS = 8
RET_DK = 128
RET_DV = 256
RET_CHUNK = 128
RET_QK_WIDTH = RET_HEADS * RET_DK
RET_V_WIDTH = RET_HEADS * RET_DV
IN1_WIDTH = 2 * RET_QK_WIDTH + 3 * RET_V_WIDTH
N_GROUPS = 4
EXPERTS_PER_GROUP = 4
N_EXPERTS = N_GROUPS * EXPERTS_PER_GROUP
EXPERT_FF = 512

LANES = 128
SUBLANES = 8
VMEM_LIMIT_BYTES = 56 * 1024 * 1024

MOD_SLOTS = 8
ROW_TILE = 512
ROW_CHUNKS = D_MODEL // LANES
PAYLOAD_CHUNKS = 2 * ROW_CHUNKS
FFN_TILE = 256
COMBINE_TILE = 256
ROUTE_GROUP, ROUTE_RANK = 0, 1
CONV_TILE = 256
CONV_HALO = 16
CONV_ROWS = 64
ROUTE_LANES = LANES
EXPERT_LANE0 = N_GROUPS
NEG_BIG = -1e30
LOG2_E = 1.4426950408889634
ATTN_KV_CHUNK = 512

F32 = jnp.float32
BF16 = jnp.bfloat16


def _params(*semantics):
    return pltpu.CompilerParams(dimension_semantics=semantics, vmem_limit_bytes=VMEM_LIMIT_BYTES)


def _sigmoid(x):
    return 1.0 / (1.0 + jnp.exp(-x))


def _silu(x):
    return x * _sigmoid(x)


def _rms(x):
    return x * lax.rsqrt(jnp.mean(x * x, axis=-1, keepdims=True) + EPS)


def _rope(x, cos, sin):
    lane = lax.broadcasted_iota(jnp.int32, x.shape, 1)
    take_upper = (lane % (HEAD_DIM // 2)) < (HEAD_DIM // 4)
    partner = jnp.where(take_upper, pltpu.roll(x, HEAD_DIM - HEAD_DIM // 4, 1), pltpu.roll(x, HEAD_DIM // 4, 1))
    return x * cos + partner * sin


def _dot(a, b):
    return jnp.dot(a, b, preferred_element_type=F32)


def _dot_nt(a, b):
    return lax.dot_general(a, b, (((1,), (1,)), ((), ())), preferred_element_type=F32)


def _dot_tn(a, b):
    return lax.dot_general(a, b, (((0,), (0,)), ((), ())), preferred_element_type=F32)


def _mod_kernel(c_ref, w_ref, b_ref, o_ref):
    c = c_ref[...]
    o_ref[...] = _dot(_silu(c).astype(BF16), w_ref[...].astype(BF16)) + b_ref[...]


def _modulation(cond, mod_w, mod_b):
    n_out = mod_w.shape[1]
    col = D_MODEL
    out = pl.pallas_call(
        _mod_kernel,
        grid=(n_out // col,),
        in_specs=[
            pl.BlockSpec((MOD_SLOTS, D_MODEL), lambda j: (0, 0)),
            pl.BlockSpec((D_MODEL, col), lambda j: (0, j)),
            pl.BlockSpec((1, col), lambda j: (0, j)),
        ],
        out_specs=pl.BlockSpec((MOD_SLOTS, col), lambda j: (0, j)),
        out_shape=jax.ShapeDtypeStruct((MOD_SLOTS, n_out), F32),
        compiler_params=_params("parallel"),
        name="modulation",
    )(cond, mod_w, mod_b.reshape(1, n_out))
    return out.reshape(MOD_SLOTS, 6, D_MODEL)


def _slot_map(stream):
    if stream.first_slot == 0:
        return lambda i: 0
    return lambda i: stream.first_slot + (i * ROW_TILE) // stream.seq_len


class _Stream:
    def __init__(self, n_seq, seq_len, first_slot, rope):
        self.n_seq = n_seq
        self.seq_len = seq_len
        self.first_slot = first_slot
        self.rope = rope
        self.n_tok = n_seq * seq_len


def _modulated_norm(x, g, shift, scale):
    return _rms(x) * g * (1.0 + scale) + shift


def _inproj0_kernel(x_ref, mod_ref, g_ref, w_ref, qg_ref, kg_ref, cos_ref, sin_ref,
                    a_ref, q_ref, k_ref, v_ref, *, rope):
    h = _modulated_norm(x_ref[...], g_ref[...], mod_ref[0, 0:1, :], mod_ref[0, 1:2, :])
    p = _dot(h.astype(BF16), w_ref[...])
    a_ref[...] = p[:, :CONV_CH] * _sigmoid(p[:, CONV_CH:2 * CONV_CH])
    q0 = 2 * CONV_CH
    k0 = q0 + ATTN_WIDTH
    v0 = k0 + KV_WIDTH
    cos = cos_ref[...]
    sin = sin_ref[...]
    q_scale = HEAD_DIM ** -0.5 * LOG2_E

    def normed_head(col, partner_col, gain_ref):
        ph = p[:, col:col + HEAD_DIM]
        inv = lax.rsqrt(jnp.mean(ph * ph, axis=-1, keepdims=True) + EPS)
        out = ph * inv * gain_ref[0:1, :]
        if rope:
            out = out * cos + (p[:, partner_col:partner_col + HEAD_DIM] * inv * gain_ref[1:2, :]) * sin
        return out

    for hh in range(N_Q_HEADS):
        qh = normed_head(q0 + hh * HEAD_DIM, IN0_WIDTH + hh * HEAD_DIM, qg_ref)
        q_ref[:, hh * HEAD_DIM:(hh + 1) * HEAD_DIM] = (qh * q_scale).astype(q_ref.dtype)
    seqs, _, rows, _ = k_ref.shape
    for hh in range(N_KV_HEADS):
        kh = normed_head(k0 + hh * HEAD_DIM, IN0_WIDTH + ATTN_WIDTH + hh * HEAD_DIM, kg_ref)
        vh = p[:, v0 + hh * HEAD_DIM:v0 + (hh + 1) * HEAD_DIM]
        for sq in range(seqs):
            k_ref[sq, hh] = kh[sq * rows:(sq + 1) * rows].astype(k_ref.dtype)
            v_ref[sq, hh] = vh[sq * rows:(sq + 1) * rows].astype(v_ref.dtype)


def _inproj0(stream, x, mod, norm_g, w_bf16, q_gains, k_gains, cos, sin, kv_dtype):
    n = stream.n_tok
    tiles_per_seq = max(stream.seq_len // ROW_TILE, 1)
    rope_map = (lambda i: (i % tiles_per_seq, 0)) if stream.rope else (lambda i: (0, 0))
    slot = _slot_map(stream)
    row = lambda i: (i, 0)
    fixed = lambda i: (0, 0)
    seqs_per_tile = max(ROW_TILE // stream.seq_len, 1)
    kv_spec = pl.BlockSpec((seqs_per_tile, N_KV_HEADS, ROW_TILE // seqs_per_tile, HEAD_DIM),
                           lambda i: (i // tiles_per_seq, 0, i % tiles_per_seq, 0))
    kv_shape = jax.ShapeDtypeStruct((stream.n_seq, N_KV_HEADS, stream.seq_len, HEAD_DIM), kv_dtype)
    return pl.pallas_call(
        functools.partial(_inproj0_kernel, rope=stream.rope),
        grid=(n // ROW_TILE,),
        in_specs=[
            pl.BlockSpec((ROW_TILE, D_MODEL), row),
            pl.BlockSpec((1, 6, D_MODEL), lambda i: (slot(i), 0, 0)),
            pl.BlockSpec((1, D_MODEL), fixed),
            pl.BlockSpec((D_MODEL, w_bf16.shape[1]), fixed),
            pl.BlockSpec((2, HEAD_DIM), fixed),
            pl.BlockSpec((2, HEAD_DIM), fixed),
            pl.BlockSpec((ROW_TILE, HEAD_DIM), rope_map),
            pl.BlockSpec((ROW_TILE, HEAD_DIM), rope_map),
        ],
        out_specs=[
            pl.BlockSpec((ROW_TILE, CONV_CH), row),
            pl.BlockSpec((ROW_TILE, ATTN_WIDTH), row),
            kv_spec,
            kv_spec,
        ],
        out_shape=[
            jax.ShapeDtypeStruct((n, CONV_CH), F32),
            jax.ShapeDtypeStruct((n, ATTN_WIDTH), BF16),
            kv_shape,
            kv_shape,
        ],
        compiler_params=_params("parallel"),
        name="inproj0",
    )(x, mod, norm_g.reshape(1, D_MODEL), w_bf16, q_gains, k_gains, cos, sin)


def _conv_kernel(prev_ref, main_ref, next_ref, w_ref, b_ref, g_ref, beta_ref, o_ref, pad_ref, acc_ref, *,
                 tiles_per_seq):
    i = pl.program_id(0)
    has_prev = (i % tiles_per_seq) != 0
    has_next = (i % tiles_per_seq) != (tiles_per_seq - 1)
    pad_ref[0:CONV_HALO, :] = jnp.where(has_prev, prev_ref[...], 0.0)
    pad_ref[CONV_HALO:CONV_HALO + CONV_TILE, :] = main_ref[...]
    pad_ref[CONV_HALO + CONV_TILE:, :] = jnp.where(has_next, next_ref[...], 0.0)
    first = CONV_HALO - CONV_HALF
    window = CONV_ROWS + SUBLANES
    for c0 in range(0, CONV_CH, LANES):
        for r0 in range(0, CONV_TILE, CONV_ROWS):
            acc = jnp.zeros((CONV_ROWS, LANES), F32)
            for phase in range(SUBLANES):
                partial = jnp.zeros((window, LANES), F32)
                for j in range(phase, CONV_WIDTH, SUBLANES):
                    base = r0 + j - phase
                    partial = partial + w_ref[j:j + 1, c0:c0 + LANES] * pad_ref[base:base + window, c0:c0 + LANES]
                acc = acc + partial[first + phase:first + phase + CONV_ROWS]
            acc_ref[r0:r0 + CONV_ROWS, c0:c0 + LANES] = acc
    a = acc_ref[...] + b_ref[...]
    mu = jnp.mean(a, axis=-1, keepdims=True)
    d = a - mu
    var = jnp.mean(d * d, axis=-1, keepdims=True)
    y = d * lax.rsqrt(var + EPS) * g_ref[...] + beta_ref[...]
    o_ref[...] = _silu(y).astype(o_ref.dtype)


def _conv_branch(stream, a, conv_w, conv_b, norm_g, norm_b):
    n = stream.n_tok
    tiles_per_seq = stream.seq_len // CONV_TILE
    halo_per_tile = CONV_TILE // CONV_HALO
    n_halo = n // CONV_HALO
    fixed = lambda i: (0, 0)
    return pl.pallas_call(
        functools.partial(_conv_kernel, tiles_per_seq=tiles_per_seq),
        grid=(n // CONV_TILE,),
        in_specs=[
            pl.BlockSpec((CONV_HALO, CONV_CH), lambda i: (jnp.maximum(i * halo_per_tile - 1, 0), 0)),
            pl.BlockSpec((CONV_TILE, CONV_CH), lambda i: (i, 0)),
            pl.BlockSpec((CONV_HALO, CONV_CH), lambda i: (jnp.minimum((i + 1) * halo_per_tile, n_halo - 1), 0)),
            pl.BlockSpec((CONV_WIDTH, CONV_CH), fixed),
            pl.BlockSpec((1, CONV_CH), fixed),
            pl.BlockSpec((1, CONV_CH), fixed),
            pl.BlockSpec((1, CONV_CH), fixed),
        ],
        out_specs=pl.BlockSpec((CONV_TILE, CONV_CH), lambda i: (i, 0)),
        out_shape=jax.ShapeDtypeStruct((n, CONV_CH), BF16),
        scratch_shapes=[
            pltpu.VMEM((CONV_TILE + 2 * CONV_HALO, CONV_CH), F32),
            pltpu.VMEM((CONV_TILE, CONV_CH), F32),
        ],
        compiler_params=_params("parallel"),
        name="conv_branch",
    )(a, a, a, conv_w, conv_b.reshape(1, CONV_CH), norm_g.reshape(1, CONV_CH), norm_b.reshape(1, CONV_CH))


def _attn_kernel(*refs, tq, past):
    if past:
        q_ref, k_ref, v_ref, kc_ref, vc_ref, o_ref, ks_ref, vs_ref = refs
    else:
        q_ref, k_ref, v_ref, o_ref, ks_ref, vs_ref = refs
    kv_heads, n_keys, _ = ks_ref.shape
    group_w = Q_PER_KV * HEAD_DIM

    @pl.when(pl.program_id(2) == 0)
    def _():
        lane = lax.broadcasted_iota(jnp.int32, (n_keys, HEAD_DIM), 1)
        for hk in range(kv_heads):
            vs_ref[hk, :, HEAD_DIM:] = jnp.where(lane == 0, 1.0, 0.0).astype(BF16)
            if past:
                ks_ref[hk, 0:past, :] = kc_ref[0, hk].astype(BF16)
                vs_ref[hk, 0:past, 0:HEAD_DIM] = vc_ref[0, hk].astype(BF16)
            ks_ref[hk, past:, :] = k_ref[0, hk].astype(BF16)
            vs_ref[hk, past:, 0:HEAD_DIM] = v_ref[0, hk].astype(BF16)

    for hk in range(kv_heads):
        q = q_ref[:, hk * group_w:(hk + 1) * group_w]
        q4 = jnp.concatenate([q[:, g * HEAD_DIM:(g + 1) * HEAD_DIM] for g in range(Q_PER_KV)], axis=0)
        rows = q4.shape[0]
        m = jnp.full((rows, 1), NEG_BIG, F32)
        acc = jnp.zeros((rows, 2 * HEAD_DIM), F32)
        for c0 in range(0, n_keys, ATTN_KV_CHUNK):
            c1 = min(c0 + ATTN_KV_CHUNK, n_keys)
            s = _dot_nt(q4, ks_ref[hk, c0:c1, :])
            m_new = jnp.maximum(m, jnp.max(s, axis=-1, keepdims=True))
            p = jnp.exp2(s - m_new).astype(BF16)
            acc = jnp.exp2(m - m_new) * acc + _dot(p, vs_ref[hk, c0:c1, :])
            m = m_new
        o = acc[:, 0:HEAD_DIM] * (1.0 / acc[:, HEAD_DIM:HEAD_DIM + 1])
        for g in range(Q_PER_KV):
            col = hk * group_w + g * HEAD_DIM
            o_ref[:, col:col + HEAD_DIM] = o[g * tq:(g + 1) * tq].astype(o_ref.dtype)


def _attention(stream, q, k, v, cache, tq, kv_heads):
    n = stream.n_tok
    t = stream.seq_len
    q_tiles = t // tq
    width = kv_heads * Q_PER_KV * HEAD_DIM
    in_specs = [
        pl.BlockSpec((tq, width), lambda b, h, i: (b * q_tiles + i, h)),
        pl.BlockSpec((1, kv_heads, t, HEAD_DIM), lambda b, h, i: (b, h, 0, 0)),
        pl.BlockSpec((1, kv_heads, t, HEAD_DIM), lambda b, h, i: (b, h, 0, 0)),
    ]
    args = [q, k, v]
    past = 0
    if cache is not None:
        past = cache[0].shape[2]
        in_specs += [pl.BlockSpec((1, kv_heads, past, HEAD_DIM), lambda b, h, i: (b, h, 0, 0))] * 2
        args += list(cache)
    return pl.pallas_call(
        functools.partial(_attn_kernel, tq=tq, past=past),
        grid=(stream.n_seq, N_KV_HEADS // kv_heads, q_tiles),
        in_specs=in_specs,
        out_specs=pl.BlockSpec((tq, width), lambda b, h, i: (b * q_tiles + i, h)),
        out_shape=jax.ShapeDtypeStruct((n, ATTN_WIDTH), BF16),
        scratch_shapes=[pltpu.VMEM((kv_heads, past + t, HEAD_DIM), BF16),
                        pltpu.VMEM((kv_heads, past + t, 2 * HEAD_DIM), BF16)],
        compiler_params=_params("parallel", "parallel", "arbitrary"),
        name="attention",
    )(*args)


def _route(logits, running):
    rows = logits.shape[0]
    lane = lax.broadcasted_iota(jnp.int32, logits.shape, 1)
    lane_f = lane.astype(F32)
    far = float(ROUTE_LANES)
    is_group = lane < N_GROUPS
    gl = jnp.where(is_group, logits, NEG_BIG)
    gmax = jnp.max(gl, axis=-1, keepdims=True)
    gsum = jnp.sum(jnp.where(is_group, jnp.exp(gl - gmax), 0.0), axis=-1, keepdims=True)
    g_w = 1.0 / gsum
    gidx = jnp.min(jnp.where(gl == gmax, lane_f, far), axis=-1, keepdims=True)
    lo = EXPERT_LANE0 + EXPERTS_PER_GROUP * gidx
    in_group = (lane_f >= lo) & (lane_f < lo + EXPERTS_PER_GROUP)
    el = jnp.where(in_group, logits, NEG_BIG)
    v1 = jnp.max(el, axis=-1, keepdims=True)
    i1 = jnp.min(jnp.where(el == v1, lane_f, far), axis=-1, keepdims=True)
    el2 = jnp.where(lane_f == i1, NEG_BIG, el)
    v2 = jnp.max(el2, axis=-1, keepdims=True)
    i2 = jnp.min(jnp.where(el2 == v2, lane_f, far), axis=-1, keepdims=True)
    e2 = jnp.exp(v2 - v1)
    w1 = g_w / (1.0 + e2)
    w2 = w1 * e2
    in_my_group = lane_f == gidx
    chosen = in_my_group.astype(BF16)
    earlier = (lax.broadcasted_iota(jnp.int32, (rows, rows), 0)
               > lax.broadcasted_iota(jnp.int32, (rows, rows), 1)).astype(BF16)
    before = running + _dot(earlier, chosen)
    rank = jnp.sum(jnp.where(in_my_group, before, 0.0), axis=-1, keepdims=True)
    record = jnp.where(lane == ROUTE_GROUP, gidx, jnp.where(lane == ROUTE_RANK, rank, 0.0))
    weights = jnp.where(lane_f == i1 - lo, w1, 0.0) + jnp.where(lane_f == i2 - lo, w2, 0.0)
    return record, weights, running + jnp.sum(chosen.astype(F32), axis=0, keepdims=True)


def _outproj_kernel(*refs, widths):
    n_in = len(widths)
    in_refs = refs[:n_in]
    w_ref, x_ref, mod_ref, g_ref, rw_ref, rb_ref, x1_ref, pay_ref, route_t_ref, count_ref, run_ref = refs[n_in:]

    @pl.when(pl.program_id(0) == 0)
    def _():
        run_ref[...] = jnp.zeros_like(run_ref)

    acc = None
    off = 0
    for r, width in zip(in_refs, widths):
        part = _dot(r[...], w_ref[off:off + width, :])
        acc = part if acc is None else acc + part
        off += width
    x1 = x_ref[...] + mod_ref[0, 2:3, :] * acc
    x1_ref[...] = x1
    h = _modulated_norm(x1, g_ref[...], mod_ref[0, 3:4, :], mod_ref[0, 4:5, :])
    h_hi = h.astype(BF16)
    h_lo = (h - h_hi.astype(F32)).astype(BF16)
    both = _dot(h_hi, rw_ref[...])
    logits = (both[:, :ROUTE_LANES] + both[:, ROUTE_LANES:] + _dot(h_lo, rw_ref[:, :ROUTE_LANES]) + rb_ref[...])
    record, weights, running = _route(logits, run_ref[...])
    route_t_ref[...] = record.T[:SUBLANES]
    run_ref[...] = running
    count_ref[...] = jnp.broadcast_to(running, count_ref.shape)
    rows = h.shape[0]
    _store_row_major(pay_ref, h, PAYLOAD_CHUNKS)
    pay_ref[pl.ds(ROW_CHUNKS, rows, stride=PAYLOAD_CHUNKS), :] = weights
    for c in range(ROW_CHUNKS + 1, PAYLOAD_CHUNKS):
        pay_ref[pl.ds(c, rows, stride=PAYLOAD_CHUNKS), :] = jnp.zeros((rows, LANES), F32)


def _store_row_major(ref, x, pitch=ROW_CHUNKS):
    rows = x.shape[0]
    for c in range(ROW_CHUNKS):
        ref[pl.ds(c, rows, stride=pitch), :] = x[:, c * LANES:(c + 1) * LANES]


def _load_row_major(ref, rows, pitch=ROW_CHUNKS):
    return jnp.concatenate([ref[pl.ds(c, rows, stride=pitch), :] for c in range(ROW_CHUNKS)], axis=-1)


def _outproj_route(stream, inputs, w_bf16, x, mod, ffn_norm_g, route_w, route_b):
    n = stream.n_tok
    widths = tuple(a.shape[1] for a in inputs)
    slot = _slot_map(stream)
    row = lambda i: (i, 0)
    fixed = lambda i: (0, 0)
    return pl.pallas_call(
        functools.partial(_outproj_kernel, widths=widths),
        grid=(n // ROW_TILE,),
        in_specs=[pl.BlockSpec((ROW_TILE, width), row) for width in widths] + [
            pl.BlockSpec((sum(widths), D_MODEL), fixed),
            pl.BlockSpec((ROW_TILE, D_MODEL), row),
            pl.BlockSpec((1, 6, D_MODEL), lambda i: (slot(i), 0, 0)),
            pl.BlockSpec((1, D_MODEL), fixed),
            pl.BlockSpec((D_MODEL, 2 * ROUTE_LANES), fixed),
            pl.BlockSpec((1, ROUTE_LANES), fixed),
        ],
        out_specs=[
            pl.BlockSpec((ROW_TILE, D_MODEL), row),
            pl.BlockSpec((ROW_TILE * PAYLOAD_CHUNKS, LANES), row),
            pl.BlockSpec((SUBLANES, ROW_TILE), lambda i: (0, i)),
            pl.BlockSpec((SUBLANES, ROUTE_LANES), fixed),
        ],
        out_shape=[
            jax.ShapeDtypeStruct((n, D_MODEL), F32),
            jax.ShapeDtypeStruct((n * PAYLOAD_CHUNKS, LANES), F32),
            jax.ShapeDtypeStruct((SUBLANES, n), F32),
            jax.ShapeDtypeStruct((SUBLANES, ROUTE_LANES), F32),
        ],
        scratch_shapes=[pltpu.VMEM((1, ROUTE_LANES), F32)],
        compiler_params=_params("arbitrary"),
        name="outproj_route",
    )(*inputs, w_bf16, x, mod, ffn_norm_g.reshape(1, D_MODEL), route_w, route_b)


def _router_params(grp_w, grp_b, rtr_w, rtr_b):
    w = jnp.concatenate([grp_w, jnp.moveaxis(rtr_w, 0, 1).reshape(D_MODEL, N_EXPERTS)], axis=1)
    w = jnp.pad(w, ((0, 0), (0, ROUTE_LANES - w.shape[1])))
    hi = w.astype(BF16)
    lo = (w - hi.astype(F32)).astype(BF16)
    b = jnp.concatenate([grp_b, rtr_b.reshape(N_EXPERTS)])
    b = jnp.pad(b, (0, ROUTE_LANES - b.shape[0])).reshape(1, ROUTE_LANES)
    return jnp.concatenate([hi, lo], axis=1), b


def _sorted_tiles(n_tok):
    return n_tok // FFN_TILE + N_GROUPS


def _dispatch_plan(route_t, counts, n_tok):
    n_tiles = _sorted_tiles(n_tok)
    count = counts[0, :N_GROUPS].astype(jnp.int32)
    tiles = (count + (FFN_TILE - 1)) // FFN_TILE
    start = (jnp.cumsum(tiles) - tiles) * FFN_TILE
    tile_end = jnp.cumsum(tiles)
    j = jnp.arange(n_tiles, dtype=jnp.int32)
    tile_group = jnp.minimum(jnp.sum(j[:, None] >= tile_end[None, :], axis=1), N_GROUPS - 1).astype(jnp.int32)
    group_end = jnp.take(start + count, tile_group)
    valid = jnp.where(j < tile_end[-1], jnp.clip(group_end - j * FFN_TILE, 0, FFN_TILE), 0).astype(jnp.int32)
    where = jnp.concatenate([start, route_t[:2].astype(jnp.int32).reshape(-1)])
    return where, tile_group, valid


def _sorted_row(where_ref, n_tok, tok):
    return where_ref[where_ref[N_GROUPS + ROUTE_GROUP * n_tok + tok]] + where_ref[N_GROUPS + ROUTE_RANK * n_tok + tok]


def _token_rows(t, pitch=ROW_CHUNKS):
    return pl.ds(pl.multiple_of(t * pitch, pitch), pitch)


def _ffn_kernel(tg_ref, valid_ref, where_ref, pay_ref, wg_ref, wu_ref, wd_ref, ys_ref, x_buf, src_ref, sems, *, n_tok):
    j = pl.program_id(0)
    n_tiles = pl.num_programs(0)

    @pl.when(j == 0)
    def _():
        def clear(r, carry):
            src_ref[r] = 0
            return carry

        def place(t, carry):
            src_ref[_sorted_row(where_ref, n_tok, t)] = t
            return carry

        lax.fori_loop(0, src_ref.shape[0], clear, 0, unroll=8)
        lax.fori_loop(0, n_tok, place, 0, unroll=8)

    def gather_copy(tile, r):
        tok = src_ref[tile * FFN_TILE + r]
        return pltpu.make_async_copy(pay_ref.at[_token_rows(tok, PAYLOAD_CHUNKS)],
                                     x_buf.at[tile % 2, _token_rows(r, PAYLOAD_CHUNKS)], sems.at[tile % 2])

    def start_gather(tile):
        def issue_pair(r2, carry):
            for k in range(2):
                gather_copy(tile, 2 * r2 + k).start(priority=k)
            return carry

        def issue(r, carry):
            gather_copy(tile, r).start()
            return carry

        @pl.when(valid_ref[tile] == FFN_TILE)
        def _():
            lax.fori_loop(0, FFN_TILE // 2, issue_pair, 0, unroll=4)

        @pl.when(valid_ref[tile] < FFN_TILE)
        def _():
            lax.fori_loop(0, valid_ref[tile], issue, 0)

    @pl.when(j == 0)
    def _():
        x_buf[...] = jnp.zeros_like(x_buf)
        start_gather(j)

    @pl.when(j + 1 < n_tiles)
    def _():
        @pl.when(valid_ref[j + 1] > 0)
        def _():
            start_gather(j + 1)

    @pl.when(valid_ref[j] == 0)
    def _():
        ys_ref[...] = jnp.zeros_like(ys_ref)

    @pl.when(valid_ref[j] > 0)
    def _():
        slot = j % 2
        n = pl.multiple_of(valid_ref[j] * PAYLOAD_CHUNKS, PAYLOAD_CHUNKS)
        pltpu.make_async_copy(pay_ref.at[pl.ds(0, n)], x_buf.at[slot, pl.ds(0, n)], sems.at[slot]).wait()
        x = _load_row_major(x_buf.at[slot], FFN_TILE, PAYLOAD_CHUNKS).astype(BF16)
        w = x_buf[slot, pl.ds(ROW_CHUNKS, FFN_TILE, stride=PAYLOAD_CHUNKS), :]
        y = None
        for e in range(EXPERTS_PER_GROUP):
            hidden = _silu(_dot(x, wg_ref[e])) * _dot(x, wu_ref[e]) * w[:, e:e + 1]
            part = _dot(hidden.astype(BF16), wd_ref[e])
            y = part if y is None else y + part
        _store_row_major(ys_ref, y)


def _expert_ffn(n_tok, tile_group, valid, where, payload, wg, wu, wd):
    group = lambda j, tg, va, sr: (tg[j], 0, 0)
    n_tiles = _sorted_tiles(n_tok)
    return pl.pallas_call(
        functools.partial(_ffn_kernel, n_tok=n_tok),
        grid_spec=pltpu.PrefetchScalarGridSpec(
            num_scalar_prefetch=3,
            grid=(n_tiles,),
            in_specs=[
                pl.BlockSpec(memory_space=pl.ANY),
                pl.BlockSpec((EXPERTS_PER_GROUP, D_MODEL, EXPERT_FF), group),
                pl.BlockSpec((EXPERTS_PER_GROUP, D_MODEL, EXPERT_FF), group),
                pl.BlockSpec((EXPERTS_PER_GROUP, EXPERT_FF, D_MODEL), group),
            ],
            out_specs=pl.BlockSpec((FFN_TILE * ROW_CHUNKS, LANES), lambda j, tg, va, sr: (j, 0)),
            scratch_shapes=[pltpu.VMEM((2, FFN_TILE * PAYLOAD_CHUNKS, LANES), F32),
                            pltpu.SMEM((n_tiles * FFN_TILE,), jnp.int32), pltpu.SemaphoreType.DMA((2,))],
        ),
        out_shape=jax.ShapeDtypeStruct((n_tiles * FFN_TILE * ROW_CHUNKS, LANES), F32),
        compiler_params=_params("arbitrary"),
        name="moe_ffn",
    )(tile_group, valid, where, payload, wg, wu, wd)


def _combine_kernel(where_ref, x_ref, mod_ref, ng_ref, nmod_ref, ys_ref, o_ref, *rest, n_tok, last_layer):
    if last_layer:
        y_buf, sems = rest
    else:
        h_ref, y_buf, sems = rest
    i = pl.program_id(0)
    n_tiles = pl.num_programs(0)

    def row_copy(tile, t, k):
        slot = tile % 2
        d = _sorted_row(where_ref, n_tok, tile * COMBINE_TILE + t)
        return pltpu.make_async_copy(ys_ref.at[_token_rows(d)], y_buf.at[slot, _token_rows(t)], sems.at[slot, k])

    def start_tile(tile):
        def issue(t2, carry):
            for k in range(2):
                row_copy(tile, 2 * t2 + k, k).start(priority=k)
            return carry

        lax.fori_loop(0, COMBINE_TILE // 2, issue, 0, unroll=4)

    @pl.when(i == 0)
    def _():
        start_tile(i)

    @pl.when(i + 1 < n_tiles)
    def _():
        start_tile(i + 1)

    slot = i % 2
    half = COMBINE_TILE * ROW_CHUNKS // 2
    for k in range(2):
        pltpu.make_async_copy(ys_ref.at[pl.ds(0, half)], y_buf.at[slot, pl.ds(0, half)], sems.at[slot, k]).wait()
    x = x_ref[...] + mod_ref[0, 5:6, :] * _load_row_major(y_buf.at[slot], COMBINE_TILE)
    if last_layer:
        o_ref[...] = _rms(x) * ng_ref[...]
    else:
        o_ref[...] = x
        h_ref[...] = _modulated_norm(x, ng_ref[...], nmod_ref[0, 0:1, :], nmod_ref[0, 1:2, :]).astype(h_ref.dtype)


def _combine(stream, where, ys, x, mod, norm_g, next_mod, last_layer):
    n = stream.n_tok
    first_slot, seq_len = stream.first_slot, stream.seq_len
    slot = (lambda i: 0) if first_slot == 0 else (lambda i: first_slot + (i * COMBINE_TILE) // seq_len)
    row = lambda i, w: (i, 0)
    out_specs = [pl.BlockSpec((COMBINE_TILE, D_MODEL), row)]
    out_shape = [jax.ShapeDtypeStruct((n, D_MODEL), F32)]
    if not last_layer:
        out_specs.append(pl.BlockSpec((COMBINE_TILE, D_MODEL), row))
        out_shape.append(jax.ShapeDtypeStruct((n, D_MODEL), BF16))
    return pl.pallas_call(
        functools.partial(_combine_kernel, n_tok=n, last_layer=last_layer),
        grid_spec=pltpu.PrefetchScalarGridSpec(
            num_scalar_prefetch=1,
            grid=(n // COMBINE_TILE,),
            in_specs=[
                pl.BlockSpec((COMBINE_TILE, D_MODEL), row),
                pl.BlockSpec((1, 6, D_MODEL), lambda i, w: (slot(i), 0, 0)),
                pl.BlockSpec((1, D_MODEL), lambda i, w: (0, 0)),
                pl.BlockSpec((1, 6, D_MODEL), lambda i, w: (slot(i), 0, 0)),
                pl.BlockSpec(memory_space=pl.ANY),
            ],
            out_specs=out_specs,
            scratch_shapes=[pltpu.VMEM((2, COMBINE_TILE * ROW_CHUNKS, LANES), F32),
                            pltpu.SemaphoreType.DMA((2, 2))],
        ),
        out_shape=out_shape,
        compiler_params=_params("arbitrary"),
        name="moe_combine",
    )(where, x, mod, norm_g.reshape(1, D_MODEL), next_mod, ys)


def _moe(stream, payload, route_t, counts, wg, wu, wd, x, mod, norm_g, next_mod, last_layer):
    n = stream.n_tok
    where, tile_group, valid = _dispatch_plan(route_t, counts, n)
    ys = _expert_ffn(n, tile_group, valid, where, payload, wg, wu, wd)
    return _combine(stream, where, ys, x, mod, norm_g, next_mod, last_layer)


IN1_COL = 1024
IN1_ROW_TILE = 1024
QK_BLOCKS = 2 * RET_QK_WIDTH // IN1_COL
Q_BLOCKS = RET_QK_WIDTH // IN1_COL
V_BLOCKS = RET_V_WIDTH // IN1_COL


def _inproj1_qk_kernel(h_ref, w_ref, cos_ref, sin_ref, o_ref, *, rope):
    p = _dot(h_ref[...], w_ref[...])
    scale = jnp.where(pl.program_id(0) < Q_BLOCKS, 1.0, RET_DK ** -0.5)
    cos = cos_ref[...]
    sin = sin_ref[...]
    for hh in range(IN1_COL // RET_DK):
        ph = p[:, hh * RET_DK:(hh + 1) * RET_DK]
        if rope:
            ph = _rope(ph, cos, sin)
        o_ref[:, hh * RET_DK:(hh + 1) * RET_DK] = (ph * scale).astype(o_ref.dtype)


def _inproj1_vg_kernel(h_ref, w_ref, o_ref):
    p = _dot(h_ref[...], w_ref[...])
    o_ref[...] = jnp.where(pl.program_id(0) < V_BLOCKS, p, _silu(p)).astype(o_ref.dtype)


def _inproj1(stream, h, w_bf16, cos, sin):
    n = stream.n_tok
    tm = IN1_ROW_TILE
    tiles_per_seq = max(stream.seq_len // tm, 1)
    rope_map = (lambda j, i: (i % tiles_per_seq, 0)) if stream.rope else (lambda j, i: (0, 0))
    h_spec = pl.BlockSpec((tm, D_MODEL), lambda j, i: (i, 0))
    out_spec = pl.BlockSpec((tm, IN1_COL), lambda j, i: (i, j))
    qk = pl.pallas_call(
        functools.partial(_inproj1_qk_kernel, rope=stream.rope),
        grid=(QK_BLOCKS, n // tm),
        in_specs=[
            h_spec,
            pl.BlockSpec((D_MODEL, IN1_COL), lambda j, i: (0, j)),
            pl.BlockSpec((tm, RET_DK), rope_map),
            pl.BlockSpec((tm, RET_DK), rope_map),
        ],
        out_specs=out_spec,
        out_shape=jax.ShapeDtypeStruct((n, QK_BLOCKS * IN1_COL), BF16),
        compiler_params=_params("parallel", "parallel"),
        name="inproj1_qk",
    )(h, w_bf16, cos, sin)
    vg_blocks = IN1_WIDTH // IN1_COL - QK_BLOCKS
    vg = pl.pallas_call(
        _inproj1_vg_kernel,
        grid=(vg_blocks, n // tm),
        in_specs=[h_spec, pl.BlockSpec((D_MODEL, IN1_COL), lambda j, i: (0, QK_BLOCKS + j))],
        out_specs=out_spec,
        out_shape=jax.ShapeDtypeStruct((n, vg_blocks * IN1_COL), BF16),
        compiler_params=_params("parallel", "parallel"),
        name="inproj1_vg",
    )(h, w_bf16)
    return qk, vg


def _retention_kernel(*refs, n_chunks, has_state, heads, unroll):
    if has_state:
        decay_ref, q_ref, k_ref, v_ref, gf_ref, gb_ref, sf0_ref, sb0_ref, o_ref, inc_ref, seen_ref = refs
    else:
        decay_ref, q_ref, k_ref, v_ref, gf_ref, gb_ref, o_ref, sf_ref, sb_ref, inc_ref, seen_ref = refs
    c = RET_CHUNK
    row = lax.broadcasted_iota(jnp.int32, (c, c), 0).astype(F32)
    col = lax.broadcasted_iota(jnp.int32, (c, c), 1).astype(F32)
    pos = lax.broadcasted_iota(jnp.int32, (c, 1), 0).astype(F32)

    def chunk(i):
        return pl.ds(pl.multiple_of(i * c, c), c)

    diff = [row - col, col - row]
    qk_cols = [slice(g * RET_DK, (g + 1) * RET_DK) for g in range(heads)]
    v_cols = [slice(g * RET_DV, (g + 1) * RET_DV) for g in range(heads)]
    intra, q_dec, k_dec, chunk_dec = [], [], [], []
    for g in range(heads):
        hd = pl.program_id(1) * heads + g
        lg = [-jnp.exp(jnp.full((1, 1), decay_ref[d, hd], F32)) for d in range(2)]
        intra.append([jnp.where(diff[d] >= 0, jnp.exp(lg[d] * jnp.maximum(diff[d], 0.0)), 0.0) for d in range(2)])
        q_dec.append([jnp.exp(lg[0] * (pos + 1.0)), jnp.exp(lg[1] * (c - pos))])
        k_dec.append([jnp.exp(lg[0] * ((c - 1.0) - pos)), jnp.exp(lg[1] * pos)])
        chunk_dec.append([jnp.exp(lg[d] * float(c)) for d in range(2)])

    def increments(i, carry):
        rows = chunk(i)
        for g in range(heads):
            ki = k_ref[rows, qk_cols[g]].astype(F32)
            vi = v_ref[rows, v_cols[g]]
            for d in range(2):
                inc_ref[g, d, i] = _dot_tn((ki * k_dec[g][d]).astype(BF16), vi)
        return carry

    lax.fori_loop(0, n_chunks, increments, 0, unroll=unroll)

    for g in range(heads):
        finals = []
        for d in range(2):
            def scan(t, state):
                i = t if d == 0 else n_chunks - 1 - t
                seen_ref[g, d, i] = state.astype(BF16)
                return state * chunk_dec[g][d] + inc_ref[g, d, i]

            if has_state:
                state0 = (sf0_ref if d == 0 else sb0_ref)[0, g]
            else:
                state0 = jnp.zeros((RET_DK, RET_DV), F32)
            finals.append(lax.fori_loop(0, n_chunks, scan, state0))
        if not has_state:
            sf_ref[0, g] = finals[0]
            sb_ref[0, g] = finals[1]

    def outputs(i, carry):
        rows = chunk(i)
        for g in range(heads):
            qi = q_ref[rows, qk_cols[g]]
            vi = v_ref[rows, v_cols[g]]
            s = _dot_nt(qi, k_ref[rows, qk_cols[g]])
            mixed = None
            for d, g_ref in enumerate((gf_ref, gb_ref)):
                y = _dot((s * intra[g][d]).astype(BF16), vi) + _dot(qi, seen_ref[g, d, i]) * q_dec[g][d]
                gated = g_ref[rows, v_cols[g]].astype(F32) * _rms(y)
                mixed = gated if mixed is None else mixed + gated
            o_ref[rows, v_cols[g]] = mixed.astype(o_ref.dtype)
        return carry

    lax.fori_loop(0, n_chunks, outputs, 0, unroll=unroll)


def _retention(stream, qk, vg, decays, states, heads):
    n = stream.n_tok
    t = stream.seq_len
    n_chunks = t // RET_CHUNK
    has_state = states is not None
    qk_w = heads * RET_DK
    v_w = heads * RET_DV
    k_blk = RET_QK_WIDTH // qk_w
    gf_blk = RET_V_WIDTH // v_w
    gb_blk = 2 * gf_blk
    in_specs = [
        pl.BlockSpec(memory_space=pltpu.SMEM),
        pl.BlockSpec((t, qk_w), lambda b, h: (b, h)),
        pl.BlockSpec((t, qk_w), lambda b, h: (b, k_blk + h)),
        pl.BlockSpec((t, v_w), lambda b, h: (b, h)),
        pl.BlockSpec((t, v_w), lambda b, h: (b, gf_blk + h)),
        pl.BlockSpec((t, v_w), lambda b, h: (b, gb_blk + h)),
    ]
    args = [decays, qk, qk, vg, vg, vg]
    state_spec = pl.BlockSpec((1, heads, RET_DK, RET_DV), lambda b, h: (b, h, 0, 0))
    out_specs = [pl.BlockSpec((t, v_w), lambda b, h: (b, h))]
    out_shape = [jax.ShapeDtypeStruct((n, RET_V_WIDTH), BF16)]
    if has_state:
        in_specs += [state_spec, state_spec]
        args += list(states)
    else:
        out_specs += [state_spec, state_spec]
        out_shape += [jax.ShapeDtypeStruct((stream.n_seq, RET_HEADS, RET_DK, RET_DV), F32)] * 2
    return pl.pallas_call(
        functools.partial(_retention_kernel, n_chunks=n_chunks, has_state=has_state, heads=heads,
                          unroll=min(n_chunks, 2)),
        grid=(stream.n_seq, RET_HEADS // heads),
        in_specs=in_specs,
        out_specs=out_specs,
        out_shape=out_shape,
        scratch_shapes=[pltpu.VMEM((heads, 2, n_chunks, RET_DK, RET_DV), F32),
                        pltpu.VMEM((heads, 2, n_chunks, RET_DK, RET_DV), BF16)],
        compiler_params=_params("parallel", "parallel"),
        name="retention",
    )(*args)


def _rope_tables(n_tokens, dim):
    t = jnp.arange(n_tokens)
    pos = jnp.stack([t // GRID_W, t % GRID_W]).astype(F32)
    n_freq = dim // 4
    freqs = ROPE_THETA ** (-jnp.arange(n_freq, dtype=F32) / n_freq)
    ang = pos[:, :, None] * freqs
    cos, sin = jnp.cos(ang), jnp.sin(ang)
    cos_t = jnp.concatenate([cos[0], cos[0], cos[1], cos[1]], axis=-1)
    sin_t = jnp.concatenate([-sin[0], sin[0], -sin[1], sin[1]], axis=-1)
    return cos_t, sin_t


def kernel(x_prompt, x_sample, cache_k0, cache_v0, state_ret_fwd1, state_ret_bwd1, c, c_ctx,
           norm_mix_g0, mod_w0, mod_b0, in_w0, conv_w0, conv_b0, conv_norm_g0, conv_norm_b0,
           q_norm_g0, k_norm_g0, out_w0, norm_ffn_g0,
           moe_grp_w0, moe_grp_b0, moe_rtr_w0, moe_rtr_b0, moe_w_gate0, moe_w_up0, moe_w_down0,
           norm_mix_g1, mod_w1, mod_b1, in_w1, ret_decay_fwd1, ret_decay_bwd1, out_w1, norm_ffn_g1,
           moe_grp_w1, moe_grp_b1, moe_rtr_w1, moe_rtr_b1, moe_w_gate1, moe_w_up1, moe_w_down1,
           final_norm_g):
    batch, seq, d = x_prompt.shape
    dec_batch, dec_seq, _ = x_sample.shape
    assert d == D_MODEL and 1 + dec_batch <= MOD_SLOTS
    ctx = _Stream(batch, seq, first_slot=0, rope=False)
    lat = _Stream(dec_batch, dec_seq, first_slot=1, rope=True)
    assert ctx.n_tok % ROW_TILE == 0 and lat.n_tok % ROW_TILE == 0 and ROW_CHUNKS == SUBLANES
    assert seq % CONV_TILE == 0 and dec_seq % ROW_TILE == 0 and ROW_TILE % seq == 0

    cond = jnp.concatenate([c_ctx[None, :], c, jnp.zeros((MOD_SLOTS - 1 - dec_batch, d), F32)], axis=0)
    mod0 = _modulation(cond, mod_w0, mod_b0)
    mod1 = _modulation(cond, mod_w1, mod_b1)
    cos, sin = _rope_tables(dec_seq, HEAD_DIM)
    decays = jnp.stack([ret_decay_fwd1, ret_decay_bwd1]).astype(F32)

    in_w0_b = in_w0.astype(BF16)
    partner = jnp.arange(HEAD_DIM) ^ (HEAD_DIM // 4)
    qk0 = 2 * CONV_CH
    qk_partner = (qk0 + HEAD_DIM * jnp.arange(N_Q_HEADS + N_KV_HEADS)[:, None] + partner[None, :]).reshape(-1)
    in_w0_rot = jnp.concatenate([in_w0_b, jnp.take(in_w0_b, qk_partner, axis=1)], axis=1)
    q_gains = jnp.stack([q_norm_g0, q_norm_g0[partner]])
    k_gains = jnp.stack([k_norm_g0, k_norm_g0[partner]])
    out_w0_b = out_w0.astype(BF16)
    in_w1_b = in_w1.astype(BF16)
    out_w1_b = out_w1.astype(BF16)
    moe0 = (moe_w_gate0.astype(BF16), moe_w_up0.astype(BF16), moe_w_down0.astype(BF16))
    moe1 = (moe_w_gate1.astype(BF16), moe_w_up1.astype(BF16), moe_w_down1.astype(BF16))
    route0 = _router_params(moe_grp_w0, moe_grp_b0, moe_rtr_w0, moe_rtr_b0)
    route1 = _router_params(moe_grp_w1, moe_grp_b1, moe_rtr_w1, moe_rtr_b1)

    def run(stream, x, cache, states):
        x = x.reshape(stream.n_tok, d)
        kv_dtype = BF16 if stream.rope else F32
        a, q, k, v = _inproj0(stream, x, mod0, norm_mix_g0, in_w0_rot if stream.rope else in_w0_b,
                              q_gains, k_gains, cos, sin, kv_dtype)
        a = _conv_branch(stream, a, conv_w0, conv_b0, conv_norm_g0, conv_norm_b0)
        tq = min(stream.seq_len, 256)
        o = _attention(stream, q, k, v, cache, tq, kv_heads=1 if cache is not None else N_KV_HEADS)
        x, *routed = _outproj_route(stream, [a, o], out_w0_b, x, mod0, norm_ffn_g0, *route0)
        x, h = _moe(stream, *routed, *moe0, x, mod0, norm_mix_g1, mod1, last_layer=False)
        qk, vg = _inproj1(stream, h, in_w1_b, cos, sin)
        ret = _retention(stream, qk, vg, decays, states, heads=2 if states is not None else RET_HEADS)
        x, *routed = _outproj_route(stream, [ret[0]], out_w1_b, x, mod1, norm_ffn_g1, *route1)
        y, = _moe(stream, *routed, *moe1, x, mod1, final_norm_g, mod1, last_layer=True)
        return y.reshape(stream.n_seq, stream.seq_len, d), k, v, ret[1:]

    y_prompt, k_ctx, v_ctx, new_states = run(ctx, x_prompt, None, None)
    y_sample, _, _, _ = run(lat, x_sample, (cache_k0, cache_v0), (state_ret_fwd1, state_ret_bwd1))
    return (y_prompt, y_sample, k_ctx, v_ctx, new_states[0], new_states[1])
```

```python
import functools

import jax
import jax.numpy as jnp
import numpy as np
from jax import lax
from jax.experimental import pallas as pl
from jax.experimental.pallas import tpu as pltpu

D_MODEL = 1024
GRID_W = 64
EPS = 1e-6
CONV_CH = 512
CONV_WIDTH = 31
CONV_HALF = CONV_WIDTH // 2
N_Q_HEADS = 8
N_KV_HEADS = 2
Q_PER_KV = N_Q_HEADS // N_KV_HEADS
HEAD_DIM = 128
ROPE_THETA = 10000.0
ATTN_WIDTH = N_Q_HEADS * HEAD_DIM
KV_WIDTH = N_KV_HEADS * HEAD_DIM
IN0_WIDTH = 2 * CONV_CH + ATTN_WIDTH + 2 * KV_WIDTH
RET_HEADS = 8
RET_DK = 128
RET_DV = 256
RET_CHUNK = 128
RET_QK_WIDTH = RET_HEADS * RET_DK
RET_V_WIDTH = RET_HEADS * RET_DV
IN1_WIDTH = 2 * RET_QK_WIDTH + 3 * RET_V_WIDTH
N_GROUPS = 4
EXPERTS_PER_GROUP = 4
N_EXPERTS = N_GROUPS * EXPERTS_PER_GROUP
EXPERT_FF = 512

LANES = 128
SUBLANES = 8
VMEM_LIMIT_BYTES = 56 * 1024 * 1024

MOD_SLOTS = 8
ROW_TILE = 512
ROW_CHUNKS = D_MODEL // LANES
PAYLOAD_CHUNKS = 2 * ROW_CHUNKS
FFN_TILE = 256
COMBINE_TILE = 256
ROUTE_GROUP, ROUTE_RANK = 0, 1
CONV_TILE = 256
CONV_HALO = 16
CONV_ROWS = 64
ROUTE_LANES = LANES
EXPERT_LANE0 = N_GROUPS
NEG_BIG = -1e30
LOG2_E = 1.4426950408889634
ATTN_KV_CHUNK = 512

F32 = jnp.float32
BF16 = jnp.bfloat16


def _params(*semantics):
    return pltpu.CompilerParams(dimension_semantics=semantics, vmem_limit_bytes=VMEM_LIMIT_BYTES)


def _sigmoid(x):
    return 1.0 / (1.0 + jnp.exp(-x))


def _silu(x):
    return x * _sigmoid(x)


def _rms(x):
    return x * lax.rsqrt(jnp.mean(x * x, axis=-1, keepdims=True) + EPS)


def _rope(x, cos, sin):
    lane = lax.broadcasted_iota(jnp.int32, x.shape, 1)
    take_upper = (lane % (HEAD_DIM // 2)) < (HEAD_DIM // 4)
    partner = jnp.where(take_upper, pltpu.roll(x, HEAD_DIM - HEAD_DIM // 4, 1), pltpu.roll(x, HEAD_DIM // 4, 1))
    return x * cos + partner * sin


def _dot(a, b):
    return jnp.dot(a, b, preferred_element_type=F32)


def _dot_nt(a, b):
    return lax.dot_general(a, b, (((1,), (1,)), ((), ())), preferred_element_type=F32)


def _dot_tn(a, b):
    return lax.dot_general(a, b, (((0,), (0,)), ((), ())), preferred_element_type=F32)


def _mod_kernel(c_ref, w_ref, b_ref, o_ref):
    c = c_ref[...]
    o_ref[...] = _dot(_silu(c).astype(BF16), w_ref[...].astype(BF16)) + b_ref[...]


def _modulation(cond, mod_w, mod_b):
    n_out = mod_w.shape[1]
    col = D_MODEL
    out = pl.pallas_call(
        _mod_kernel,
        grid=(n_out // col,),
        in_specs=[
            pl.BlockSpec((MOD_SLOTS, D_MODEL), lambda j: (0, 0)),
            pl.BlockSpec((D_MODEL, col), lambda j: (0, j)),
            pl.BlockSpec((1, col), lambda j: (0, j)),
        ],
        out_specs=pl.BlockSpec((MOD_SLOTS, col), lambda j: (0, j)),
        out_shape=jax.ShapeDtypeStruct((MOD_SLOTS, n_out), F32),
        compiler_params=_params("parallel"),
        name="modulation",
    )(cond, mod_w, mod_b.reshape(1, n_out))
    return out.reshape(MOD_SLOTS, 6, D_MODEL)


def _slot_map(stream):
    if stream.first_slot == 0:
        return lambda i: 0
    return lambda i: stream.first_slot + (i * ROW_TILE) // stream.seq_len


class _Stream:
    def __init__(self, n_seq, seq_len, first_slot, rope):
        self.n_seq = n_seq
        self.seq_len = seq_len
        self.first_slot = first_slot
        self.rope = rope
        self.n_tok = n_seq * seq_len


def _modulated_norm(x, g, shift, scale):
    return _rms(x) * g * (1.0 + scale) + shift


def _inproj0_kernel(x_ref, mod_ref, g_ref, w_ref, wp_ref, qg_ref, kg_ref, cos_ref, sin_ref,
                    a_ref, q_ref, k_ref, v_ref, *, rope):
    h = _modulated_norm(x_ref[...], g_ref[...], mod_ref[0, 0:1, :], mod_ref[0, 1:2, :]).astype(BF16)
    p = _dot(h, w_ref[...])
    if rope:
        p_partner = _dot(h, wp_ref[...])
    a_ref[...] = p[:, :CONV_CH] * _sigmoid(p[:, CONV_CH:2 * CONV_CH])
    q0 = 2 * CONV_CH
    k0 = q0 + ATTN_WIDTH
    v0 = k0 + KV_WIDTH
    cos = cos_ref[...]
    sin = sin_ref[...]
    q_scale = HEAD_DIM ** -0.5 * LOG2_E

    def normed_head(col, partner_col, gain_ref):
        ph = p[:, col:col + HEAD_DIM]
        inv = lax.rsqrt(jnp.mean(ph * ph, axis=-1, keepdims=True) + EPS)
        out = ph * inv * gain_ref[0:1, :]
        if rope:
            out = out * cos + (p_partner[:, partner_col:partner_col + HEAD_DIM] * inv * gain_ref[1:2, :]) * sin
        return out

    for hh in range(N_Q_HEADS):
        qh = normed_head(q0 + hh * HEAD_DIM, hh * HEAD_DIM, qg_ref)
        q_ref[:, hh * HEAD_DIM:(hh + 1) * HEAD_DIM] = (qh * q_scale).astype(q_ref.dtype)
    seqs, _, rows, _ = k_ref.shape
    for hh in range(N_KV_HEADS):
        kh = normed_head(k0 + hh * HEAD_DIM, ATTN_WIDTH + hh * HEAD_DIM, kg_ref)
        vh = p[:, v0 + hh * HEAD_DIM:v0 + (hh + 1) * HEAD_DIM]
        for sq in range(seqs):
            k_ref[sq, hh] = kh[sq * rows:(sq + 1) * rows].astype(k_ref.dtype)
            v_ref[sq, hh] = vh[sq * rows:(sq + 1) * rows].astype(v_ref.dtype)


def _inproj0(stream, x, mod, norm_g, w_bf16, w_partner, q_gains, k_gains, cos, sin, kv_dtype):
    n = stream.n_tok
    tiles_per_seq = max(stream.seq_len // ROW_TILE, 1)
    rope_map = (lambda i: (i % tiles_per_seq, 0)) if stream.rope else (lambda i: (0, 0))
    slot = _slot_map(stream)
    row = lambda i: (i, 0)
    fixed = lambda i: (0, 0)
    seqs_per_tile = max(ROW_TILE // stream.seq_len, 1)
    kv_spec = pl.BlockSpec((seqs_per_tile, N_KV_HEADS, ROW_TILE // seqs_per_tile, HEAD_DIM),
                           lambda i: (i // tiles_per_seq, 0, i % tiles_per_seq, 0))
    kv_shape = jax.ShapeDtypeStruct((stream.n_seq, N_KV_HEADS, stream.seq_len, HEAD_DIM), kv_dtype)
    return pl.pallas_call(
        functools.partial(_inproj0_kernel, rope=stream.rope),
        grid=(n // ROW_TILE,),
        in_specs=[
            pl.BlockSpec((ROW_TILE, D_MODEL), row),
            pl.BlockSpec((1, 6, D_MODEL), lambda i: (slot(i), 0, 0)),
            pl.BlockSpec((1, D_MODEL), fixed),
            pl.BlockSpec((D_MODEL, IN0_WIDTH), fixed),
            pl.BlockSpec((D_MODEL, ATTN_WIDTH + KV_WIDTH), fixed),
            pl.BlockSpec((2, HEAD_DIM), fixed),
            pl.BlockSpec((2, HEAD_DIM), fixed),
            pl.BlockSpec((ROW_TILE, HEAD_DIM), rope_map),
            pl.BlockSpec((ROW_TILE, HEAD_DIM), rope_map),
        ],
        out_specs=[
            pl.BlockSpec((ROW_TILE, CONV_CH), row),
            pl.BlockSpec((ROW_TILE, ATTN_WIDTH), row),
            kv_spec,
            kv_spec,
        ],
        out_shape=[
            jax.ShapeDtypeStruct((n, CONV_CH), F32),
            jax.ShapeDtypeStruct((n, ATTN_WIDTH), BF16),
            kv_shape,
            kv_shape,
        ],
        compiler_params=_params("parallel"),
        name="inproj0",
    )(x, mod, norm_g.reshape(1, D_MODEL), w_bf16, w_partner, q_gains, k_gains, cos, sin)


def _conv_kernel(prev_ref, main_ref, next_ref, w_ref, b_ref, g_ref, beta_ref, o_ref, pad_ref, acc_ref, *,
                 tiles_per_seq):
    i = pl.program_id(0)
    has_prev = (i % tiles_per_seq) != 0
    has_next = (i % tiles_per_seq) != (tiles_per_seq - 1)
    pad_ref[0:CONV_HALO, :] = jnp.where(has_prev, prev_ref[...], 0.0)
    pad_ref[CONV_HALO:CONV_HALO + CONV_TILE, :] = main_ref[...]
    pad_ref[CONV_HALO + CONV_TILE:, :] = jnp.where(has_next, next_ref[...], 0.0)
    first = CONV_HALO - CONV_HALF
    window = CONV_ROWS + SUBLANES
    for c0 in range(0, CONV_CH, LANES):
        for r0 in range(0, CONV_TILE, CONV_ROWS):
            acc = jnp.zeros((CONV_ROWS, LANES), F32)
            for phase in range(SUBLANES):
                partial = jnp.zeros((window, LANES), F32)
                for j in range(phase, CONV_WIDTH, SUBLANES):
                    base = r0 + j - phase
                    partial = partial + w_ref[j:j + 1, c0:c0 + LANES] * pad_ref[base:base + window, c0:c0 + LANES]
                acc = acc + partial[first + phase:first + phase + CONV_ROWS]
            acc_ref[r0:r0 + CONV_ROWS, c0:c0 + LANES] = acc
    a = acc_ref[...] + b_ref[...]
    mu = jnp.mean(a, axis=-1, keepdims=True)
    d = a - mu
    var = jnp.mean(d * d, axis=-1, keepdims=True)
    y = d * lax.rsqrt(var + EPS) * g_ref[...] + beta_ref[...]
    o_ref[...] = _silu(y).astype(o_ref.dtype)


def _conv_branch(stream, a, conv_w, conv_b, norm_g, norm_b):
    n = stream.n_tok
    tiles_per_seq = stream.seq_len // CONV_TILE
    halo_per_tile = CONV_TILE // CONV_HALO
    n_halo = n // CONV_HALO
    fixed = lambda i: (0, 0)
    return pl.pallas_call(
        functools.partial(_conv_kernel, tiles_per_seq=tiles_per_seq),
        grid=(n // CONV_TILE,),
        in_specs=[
            pl.BlockSpec((CONV_HALO, CONV_CH), lambda i: (jnp.maximum(i * halo_per_tile - 1, 0), 0)),
            pl.BlockSpec((CONV_TILE, CONV_CH), lambda i: (i, 0)),
            pl.BlockSpec((CONV_HALO, CONV_CH), lambda i: (jnp.minimum((i + 1) * halo_per_tile, n_halo - 1), 0)),
            pl.BlockSpec((CONV_WIDTH, CONV_CH), fixed),
            pl.BlockSpec((1, CONV_CH), fixed),
            pl.BlockSpec((1, CONV_CH), fixed),
            pl.BlockSpec((1, CONV_CH), fixed),
        ],
        out_specs=pl.BlockSpec((CONV_TILE, CONV_CH), lambda i: (i, 0)),
        out_shape=jax.ShapeDtypeStruct((n, CONV_CH), BF16),
        scratch_shapes=[
            pltpu.VMEM((CONV_TILE + 2 * CONV_HALO, CONV_CH), F32),
            pltpu.VMEM((CONV_TILE, CONV_CH), F32),
        ],
        compiler_params=_params("parallel"),
        name="conv_branch",
    )(a, a, a, conv_w, conv_b.reshape(1, CONV_CH), norm_g.reshape(1, CONV_CH), norm_b.reshape(1, CONV_CH))


def _attn_kernel(*refs, tq, past):
    if past:
        q_ref, k_ref, v_ref, kc_ref, vc_ref, o_ref, ks_ref, vs_ref = refs
    else:
        q_ref, k_ref, v_ref, o_ref, ks_ref, vs_ref = refs
    kv_heads, n_keys, _ = ks_ref.shape
    group_w = Q_PER_KV * HEAD_DIM

    @pl.when(pl.program_id(2) == 0)
    def _():
        lane = lax.broadcasted_iota(jnp.int32, (n_keys, HEAD_DIM), 1)
        for hk in range(kv_heads):
            vs_ref[hk, :, HEAD_DIM:] = jnp.where(lane == 0, 1.0, 0.0).astype(BF16)
            if past:
                ks_ref[hk, 0:past, :] = kc_ref[0, hk].astype(BF16)
                vs_ref[hk, 0:past, 0:HEAD_DIM] = vc_ref[0, hk].astype(BF16)
            ks_ref[hk, past:, :] = k_ref[0, hk].astype(BF16)
            vs_ref[hk, past:, 0:HEAD_DIM] = v_ref[0, hk].astype(BF16)

    for hk in range(kv_heads):
        q = q_ref[:, hk * group_w:(hk + 1) * group_w]
        q4 = jnp.concatenate([q[:, g * HEAD_DIM:(g + 1) * HEAD_DIM] for g in range(Q_PER_KV)], axis=0)
        rows = q4.shape[0]
        m = jnp.full((rows, 1), NEG_BIG, F32)
        acc = jnp.zeros((rows, 2 * HEAD_DIM), F32)
        for c0 in range(0, n_keys, ATTN_KV_CHUNK):
            c1 = min(c0 + ATTN_KV_CHUNK, n_keys)
            s = _dot_nt(q4, ks_ref[hk, c0:c1, :])
            m_new = jnp.maximum(m, jnp.max(s, axis=-1, keepdims=True))
            p = jnp.exp2(s - m_new).astype(BF16)
            acc = jnp.exp2(m - m_new) * acc + _dot(p, vs_ref[hk, c0:c1, :])
            m = m_new
        o = acc[:, 0:HEAD_DIM] * (1.0 / acc[:, HEAD_DIM:HEAD_DIM + 1])
        for g in range(Q_PER_KV):
            col = hk * group_w + g * HEAD_DIM
            o_ref[:, col:col + HEAD_DIM] = o[g * tq:(g + 1) * tq].astype(o_ref.dtype)


def _attention(stream, q, k, v, cache, tq, kv_heads):
    n = stream.n_tok
    t = stream.seq_len
    q_tiles = t // tq
    width = kv_heads * Q_PER_KV * HEAD_DIM
    in_specs = [
        pl.BlockSpec((tq, width), lambda b, h, i: (b * q_tiles + i, h)),
        pl.BlockSpec((1, kv_heads, t, HEAD_DIM), lambda b, h, i: (b, h, 0, 0)),
        pl.BlockSpec((1, kv_heads, t, HEAD_DIM), lambda b, h, i: (b, h, 0, 0)),
    ]
    args = [q, k, v]
    past = 0
    if cache is not None:
        past = cache[0].shape[2]
        in_specs += [pl.BlockSpec((1, kv_heads, past, HEAD_DIM), lambda b, h, i: (b, h, 0, 0))] * 2
        args += list(cache)
    return pl.pallas_call(
        functools.partial(_attn_kernel, tq=tq, past=past),
        grid=(stream.n_seq, N_KV_HEADS // kv_heads, q_tiles),
        in_specs=in_specs,
        out_specs=pl.BlockSpec((tq, width), lambda b, h, i: (b * q_tiles + i, h)),
        out_shape=jax.ShapeDtypeStruct((n, ATTN_WIDTH), BF16),
        scratch_shapes=[pltpu.VMEM((kv_heads, past + t, HEAD_DIM), BF16),
                        pltpu.VMEM((kv_heads, past + t, 2 * HEAD_DIM), BF16)],
        compiler_params=_params("parallel", "parallel", "arbitrary"),
        name="attention",
    )(*args)


def _route(logits, running):
    rows = logits.shape[0]
    lane = lax.broadcasted_iota(jnp.int32, logits.shape, 1)
    lane_f = lane.astype(F32)
    far = float(ROUTE_LANES)
    is_group = lane < N_GROUPS
    gl = jnp.where(is_group, logits, NEG_BIG)
    gmax = jnp.max(gl, axis=-1, keepdims=True)
    gsum = jnp.sum(jnp.where(is_group, jnp.exp(gl - gmax), 0.0), axis=-1, keepdims=True)
    g_w = 1.0 / gsum
    gidx = jnp.min(jnp.where(gl == gmax, lane_f, far), axis=-1, keepdims=True)
    lo = EXPERT_LANE0 + EXPERTS_PER_GROUP * gidx
    in_group = (lane_f >= lo) & (lane_f < lo + EXPERTS_PER_GROUP)
    el = jnp.where(in_group, logits, NEG_BIG)
    v1 = jnp.max(el, axis=-1, keepdims=True)
    i1 = jnp.min(jnp.where(el == v1, lane_f, far), axis=-1, keepdims=True)
    el2 = jnp.where(lane_f == i1, NEG_BIG, el)
    v2 = jnp.max(el2, axis=-1, keepdims=True)
    i2 = jnp.min(jnp.where(el2 == v2, lane_f, far), axis=-1, keepdims=True)
    e2 = jnp.exp(v2 - v1)
    w1 = g_w / (1.0 + e2)
    w2 = w1 * e2
    in_my_group = lane_f == gidx
    chosen = in_my_group.astype(BF16)
    earlier = (lax.broadcasted_iota(jnp.int32, (rows, rows), 0)
               > lax.broadcasted_iota(jnp.int32, (rows, rows), 1)).astype(BF16)
    before = running + _dot(earlier, chosen)
    rank = jnp.sum(jnp.where(in_my_group, before, 0.0), axis=-1, keepdims=True)
    record = jnp.where(lane == ROUTE_GROUP, gidx, jnp.where(lane == ROUTE_RANK, rank, 0.0))
    weights = jnp.where(lane_f == i1 - lo, w1, 0.0) + jnp.where(lane_f == i2 - lo, w2, 0.0)
    return record, weights, running + jnp.sum(chosen.astype(F32), axis=0, keepdims=True)


def _outproj_kernel(*refs, widths):
    n_in = len(widths)
    in_refs = refs[:n_in]
    w_ref, x_ref, mod_ref, g_ref, rw_ref, rb_ref, x1_ref, pay_ref, route_t_ref, count_ref, run_ref = refs[n_in:]

    @pl.when(pl.program_id(0) == 0)
    def _():
        run_ref[...] = jnp.zeros_like(run_ref)

    acc = None
    off = 0
    for r, width in zip(in_refs, widths):
        part = _dot(r[...], w_ref[off:off + width, :])
        acc = part if acc is None else acc + part
        off += width
    x1 = x_ref[...] + mod_ref[0, 2:3, :] * acc
    x1_ref[...] = x1
    h = _modulated_norm(x1, g_ref[...], mod_ref[0, 3:4, :], mod_ref[0, 4:5, :])
    h_hi = h.astype(BF16)
    h_lo = (h - h_hi.astype(F32)).astype(BF16)
    both = _dot(h_hi, rw_ref[...])
    logits = (both[:, :ROUTE_LANES] + both[:, ROUTE_LANES:] + _dot(h_lo, rw_ref[:, :ROUTE_LANES]) + rb_ref[...])
    record, weights, running = _route(logits, run_ref[...])
    route_t_ref[...] = record.T[:SUBLANES]
    run_ref[...] = running
    count_ref[...] = jnp.broadcast_to(running, count_ref.shape)
    rows = h.shape[0]
    _store_row_major(pay_ref, h, PAYLOAD_CHUNKS)
    pay_ref[pl.ds(ROW_CHUNKS, rows, stride=PAYLOAD_CHUNKS), :] = weights
    for c in range(ROW_CHUNKS + 1, PAYLOAD_CHUNKS):
        pay_ref[pl.ds(c, rows, stride=PAYLOAD_CHUNKS), :] = jnp.zeros((rows, LANES), F32)


def _store_row_major(ref, x, pitch=ROW_CHUNKS):
    rows = x.shape[0]
    for c in range(ROW_CHUNKS):
        ref[pl.ds(c, rows, stride=pitch), :] = x[:, c * LANES:(c + 1) * LANES]


def _load_row_major(ref, rows, pitch=ROW_CHUNKS):
    return jnp.concatenate([ref[pl.ds(c, rows, stride=pitch), :] for c in range(ROW_CHUNKS)], axis=-1)


def _outproj_route(stream, inputs, w_bf16, x, mod, ffn_norm_g, route_w, route_b):
    n = stream.n_tok
    widths = tuple(a.shape[1] for a in inputs)
    slot = _slot_map(stream)
    row = lambda i: (i, 0)
    fixed = lambda i: (0, 0)
    return pl.pallas_call(
        functools.partial(_outproj_kernel, widths=widths),
        grid=(n // ROW_TILE,),
        in_specs=[pl.BlockSpec((ROW_TILE, width), row) for width in widths] + [
            pl.BlockSpec((sum(widths), D_MODEL), fixed),
            pl.BlockSpec((ROW_TILE, D_MODEL), row),
            pl.BlockSpec((1, 6, D_MODEL), lambda i: (slot(i), 0, 0)),
            pl.BlockSpec((1, D_MODEL), fixed),
            pl.BlockSpec((D_MODEL, 2 * ROUTE_LANES), fixed),
            pl.BlockSpec((1, ROUTE_LANES), fixed),
        ],
        out_specs=[
            pl.BlockSpec((ROW_TILE, D_MODEL), row),
            pl.BlockSpec((ROW_TILE * PAYLOAD_CHUNKS, LANES), row),
            pl.BlockSpec((SUBLANES, ROW_TILE), lambda i: (0, i)),
            pl.BlockSpec((SUBLANES, ROUTE_LANES), fixed),
        ],
        out_shape=[
            jax.ShapeDtypeStruct((n, D_MODEL), F32),
            jax.ShapeDtypeStruct((n * PAYLOAD_CHUNKS, LANES), F32),
            jax.ShapeDtypeStruct((SUBLANES, n), F32),
            jax.ShapeDtypeStruct((SUBLANES, ROUTE_LANES), F32),
        ],
        scratch_shapes=[pltpu.VMEM((1, ROUTE_LANES), F32)],
        compiler_params=_params("arbitrary"),
        name="outproj_route",
    )(*inputs, w_bf16, x, mod, ffn_norm_g.reshape(1, D_MODEL), route_w, route_b)


def _router_params(grp_w, grp_b, rtr_w, rtr_b):
    w = jnp.concatenate([grp_w, jnp.moveaxis(rtr_w, 0, 1).reshape(D_MODEL, N_EXPERTS)], axis=1)
    w = jnp.pad(w, ((0, 0), (0, ROUTE_LANES - w.shape[1])))
    hi = w.astype(BF16)
    lo = (w - hi.astype(F32)).astype(BF16)
    b = jnp.concatenate([grp_b, rtr_b.reshape(N_EXPERTS)])
    b = jnp.pad(b, (0, ROUTE_LANES - b.shape[0])).reshape(1, ROUTE_LANES)
    return jnp.concatenate([hi, lo], axis=1), b


def _sorted_tiles(n_tok):
    return n_tok // FFN_TILE + N_GROUPS


def _dispatch_plan(route_t, counts, n_tok):
    n_tiles = _sorted_tiles(n_tok)
    count = counts[0, :N_GROUPS].astype(jnp.int32)
    tiles = (count + (FFN_TILE - 1)) // FFN_TILE
    start = (jnp.cumsum(tiles) - tiles) * FFN_TILE
    tile_end = jnp.cumsum(tiles)
    j = jnp.arange(n_tiles, dtype=jnp.int32)
    tile_group = jnp.minimum(jnp.sum(j[:, None] >= tile_end[None, :], axis=1), N_GROUPS - 1).astype(jnp.int32)
    group_end = jnp.take(start + count, tile_group)
    valid = jnp.where(j < tile_end[-1], jnp.clip(group_end - j * FFN_TILE, 0, FFN_TILE), 0).astype(jnp.int32)
    where = jnp.concatenate([start, route_t[:2].astype(jnp.int32).reshape(-1)])
    return where, tile_group, valid


def _sorted_row(where_ref, n_tok, tok):
    return where_ref[where_ref[N_GROUPS + ROUTE_GROUP * n_tok + tok]] + where_ref[N_GROUPS + ROUTE_RANK * n_tok + tok]


def _token_rows(t, pitch=ROW_CHUNKS):
    return pl.ds(pl.multiple_of(t * pitch, pitch), pitch)


def _ffn_kernel(tg_ref, valid_ref, where_ref, pay_ref, wg_ref, wu_ref, wd_ref, ys_ref, x_buf, src_ref, sems, *, n_tok):
    j = pl.program_id(0)
    n_tiles = pl.num_programs(0)

    @pl.when(j == 0)
    def _():
        def clear(r, carry):
            src_ref[r] = 0
            return carry

        def place(t, carry):
            src_ref[_sorted_row(where_ref, n_tok, t)] = t
            return carry

        lax.fori_loop(0, src_ref.shape[0], clear, 0, unroll=8)
        lax.fori_loop(0, n_tok, place, 0, unroll=8)

    def gather_copy(tile, r):
        tok = src_ref[tile * FFN_TILE + r]
        return pltpu.make_async_copy(pay_ref.at[_token_rows(tok, PAYLOAD_CHUNKS)],
                                     x_buf.at[tile % 2, _token_rows(r, PAYLOAD_CHUNKS)], sems.at[tile % 2])

    def start_gather(tile):
        def issue_pair(r2, carry):
            for k in range(2):
                gather_copy(tile, 2 * r2 + k).start(priority=k)
            return carry

        def issue(r, carry):
            gather_copy(tile, r).start()
            return carry

        @pl.when(valid_ref[tile] == FFN_TILE)
        def _():
            lax.fori_loop(0, FFN_TILE // 2, issue_pair, 0, unroll=4)

        @pl.when(valid_ref[tile] < FFN_TILE)
        def _():
            lax.fori_loop(0, valid_ref[tile], issue, 0)

    @pl.when(j == 0)
    def _():
        x_buf[...] = jnp.zeros_like(x_buf)
        start_gather(j)

    @pl.when(j + 1 < n_tiles)
    def _():
        @pl.when(valid_ref[j + 1] > 0)
        def _():
            start_gather(j + 1)

    @pl.when(valid_ref[j] == 0)
    def _():
        ys_ref[...] = jnp.zeros_like(ys_ref)

    @pl.when(valid_ref[j] > 0)
    def _():
        slot = j % 2
        n = pl.multiple_of(valid_ref[j] * PAYLOAD_CHUNKS, PAYLOAD_CHUNKS)
        pltpu.make_async_copy(pay_ref.at[pl.ds(0, n)], x_buf.at[slot, pl.ds(0, n)], sems.at[slot]).wait()
        x = _load_row_major(x_buf.at[slot], FFN_TILE, PAYLOAD_CHUNKS).astype(BF16)
        w = x_buf[slot, pl.ds(ROW_CHUNKS, FFN_TILE, stride=PAYLOAD_CHUNKS), :]
        y = None
        for e in range(EXPERTS_PER_GROUP):
            hidden = _silu(_dot(x, wg_ref[e])) * _dot(x, wu_ref[e]) * w[:, e:e + 1]
            part = _dot(hidden.astype(BF16), wd_ref[e])
            y = part if y is None else y + part
        _store_row_major(ys_ref, y)


def _expert_ffn(n_tok, tile_group, valid, where, payload, wg, wu, wd):
    group = lambda j, tg, va, sr: (tg[j], 0, 0)
    n_tiles = _sorted_tiles(n_tok)
    return pl.pallas_call(
        functools.partial(_ffn_kernel, n_tok=n_tok),
        grid_spec=pltpu.PrefetchScalarGridSpec(
            num_scalar_prefetch=3,
            grid=(n_tiles,),
            in_specs=[
                pl.BlockSpec(memory_space=pl.ANY),
                pl.BlockSpec((EXPERTS_PER_GROUP, D_MODEL, EXPERT_FF), group),
                pl.BlockSpec((EXPERTS_PER_GROUP, D_MODEL, EXPERT_FF), group),
                pl.BlockSpec((EXPERTS_PER_GROUP, EXPERT_FF, D_MODEL), group),
            ],
            out_specs=pl.BlockSpec((FFN_TILE * ROW_CHUNKS, LANES), lambda j, tg, va, sr: (j, 0)),
            scratch_shapes=[pltpu.VMEM((2, FFN_TILE * PAYLOAD_CHUNKS, LANES), F32),
                            pltpu.SMEM((n_tiles * FFN_TILE,), jnp.int32), pltpu.SemaphoreType.DMA((2,))],
        ),
        out_shape=jax.ShapeDtypeStruct((n_tiles * FFN_TILE * ROW_CHUNKS, LANES), F32),
        compiler_params=_params("arbitrary"),
        name="moe_ffn",
    )(tile_group, valid, where, payload, wg, wu, wd)


def _combine_kernel(where_ref, x_ref, mod_ref, ng_ref, nmod_ref, ys_ref, o_ref, *rest, n_tok, last_layer):
    if last_layer:
        y_buf, sems = rest
    else:
        h_ref, y_buf, sems = rest
    i = pl.program_id(0)
    n_tiles = pl.num_programs(0)

    def row_copy(tile, t, k):
        slot = tile % 2
        d = _sorted_row(where_ref, n_tok, tile * COMBINE_TILE + t)
        return pltpu.make_async_copy(ys_ref.at[_token_rows(d)], y_buf.at[slot, _token_rows(t)], sems.at[slot, k])

    def start_tile(tile):
        def issue(t2, carry):
            for k in range(2):
                row_copy(tile, 2 * t2 + k, k).start(priority=k)
            return carry

        lax.fori_loop(0, COMBINE_TILE // 2, issue, 0, unroll=4)

    @pl.when(i == 0)
    def _():
        start_tile(i)

    @pl.when(i + 1 < n_tiles)
    def _():
        start_tile(i + 1)

    slot = i % 2
    half = COMBINE_TILE * ROW_CHUNKS // 2
    for k in range(2):
        pltpu.make_async_copy(ys_ref.at[pl.ds(0, half)], y_buf.at[slot, pl.ds(0, half)], sems.at[slot, k]).wait()
    x = x_ref[...] + mod_ref[0, 5:6, :] * _load_row_major(y_buf.at[slot], COMBINE_TILE)
    if last_layer:
        o_ref[...] = _rms(x) * ng_ref[...]
    else:
        o_ref[...] = x
        h_ref[...] = _modulated_norm(x, ng_ref[...], nmod_ref[0, 0:1, :], nmod_ref[0, 1:2, :]).astype(h_ref.dtype)


def _combine(stream, where, ys, x, mod, norm_g, next_mod, last_layer):
    n = stream.n_tok
    first_slot, seq_len = stream.first_slot, stream.seq_len
    slot = (lambda i: 0) if first_slot == 0 else (lambda i: first_slot + (i * COMBINE_TILE) // seq_len)
    row = lambda i, w: (i, 0)
    out_specs = [pl.BlockSpec((COMBINE_TILE, D_MODEL), row)]
    out_shape = [jax.ShapeDtypeStruct((n, D_MODEL), F32)]
    if not last_layer:
        out_specs.append(pl.BlockSpec((COMBINE_TILE, D_MODEL), row))
        out_shape.append(jax.ShapeDtypeStruct((n, D_MODEL), BF16))
    return pl.pallas_call(
        functools.partial(_combine_kernel, n_tok=n, last_layer=last_layer),
        grid_spec=pltpu.PrefetchScalarGridSpec(
            num_scalar_prefetch=1,
            grid=(n // COMBINE_TILE,),
            in_specs=[
                pl.BlockSpec((COMBINE_TILE, D_MODEL), row),
                pl.BlockSpec((1, 6, D_MODEL), lambda i, w: (slot(i), 0, 0)),
                pl.BlockSpec((1, D_MODEL), lambda i, w: (0, 0)),
                pl.BlockSpec((1, 6, D_MODEL), lambda i, w: (slot(i), 0, 0)),
                pl.BlockSpec(memory_space=pl.ANY),
            ],
            out_specs=out_specs,
            scratch_shapes=[pltpu.VMEM((2, COMBINE_TILE * ROW_CHUNKS, LANES), F32),
                            pltpu.SemaphoreType.DMA((2, 2))],
        ),
        out_shape=out_shape,
        compiler_params=_params("arbitrary"),
        name="moe_combine",
    )(where, x, mod, norm_g.reshape(1, D_MODEL), next_mod, ys)


def _moe(stream, payload, route_t, counts, wg, wu, wd, x, mod, norm_g, next_mod, last_layer):
    n = stream.n_tok
    where, tile_group, valid = _dispatch_plan(route_t, counts, n)
    ys = _expert_ffn(n, tile_group, valid, where, payload, wg, wu, wd)
    return _combine(stream, where, ys, x, mod, norm_g, next_mod, last_layer)


IN1_COL = 1024
IN1_ROW_TILE = 1024
QK_BLOCKS = 2 * RET_QK_WIDTH // IN1_COL
Q_BLOCKS = RET_QK_WIDTH // IN1_COL
V_BLOCKS = RET_V_WIDTH // IN1_COL


def _inproj1_qk_kernel(h_ref, w_ref, cos_ref, sin_ref, o_ref, *, rope):
    p = _dot(h_ref[...], w_ref[...])
    scale = jnp.where(pl.program_id(0) < Q_BLOCKS, 1.0, RET_DK ** -0.5)
    cos = cos_ref[...]
    sin = sin_ref[...]
    for hh in range(IN1_COL // RET_DK):
        ph = p[:, hh * RET_DK:(hh + 1) * RET_DK]
        if rope:
            ph = _rope(ph, cos, sin)
        o_ref[:, hh * RET_DK:(hh + 1) * RET_DK] = (ph * scale).astype(o_ref.dtype)


def _inproj1_vg_kernel(h_ref, w_ref, o_ref):
    p = _dot(h_ref[...], w_ref[...])
    o_ref[...] = jnp.where(pl.program_id(0) < V_BLOCKS, p, _silu(p)).astype(o_ref.dtype)


def _inproj1(stream, h, w_bf16, cos, sin):
    n = stream.n_tok
    tm = IN1_ROW_TILE
    tiles_per_seq = max(stream.seq_len // tm, 1)
    rope_map = (lambda j, i: (i % tiles_per_seq, 0)) if stream.rope else (lambda j, i: (0, 0))
    h_spec = pl.BlockSpec((tm, D_MODEL), lambda j, i: (i, 0))
    out_spec = pl.BlockSpec((tm, IN1_COL), lambda j, i: (i, j))
    qk = pl.pallas_call(
        functools.partial(_inproj1_qk_kernel, rope=stream.rope),
        grid=(QK_BLOCKS, n // tm),
        in_specs=[
            h_spec,
            pl.BlockSpec((D_MODEL, IN1_COL), lambda j, i: (0, j)),
            pl.BlockSpec((tm, RET_DK), rope_map),
            pl.BlockSpec((tm, RET_DK), rope_map),
        ],
        out_specs=out_spec,
        out_shape=jax.ShapeDtypeStruct((n, QK_BLOCKS * IN1_COL), BF16),
        compiler_params=_params("parallel", "parallel"),
        name="inproj1_qk",
    )(h, w_bf16, cos, sin)
    vg_blocks = IN1_WIDTH // IN1_COL - QK_BLOCKS
    vg = pl.pallas_call(
        _inproj1_vg_kernel,
        grid=(vg_blocks, n // tm),
        in_specs=[h_spec, pl.BlockSpec((D_MODEL, IN1_COL), lambda j, i: (0, QK_BLOCKS + j))],
        out_specs=out_spec,
        out_shape=jax.ShapeDtypeStruct((n, vg_blocks * IN1_COL), BF16),
        compiler_params=_params("parallel", "parallel"),
        name="inproj1_vg",
    )(h, w_bf16)
    return qk, vg


def _retention_kernel(*refs, n_chunks, has_state, heads, unroll):
    if has_state:
        decay_ref, q_ref, k_ref, v_ref, gf_ref, gb_ref, sf0_ref, sb0_ref, o_ref, inc_ref, seen_ref = refs
    else:
        decay_ref, q_ref, k_ref, v_ref, gf_ref, gb_ref, o_ref, sf_ref, sb_ref, inc_ref, seen_ref = refs
    c = RET_CHUNK
    row = lax.broadcasted_iota(jnp.int32, (c, c), 0).astype(F32)
    col = lax.broadcasted_iota(jnp.int32, (c, c), 1).astype(F32)
    pos = lax.broadcasted_iota(jnp.int32, (c, 1), 0).astype(F32)

    def chunk(i):
        return pl.ds(pl.multiple_of(i * c, c), c)

    diff = [row - col, col - row]
    qk_cols = [slice(g * RET_DK, (g + 1) * RET_DK) for g in range(heads)]
    v_cols = [slice(g * RET_DV, (g + 1) * RET_DV) for g in range(heads)]
    intra, q_dec, k_dec, chunk_dec = [], [], [], []
    for g in range(heads):
        hd = pl.program_id(1) * heads + g
        lg = [-jnp.exp(jnp.full((1, 1), decay_ref[d, hd], F32)) for d in range(2)]
        intra.append([jnp.where(diff[d] >= 0, jnp.exp(lg[d] * jnp.maximum(diff[d], 0.0)), 0.0) for d in range(2)])
        q_dec.append([jnp.exp(lg[0] * (pos + 1.0)), jnp.exp(lg[1] * (c - pos))])
        k_dec.append([jnp.exp(lg[0] * ((c - 1.0) - pos)), jnp.exp(lg[1] * pos)])
        chunk_dec.append([jnp.exp(lg[d] * float(c)) for d in range(2)])

    def increments(i, carry):
        rows = chunk(i)
        for g in range(heads):
            ki = k_ref[rows, qk_cols[g]].astype(F32)
            vi = v_ref[rows, v_cols[g]]
            for d in range(2):
                inc_ref[g, d, i] = _dot_tn((ki * k_dec[g][d]).astype(BF16), vi)
        return carry

    lax.fori_loop(0, n_chunks, increments, 0, unroll=unroll)

    for g in range(heads):
        finals = []
        for d in range(2):
            def scan(t, state):
                i = t if d == 0 else n_chunks - 1 - t
                seen_ref[g, d, i] = state.astype(BF16)
                return state * chunk_dec[g][d] + inc_ref[g, d, i]

            if has_state:
                state0 = (sf0_ref if d == 0 else sb0_ref)[0, g]
            else:
                state0 = jnp.zeros((RET_DK, RET_DV), F32)
            finals.append(lax.fori_loop(0, n_chunks, scan, state0))
        if not has_state:
            sf_ref[0, g] = finals[0]
            sb_ref[0, g] = finals[1]

    def outputs(i, carry):
        rows = chunk(i)
        for g in range(heads):
            qi = q_ref[rows, qk_cols[g]]
            vi = v_ref[rows, v_cols[g]]
            s = _dot_nt(qi, k_ref[rows, qk_cols[g]])
            mixed = None
            for d, g_ref in enumerate((gf_ref, gb_ref)):
                y = _dot((s * intra[g][d]).astype(BF16), vi) + _dot(qi, seen_ref[g, d, i]) * q_dec[g][d]
                gated = g_ref[rows, v_cols[g]].astype(F32) * _rms(y)
                mixed = gated if mixed is None else mixed + gated
            o_ref[rows, v_cols[g]] = mixed.astype(o_ref.dtype)
        return carry

    lax.fori_loop(0, n_chunks, outputs, 0, unroll=unroll)


def _retention(stream, qk, vg, decays, states, heads):
    n = stream.n_tok
    t = stream.seq_len
    n_chunks = t // RET_CHUNK
    has_state = states is not None
    qk_w = heads * RET_DK
    v_w = heads * RET_DV
    k_blk = RET_QK_WIDTH // qk_w
    gf_blk = RET_V_WIDTH // v_w
    gb_blk = 2 * gf_blk
    in_specs = [
        pl.BlockSpec(memory_space=pltpu.SMEM),
        pl.BlockSpec((t, qk_w), lambda b, h: (b, h)),
        pl.BlockSpec((t, qk_w), lambda b, h: (b, k_blk + h)),
        pl.BlockSpec((t, v_w), lambda b, h: (b, h)),
        pl.BlockSpec((t, v_w), lambda b, h: (b, gf_blk + h)),
        pl.BlockSpec((t, v_w), lambda b, h: (b, gb_blk + h)),
    ]
    args = [decays, qk, qk, vg, vg, vg]
    state_spec = pl.BlockSpec((1, heads, RET_DK, RET_DV), lambda b, h: (b, h, 0, 0))
    out_specs = [pl.BlockSpec((t, v_w), lambda b, h: (b, h))]
    out_shape = [jax.ShapeDtypeStruct((n, RET_V_WIDTH), BF16)]
    if has_state:
        in_specs += [state_spec, state_spec]
        args += list(states)
    else:
        out_specs += [state_spec, state_spec]
        out_shape += [jax.ShapeDtypeStruct((stream.n_seq, RET_HEADS, RET_DK, RET_DV), F32)] * 2
    return pl.pallas_call(
        functools.partial(_retention_kernel, n_chunks=n_chunks, has_state=has_state, heads=heads,
                          unroll=min(n_chunks, 2)),
        grid=(stream.n_seq, RET_HEADS // heads),
        in_specs=in_specs,
        out_specs=out_specs,
        out_shape=out_shape,
        scratch_shapes=[pltpu.VMEM((heads, 2, n_chunks, RET_DK, RET_DV), F32),
                        pltpu.VMEM((heads, 2, n_chunks, RET_DK, RET_DV), BF16)],
        compiler_params=_params("parallel", "parallel"),
        name="retention",
    )(*args)


def _rope_tables(n_tokens, dim):
    t = np.arange(n_tokens)
    pos = np.stack([t // GRID_W, t % GRID_W]).astype(np.float64)
    n_freq = dim // 4
    freqs = ROPE_THETA ** (-np.arange(n_freq, dtype=np.float64) / n_freq)
    ang = pos[:, :, None] * freqs
    cos, sin = np.cos(ang), np.sin(ang)
    cos_t = np.concatenate([cos[0], cos[0], cos[1], cos[1]], axis=-1)
    sin_t = np.concatenate([-sin[0], sin[0], -sin[1], sin[1]], axis=-1)
    return jnp.asarray(cos_t, F32), jnp.asarray(sin_t, F32)


def kernel(x_prompt, x_sample, cache_k0, cache_v0, state_ret_fwd1, state_ret_bwd1, c, c_ctx,
           norm_mix_g0, mod_w0, mod_b0, in_w0, conv_w0, conv_b0, conv_norm_g0, conv_norm_b0,
           q_norm_g0, k_norm_g0, out_w0, norm_ffn_g0,
           moe_grp_w0, moe_grp_b0, moe_rtr_w0, moe_rtr_b0, moe_w_gate0, moe_w_up0, moe_w_down0,
           norm_mix_g1, mod_w1, mod_b1, in_w1, ret_decay_fwd1, ret_decay_bwd1, out_w1, norm_ffn_g1,
           moe_grp_w1, moe_grp_b1, moe_rtr_w1, moe_rtr_b1, moe_w_gate1, moe_w_up1, moe_w_down1,
           final_norm_g):
    batch, seq, d = x_prompt.shape
    dec_batch, dec_seq, _ = x_sample.shape
    assert d == D_MODEL and 1 + dec_batch <= MOD_SLOTS
    ctx = _Stream(batch, seq, first_slot=0, rope=False)
    lat = _Stream(dec_batch, dec_seq, first_slot=1, rope=True)
    assert ctx.n_tok % ROW_TILE == 0 and lat.n_tok % ROW_TILE == 0 and ROW_CHUNKS == SUBLANES
    assert seq % CONV_TILE == 0 and dec_seq % ROW_TILE == 0 and ROW_TILE % seq == 0

    cond = jnp.concatenate([c_ctx[None, :], c, jnp.zeros((MOD_SLOTS - 1 - dec_batch, d), F32)], axis=0)
    mod0 = _modulation(cond, mod_w0, mod_b0)
    mod1 = _modulation(cond, mod_w1, mod_b1)
    cos, sin = _rope_tables(dec_seq, HEAD_DIM)
    decays = jnp.stack([ret_decay_fwd1, ret_decay_bwd1]).astype(F32)

    in_w0_b = in_w0.astype(BF16)
    def partner_order(a):
        quarter = HEAD_DIM // 4
        blocks = a.reshape(a.shape[:-1] + (a.shape[-1] // (2 * quarter), 2, quarter))
        return blocks[..., ::-1, :].reshape(a.shape)

    qk0 = 2 * CONV_CH
    in_w0_partner = partner_order(in_w0_b[:, qk0:qk0 + ATTN_WIDTH + KV_WIDTH])
    q_gains = jnp.stack([q_norm_g0, partner_order(q_norm_g0)])
    k_gains = jnp.stack([k_norm_g0, partner_order(k_norm_g0)])
    out_w0_b = out_w0.astype(BF16)
    in_w1_b = in_w1.astype(BF16)
    out_w1_b = out_w1.astype(BF16)
    moe0 = (moe_w_gate0.astype(BF16), moe_w_up0.astype(BF16), moe_w_down0.astype(BF16))
    moe1 = (moe_w_gate1.astype(BF16), moe_w_up1.astype(BF16), moe_w_down1.astype(BF16))
    route0 = _router_params(moe_grp_w0, moe_grp_b0, moe_rtr_w0, moe_rtr_b0)
    route1 = _router_params(moe_grp_w1, moe_grp_b1, moe_rtr_w1, moe_rtr_b1)

    def run(stream, x, cache, states):
        x = x.reshape(stream.n_tok, d)
        kv_dtype = BF16 if stream.rope else F32
        a, q, k, v = _inproj0(stream, x, mod0, norm_mix_g0, in_w0_b, in_w0_partner, q_gains, k_gains, cos, sin,
                              kv_dtype)
        a = _conv_branch(stream, a, conv_w0, conv_b0, conv_norm_g0, conv_norm_b0)
        tq = min(stream.seq_len, 256)
        o = _attention(stream, q, k, v, cache, tq, kv_heads=1 if cache is not None else N_KV_HEADS)
        x, *routed = _outproj_route(stream, [a, o], out_w0_b, x, mod0, norm_ffn_g0, *route0)
        x, h = _moe(stream, *routed, *moe0, x, mod0, norm_mix_g1, mod1, last_layer=False)
        qk, vg = _inproj1(stream, h, in_w1_b, cos, sin)
        ret = _retention(stream, qk, vg, decays, states, heads=2 if states is not None else RET_HEADS)
        x, *routed = _outproj_route(stream, [ret[0]], out_w1_b, x, mod1, norm_ffn_g1, *route1)
        y, = _moe(stream, *routed, *moe1, x, mod1, final_norm_g, mod1, last_layer=True)
        return y.reshape(stream.n_seq, stream.seq_len, d), k, v, ret[1:]

    y_prompt, k_ctx, v_ctx, new_states = run(ctx, x_prompt, None, None)
    y_sample, _, _, _ = run(lat, x_sample, (cache_k0, cache_v0), (state_ret_fwd1, state_ret_bwd1))
    return (y_prompt, y_sample, k_ctx, v_ctx, new_states[0], new_states[1])
```

```python
import functools

import jax
import jax.numpy as jnp
import numpy as np
from jax import lax
from jax.experimental import pallas as pl
from jax.experimental.pallas import tpu as pltpu

D_MODEL = 1024
GRID_W = 64
EPS = 1e-6
CONV_CH = 512
CONV_WIDTH = 31
CONV_HALF = CONV_WIDTH // 2
N_Q_HEADS = 8
N_KV_HEADS = 2
Q_PER_KV = N_Q_HEADS // N_KV_HEADS
HEAD_DIM = 128
ROPE_THETA = 10000.0
ATTN_WIDTH = N_Q_HEADS * HEAD_DIM
KV_WIDTH = N_KV_HEADS * HEAD_DIM
IN0_WIDTH = 2 * CONV_CH + ATTN_WIDTH + 2 * KV_WIDTH
RET_HEADS = 8
RET_DK = 128
RET_DV = 256
RET_CHUNK = 128
RET_QK_WIDTH = RET_HEADS * RET_DK
RET_V_WIDTH = RET_HEADS * RET_DV
IN1_WIDTH = 2 * RET_QK_WIDTH + 3 * RET_V_WIDTH
N_GROUPS = 4
EXPERTS_PER_GROUP = 4
N_EXPERTS = N_GROUPS * EXPERTS_PER_GROUP
EXPERT_FF = 512

LANES = 128
SUBLANES = 8
VMEM_LIMIT_BYTES = 56 * 1024 * 1024

MOD_SLOTS = 8
ROW_TILE = 512
ROW_CHUNKS = D_MODEL // LANES
PAYLOAD_CHUNKS = 2 * ROW_CHUNKS
FFN_TILE = 256
COMBINE_TILE = 256
ROUTE_GROUP, ROUTE_RANK = 0, 1
CONV_TILE = 256
CONV_HALO = 16
CONV_ROWS = 64
ROUTE_LANES = LANES
EXPERT_LANE0 = N_GROUPS
NEG_BIG = -1e30
LOG2_E = 1.4426950408889634
ATTN_KV_CHUNK = 512

F32 = jnp.float32
BF16 = jnp.bfloat16


def _params(*semantics):
    return pltpu.CompilerParams(dimension_semantics=semantics, vmem_limit_bytes=VMEM_LIMIT_BYTES)


def _sigmoid(x):
    return 1.0 / (1.0 + jnp.exp(-x))


def _silu(x):
    return x * _sigmoid(x)


def _rms(x):
    return x * lax.rsqrt(jnp.mean(x * x, axis=-1, keepdims=True) + EPS)


def _rope(x, cos, sin):
    lane = lax.broadcasted_iota(jnp.int32, x.shape, 1)
    take_upper = (lane % (HEAD_DIM // 2)) < (HEAD_DIM // 4)
    partner = jnp.where(take_upper, pltpu.roll(x, HEAD_DIM - HEAD_DIM // 4, 1), pltpu.roll(x, HEAD_DIM // 4, 1))
    return x * cos + partner * sin


def _dot(a, b):
    return jnp.dot(a, b, preferred_element_type=F32)


def _dot_nt(a, b):
    return lax.dot_general(a, b, (((1,), (1,)), ((), ())), preferred_element_type=F32)


def _dot_tn(a, b):
    return lax.dot_general(a, b, (((0,), (0,)), ((), ())), preferred_element_type=F32)


def _mod_kernel(c_ref, w_ref, b_ref, o_ref):
    c = c_ref[...]
    o_ref[...] = _dot(_silu(c).astype(BF16), w_ref[...].astype(BF16)) + b_ref[...]


def _modulation(cond, mod_w, mod_b):
    n_out = mod_w.shape[1]
    col = D_MODEL
    out = pl.pallas_call(
        _mod_kernel,
        grid=(n_out // col,),
        in_specs=[
            pl.BlockSpec((MOD_SLOTS, D_MODEL), lambda j: (0, 0)),
            pl.BlockSpec((D_MODEL, col), lambda j: (0, j)),
            pl.BlockSpec((1, col), lambda j: (0, j)),
        ],
        out_specs=pl.BlockSpec((MOD_SLOTS, col), lambda j: (0, j)),
        out_shape=jax.ShapeDtypeStruct((MOD_SLOTS, n_out), F32),
        compiler_params=_params("parallel"),
        name="modulation",
    )(cond, mod_w, mod_b.reshape(1, n_out))
    return out.reshape(MOD_SLOTS, 6, D_MODEL)


def _slot_map(stream):
    if stream.first_slot == 0:
        return lambda i: 0
    return lambda i: stream.first_slot + (i * ROW_TILE) // stream.seq_len


class _Stream:
    def __init__(self, n_seq, seq_len, first_slot, rope):
        self.n_seq = n_seq
        self.seq_len = seq_len
        self.first_slot = first_slot
        self.rope = rope
        self.n_tok = n_seq * seq_len


def _modulated_norm(x, g, shift, scale):
    return _rms(x) * g * (1.0 + scale) + shift


def _inproj0_kernel(x_ref, mod_ref, g_ref, w_ref, wp_ref, qg_ref, kg_ref, cos_ref, sin_ref,
                    a_ref, q_ref, k_ref, v_ref, *, rope):
    h = _modulated_norm(x_ref[...], g_ref[...], mod_ref[0, 0:1, :], mod_ref[0, 1:2, :]).astype(BF16)
    p = _dot(h, w_ref[...])
    if rope:
        p_partner = _dot(h, wp_ref[...])
    a_ref[...] = p[:, :CONV_CH] * _sigmoid(p[:, CONV_CH:2 * CONV_CH])
    q0 = 2 * CONV_CH
    k0 = q0 + ATTN_WIDTH
    v0 = k0 + KV_WIDTH
    cos = cos_ref[...]
    sin = sin_ref[...]
    q_scale = HEAD_DIM ** -0.5 * LOG2_E

    def normed_head(col, partner_col, gain_ref):
        ph = p[:, col:col + HEAD_DIM]
        inv = lax.rsqrt(jnp.mean(ph * ph, axis=-1, keepdims=True) + EPS)
        out = ph * inv * gain_ref[0:1, :]
        if rope:
            out = out * cos + (p_partner[:, partner_col:partner_col + HEAD_DIM] * inv * gain_ref[1:2, :]) * sin
        return out

    for hh in range(N_Q_HEADS):
        qh = normed_head(q0 + hh * HEAD_DIM, hh * HEAD_DIM, qg_ref)
        q_ref[:, hh * HEAD_DIM:(hh + 1) * HEAD_DIM] = (qh * q_scale).astype(q_ref.dtype)
    seqs, _, rows, _ = k_ref.shape
    for hh in range(N_KV_HEADS):
        kh = normed_head(k0 + hh * HEAD_DIM, ATTN_WIDTH + hh * HEAD_DIM, kg_ref)
        vh = p[:, v0 + hh * HEAD_DIM:v0 + (hh + 1) * HEAD_DIM]
        for sq in range(seqs):
            k_ref[sq, hh] = kh[sq * rows:(sq + 1) * rows].astype(k_ref.dtype)
            v_ref[sq, hh] = vh[sq * rows:(sq + 1) * rows].astype(v_ref.dtype)


def _inproj0(stream, x, mod, norm_g, w_bf16, w_partner, q_gains, k_gains, cos, sin, kv_dtype):
    n = stream.n_tok
    tiles_per_seq = max(stream.seq_len // ROW_TILE, 1)
    rope_map = (lambda i: (i % tiles_per_seq, 0)) if stream.rope else (lambda i: (0, 0))
    slot = _slot_map(stream)
    row = lambda i: (i, 0)
    fixed = lambda i: (0, 0)
    seqs_per_tile = max(ROW_TILE // stream.seq_len, 1)
    kv_spec = pl.BlockSpec((seqs_per_tile, N_KV_HEADS, ROW_TILE // seqs_per_tile, HEAD_DIM),
                           lambda i: (i // tiles_per_seq, 0, i % tiles_per_seq, 0))
    kv_shape = jax.ShapeDtypeStruct((stream.n_seq, N_KV_HEADS, stream.seq_len, HEAD_DIM), kv_dtype)
    return pl.pallas_call(
        functools.partial(_inproj0_kernel, rope=stream.rope),
        grid=(n // ROW_TILE,),
        in_specs=[
            pl.BlockSpec((ROW_TILE, D_MODEL), row),
            pl.BlockSpec((1, 6, D_MODEL), lambda i: (slot(i), 0, 0)),
            pl.BlockSpec((1, D_MODEL), fixed),
            pl.BlockSpec((D_MODEL, IN0_WIDTH), fixed),
            pl.BlockSpec((D_MODEL, ATTN_WIDTH + KV_WIDTH), fixed),
            pl.BlockSpec((2, HEAD_DIM), fixed),
            pl.BlockSpec((2, HEAD_DIM), fixed),
            pl.BlockSpec((ROW_TILE, HEAD_DIM), rope_map),
            pl.BlockSpec((ROW_TILE, HEAD_DIM), rope_map),
        ],
        out_specs=[
            pl.BlockSpec((ROW_TILE, CONV_CH), row),
            pl.BlockSpec((ROW_TILE, ATTN_WIDTH), row),
            kv_spec,
            kv_spec,
        ],
        out_shape=[
            jax.ShapeDtypeStruct((n, CONV_CH), F32),
            jax.ShapeDtypeStruct((n, ATTN_WIDTH), BF16),
            kv_shape,
            kv_shape,
        ],
        compiler_params=_params("parallel"),
        name="inproj0",
    )(x, mod, norm_g.reshape(1, D_MODEL), w_bf16, w_partner, q_gains, k_gains, cos, sin)


def _conv_kernel(prev_ref, main_ref, next_ref, w_ref, b_ref, g_ref, beta_ref, o_ref, pad_ref, acc_ref, *,
                 tiles_per_seq):
    i = pl.program_id(0)
    has_prev = (i % tiles_per_seq) != 0
    has_next = (i % tiles_per_seq) != (tiles_per_seq - 1)
    pad_ref[0:CONV_HALO, :] = jnp.where(has_prev, prev_ref[...], 0.0)
    pad_ref[CONV_HALO:CONV_HALO + CONV_TILE, :] = main_ref[...]
    pad_ref[CONV_HALO + CONV_TILE:, :] = jnp.where(has_next, next_ref[...], 0.0)
    first = CONV_HALO - CONV_HALF
    window = CONV_ROWS + SUBLANES
    for c0 in range(0, CONV_CH, LANES):
        for r0 in range(0, CONV_TILE, CONV_ROWS):
            acc = jnp.zeros((CONV_ROWS, LANES), F32)
            for phase in range(SUBLANES):
                partial = jnp.zeros((window, LANES), F32)
                for j in range(phase, CONV_WIDTH, SUBLANES):
                    base = r0 + j - phase
                    partial = partial + w_ref[j:j + 1, c0:c0 + LANES] * pad_ref[base:base + window, c0:c0 + LANES]
                acc = acc + partial[first + phase:first + phase + CONV_ROWS]
            acc_ref[r0:r0 + CONV_ROWS, c0:c0 + LANES] = acc
    a = acc_ref[...] + b_ref[...]
    mu = jnp.mean(a, axis=-1, keepdims=True)
    d = a - mu
    var = jnp.mean(d * d, axis=-1, keepdims=True)
    y = d * lax.rsqrt(var + EPS) * g_ref[...] + beta_ref[...]
    o_ref[...] = _silu(y).astype(o_ref.dtype)


def _conv_branch(stream, a, conv_w, conv_b, norm_g, norm_b):
    n = stream.n_tok
    tiles_per_seq = stream.seq_len // CONV_TILE
    halo_per_tile = CONV_TILE // CONV_HALO
    n_halo = n // CONV_HALO
    fixed = lambda i: (0, 0)
    return pl.pallas_call(
        functools.partial(_conv_kernel, tiles_per_seq=tiles_per_seq),
        grid=(n // CONV_TILE,),
        in_specs=[
            pl.BlockSpec((CONV_HALO, CONV_CH), lambda i: (jnp.maximum(i * halo_per_tile - 1, 0), 0)),
            pl.BlockSpec((CONV_TILE, CONV_CH), lambda i: (i, 0)),
            pl.BlockSpec((CONV_HALO, CONV_CH), lambda i: (jnp.minimum((i + 1) * halo_per_tile, n_halo - 1), 0)),
            pl.BlockSpec((CONV_WIDTH, CONV_CH), fixed),
            pl.BlockSpec((1, CONV_CH), fixed),
            pl.BlockSpec((1, CONV_CH), fixed),
            pl.BlockSpec((1, CONV_CH), fixed),
        ],
        out_specs=pl.BlockSpec((CONV_TILE, CONV_CH), lambda i: (i, 0)),
        out_shape=jax.ShapeDtypeStruct((n, CONV_CH), BF16),
        scratch_shapes=[
            pltpu.VMEM((CONV_TILE + 2 * CONV_HALO, CONV_CH), F32),
            pltpu.VMEM((CONV_TILE, CONV_CH), F32),
        ],
        compiler_params=_params("parallel"),
        name="conv_branch",
    )(a, a, a, conv_w, conv_b.reshape(1, CONV_CH), norm_g.reshape(1, CONV_CH), norm_b.reshape(1, CONV_CH))


def _attn_kernel(*refs, tq, past):
    if past:
        q_ref, k_ref, v_ref, kc_ref, vc_ref, o_ref, ks_ref, vs_ref = refs
    else:
        q_ref, k_ref, v_ref, o_ref, ks_ref, vs_ref = refs
    kv_heads, n_keys, _ = ks_ref.shape
    group_w = Q_PER_KV * HEAD_DIM

    @pl.when(pl.program_id(2) == 0)
    def _():
        lane = lax.broadcasted_iota(jnp.int32, (n_keys, HEAD_DIM), 1)
        for hk in range(kv_heads):
            vs_ref[hk, :, HEAD_DIM:] = jnp.where(lane == 0, 1.0, 0.0).astype(BF16)
            if past:
                ks_ref[hk, 0:past, :] = kc_ref[0, hk].astype(BF16)
                vs_ref[hk, 0:past, 0:HEAD_DIM] = vc_ref[0, hk].astype(BF16)
            ks_ref[hk, past:, :] = k_ref[0, hk].astype(BF16)
            vs_ref[hk, past:, 0:HEAD_DIM] = v_ref[0, hk].astype(BF16)

    for hk in range(kv_heads):
        q = q_ref[:, hk * group_w:(hk + 1) * group_w]
        q4 = jnp.concatenate([q[:, g * HEAD_DIM:(g + 1) * HEAD_DIM] for g in range(Q_PER_KV)], axis=0)
        rows = q4.shape[0]
        m = jnp.full((rows, 1), NEG_BIG, F32)
        acc = jnp.zeros((rows, 2 * HEAD_DIM), F32)
        for c0 in range(0, n_keys, ATTN_KV_CHUNK):
            c1 = min(c0 + ATTN_KV_CHUNK, n_keys)
            s = _dot_nt(q4, ks_ref[hk, c0:c1, :])
            m_new = jnp.maximum(m, jnp.max(s, axis=-1, keepdims=True))
            p = jnp.exp2(s - m_new).astype(BF16)
            acc = jnp.exp2(m - m_new) * acc + _dot(p, vs_ref[hk, c0:c1, :])
            m = m_new
        o = acc[:, 0:HEAD_DIM] * (1.0 / acc[:, HEAD_DIM:HEAD_DIM + 1])
        for g in range(Q_PER_KV):
            col = hk * group_w + g * HEAD_DIM
            o_ref[:, col:col + HEAD_DIM] = o[g * tq:(g + 1) * tq].astype(o_ref.dtype)


def _attention(stream, q, k, v, cache, tq, kv_heads):
    n = stream.n_tok
    t = stream.seq_len
    q_tiles = t // tq
    width = kv_heads * Q_PER_KV * HEAD_DIM
    in_specs = [
        pl.BlockSpec((tq, width), lambda b, h, i: (b * q_tiles + i, h)),
        pl.BlockSpec((1, kv_heads, t, HEAD_DIM), lambda b, h, i: (b, h, 0, 0)),
        pl.BlockSpec((1, kv_heads, t, HEAD_DIM), lambda b, h, i: (b, h, 0, 0)),
    ]
    args = [q, k, v]
    past = 0
    if cache is not None:
        past = cache[0].shape[2]
        in_specs += [pl.BlockSpec((1, kv_heads, past, HEAD_DIM), lambda b, h, i: (b, h, 0, 0))] * 2
        args += list(cache)
    return pl.pallas_call(
        functools.partial(_attn_kernel, tq=tq, past=past),
        grid=(stream.n_seq, N_KV_HEADS // kv_heads, q_tiles),
        in_specs=in_specs,
        out_specs=pl.BlockSpec((tq, width), lambda b, h, i: (b * q_tiles + i, h)),
        out_shape=jax.ShapeDtypeStruct((n, ATTN_WIDTH), BF16),
        scratch_shapes=[pltpu.VMEM((kv_heads, past + t, HEAD_DIM), BF16),
                        pltpu.VMEM((kv_heads, past + t, 2 * HEAD_DIM), BF16)],
        compiler_params=_params("parallel", "parallel", "arbitrary"),
        name="attention",
    )(*args)


def _route(logits, running):
    rows = logits.shape[0]
    lane = lax.broadcasted_iota(jnp.int32, logits.shape, 1)
    lane_f = lane.astype(F32)
    far = float(ROUTE_LANES)
    is_group = lane < N_GROUPS
    gl = jnp.where(is_group, logits, NEG_BIG)
    gmax = jnp.max(gl, axis=-1, keepdims=True)
    gsum = jnp.sum(jnp.where(is_group, jnp.exp(gl - gmax), 0.0), axis=-1, keepdims=True)
    g_w = 1.0 / gsum
    gidx = jnp.min(jnp.where(gl == gmax, lane_f, far), axis=-1, keepdims=True)
    lo = EXPERT_LANE0 + EXPERTS_PER_GROUP * gidx
    in_group = (lane_f >= lo) & (lane_f < lo + EXPERTS_PER_GROUP)
    el = jnp.where(in_group, logits, NEG_BIG)
    v1 = jnp.max(el, axis=-1, keepdims=True)
    i1 = jnp.min(jnp.where(el == v1, lane_f, far), axis=-1, keepdims=True)
    el2 = jnp.where(lane_f == i1, NEG_BIG, el)
    v2 = jnp.max(el2, axis=-1, keepdims=True)
    i2 = jnp.min(jnp.where(el2 == v2, lane_f, far), axis=-1, keepdims=True)
    e2 = jnp.exp(v2 - v1)
    w1 = g_w / (1.0 + e2)
    w2 = w1 * e2
    in_my_group = lane_f == gidx
    chosen = in_my_group.astype(BF16)
    earlier = (lax.broadcasted_iota(jnp.int32, (rows, rows), 0)
               > lax.broadcasted_iota(jnp.int32, (rows, rows), 1)).astype(BF16)
    before = running + _dot(earlier, chosen)
    rank = jnp.sum(jnp.where(in_my_group, before, 0.0), axis=-1, keepdims=True)
    record = jnp.where(lane == ROUTE_GROUP, gidx, jnp.where(lane == ROUTE_RANK, rank, 0.0))
    weights = jnp.where(lane_f == i1 - lo, w1, 0.0) + jnp.where(lane_f == i2 - lo, w2, 0.0)
    return record, weights, running + jnp.sum(chosen.astype(F32), axis=0, keepdims=True)


def _outproj_kernel(*refs, widths):
    n_in = len(widths)
    in_refs = refs[:n_in]
    w_ref, x_ref, mod_ref, g_ref, rw_ref, rb_ref, x1_ref, pay_ref, route_t_ref, count_ref, run_ref = refs[n_in:]

    @pl.when(pl.program_id(0) == 0)
    def _():
        run_ref[...] = jnp.zeros_like(run_ref)

    acc = None
    off = 0
    for r, width in zip(in_refs, widths):
        part = _dot(r[...], w_ref[off:off + width, :])
        acc = part if acc is None else acc + part
        off += width
    x1 = x_ref[...] + mod_ref[0, 2:3, :] * acc
    x1_ref[...] = x1
    h = _modulated_norm(x1, g_ref[...], mod_ref[0, 3:4, :], mod_ref[0, 4:5, :])
    h_hi = h.astype(BF16)
    h_lo = (h - h_hi.astype(F32)).astype(BF16)
    both = _dot(h_hi, rw_ref[...])
    logits = (both[:, :ROUTE_LANES] + both[:, ROUTE_LANES:] + _dot(h_lo, rw_ref[:, :ROUTE_LANES]) + rb_ref[...])
    record, weights, running = _route(logits, run_ref[...])
    route_t_ref[...] = record.T[:SUBLANES]
    run_ref[...] = running
    count_ref[...] = jnp.broadcast_to(running, count_ref.shape)
    rows = h.shape[0]
    _store_row_major(pay_ref, h, PAYLOAD_CHUNKS)
    pay_ref[pl.ds(ROW_CHUNKS, rows, stride=PAYLOAD_CHUNKS), :] = weights
    for c in range(ROW_CHUNKS + 1, PAYLOAD_CHUNKS):
        pay_ref[pl.ds(c, rows, stride=PAYLOAD_CHUNKS), :] = jnp.zeros((rows, LANES), F32)


def _store_row_major(ref, x, pitch=ROW_CHUNKS):
    rows = x.shape[0]
    for c in range(ROW_CHUNKS):
        ref[pl.ds(c, rows, stride=pitch), :] = x[:, c * LANES:(c + 1) * LANES]


def _load_row_major(ref, rows, pitch=ROW_CHUNKS):
    return jnp.concatenate([ref[pl.ds(c, rows, stride=pitch), :] for c in range(ROW_CHUNKS)], axis=-1)


def _outproj_route(stream, inputs, w_bf16, x, mod, ffn_norm_g, route_w, route_b):
    n = stream.n_tok
    widths = tuple(a.shape[1] for a in inputs)
    slot = _slot_map(stream)
    row = lambda i: (i, 0)
    fixed = lambda i: (0, 0)
    return pl.pallas_call(
        functools.partial(_outproj_kernel, widths=widths),
        grid=(n // ROW_TILE,),
        in_specs=[pl.BlockSpec((ROW_TILE, width), row) for width in widths] + [
            pl.BlockSpec((sum(widths), D_MODEL), fixed),
            pl.BlockSpec((ROW_TILE, D_MODEL), row),
            pl.BlockSpec((1, 6, D_MODEL), lambda i: (slot(i), 0, 0)),
            pl.BlockSpec((1, D_MODEL), fixed),
            pl.BlockSpec((D_MODEL, 2 * ROUTE_LANES), fixed),
            pl.BlockSpec((1, ROUTE_LANES), fixed),
        ],
        out_specs=[
            pl.BlockSpec((ROW_TILE, D_MODEL), row),
            pl.BlockSpec((ROW_TILE * PAYLOAD_CHUNKS, LANES), row),
            pl.BlockSpec((SUBLANES, ROW_TILE), lambda i: (0, i)),
            pl.BlockSpec((SUBLANES, ROUTE_LANES), fixed),
        ],
        out_shape=[
            jax.ShapeDtypeStruct((n, D_MODEL), F32),
            jax.ShapeDtypeStruct((n * PAYLOAD_CHUNKS, LANES), F32),
            jax.ShapeDtypeStruct((SUBLANES, n), F32),
            jax.ShapeDtypeStruct((SUBLANES, ROUTE_LANES), F32),
        ],
        scratch_shapes=[pltpu.VMEM((1, ROUTE_LANES), F32)],
        compiler_params=_params("arbitrary"),
        name="outproj_route",
    )(*inputs, w_bf16, x, mod, ffn_norm_g.reshape(1, D_MODEL), route_w, route_b)


def _router_params(grp_w, grp_b, rtr_w, rtr_b):
    w = jnp.concatenate([grp_w, jnp.moveaxis(rtr_w, 0, 1).reshape(D_MODEL, N_EXPERTS)], axis=1)
    w = jnp.pad(w, ((0, 0), (0, ROUTE_LANES - w.shape[1])))
    hi = w.astype(BF16)
    lo = (w - hi.astype(F32)).astype(BF16)
    b = jnp.concatenate([grp_b, rtr_b.reshape(N_EXPERTS)])
    b = jnp.pad(b, (0, ROUTE_LANES - b.shape[0])).reshape(1, ROUTE_LANES)
    return jnp.concatenate([hi, lo], axis=1), b


def _sorted_tiles(n_tok):
    return n_tok // FFN_TILE + N_GROUPS


def _dispatch_plan(route_t, counts, n_tok):
    n_tiles = _sorted_tiles(n_tok)
    count = counts[0, :N_GROUPS].astype(jnp.int32)
    tiles = (count + (FFN_TILE - 1)) // FFN_TILE
    start = (jnp.cumsum(tiles) - tiles) * FFN_TILE
    tile_end = jnp.cumsum(tiles)
    j = jnp.arange(n_tiles, dtype=jnp.int32)
    tile_group = jnp.minimum(jnp.sum(j[:, None] >= tile_end[None, :], axis=1), N_GROUPS - 1).astype(jnp.int32)
    group_end = jnp.take(start + count, tile_group)
    valid = jnp.where(j < tile_end[-1], jnp.clip(group_end - j * FFN_TILE, 0, FFN_TILE), 0).astype(jnp.int32)
    where = jnp.concatenate([start, route_t[:2].astype(jnp.int32).reshape(-1)])
    return where, tile_group, valid


def _sorted_row(where_ref, n_tok, tok):
    return where_ref[where_ref[N_GROUPS + ROUTE_GROUP * n_tok + tok]] + where_ref[N_GROUPS + ROUTE_RANK * n_tok + tok]


def _token_rows(t, pitch=ROW_CHUNKS):
    return pl.ds(pl.multiple_of(t * pitch, pitch), pitch)


def _ffn_kernel(tg_ref, valid_ref, where_ref, pay_ref, wg_ref, wu_ref, wd_ref, ys_ref, x_buf, src_ref, sems, *, n_tok):
    j = pl.program_id(0)
    n_tiles = pl.num_programs(0)

    @pl.when(j == 0)
    def _():
        def place(t, carry):
            src_ref[_sorted_row(where_ref, n_tok, t)] = t
            return carry

        lax.fori_loop(0, n_tok, place, 0, unroll=8)

    def gather_copy(tile, r):
        tok = src_ref[tile * FFN_TILE + r]
        return pltpu.make_async_copy(pay_ref.at[_token_rows(tok, PAYLOAD_CHUNKS)],
                                     x_buf.at[tile % 2, _token_rows(r, PAYLOAD_CHUNKS)], sems.at[tile % 2])

    def start_gather(tile):
        def issue_pair(r2, carry):
            for k in range(2):
                gather_copy(tile, 2 * r2 + k).start(priority=k)
            return carry

        def issue(r, carry):
            gather_copy(tile, r).start()
            return carry

        @pl.when(valid_ref[tile] == FFN_TILE)
        def _():
            lax.fori_loop(0, FFN_TILE // 2, issue_pair, 0, unroll=4)

        @pl.when(valid_ref[tile] < FFN_TILE)
        def _():
            lax.fori_loop(0, valid_ref[tile], issue, 0)

    @pl.when(j == 0)
    def _():
        x_buf[...] = jnp.zeros_like(x_buf)
        start_gather(j)

    @pl.when(j + 1 < n_tiles)
    def _():
        @pl.when(valid_ref[j + 1] > 0)
        def _():
            start_gather(j + 1)

    @pl.when(valid_ref[j] == 0)
    def _():
        ys_ref[...] = jnp.zeros_like(ys_ref)

    @pl.when(valid_ref[j] > 0)
    def _():
        slot = j % 2
        n = pl.multiple_of(valid_ref[j] * PAYLOAD_CHUNKS, PAYLOAD_CHUNKS)
        pltpu.make_async_copy(pay_ref.at[pl.ds(0, n)], x_buf.at[slot, pl.ds(0, n)], sems.at[slot]).wait()
        x = _load_row_major(x_buf.at[slot], FFN_TILE, PAYLOAD_CHUNKS).astype(BF16)
        w = x_buf[slot, pl.ds(ROW_CHUNKS, FFN_TILE, stride=PAYLOAD_CHUNKS), :]
        y = None
        for e in range(EXPERTS_PER_GROUP):
            hidden = _silu(_dot(x, wg_ref[e])) * _dot(x, wu_ref[e]) * w[:, e:e + 1]
            part = _dot(hidden.astype(BF16), wd_ref[e])
            y = part if y is None else y + part
        _store_row_major(ys_ref, y)


def _expert_ffn(n_tok, tile_group, valid, where, payload, wg, wu, wd):
    group = lambda j, tg, va, sr: (tg[j], 0, 0)
    n_tiles = _sorted_tiles(n_tok)
    return pl.pallas_call(
        functools.partial(_ffn_kernel, n_tok=n_tok),
        grid_spec=pltpu.PrefetchScalarGridSpec(
            num_scalar_prefetch=3,
            grid=(n_tiles,),
            in_specs=[
                pl.BlockSpec(memory_space=pl.ANY),
                pl.BlockSpec((EXPERTS_PER_GROUP, D_MODEL, EXPERT_FF), group),
                pl.BlockSpec((EXPERTS_PER_GROUP, D_MODEL, EXPERT_FF), group),
                pl.BlockSpec((EXPERTS_PER_GROUP, EXPERT_FF, D_MODEL), group),
            ],
            out_specs=pl.BlockSpec((FFN_TILE * ROW_CHUNKS, LANES), lambda j, tg, va, sr: (j, 0)),
            scratch_shapes=[pltpu.VMEM((2, FFN_TILE * PAYLOAD_CHUNKS, LANES), F32),
                            pltpu.SMEM((n_tiles * FFN_TILE,), jnp.int32), pltpu.SemaphoreType.DMA((2,))],
        ),
        out_shape=jax.ShapeDtypeStruct((n_tiles * FFN_TILE * ROW_CHUNKS, LANES), F32),
        compiler_params=_params("arbitrary"),
        name="moe_ffn",
    )(tile_group, valid, where, payload, wg, wu, wd)


def _combine_kernel(where_ref, x_ref, mod_ref, ng_ref, nmod_ref, ys_ref, o_ref, *rest, n_tok, last_layer):
    if last_layer:
        y_buf, sems = rest
    else:
        h_ref, y_buf, sems = rest
    i = pl.program_id(0)
    n_tiles = pl.num_programs(0)

    def row_copy(tile, t, k):
        slot = tile % 2
        d = _sorted_row(where_ref, n_tok, tile * COMBINE_TILE + t)
        return pltpu.make_async_copy(ys_ref.at[_token_rows(d)], y_buf.at[slot, _token_rows(t)], sems.at[slot, k])

    def start_tile(tile):
        def issue(t2, carry):
            for k in range(2):
                row_copy(tile, 2 * t2 + k, k).start(priority=k)
            return carry

        lax.fori_loop(0, COMBINE_TILE // 2, issue, 0, unroll=4)

    @pl.when(i == 0)
    def _():
        start_tile(i)

    @pl.when(i + 1 < n_tiles)
    def _():
        start_tile(i + 1)

    slot = i % 2
    half = COMBINE_TILE * ROW_CHUNKS // 2
    for k in range(2):
        pltpu.make_async_copy(ys_ref.at[pl.ds(0, half)], y_buf.at[slot, pl.ds(0, half)], sems.at[slot, k]).wait()
    x = x_ref[...] + mod_ref[0, 5:6, :] * _load_row_major(y_buf.at[slot], COMBINE_TILE)
    if last_layer:
        o_ref[...] = _rms(x) * ng_ref[...]
    else:
        o_ref[...] = x
        h_ref[...] = _modulated_norm(x, ng_ref[...], nmod_ref[0, 0:1, :], nmod_ref[0, 1:2, :]).astype(h_ref.dtype)


def _combine(stream, where, ys, x, mod, norm_g, next_mod, last_layer):
    n = stream.n_tok
    first_slot, seq_len = stream.first_slot, stream.seq_len
    slot = (lambda i: 0) if first_slot == 0 else (lambda i: first_slot + (i * COMBINE_TILE) // seq_len)
    row = lambda i, w: (i, 0)
    out_specs = [pl.BlockSpec((COMBINE_TILE, D_MODEL), row)]
    out_shape = [jax.ShapeDtypeStruct((n, D_MODEL), F32)]
    if not last_layer:
        out_specs.append(pl.BlockSpec((COMBINE_TILE, D_MODEL), row))
        out_shape.append(jax.ShapeDtypeStruct((n, D_MODEL), BF16))
    return pl.pallas_call(
        functools.partial(_combine_kernel, n_tok=n, last_layer=last_layer),
        grid_spec=pltpu.PrefetchScalarGridSpec(
            num_scalar_prefetch=1,
            grid=(n // COMBINE_TILE,),
            in_specs=[
                pl.BlockSpec((COMBINE_TILE, D_MODEL), row),
                pl.BlockSpec((1, 6, D_MODEL), lambda i, w: (slot(i), 0, 0)),
                pl.BlockSpec((1, D_MODEL), lambda i, w: (0, 0)),
                pl.BlockSpec((1, 6, D_MODEL), lambda i, w: (slot(i), 0, 0)),
                pl.BlockSpec(memory_space=pl.ANY),
            ],
            out_specs=out_specs,
            scratch_shapes=[pltpu.VMEM((2, COMBINE_TILE * ROW_CHUNKS, LANES), F32),
                            pltpu.SemaphoreType.DMA((2, 2))],
        ),
        out_shape=out_shape,
        compiler_params=_params("arbitrary"),
        name="moe_combine",
    )(where, x, mod, norm_g.reshape(1, D_MODEL), next_mod, ys)


def _moe(stream, payload, route_t, counts, wg, wu, wd, x, mod, norm_g, next_mod, last_layer):
    n = stream.n_tok
    where, tile_group, valid = _dispatch_plan(route_t, counts, n)
    ys = _expert_ffn(n, tile_group, valid, where, payload, wg, wu, wd)
    return _combine(stream, where, ys, x, mod, norm_g, next_mod, last_layer)


IN1_COL = 1024
IN1_ROW_TILE = 1024
QK_BLOCKS = 2 * RET_QK_WIDTH // IN1_COL
Q_BLOCKS = RET_QK_WIDTH // IN1_COL
V_BLOCKS = RET_V_WIDTH // IN1_COL


def _inproj1_qk_kernel(h_ref, w_ref, cos_ref, sin_ref, o_ref, *, rope):
    p = _dot(h_ref[...], w_ref[...])
    scale = jnp.where(pl.program_id(0) < Q_BLOCKS, 1.0, RET_DK ** -0.5)
    cos = cos_ref[...]
    sin = sin_ref[...]
    for hh in range(IN1_COL // RET_DK):
        ph = p[:, hh * RET_DK:(hh + 1) * RET_DK]
        if rope:
            ph = _rope(ph, cos, sin)
        o_ref[:, hh * RET_DK:(hh + 1) * RET_DK] = (ph * scale).astype(o_ref.dtype)


def _inproj1_vg_kernel(h_ref, w_ref, o_ref):
    p = _dot(h_ref[...], w_ref[...])
    o_ref[...] = jnp.where(pl.program_id(0) < V_BLOCKS, p, _silu(p)).astype(o_ref.dtype)


def _inproj1(stream, h, w_bf16, cos, sin):
    n = stream.n_tok
    tm = IN1_ROW_TILE
    tiles_per_seq = max(stream.seq_len // tm, 1)
    rope_map = (lambda j, i: (i % tiles_per_seq, 0)) if stream.rope else (lambda j, i: (0, 0))
    h_spec = pl.BlockSpec((tm, D_MODEL), lambda j, i: (i, 0))
    out_spec = pl.BlockSpec((tm, IN1_COL), lambda j, i: (i, j))
    qk = pl.pallas_call(
        functools.partial(_inproj1_qk_kernel, rope=stream.rope),
        grid=(QK_BLOCKS, n // tm),
        in_specs=[
            h_spec,
            pl.BlockSpec((D_MODEL, IN1_COL), lambda j, i: (0, j)),
            pl.BlockSpec((tm, RET_DK), rope_map),
            pl.BlockSpec((tm, RET_DK), rope_map),
        ],
        out_specs=out_spec,
        out_shape=jax.ShapeDtypeStruct((n, QK_BLOCKS * IN1_COL), BF16),
        compiler_params=_params("parallel", "parallel"),
        name="inproj1_qk",
    )(h, w_bf16, cos, sin)
    vg_blocks = IN1_WIDTH // IN1_COL - QK_BLOCKS
    vg = pl.pallas_call(
        _inproj1_vg_kernel,
        grid=(vg_blocks, n // tm),
        in_specs=[h_spec, pl.BlockSpec((D_MODEL, IN1_COL), lambda j, i: (0, QK_BLOCKS + j))],
        out_specs=out_spec,
        out_shape=jax.ShapeDtypeStruct((n, vg_blocks * IN1_COL), BF16),
        compiler_params=_params("parallel", "parallel"),
        name="inproj1_vg",
    )(h, w_bf16)
    return qk, vg


def _retention_kernel(*refs, n_chunks, has_state, heads, unroll):
    if has_state:
        decay_ref, q_ref, k_ref, v_ref, gf_ref, gb_ref, sf0_ref, sb0_ref, o_ref, inc_ref, seen_ref = refs
    else:
        decay_ref, q_ref, k_ref, v_ref, gf_ref, gb_ref, o_ref, sf_ref, sb_ref, inc_ref, seen_ref = refs
    c = RET_CHUNK
    row = lax.broadcasted_iota(jnp.int32, (c, c), 0).astype(F32)
    col = lax.broadcasted_iota(jnp.int32, (c, c), 1).astype(F32)
    pos = lax.broadcasted_iota(jnp.int32, (c, 1), 0).astype(F32)

    def chunk(i):
        return pl.ds(pl.multiple_of(i * c, c), c)

    diff = [row - col, col - row]
    qk_cols = [slice(g * RET_DK, (g + 1) * RET_DK) for g in range(heads)]
    v_cols = [slice(g * RET_DV, (g + 1) * RET_DV) for g in range(heads)]
    intra, q_dec, k_dec, chunk_dec = [], [], [], []
    for g in range(heads):
        hd = pl.program_id(1) * heads + g
        lg = [-jnp.exp(jnp.full((1, 1), decay_ref[d, hd], F32)) for d in range(2)]
        intra.append([jnp.where(diff[d] >= 0, jnp.exp(lg[d] * jnp.maximum(diff[d], 0.0)), 0.0) for d in range(2)])
        q_dec.append([jnp.exp(lg[0] * (pos + 1.0)), jnp.exp(lg[1] * (c - pos))])
        k_dec.append([jnp.exp(lg[0] * ((c - 1.0) - pos)), jnp.exp(lg[1] * pos)])
        chunk_dec.append([jnp.exp(lg[d] * float(c)) for d in range(2)])

    def increments(i, carry):
        rows = chunk(i)
        for g in range(heads):
            ki = k_ref[rows, qk_cols[g]].astype(F32)
            vi = v_ref[rows, v_cols[g]]
            for d in range(2):
                inc_ref[g, d, i] = _dot_tn((ki * k_dec[g][d]).astype(BF16), vi)
        return carry

    lax.fori_loop(0, n_chunks, increments, 0, unroll=unroll)

    for g in range(heads):
        finals = []
        for d in range(2):
            def scan(t, state):
                i = t if d == 0 else n_chunks - 1 - t
                seen_ref[g, d, i] = state.astype(BF16)
                return state * chunk_dec[g][d] + inc_ref[g, d, i]

            if has_state:
                state0 = (sf0_ref if d == 0 else sb0_ref)[0, g]
            else:
                state0 = jnp.zeros((RET_DK, RET_DV), F32)
            finals.append(lax.fori_loop(0, n_chunks, scan, state0))
        if not has_state:
            sf_ref[0, g] = finals[0]
            sb_ref[0, g] = finals[1]

    def outputs(i, carry):
        rows = chunk(i)
        for g in range(heads):
            qi = q_ref[rows, qk_cols[g]]
            vi = v_ref[rows, v_cols[g]]
            s = _dot_nt(qi, k_ref[rows, qk_cols[g]])
            mixed = None
            for d, g_ref in enumerate((gf_ref, gb_ref)):
                y = _dot((s * intra[g][d]).astype(BF16), vi) + _dot(qi, seen_ref[g, d, i]) * q_dec[g][d]
                gated = g_ref[rows, v_cols[g]].astype(F32) * _rms(y)
                mixed = gated if mixed is None else mixed + gated
            o_ref[rows, v_cols[g]] = mixed.astype(o_ref.dtype)
        return carry

    lax.fori_loop(0, n_chunks, outputs, 0, unroll=unroll)


def _retention(stream, qk, vg, decays, states, heads):
    n = stream.n_tok
    t = stream.seq_len
    n_chunks = t // RET_CHUNK
    has_state = states is not None
    qk_w = heads * RET_DK
    v_w = heads * RET_DV
    k_blk = RET_QK_WIDTH // qk_w
    gf_blk = RET_V_WIDTH // v_w
    gb_blk = 2 * gf_blk
    in_specs = [
        pl.BlockSpec(memory_space=pltpu.SMEM),
        pl.BlockSpec((t, qk_w), lambda b, h: (b, h)),
        pl.BlockSpec((t, qk_w), lambda b, h: (b, k_blk + h)),
        pl.BlockSpec((t, v_w), lambda b, h: (b, h)),
        pl.BlockSpec((t, v_w), lambda b, h: (b, gf_blk + h)),
        pl.BlockSpec((t, v_w), lambda b, h: (b, gb_blk + h)),
    ]
    args = [decays, qk, qk, vg, vg, vg]
    state_spec = pl.BlockSpec((1, heads, RET_DK, RET_DV), lambda b, h: (b, h, 0, 0))
    out_specs = [pl.BlockSpec((t, v_w), lambda b, h: (b, h))]
    out_shape = [jax.ShapeDtypeStruct((n, RET_V_WIDTH), BF16)]
    if has_state:
        in_specs += [state_spec, state_spec]
        args += list(states)
    else:
        out_specs += [state_spec, state_spec]
        out_shape += [jax.ShapeDtypeStruct((stream.n_seq, RET_HEADS, RET_DK, RET_DV), F32)] * 2
    return pl.pallas_call(
        functools.partial(_retention_kernel, n_chunks=n_chunks, has_state=has_state, heads=heads,
                          unroll=min(n_chunks, 2)),
        grid=(stream.n_seq, RET_HEADS // heads),
        in_specs=in_specs,
        out_specs=out_specs,
        out_shape=out_shape,
        scratch_shapes=[pltpu.VMEM((heads, 2, n_chunks, RET_DK, RET_DV), F32),
                        pltpu.VMEM((heads, 2, n_chunks, RET_DK, RET_DV), BF16)],
        compiler_params=_params("parallel", "parallel"),
        name="retention",
    )(*args)


def _rope_tables(n_tokens, dim):
    t = np.arange(n_tokens)
    pos = np.stack([t // GRID_W, t % GRID_W]).astype(np.float64)
    n_freq = dim // 4
    freqs = ROPE_THETA ** (-np.arange(n_freq, dtype=np.float64) / n_freq)
    ang = pos[:, :, None] * freqs
    cos, sin = np.cos(ang), np.sin(ang)
    cos_t = np.concatenate([cos[0], cos[0], cos[1], cos[1]], axis=-1)
    sin_t = np.concatenate([-sin[0], sin[0], -sin[1], sin[1]], axis=-1)
    return jnp.asarray(cos_t, F32), jnp.asarray(sin_t, F32)


def kernel(x_prompt, x_sample, cache_k0, cache_v0, state_ret_fwd1, state_ret_bwd1, c, c_ctx,
           norm_mix_g0, mod_w0, mod_b0, in_w0, conv_w0, conv_b0, conv_norm_g0, conv_norm_b0,
           q_norm_g0, k_norm_g0, out_w0, norm_ffn_g0,
           moe_grp_w0, moe_grp_b0, moe_rtr_w0, moe_rtr_b0, moe_w_gate0, moe_w_up0, moe_w_down0,
           norm_mix_g1, mod_w1, mod_b1, in_w1, ret_decay_fwd1, ret_decay_bwd1, out_w1, norm_ffn_g1,
           moe_grp_w1, moe_grp_b1, moe_rtr_w1, moe_rtr_b1, moe_w_gate1, moe_w_up1, moe_w_down1,
           final_norm_g):
    batch, seq, d = x_prompt.shape
    dec_batch, dec_seq, _ = x_sample.shape
    assert d == D_MODEL and 1 + dec_batch <= MOD_SLOTS
    ctx = _Stream(batch, seq, first_slot=0, rope=False)
    lat = _Stream(dec_batch, dec_seq, first_slot=1, rope=True)
    assert ctx.n_tok % ROW_TILE == 0 and lat.n_tok % ROW_TILE == 0 and ROW_CHUNKS == SUBLANES
    assert seq % CONV_TILE == 0 and dec_seq % ROW_TILE == 0 and ROW_TILE % seq == 0

    cond = jnp.concatenate([c_ctx[None, :], c, jnp.zeros((MOD_SLOTS - 1 - dec_batch, d), F32)], axis=0)
    mod0 = _modulation(cond, mod_w0, mod_b0)
    mod1 = _modulation(cond, mod_w1, mod_b1)
    cos, sin = _rope_tables(dec_seq, HEAD_DIM)
    decays = jnp.stack([ret_decay_fwd1, ret_decay_bwd1]).astype(F32)

    in_w0_b = in_w0.astype(BF16)
    def partner_order(a):
        quarter = HEAD_DIM // 4
        blocks = a.reshape(a.shape[:-1] + (a.shape[-1] // (2 * quarter), 2, quarter))
        return blocks[..., ::-1, :].reshape(a.shape)

    qk0 = 2 * CONV_CH
    in_w0_partner = partner_order(in_w0_b[:, qk0:qk0 + ATTN_WIDTH + KV_WIDTH])
    q_gains = jnp.stack([q_norm_g0, partner_order(q_norm_g0)])
    k_gains = jnp.stack([k_norm_g0, partner_order(k_norm_g0)])
    out_w0_b = out_w0.astype(BF16)
    in_w1_b = in_w1.astype(BF16)
    out_w1_b = out_w1.astype(BF16)
    moe0 = (moe_w_gate0.astype(BF16), moe_w_up0.astype(BF16), moe_w_down0.astype(BF16))
    moe1 = (moe_w_gate1.astype(BF16), moe_w_up1.astype(BF16), moe_w_down1.astype(BF16))
    route0 = _router_params(moe_grp_w0, moe_grp_b0, moe_rtr_w0, moe_rtr_b0)
    route1 = _router_params(moe_grp_w1, moe_grp_b1, moe_rtr_w1, moe_rtr_b1)

    def run(stream, x, cache, states):
        x = x.reshape(stream.n_tok, d)
        kv_dtype = BF16 if stream.rope else F32
        a, q, k, v = _inproj0(stream, x, mod0, norm_mix_g0, in_w0_b, in_w0_partner, q_gains, k_gains, cos, sin,
                              kv_dtype)
        a = _conv_branch(stream, a, conv_w0, conv_b0, conv_norm_g0, conv_norm_b0)
        tq = min(stream.seq_len, 256)
        o = _attention(stream, q, k, v, cache, tq, kv_heads=1 if cache is not None else N_KV_HEADS)
        x, *routed = _outproj_route(stream, [a, o], out_w0_b, x, mod0, norm_ffn_g0, *route0)
        x, h = _moe(stream, *routed, *moe0, x, mod0, norm_mix_g1, mod1, last_layer=False)
        qk, vg = _inproj1(stream, h, in_w1_b, cos, sin)
        ret = _retention(stream, qk, vg, decays, states, heads=2 if states is not None else RET_HEADS)
        x, *routed = _outproj_route(stream, [ret[0]], out_w1_b, x, mod1, norm_ffn_g1, *route1)
        y, = _moe(stream, *routed, *moe1, x, mod1, final_norm_g, mod1, last_layer=True)
        return y.reshape(stream.n_seq, stream.seq_len, d), k, v, ret[1:]

    y_prompt, k_ctx, v_ctx, new_states = run(ctx, x_prompt, None, None)
    y_sample, _, _, _ = run(lat, x_sample, (cache_k0, cache_v0), (state_ret_fwd1, state_ret_bwd1))
    return (y_prompt, y_sample, k_ctx, v_ctx, new_states[0], new_states[1])
```

```python
import functools

import jax
import jax.numpy as jnp
import numpy as np
from jax import lax
from jax.experimental import pallas as pl
from jax.experimental.pallas import tpu as pltpu

D_MODEL = 1024
GRID_W = 64
EPS = 1e-6
CONV_CH = 512
CONV_WIDTH = 31
CONV_HALF = CONV_WIDTH // 2
N_Q_HEADS = 8
N_KV_HEADS = 2
Q_PER_KV = N_Q_HEADS // N_KV_HEADS
HEAD_DIM = 128
ROPE_THETA = 10000.0
ATTN_WIDTH = N_Q_HEADS * HEAD_DIM
KV_WIDTH = N_KV_HEADS * HEAD_DIM
IN0_WIDTH = 2 * CONV_CH + ATTN_WIDTH + 2 * KV_WIDTH
RET_HEADS = 8
RET_DK = 128
RET_DV = 256
RET_CHUNK = 128
RET_QK_WIDTH = RET_HEADS * RET_DK
RET_V_WIDTH = RET_HEADS * RET_DV
IN1_WIDTH = 2 * RET_QK_WIDTH + 3 * RET_V_WIDTH
N_GROUPS = 4
EXPERTS_PER_GROUP = 4
N_EXPERTS = N_GROUPS * EXPERTS_PER_GROUP
EXPERT_FF = 512

LANES = 128
SUBLANES = 8
VMEM_LIMIT_BYTES = 56 * 1024 * 1024

MOD_SLOTS = 8
ROW_TILE = 512
ROW_CHUNKS = D_MODEL // LANES
PAYLOAD_CHUNKS = 2 * ROW_CHUNKS
FFN_TILE = 256
COMBINE_TILE = 256
ROUTE_GROUP, ROUTE_RANK = 0, 1
CONV_TILE = 256
CONV_HALO = 16
CONV_ROWS = 64
ROUTE_LANES = LANES
EXPERT_LANE0 = N_GROUPS
NEG_BIG = -1e30
LOG2_E = 1.4426950408889634
ATTN_KV_CHUNK = 512

F32 = jnp.float32
BF16 = jnp.bfloat16


def _params(*semantics):
    return pltpu.CompilerParams(dimension_semantics=semantics, vmem_limit_bytes=VMEM_LIMIT_BYTES)


def _sigmoid(x):
    return 1.0 / (1.0 + jnp.exp(-x))


def _silu(x):
    return x * _sigmoid(x)


def _rms(x):
    return x * lax.rsqrt(jnp.mean(x * x, axis=-1, keepdims=True) + EPS)


def _rope(x, cos, sin):
    lane = lax.broadcasted_iota(jnp.int32, x.shape, 1)
    take_upper = (lane % (HEAD_DIM // 2)) < (HEAD_DIM // 4)
    partner = jnp.where(take_upper, pltpu.roll(x, HEAD_DIM - HEAD_DIM // 4, 1), pltpu.roll(x, HEAD_DIM // 4, 1))
    return x * cos + partner * sin


def _dot(a, b):
    return jnp.dot(a, b, preferred_element_type=F32)


def _dot_nt(a, b):
    return lax.dot_general(a, b, (((1,), (1,)), ((), ())), preferred_element_type=F32)


def _dot_tn(a, b):
    return lax.dot_general(a, b, (((0,), (0,)), ((), ())), preferred_element_type=F32)


def _mod_kernel(c_ref, w_ref, b_ref, o_ref):
    c = c_ref[...]
    o_ref[...] = _dot(_silu(c).astype(BF16), w_ref[...].astype(BF16)) + b_ref[...]


def _modulation(cond, mod_w, mod_b):
    n_out = mod_w.shape[1]
    col = D_MODEL
    out = pl.pallas_call(
        _mod_kernel,
        grid=(n_out // col,),
        in_specs=[
            pl.BlockSpec((MOD_SLOTS, D_MODEL), lambda j: (0, 0)),
            pl.BlockSpec((D_MODEL, col), lambda j: (0, j)),
            pl.BlockSpec((1, col), lambda j: (0, j)),
        ],
        out_specs=pl.BlockSpec((MOD_SLOTS, col), lambda j: (0, j)),
        out_shape=jax.ShapeDtypeStruct((MOD_SLOTS, n_out), F32),
        compiler_params=_params("parallel"),
        name="modulation",
    )(cond, mod_w, mod_b.reshape(1, n_out))
    return out.reshape(MOD_SLOTS, 6, D_MODEL)


def _slot_map(stream):
    if stream.first_slot == 0:
        return lambda i: 0
    return lambda i: stream.first_slot + (i * ROW_TILE) // stream.seq_len


class _Stream:
    def __init__(self, n_seq, seq_len, first_slot, rope):
        self.n_seq = n_seq
        self.seq_len = seq_len
        self.first_slot = first_slot
        self.rope = rope
        self.n_tok = n_seq * seq_len


def _modulated_norm(x, g, shift, scale):
    return _rms(x) * g * (1.0 + scale) + shift


def _inproj0_kernel(x_ref, mod_ref, g_ref, w_ref, wp_ref, qg_ref, kg_ref, cos_ref, sin_ref,
                    a_ref, q_ref, k_ref, v_ref, *, rope):
    h = _modulated_norm(x_ref[...], g_ref[...], mod_ref[0, 0:1, :], mod_ref[0, 1:2, :]).astype(BF16)
    p = _dot(h, w_ref[...])
    if rope:
        p_partner = _dot(h, wp_ref[...])
    a_ref[...] = p[:, :CONV_CH] * _sigmoid(p[:, CONV_CH:2 * CONV_CH])
    q0 = 2 * CONV_CH
    k0 = q0 + ATTN_WIDTH
    v0 = k0 + KV_WIDTH
    cos = cos_ref[...]
    sin = sin_ref[...]
    q_scale = HEAD_DIM ** -0.5 * LOG2_E

    def normed_head(col, partner_col, gain_ref):
        ph = p[:, col:col + HEAD_DIM]
        inv = lax.rsqrt(jnp.mean(ph * ph, axis=-1, keepdims=True) + EPS)
        out = ph * inv * gain_ref[0:1, :]
        if rope:
            out = out * cos + (p_partner[:, partner_col:partner_col + HEAD_DIM] * inv * gain_ref[1:2, :]) * sin
        return out

    for hh in range(N_Q_HEADS):
        qh = normed_head(q0 + hh * HEAD_DIM, hh * HEAD_DIM, qg_ref)
        q_ref[:, hh * HEAD_DIM:(hh + 1) * HEAD_DIM] = (qh * q_scale).astype(q_ref.dtype)
    seqs, _, rows, _ = k_ref.shape
    for hh in range(N_KV_HEADS):
        kh = normed_head(k0 + hh * HEAD_DIM, ATTN_WIDTH + hh * HEAD_DIM, kg_ref)
        vh = p[:, v0 + hh * HEAD_DIM:v0 + (hh + 1) * HEAD_DIM]
        for sq in range(seqs):
            k_ref[sq, hh] = kh[sq * rows:(sq + 1) * rows].astype(k_ref.dtype)
            v_ref[sq, hh] = vh[sq * rows:(sq + 1) * rows].astype(v_ref.dtype)


def _inproj0(stream, x, mod, norm_g, w_bf16, w_partner, q_gains, k_gains, cos, sin, kv_dtype):
    n = stream.n_tok
    tiles_per_seq = max(stream.seq_len // ROW_TILE, 1)
    rope_map = (lambda i: (i % tiles_per_seq, 0)) if stream.rope else (lambda i: (0, 0))
    slot = _slot_map(stream)
    row = lambda i: (i, 0)
    fixed = lambda i: (0, 0)
    seqs_per_tile = max(ROW_TILE // stream.seq_len, 1)
    kv_spec = pl.BlockSpec((seqs_per_tile, N_KV_HEADS, ROW_TILE // seqs_per_tile, HEAD_DIM),
                           lambda i: (i // tiles_per_seq, 0, i % tiles_per_seq, 0))
    kv_shape = jax.ShapeDtypeStruct((stream.n_seq, N_KV_HEADS, stream.seq_len, HEAD_DIM), kv_dtype)
    return pl.pallas_call(
        functools.partial(_inproj0_kernel, rope=stream.rope),
        grid=(n // ROW_TILE,),
        in_specs=[
            pl.BlockSpec((ROW_TILE, D_MODEL), row),
            pl.BlockSpec((1, 6, D_MODEL), lambda i: (slot(i), 0, 0)),
            pl.BlockSpec((1, D_MODEL), fixed),
            pl.BlockSpec((D_MODEL, IN0_WIDTH), fixed),
            pl.BlockSpec((D_MODEL, ATTN_WIDTH + KV_WIDTH), fixed),
            pl.BlockSpec((2, HEAD_DIM), fixed),
            pl.BlockSpec((2, HEAD_DIM), fixed),
            pl.BlockSpec((ROW_TILE, HEAD_DIM), rope_map),
            pl.BlockSpec((ROW_TILE, HEAD_DIM), rope_map),
        ],
        out_specs=[
            pl.BlockSpec((ROW_TILE, CONV_CH), row),
            pl.BlockSpec((ROW_TILE, ATTN_WIDTH), row),
            kv_spec,
            kv_spec,
        ],
        out_shape=[
            jax.ShapeDtypeStruct((n, CONV_CH), F32),
            jax.ShapeDtypeStruct((n, ATTN_WIDTH), BF16),
            kv_shape,
            kv_shape,
        ],
        compiler_params=_params("parallel"),
        name="inproj0",
    )(x, mod, norm_g.reshape(1, D_MODEL), w_bf16, w_partner, q_gains, k_gains, cos, sin)


def _conv_kernel(prev_ref, main_ref, next_ref, w_ref, b_ref, g_ref, beta_ref, o_ref, pad_ref, acc_ref, *,
                 tiles_per_seq):
    i = pl.program_id(0)
    has_prev = (i % tiles_per_seq) != 0
    has_next = (i % tiles_per_seq) != (tiles_per_seq - 1)
    pad_ref[0:CONV_HALO, :] = jnp.where(has_prev, prev_ref[...], 0.0)
    pad_ref[CONV_HALO:CONV_HALO + CONV_TILE, :] = main_ref[...]
    pad_ref[CONV_HALO + CONV_TILE:, :] = jnp.where(has_next, next_ref[...], 0.0)
    first = CONV_HALO - CONV_HALF
    window = CONV_ROWS + SUBLANES
    for c0 in range(0, CONV_CH, LANES):
        for r0 in range(0, CONV_TILE, CONV_ROWS):
            acc = jnp.zeros((CONV_ROWS, LANES), F32)
            for phase in range(SUBLANES):
                partial = jnp.zeros((window, LANES), F32)
                for j in range(phase, CONV_WIDTH, SUBLANES):
                    base = r0 + j - phase
                    partial = partial + w_ref[j:j + 1, c0:c0 + LANES] * pad_ref[base:base + window, c0:c0 + LANES]
                acc = acc + partial[first + phase:first + phase + CONV_ROWS]
            acc_ref[r0:r0 + CONV_ROWS, c0:c0 + LANES] = acc
    a = acc_ref[...] + b_ref[...]
    mu = jnp.mean(a, axis=-1, keepdims=True)
    d = a - mu
    var = jnp.mean(d * d, axis=-1, keepdims=True)
    y = d * lax.rsqrt(var + EPS) * g_ref[...] + beta_ref[...]
    o_ref[...] = _silu(y).astype(o_ref.dtype)


def _conv_branch(stream, a, conv_w, conv_b, norm_g, norm_b):
    n = stream.n_tok
    tiles_per_seq = stream.seq_len // CONV_TILE
    halo_per_tile = CONV_TILE // CONV_HALO
    n_halo = n // CONV_HALO
    fixed = lambda i: (0, 0)
    return pl.pallas_call(
        functools.partial(_conv_kernel, tiles_per_seq=tiles_per_seq),
        grid=(n // CONV_TILE,),
        in_specs=[
            pl.BlockSpec((CONV_HALO, CONV_CH), lambda i: (jnp.maximum(i * halo_per_tile - 1, 0), 0)),
            pl.BlockSpec((CONV_TILE, CONV_CH), lambda i: (i, 0)),
            pl.BlockSpec((CONV_HALO, CONV_CH), lambda i: (jnp.minimum((i + 1) * halo_per_tile, n_halo - 1), 0)),
            pl.BlockSpec((CONV_WIDTH, CONV_CH), fixed),
            pl.BlockSpec((1, CONV_CH), fixed),
            pl.BlockSpec((1, CONV_CH), fixed),
            pl.BlockSpec((1, CONV_CH), fixed),
        ],
        out_specs=pl.BlockSpec((CONV_TILE, CONV_CH), lambda i: (i, 0)),
        out_shape=jax.ShapeDtypeStruct((n, CONV_CH), BF16),
        scratch_shapes=[
            pltpu.VMEM((CONV_TILE + 2 * CONV_HALO, CONV_CH), F32),
            pltpu.VMEM((CONV_TILE, CONV_CH), F32),
        ],
        compiler_params=_params("parallel"),
        name="conv_branch",
    )(a, a, a, conv_w, conv_b.reshape(1, CONV_CH), norm_g.reshape(1, CONV_CH), norm_b.reshape(1, CONV_CH))


def _attn_kernel(*refs, tq, past):
    if past:
        q_ref, k_ref, v_ref, kc_ref, vc_ref, o_ref, ks_ref, vs_ref = refs
    else:
        q_ref, k_ref, v_ref, o_ref, ks_ref, vs_ref = refs
    kv_heads, n_keys, _ = ks_ref.shape
    group_w = Q_PER_KV * HEAD_DIM

    @pl.when(pl.program_id(2) == 0)
    def _():
        lane = lax.broadcasted_iota(jnp.int32, (n_keys, HEAD_DIM), 1)
        for hk in range(kv_heads):
            vs_ref[hk, :, HEAD_DIM:] = jnp.where(lane == 0, 1.0, 0.0).astype(BF16)
            if past:
                ks_ref[hk, 0:past, :] = kc_ref[0, hk].astype(BF16)
                vs_ref[hk, 0:past, 0:HEAD_DIM] = vc_ref[0, hk].astype(BF16)
            ks_ref[hk, past:, :] = k_ref[0, hk].astype(BF16)
            vs_ref[hk, past:, 0:HEAD_DIM] = v_ref[0, hk].astype(BF16)

    for hk in range(kv_heads):
        q = q_ref[:, hk * group_w:(hk + 1) * group_w]
        q4 = jnp.concatenate([q[:, g * HEAD_DIM:(g + 1) * HEAD_DIM] for g in range(Q_PER_KV)], axis=0)
        rows = q4.shape[0]
        m = jnp.full((rows, 1), NEG_BIG, F32)
        acc = jnp.zeros((rows, 2 * HEAD_DIM), F32)
        for c0 in range(0, n_keys, ATTN_KV_CHUNK):
            c1 = min(c0 + ATTN_KV_CHUNK, n_keys)
            s = _dot_nt(q4, ks_ref[hk, c0:c1, :])
            m_new = jnp.maximum(m, jnp.max(s, axis=-1, keepdims=True))
            p = jnp.exp2(s - m_new).astype(BF16)
            acc = jnp.exp2(m - m_new) * acc + _dot(p, vs_ref[hk, c0:c1, :])
            m = m_new
        o = acc[:, 0:HEAD_DIM] * (1.0 / acc[:, HEAD_DIM:HEAD_DIM + 1])
        for g in range(Q_PER_KV):
            col = hk * group_w + g * HEAD_DIM
            o_ref[:, col:col + HEAD_DIM] = o[g * tq:(g + 1) * tq].astype(o_ref.dtype)


def _attention(stream, q, k, v, cache, tq, kv_heads):
    n = stream.n_tok
    t = stream.seq_len
    q_tiles = t // tq
    width = kv_heads * Q_PER_KV * HEAD_DIM
    in_specs = [
        pl.BlockSpec((tq, width), lambda b, h, i: (b * q_tiles + i, h)),
        pl.BlockSpec((1, kv_heads, t, HEAD_DIM), lambda b, h, i: (b, h, 0, 0)),
        pl.BlockSpec((1, kv_heads, t, HEAD_DIM), lambda b, h, i: (b, h, 0, 0)),
    ]
    args = [q, k, v]
    past = 0
    if cache is not None:
        past = cache[0].shape[2]
        in_specs += [pl.BlockSpec((1, kv_heads, past, HEAD_DIM), lambda b, h, i: (b, h, 0, 0))] * 2
        args += list(cache)
    return pl.pallas_call(
        functools.partial(_attn_kernel, tq=tq, past=past),
        grid=(stream.n_seq, N_KV_HEADS // kv_heads, q_tiles),
        in_specs=in_specs,
        out_specs=pl.BlockSpec((tq, width), lambda b, h, i: (b * q_tiles + i, h)),
        out_shape=jax.ShapeDtypeStruct((n, ATTN_WIDTH), BF16),
        scratch_shapes=[pltpu.VMEM((kv_heads, past + t, HEAD_DIM), BF16),
                        pltpu.VMEM((kv_heads, past + t, 2 * HEAD_DIM), BF16)],
        compiler_params=_params("parallel", "parallel", "arbitrary"),
        name="attention",
    )(*args)


def _route(logits, running):
    rows = logits.shape[0]
    lane = lax.broadcasted_iota(jnp.int32, logits.shape, 1)
    lane_f = lane.astype(F32)
    far = float(ROUTE_LANES)
    is_group = lane < N_GROUPS
    gl = jnp.where(is_group, logits, NEG_BIG)
    gmax = jnp.max(gl, axis=-1, keepdims=True)
    gsum = jnp.sum(jnp.where(is_group, jnp.exp(gl - gmax), 0.0), axis=-1, keepdims=True)
    g_w = 1.0 / gsum
    gidx = jnp.min(jnp.where(gl == gmax, lane_f, far), axis=-1, keepdims=True)
    lo = EXPERT_LANE0 + EXPERTS_PER_GROUP * gidx
    in_group = (lane_f >= lo) & (lane_f < lo + EXPERTS_PER_GROUP)
    el = jnp.where(in_group, logits, NEG_BIG)
    v1 = jnp.max(el, axis=-1, keepdims=True)
    i1 = jnp.min(jnp.where(el == v1, lane_f, far), axis=-1, keepdims=True)
    el2 = jnp.where(lane_f == i1, NEG_BIG, el)
    v2 = jnp.max(el2, axis=-1, keepdims=True)
    i2 = jnp.min(jnp.where(el2 == v2, lane_f, far), axis=-1, keepdims=True)
    e2 = jnp.exp(v2 - v1)
    w1 = g_w / (1.0 + e2)
    w2 = w1 * e2
    in_my_group = lane_f == gidx
    chosen = in_my_group.astype(BF16)
    earlier = (lax.broadcasted_iota(jnp.int32, (rows, rows), 0)
               > lax.broadcasted_iota(jnp.int32, (rows, rows), 1)).astype(BF16)
    before = running + _dot(earlier, chosen)
    rank = jnp.sum(jnp.where(in_my_group, before, 0.0), axis=-1, keepdims=True)
    record = jnp.where(lane == ROUTE_GROUP, gidx, jnp.where(lane == ROUTE_RANK, rank, 0.0))
    weights = jnp.where(lane_f == i1 - lo, w1, 0.0) + jnp.where(lane_f == i2 - lo, w2, 0.0)
    return record, weights, running + jnp.sum(chosen.astype(F32), axis=0, keepdims=True)


def _outproj_kernel(*refs, widths):
    n_in = len(widths)
    in_refs = refs[:n_in]
    w_ref, x_ref, mod_ref, g_ref, rw_ref, rb_ref, x1_ref, pay_ref, route_t_ref, count_ref, run_ref = refs[n_in:]

    @pl.when(pl.program_id(0) == 0)
    def _():
        run_ref[...] = jnp.zeros_like(run_ref)

    acc = None
    off = 0
    for r, width in zip(in_refs, widths):
        part = _dot(r[...], w_ref[off:off + width, :])
        acc = part if acc is None else acc + part
        off += width
    x1 = x_ref[...] + mod_ref[0, 2:3, :] * acc
    x1_ref[...] = x1
    h = _modulated_norm(x1, g_ref[...], mod_ref[0, 3:4, :], mod_ref[0, 4:5, :])
    h_hi = h.astype(BF16)
    h_lo = (h - h_hi.astype(F32)).astype(BF16)
    both = _dot(h_hi, rw_ref[...])
    logits = (both[:, :ROUTE_LANES] + both[:, ROUTE_LANES:] + _dot(h_lo, rw_ref[:, :ROUTE_LANES]) + rb_ref[...])
    record, weights, running = _route(logits, run_ref[...])
    route_t_ref[...] = record.T[:SUBLANES]
    run_ref[...] = running
    count_ref[...] = jnp.broadcast_to(running, count_ref.shape)
    rows = h.shape[0]
    _store_row_major(pay_ref, h, PAYLOAD_CHUNKS)
    pay_ref[pl.ds(ROW_CHUNKS, rows, stride=PAYLOAD_CHUNKS), :] = weights
    for c in range(ROW_CHUNKS + 1, PAYLOAD_CHUNKS):
        pay_ref[pl.ds(c, rows, stride=PAYLOAD_CHUNKS), :] = jnp.zeros((rows, LANES), F32)


def _store_row_major(ref, x, pitch=ROW_CHUNKS):
    rows = x.shape[0]
    for c in range(ROW_CHUNKS):
        ref[pl.ds(c, rows, stride=pitch), :] = x[:, c * LANES:(c + 1) * LANES]


def _load_row_major(ref, rows, pitch=ROW_CHUNKS):
    return jnp.concatenate([ref[pl.ds(c, rows, stride=pitch), :] for c in range(ROW_CHUNKS)], axis=-1)


def _outproj_route(stream, inputs, w_bf16, x, mod, ffn_norm_g, route_w, route_b):
    n = stream.n_tok
    widths = tuple(a.shape[1] for a in inputs)
    slot = _slot_map(stream)
    row = lambda i: (i, 0)
    fixed = lambda i: (0, 0)
    return pl.pallas_call(
        functools.partial(_outproj_kernel, widths=widths),
        grid=(n // ROW_TILE,),
        in_specs=[pl.BlockSpec((ROW_TILE, width), row) for width in widths] + [
            pl.BlockSpec((sum(widths), D_MODEL), fixed),
            pl.BlockSpec((ROW_TILE, D_MODEL), row),
            pl.BlockSpec((1, 6, D_MODEL), lambda i: (slot(i), 0, 0)),
            pl.BlockSpec((1, D_MODEL), fixed),
            pl.BlockSpec((D_MODEL, 2 * ROUTE_LANES), fixed),
            pl.BlockSpec((1, ROUTE_LANES), fixed),
        ],
        out_specs=[
            pl.BlockSpec((ROW_TILE, D_MODEL), row),
            pl.BlockSpec((ROW_TILE * PAYLOAD_CHUNKS, LANES), row),
            pl.BlockSpec((SUBLANES, ROW_TILE), lambda i: (0, i)),
            pl.BlockSpec((SUBLANES, ROUTE_LANES), fixed),
        ],
        out_shape=[
            jax.ShapeDtypeStruct((n, D_MODEL), F32),
            jax.ShapeDtypeStruct((n * PAYLOAD_CHUNKS, LANES), F32),
            jax.ShapeDtypeStruct((SUBLANES, n), F32),
            jax.ShapeDtypeStruct((SUBLANES, ROUTE_LANES), F32),
        ],
        scratch_shapes=[pltpu.VMEM((1, ROUTE_LANES), F32)],
        compiler_params=_params("arbitrary"),
        name="outproj_route",
    )(*inputs, w_bf16, x, mod, ffn_norm_g.reshape(1, D_MODEL), route_w, route_b)


def _router_params(grp_w, grp_b, rtr_w, rtr_b):
    w = jnp.concatenate([grp_w, jnp.moveaxis(rtr_w, 0, 1).reshape(D_MODEL, N_EXPERTS)], axis=1)
    w = jnp.pad(w, ((0, 0), (0, ROUTE_LANES - w.shape[1])))
    hi = w.astype(BF16)
    lo = (w - hi.astype(F32)).astype(BF16)
    b = jnp.concatenate([grp_b, rtr_b.reshape(N_EXPERTS)])
    b = jnp.pad(b, (0, ROUTE_LANES - b.shape[0])).reshape(1, ROUTE_LANES)
    return jnp.concatenate([hi, lo], axis=1), b


def _sorted_tiles(n_tok):
    return n_tok // FFN_TILE + N_GROUPS


def _dispatch_plan(route_t, counts, n_tok):
    n_tiles = _sorted_tiles(n_tok)
    count = counts[0, :N_GROUPS].astype(jnp.int32)
    tiles = (count + (FFN_TILE - 1)) // FFN_TILE
    start = (jnp.cumsum(tiles) - tiles) * FFN_TILE
    tile_end = jnp.cumsum(tiles)
    j = jnp.arange(n_tiles, dtype=jnp.int32)
    tile_group = jnp.minimum(jnp.sum(j[:, None] >= tile_end[None, :], axis=1), N_GROUPS - 1).astype(jnp.int32)
    group_end = jnp.take(start + count, tile_group)
    valid = jnp.where(j < tile_end[-1], jnp.clip(group_end - j * FFN_TILE, 0, FFN_TILE), 0).astype(jnp.int32)
    ids = route_t[:2].astype(jnp.int32)
    where = jnp.take(start, ids[ROUTE_GROUP]) + ids[ROUTE_RANK]
    return where, tile_group, valid


def _token_rows(t, pitch=ROW_CHUNKS):
    return pl.ds(pl.multiple_of(t * pitch, pitch), pitch)


def _ffn_kernel(tg_ref, valid_ref, where_ref, pay_ref, wg_ref, wu_ref, wd_ref, ys_ref, x_buf, src_ref, sems, *, n_tok):
    j = pl.program_id(0)
    n_tiles = pl.num_programs(0)

    @pl.when(j == 0)
    def _():
        def place(t, carry):
            src_ref[where_ref[t]] = t
            return carry

        lax.fori_loop(0, n_tok, place, 0, unroll=8)

    def gather_copy(tile, r):
        tok = src_ref[tile * FFN_TILE + r]
        return pltpu.make_async_copy(pay_ref.at[_token_rows(tok, PAYLOAD_CHUNKS)],
                                     x_buf.at[tile % 2, _token_rows(r, PAYLOAD_CHUNKS)], sems.at[tile % 2])

    def start_gather(tile):
        def issue_pair(r2, carry):
            for k in range(2):
                gather_copy(tile, 2 * r2 + k).start(priority=k)
            return carry

        def issue(r, carry):
            gather_copy(tile, r).start()
            return carry

        @pl.when(valid_ref[tile] == FFN_TILE)
        def _():
            lax.fori_loop(0, FFN_TILE // 2, issue_pair, 0, unroll=4)

        @pl.when(valid_ref[tile] < FFN_TILE)
        def _():
            lax.fori_loop(0, valid_ref[tile], issue, 0)

    @pl.when(j == 0)
    def _():
        x_buf[...] = jnp.zeros_like(x_buf)
        start_gather(j)

    @pl.when(j + 1 < n_tiles)
    def _():
        @pl.when(valid_ref[j + 1] > 0)
        def _():
            start_gather(j + 1)

    @pl.when(valid_ref[j] == 0)
    def _():
        ys_ref[...] = jnp.zeros_like(ys_ref)

    @pl.when(valid_ref[j] > 0)
    def _():
        slot = j % 2
        n = pl.multiple_of(valid_ref[j] * PAYLOAD_CHUNKS, PAYLOAD_CHUNKS)
        pltpu.make_async_copy(pay_ref.at[pl.ds(0, n)], x_buf.at[slot, pl.ds(0, n)], sems.at[slot]).wait()
        x = _load_row_major(x_buf.at[slot], FFN_TILE, PAYLOAD_CHUNKS).astype(BF16)
        w = x_buf[slot, pl.ds(ROW_CHUNKS, FFN_TILE, stride=PAYLOAD_CHUNKS), :]
        y = None
        for e in range(EXPERTS_PER_GROUP):
            hidden = _silu(_dot(x, wg_ref[e])) * _dot(x, wu_ref[e]) * w[:, e:e + 1]
            part = _dot(hidden.astype(BF16), wd_ref[e])
            y = part if y is None else y + part
        _store_row_major(ys_ref, y)


def _expert_ffn(n_tok, tile_group, valid, where, payload, wg, wu, wd):
    group = lambda j, tg, va, sr: (tg[j], 0, 0)
    n_tiles = _sorted_tiles(n_tok)
    return pl.pallas_call(
        functools.partial(_ffn_kernel, n_tok=n_tok),
        grid_spec=pltpu.PrefetchScalarGridSpec(
            num_scalar_prefetch=3,
            grid=(n_tiles,),
            in_specs=[
                pl.BlockSpec(memory_space=pl.ANY),
                pl.BlockSpec((EXPERTS_PER_GROUP, D_MODEL, EXPERT_FF), group),
                pl.BlockSpec((EXPERTS_PER_GROUP, D_MODEL, EXPERT_FF), group),
                pl.BlockSpec((EXPERTS_PER_GROUP, EXPERT_FF, D_MODEL), group),
            ],
            out_specs=pl.BlockSpec((FFN_TILE * ROW_CHUNKS, LANES), lambda j, tg, va, sr: (j, 0)),
            scratch_shapes=[pltpu.VMEM((2, FFN_TILE * PAYLOAD_CHUNKS, LANES), F32),
                            pltpu.SMEM((n_tiles * FFN_TILE,), jnp.int32), pltpu.SemaphoreType.DMA((2,))],
        ),
        out_shape=jax.ShapeDtypeStruct((n_tiles * FFN_TILE * ROW_CHUNKS, LANES), F32),
        compiler_params=_params("arbitrary"),
        name="moe_ffn",
    )(tile_group, valid, where, payload, wg, wu, wd)


def _combine_kernel(where_ref, x_ref, mod_ref, ng_ref, nmod_ref, ys_ref, o_ref, *rest, last_layer):
    if last_layer:
        y_buf, sems = rest
    else:
        h_ref, y_buf, sems = rest
    i = pl.program_id(0)
    n_tiles = pl.num_programs(0)

    def row_copy(tile, t, k):
        slot = tile % 2
        d = where_ref[tile * COMBINE_TILE + t]
        return pltpu.make_async_copy(ys_ref.at[_token_rows(d)], y_buf.at[slot, _token_rows(t)], sems.at[slot, k])

    def start_tile(tile):
        def issue(t2, carry):
            for k in range(2):
                row_copy(tile, 2 * t2 + k, k).start(priority=k)
            return carry

        lax.fori_loop(0, COMBINE_TILE // 2, issue, 0, unroll=4)

    @pl.when(i == 0)
    def _():
        start_tile(i)

    @pl.when(i + 1 < n_tiles)
    def _():
        start_tile(i + 1)

    slot = i % 2
    half = COMBINE_TILE * ROW_CHUNKS // 2
    for k in range(2):
        pltpu.make_async_copy(ys_ref.at[pl.ds(0, half)], y_buf.at[slot, pl.ds(0, half)], sems.at[slot, k]).wait()
    x = x_ref[...] + mod_ref[0, 5:6, :] * _load_row_major(y_buf.at[slot], COMBINE_TILE)
    if last_layer:
        o_ref[...] = _rms(x) * ng_ref[...]
    else:
        o_ref[...] = x
        h_ref[...] = _modulated_norm(x, ng_ref[...], nmod_ref[0, 0:1, :], nmod_ref[0, 1:2, :]).astype(h_ref.dtype)


def _combine(stream, where, ys, x, mod, norm_g, next_mod, last_layer):
    n = stream.n_tok
    first_slot, seq_len = stream.first_slot, stream.seq_len
    slot = (lambda i: 0) if first_slot == 0 else (lambda i: first_slot + (i * COMBINE_TILE) // seq_len)
    row = lambda i, w: (i, 0)
    out_specs = [pl.BlockSpec((COMBINE_TILE, D_MODEL), row)]
    out_shape = [jax.ShapeDtypeStruct((n, D_MODEL), F32)]
    if not last_layer:
        out_specs.append(pl.BlockSpec((COMBINE_TILE, D_MODEL), row))
        out_shape.append(jax.ShapeDtypeStruct((n, D_MODEL), BF16))
    return pl.pallas_call(
        functools.partial(_combine_kernel, last_layer=last_layer),
        grid_spec=pltpu.PrefetchScalarGridSpec(
            num_scalar_prefetch=1,
            grid=(n // COMBINE_TILE,),
            in_specs=[
                pl.BlockSpec((COMBINE_TILE, D_MODEL), row),
                pl.BlockSpec((1, 6, D_MODEL), lambda i, w: (slot(i), 0, 0)),
                pl.BlockSpec((1, D_MODEL), lambda i, w: (0, 0)),
                pl.BlockSpec((1, 6, D_MODEL), lambda i, w: (slot(i), 0, 0)),
                pl.BlockSpec(memory_space=pl.ANY),
            ],
            out_specs=out_specs,
            scratch_shapes=[pltpu.VMEM((2, COMBINE_TILE * ROW_CHUNKS, LANES), F32),
                            pltpu.SemaphoreType.DMA((2, 2))],
        ),
        out_shape=out_shape,
        compiler_params=_params("arbitrary"),
        name="moe_combine",
    )(where, x, mod, norm_g.reshape(1, D_MODEL), next_mod, ys)


def _moe(stream, payload, route_t, counts, wg, wu, wd, x, mod, norm_g, next_mod, last_layer):
    n = stream.n_tok
    where, tile_group, valid = _dispatch_plan(route_t, counts, n)
    ys = _expert_ffn(n, tile_group, valid, where, payload, wg, wu, wd)
    return _combine(stream, where, ys, x, mod, norm_g, next_mod, last_layer)


IN1_COL = 1024
IN1_ROW_TILE = 1024
QK_BLOCKS = 2 * RET_QK_WIDTH // IN1_COL
Q_BLOCKS = RET_QK_WIDTH // IN1_COL
V_BLOCKS = RET_V_WIDTH // IN1_COL


def _inproj1_qk_kernel(h_ref, w_ref, cos_ref, sin_ref, o_ref, *, rope):
    p = _dot(h_ref[...], w_ref[...])
    scale = jnp.where(pl.program_id(0) < Q_BLOCKS, 1.0, RET_DK ** -0.5)
    cos = cos_ref[...]
    sin = sin_ref[...]
    for hh in range(IN1_COL // RET_DK):
        ph = p[:, hh * RET_DK:(hh + 1) * RET_DK]
        if rope:
            ph = _rope(ph, cos, sin)
        o_ref[:, hh * RET_DK:(hh + 1) * RET_DK] = (ph * scale).astype(o_ref.dtype)


def _inproj1_vg_kernel(h_ref, w_ref, o_ref):
    p = _dot(h_ref[...], w_ref[...])
    o_ref[...] = jnp.where(pl.program_id(0) < V_BLOCKS, p, _silu(p)).astype(o_ref.dtype)


def _inproj1(stream, h, w_bf16, cos, sin):
    n = stream.n_tok
    tm = IN1_ROW_TILE
    tiles_per_seq = max(stream.seq_len // tm, 1)
    rope_map = (lambda j, i: (i % tiles_per_seq, 0)) if stream.rope else (lambda j, i: (0, 0))
    h_spec = pl.BlockSpec((tm, D_MODEL), lambda j, i: (i, 0))
    out_spec = pl.BlockSpec((tm, IN1_COL), lambda j, i: (i, j))
    qk = pl.pallas_call(
        functools.partial(_inproj1_qk_kernel, rope=stream.rope),
        grid=(QK_BLOCKS, n // tm),
        in_specs=[
            h_spec,
            pl.BlockSpec((D_MODEL, IN1_COL), lambda j, i: (0, j)),
            pl.BlockSpec((tm, RET_DK), rope_map),
            pl.BlockSpec((tm, RET_DK), rope_map),
        ],
        out_specs=out_spec,
        out_shape=jax.ShapeDtypeStruct((n, QK_BLOCKS * IN1_COL), BF16),
        compiler_params=_params("parallel", "parallel"),
        name="inproj1_qk",
    )(h, w_bf16, cos, sin)
    vg_blocks = IN1_WIDTH // IN1_COL - QK_BLOCKS
    vg = pl.pallas_call(
        _inproj1_vg_kernel,
        grid=(vg_blocks, n // tm),
        in_specs=[h_spec, pl.BlockSpec((D_MODEL, IN1_COL), lambda j, i: (0, QK_BLOCKS + j))],
        out_specs=out_spec,
        out_shape=jax.ShapeDtypeStruct((n, vg_blocks * IN1_COL), BF16),
        compiler_params=_params("parallel", "parallel"),
        name="inproj1_vg",
    )(h, w_bf16)
    return qk, vg


def _retention_kernel(*refs, n_chunks, has_state, heads, unroll):
    if has_state:
        decay_ref, q_ref, k_ref, v_ref, gf_ref, gb_ref, sf0_ref, sb0_ref, o_ref, inc_ref, seen_ref = refs
    else:
        decay_ref, q_ref, k_ref, v_ref, gf_ref, gb_ref, o_ref, sf_ref, sb_ref, inc_ref, seen_ref = refs
    c = RET_CHUNK
    row = lax.broadcasted_iota(jnp.int32, (c, c), 0).astype(F32)
    col = lax.broadcasted_iota(jnp.int32, (c, c), 1).astype(F32)
    pos = lax.broadcasted_iota(jnp.int32, (c, 1), 0).astype(F32)

    def chunk(i):
        return pl.ds(pl.multiple_of(i * c, c), c)

    diff = [row - col, col - row]
    qk_cols = [slice(g * RET_DK, (g + 1) * RET_DK) for g in range(heads)]
    v_cols = [slice(g * RET_DV, (g + 1) * RET_DV) for g in range(heads)]
    intra, q_dec, k_dec, chunk_dec = [], [], [], []
    for g in range(heads):
        hd = pl.program_id(1) * heads + g
        lg = [-jnp.exp(jnp.full((1, 1), decay_ref[d, hd], F32)) for d in range(2)]
        intra.append([jnp.where(diff[d] >= 0, jnp.exp(lg[d] * jnp.maximum(diff[d], 0.0)), 0.0) for d in range(2)])
        q_dec.append([jnp.exp(lg[0] * (pos + 1.0)), jnp.exp(lg[1] * (c - pos))])
        k_dec.append([jnp.exp(lg[0] * ((c - 1.0) - pos)), jnp.exp(lg[1] * pos)])
        chunk_dec.append([jnp.exp(lg[d] * float(c)) for d in range(2)])

    def increments(i, carry):
        rows = chunk(i)
        for g in range(heads):
            ki = k_ref[rows, qk_cols[g]].astype(F32)
            vi = v_ref[rows, v_cols[g]]
            for d in range(2):
                inc_ref[g, d, i] = _dot_tn((ki * k_dec[g][d]).astype(BF16), vi)
        return carry

    lax.fori_loop(0, n_chunks, increments, 0, unroll=unroll)

    for g in range(heads):
        finals = []
        for d in range(2):
            def scan(t, state):
                i = t if d == 0 else n_chunks - 1 - t
                seen_ref[g, d, i] = state.astype(BF16)
                return state * chunk_dec[g][d] + inc_ref[g, d, i]

            if has_state:
                state0 = (sf0_ref if d == 0 else sb0_ref)[0, g]
            else:
                state0 = jnp.zeros((RET_DK, RET_DV), F32)
            finals.append(lax.fori_loop(0, n_chunks, scan, state0))
        if not has_state:
            sf_ref[0, g] = finals[0]
            sb_ref[0, g] = finals[1]

    def outputs(i, carry):
        rows = chunk(i)
        for g in range(heads):
            qi = q_ref[rows, qk_cols[g]]
            vi = v_ref[rows, v_cols[g]]
            s = _dot_nt(qi, k_ref[rows, qk_cols[g]])
            mixed = None
            for d, g_ref in enumerate((gf_ref, gb_ref)):
                y = _dot((s * intra[g][d]).astype(BF16), vi) + _dot(qi, seen_ref[g, d, i]) * q_dec[g][d]
                gated = g_ref[rows, v_cols[g]].astype(F32) * _rms(y)
                mixed = gated if mixed is None else mixed + gated
            o_ref[rows, v_cols[g]] = mixed.astype(o_ref.dtype)
        return carry

    lax.fori_loop(0, n_chunks, outputs, 0, unroll=unroll)


def _retention(stream, qk, vg, decays, states, heads):
    n = stream.n_tok
    t = stream.seq_len
    n_chunks = t // RET_CHUNK
    has_state = states is not None
    qk_w = heads * RET_DK
    v_w = heads * RET_DV
    k_blk = RET_QK_WIDTH // qk_w
    gf_blk = RET_V_WIDTH // v_w
    gb_blk = 2 * gf_blk
    in_specs = [
        pl.BlockSpec(memory_space=pltpu.SMEM),
        pl.BlockSpec((t, qk_w), lambda b, h: (b, h)),
        pl.BlockSpec((t, qk_w), lambda b, h: (b, k_blk + h)),
        pl.BlockSpec((t, v_w), lambda b, h: (b, h)),
        pl.BlockSpec((t, v_w), lambda b, h: (b, gf_blk + h)),
        pl.BlockSpec((t, v_w), lambda b, h: (b, gb_blk + h)),
    ]
    args = [decays, qk, qk, vg, vg, vg]
    state_spec = pl.BlockSpec((1, heads, RET_DK, RET_DV), lambda b, h: (b, h, 0, 0))
    out_specs = [pl.BlockSpec((t, v_w), lambda b, h: (b, h))]
    out_shape = [jax.ShapeDtypeStruct((n, RET_V_WIDTH), BF16)]
    if has_state:
        in_specs += [state_spec, state_spec]
        args += list(states)
    else:
        out_specs += [state_spec, state_spec]
        out_shape += [jax.ShapeDtypeStruct((stream.n_seq, RET_HEADS, RET_DK, RET_DV), F32)] * 2
    return pl.pallas_call(
        functools.partial(_retention_kernel, n_chunks=n_chunks, has_state=has_state, heads=heads,
                          unroll=min(n_chunks, 2)),
        grid=(stream.n_seq, RET_HEADS // heads),
        in_specs=in_specs,
        out_specs=out_specs,
        out_shape=out_shape,
        scratch_shapes=[pltpu.VMEM((heads, 2, n_chunks, RET_DK, RET_DV), F32),
                        pltpu.VMEM((heads, 2, n_chunks, RET_DK, RET_DV), BF16)],
        compiler_params=_params("parallel", "parallel"),
        name="retention",
    )(*args)


def _rope_tables(n_tokens, dim):
    t = np.arange(n_tokens)
    pos = np.stack([t // GRID_W, t % GRID_W]).astype(np.float64)
    n_freq = dim // 4
    freqs = ROPE_THETA ** (-np.arange(n_freq, dtype=np.float64) / n_freq)
    ang = pos[:, :, None] * freqs
    cos, sin = np.cos(ang), np.sin(ang)
    cos_t = np.concatenate([cos[0], cos[0], cos[1], cos[1]], axis=-1)
    sin_t = np.concatenate([-sin[0], sin[0], -sin[1], sin[1]], axis=-1)
    return jnp.asarray(cos_t, F32), jnp.asarray(sin_t, F32)


def kernel(x_prompt, x_sample, cache_k0, cache_v0, state_ret_fwd1, state_ret_bwd1, c, c_ctx,
           norm_mix_g0, mod_w0, mod_b0, in_w0, conv_w0, conv_b0, conv_norm_g0, conv_norm_b0,
           q_norm_g0, k_norm_g0, out_w0, norm_ffn_g0,
           moe_grp_w0, moe_grp_b0, moe_rtr_w0, moe_rtr_b0, moe_w_gate0, moe_w_up0, moe_w_down0,
           norm_mix_g1, mod_w1, mod_b1, in_w1, ret_decay_fwd1, ret_decay_bwd1, out_w1, norm_ffn_g1,
           moe_grp_w1, moe_grp_b1, moe_rtr_w1, moe_rtr_b1, moe_w_gate1, moe_w_up1, moe_w_down1,
           final_norm_g):
    batch, seq, d = x_prompt.shape
    dec_batch, dec_seq, _ = x_sample.shape
    assert d == D_MODEL and 1 + dec_batch <= MOD_SLOTS
    ctx = _Stream(batch, seq, first_slot=0, rope=False)
    lat = _Stream(dec_batch, dec_seq, first_slot=1, rope=True)
    assert ctx.n_tok % ROW_TILE == 0 and lat.n_tok % ROW_TILE == 0 and ROW_CHUNKS == SUBLANES
    assert seq % CONV_TILE == 0 and dec_seq % ROW_TILE == 0 and ROW_TILE % seq == 0

    cond = jnp.concatenate([c_ctx[None, :], c, jnp.zeros((MOD_SLOTS - 1 - dec_batch, d), F32)], axis=0)
    mod0 = _modulation(cond, mod_w0, mod_b0)
    mod1 = _modulation(cond, mod_w1, mod_b1)
    cos, sin = _rope_tables(dec_seq, HEAD_DIM)
    decays = jnp.stack([ret_decay_fwd1, ret_decay_bwd1]).astype(F32)

    in_w0_b = in_w0.astype(BF16)
    def partner_order(a):
        quarter = HEAD_DIM // 4
        blocks = a.reshape(a.shape[:-1] + (a.shape[-1] // (2 * quarter), 2, quarter))
        return blocks[..., ::-1, :].reshape(a.shape)

    qk0 = 2 * CONV_CH
    in_w0_partner = partner_order(in_w0_b[:, qk0:qk0 + ATTN_WIDTH + KV_WIDTH])
    q_gains = jnp.stack([q_norm_g0, partner_order(q_norm_g0)])
    k_gains = jnp.stack([k_norm_g0, partner_order(k_norm_g0)])
    out_w0_b = out_w0.astype(BF16)
    in_w1_b = in_w1.astype(BF16)
    out_w1_b = out_w1.astype(BF16)
    moe0 = (moe_w_gate0.astype(BF16), moe_w_up0.astype(BF16), moe_w_down0.astype(BF16))
    moe1 = (moe_w_gate1.astype(BF16), moe_w_up1.astype(BF16), moe_w_down1.astype(BF16))
    route0 = _router_params(moe_grp_w0, moe_grp_b0, moe_rtr_w0, moe_rtr_b0)
    route1 = _router_params(moe_grp_w1, moe_grp_b1, moe_rtr_w1, moe_rtr_b1)

    def run(stream, x, cache, states):
        x = x.reshape(stream.n_tok, d)
        kv_dtype = BF16 if stream.rope else F32
        a, q, k, v = _inproj0(stream, x, mod0, norm_mix_g0, in_w0_b, in_w0_partner, q_gains, k_gains, cos, sin,
                              kv_dtype)
        a = _conv_branch(stream, a, conv_w0, conv_b0, conv_norm_g0, conv_norm_b0)
        tq = min(stream.seq_len, 256)
        o = _attention(stream, q, k, v, cache, tq, kv_heads=1 if cache is not None else N_KV_HEADS)
        x, *routed = _outproj_route(stream, [a, o], out_w0_b, x, mod0, norm_ffn_g0, *route0)
        x, h = _moe(stream, *routed, *moe0, x, mod0, norm_mix_g1, mod1, last_layer=False)
        qk, vg = _inproj1(stream, h, in_w1_b, cos, sin)
        ret = _retention(stream, qk, vg, decays, states, heads=2 if states is not None else RET_HEADS)
        x, *routed = _outproj_route(stream, [ret[0]], out_w1_b, x, mod1, norm_ffn_g1, *route1)
        y, = _moe(stream, *routed, *moe1, x, mod1, final_norm_g, mod1, last_layer=True)
        return y.reshape(stream.n_seq, stream.seq_len, d), k, v, ret[1:]

    y_prompt, k_ctx, v_ctx, new_states = run(ctx, x_prompt, None, None)
    y_sample, _, _, _ = run(lat, x_sample, (cache_k0, cache_v0), (state_ret_fwd1, state_ret_bwd1))
    return (y_prompt, y_sample, k_ctx, v_ctx, new_states[0], new_states[1])
```

```python
import functools

import jax
import jax.numpy as jnp
import numpy as np
from jax import lax
from jax.experimental import pallas as pl
from jax.experimental.pallas import tpu as pltpu

D_MODEL = 1024
GRID_W = 64
EPS = 1e-6
CONV_CH = 512
CONV_WIDTH = 31
CONV_HALF = CONV_WIDTH // 2
N_Q_HEADS = 8
N_KV_HEADS = 2
Q_PER_KV = N_Q_HEADS // N_KV_HEADS
HEAD_DIM = 128
ROPE_THETA = 10000.0
ATTN_WIDTH = N_Q_HEADS * HEAD_DIM
KV_WIDTH = N_KV_HEADS * HEAD_DIM
IN0_WIDTH = 2 * CONV_CH + ATTN_WIDTH + 2 * KV_WIDTH
RET_HEADS = 8
RET_DK = 128
RET_DV = 256
RET_CHUNK = 128
RET_QK_WIDTH = RET_HEADS * RET_DK
RET_V_WIDTH = RET_HEADS * RET_DV
IN1_WIDTH = 2 * RET_QK_WIDTH + 3 * RET_V_WIDTH
N_GROUPS = 4
EXPERTS_PER_GROUP = 4
N_EXPERTS = N_GROUPS * EXPERTS_PER_GROUP
EXPERT_FF = 512

LANES = 128
SUBLANES = 8
VMEM_LIMIT_BYTES = 56 * 1024 * 1024

MOD_SLOTS = 8
ROW_TILE = 512
ROW_CHUNKS = D_MODEL // LANES
PAYLOAD_CHUNKS = 2 * ROW_CHUNKS
FFN_TILE = 256
COMBINE_TILE = 512
ROUTE_GROUP, ROUTE_RANK = 0, 1
CONV_TILE = 256
CONV_HALO = 16
CONV_ROWS = 64
ROUTE_LANES = LANES
EXPERT_LANE0 = N_GROUPS
NEG_BIG = -1e30
LOG2_E = 1.4426950408889634
ATTN_KV_CHUNK = 512

F32 = jnp.float32
BF16 = jnp.bfloat16


def _params(*semantics):
    return pltpu.CompilerParams(dimension_semantics=semantics, vmem_limit_bytes=VMEM_LIMIT_BYTES)


def _sigmoid(x):
    return 1.0 / (1.0 + jnp.exp(-x))


def _silu(x):
    return x * _sigmoid(x)


def _rms(x):
    return x * lax.rsqrt(jnp.mean(x * x, axis=-1, keepdims=True) + EPS)


def _rope(x, cos, sin):
    lane = lax.broadcasted_iota(jnp.int32, x.shape, 1)
    take_upper = (lane % (HEAD_DIM // 2)) < (HEAD_DIM // 4)
    partner = jnp.where(take_upper, pltpu.roll(x, HEAD_DIM - HEAD_DIM // 4, 1), pltpu.roll(x, HEAD_DIM // 4, 1))
    return x * cos + partner * sin


def _dot(a, b):
    return jnp.dot(a, b, preferred_element_type=F32)


def _dot_nt(a, b):
    return lax.dot_general(a, b, (((1,), (1,)), ((), ())), preferred_element_type=F32)


def _dot_tn(a, b):
    return lax.dot_general(a, b, (((0,), (0,)), ((), ())), preferred_element_type=F32)


def _mod_kernel(c_ref, w_ref, b_ref, o_ref):
    c = c_ref[...]
    o_ref[...] = _dot(_silu(c).astype(BF16), w_ref[...].astype(BF16)) + b_ref[...]


def _modulation(cond, mod_w, mod_b):
    n_out = mod_w.shape[1]
    col = D_MODEL
    out = pl.pallas_call(
        _mod_kernel,
        grid=(n_out // col,),
        in_specs=[
            pl.BlockSpec((MOD_SLOTS, D_MODEL), lambda j: (0, 0)),
            pl.BlockSpec((D_MODEL, col), lambda j: (0, j)),
            pl.BlockSpec((1, col), lambda j: (0, j)),
        ],
        out_specs=pl.BlockSpec((MOD_SLOTS, col), lambda j: (0, j)),
        out_shape=jax.ShapeDtypeStruct((MOD_SLOTS, n_out), F32),
        compiler_params=_params("parallel"),
        name="modulation",
    )(cond, mod_w, mod_b.reshape(1, n_out))
    return out.reshape(MOD_SLOTS, 6, D_MODEL)


def _slot_map(stream):
    if stream.first_slot == 0:
        return lambda i: 0
    return lambda i: stream.first_slot + (i * ROW_TILE) // stream.seq_len


class _Stream:
    def __init__(self, n_seq, seq_len, first_slot, rope):
        self.n_seq = n_seq
        self.seq_len = seq_len
        self.first_slot = first_slot
        self.rope = rope
        self.n_tok = n_seq * seq_len


def _modulated_norm(x, g, shift, scale):
    return _rms(x) * g * (1.0 + scale) + shift


def _inproj0_kernel(x_ref, mod_ref, g_ref, w_ref, wp_ref, qg_ref, kg_ref, cos_ref, sin_ref,
                    a_ref, q_ref, k_ref, v_ref, *, rope):
    h = _modulated_norm(x_ref[...], g_ref[...], mod_ref[0, 0:1, :], mod_ref[0, 1:2, :]).astype(BF16)
    p = _dot(h, w_ref[...])
    if rope:
        p_partner = _dot(h, wp_ref[...])
    a_ref[...] = p[:, :CONV_CH] * _sigmoid(p[:, CONV_CH:2 * CONV_CH])
    q0 = 2 * CONV_CH
    k0 = q0 + ATTN_WIDTH
    v0 = k0 + KV_WIDTH
    cos = cos_ref[...]
    sin = sin_ref[...]
    q_scale = HEAD_DIM ** -0.5 * LOG2_E

    def normed_head(col, partner_col, gain_ref):
        ph = p[:, col:col + HEAD_DIM]
        inv = lax.rsqrt(jnp.mean(ph * ph, axis=-1, keepdims=True) + EPS)
        out = ph * inv * gain_ref[0:1, :]
        if rope:
            out = out * cos + (p_partner[:, partner_col:partner_col + HEAD_DIM] * inv * gain_ref[1:2, :]) * sin
        return out

    for hh in range(N_Q_HEADS):
        qh = normed_head(q0 + hh * HEAD_DIM, hh * HEAD_DIM, qg_ref)
        q_ref[:, hh * HEAD_DIM:(hh + 1) * HEAD_DIM] = (qh * q_scale).astype(q_ref.dtype)
    seqs, _, rows, _ = k_ref.shape
    for hh in range(N_KV_HEADS):
        kh = normed_head(k0 + hh * HEAD_DIM, ATTN_WIDTH + hh * HEAD_DIM, kg_ref)
        vh = p[:, v0 + hh * HEAD_DIM:v0 + (hh + 1) * HEAD_DIM]
        for sq in range(seqs):
            k_ref[sq, hh] = kh[sq * rows:(sq + 1) * rows].astype(k_ref.dtype)
            v_ref[sq, hh] = vh[sq * rows:(sq + 1) * rows].astype(v_ref.dtype)


def _inproj0(stream, x, mod, norm_g, w_bf16, w_partner, q_gains, k_gains, cos, sin, kv_dtype):
    n = stream.n_tok
    tiles_per_seq = max(stream.seq_len // ROW_TILE, 1)
    rope_map = (lambda i: (i % tiles_per_seq, 0)) if stream.rope else (lambda i: (0, 0))
    slot = _slot_map(stream)
    row = lambda i: (i, 0)
    fixed = lambda i: (0, 0)
    seqs_per_tile = max(ROW_TILE // stream.seq_len, 1)
    kv_spec = pl.BlockSpec((seqs_per_tile, N_KV_HEADS, ROW_TILE // seqs_per_tile, HEAD_DIM),
                           lambda i: (i // tiles_per_seq, 0, i % tiles_per_seq, 0))
    kv_shape = jax.ShapeDtypeStruct((stream.n_seq, N_KV_HEADS, stream.seq_len, HEAD_DIM), kv_dtype)
    return pl.pallas_call(
        functools.partial(_inproj0_kernel, rope=stream.rope),
        grid=(n // ROW_TILE,),
        in_specs=[
            pl.BlockSpec((ROW_TILE, D_MODEL), row),
            pl.BlockSpec((1, 6, D_MODEL), lambda i: (slot(i), 0, 0)),
            pl.BlockSpec((1, D_MODEL), fixed),
            pl.BlockSpec((D_MODEL, IN0_WIDTH), fixed),
            pl.BlockSpec((D_MODEL, ATTN_WIDTH + KV_WIDTH), fixed),
            pl.BlockSpec((2, HEAD_DIM), fixed),
            pl.BlockSpec((2, HEAD_DIM), fixed),
            pl.BlockSpec((ROW_TILE, HEAD_DIM), rope_map),
            pl.BlockSpec((ROW_TILE, HEAD_DIM), rope_map),
        ],
        out_specs=[
            pl.BlockSpec((ROW_TILE, CONV_CH), row),
            pl.BlockSpec((ROW_TILE, ATTN_WIDTH), row),
            kv_spec,
            kv_spec,
        ],
        out_shape=[
            jax.ShapeDtypeStruct((n, CONV_CH), F32),
            jax.ShapeDtypeStruct((n, ATTN_WIDTH), BF16),
            kv_shape,
            kv_shape,
        ],
        compiler_params=_params("parallel"),
        name="inproj0",
    )(x, mod, norm_g.reshape(1, D_MODEL), w_bf16, w_partner, q_gains, k_gains, cos, sin)


def _conv_kernel(prev_ref, main_ref, next_ref, w_ref, b_ref, g_ref, beta_ref, o_ref, pad_ref, acc_ref, *,
                 tiles_per_seq):
    i = pl.program_id(0)
    has_prev = (i % tiles_per_seq) != 0
    has_next = (i % tiles_per_seq) != (tiles_per_seq - 1)
    pad_ref[0:CONV_HALO, :] = jnp.where(has_prev, prev_ref[...], 0.0)
    pad_ref[CONV_HALO:CONV_HALO + CONV_TILE, :] = main_ref[...]
    pad_ref[CONV_HALO + CONV_TILE:, :] = jnp.where(has_next, next_ref[...], 0.0)
    first = CONV_HALO - CONV_HALF
    window = CONV_ROWS + SUBLANES
    for c0 in range(0, CONV_CH, LANES):
        for r0 in range(0, CONV_TILE, CONV_ROWS):
            acc = jnp.zeros((CONV_ROWS, LANES), F32)
            for phase in range(SUBLANES):
                partial = jnp.zeros((window, LANES), F32)
                for j in range(phase, CONV_WIDTH, SUBLANES):
                    base = r0 + j - phase
                    partial = partial + w_ref[j:j + 1, c0:c0 + LANES] * pad_ref[base:base + window, c0:c0 + LANES]
                acc = acc + partial[first + phase:first + phase + CONV_ROWS]
            acc_ref[r0:r0 + CONV_ROWS, c0:c0 + LANES] = acc
    a = acc_ref[...] + b_ref[...]
    mu = jnp.mean(a, axis=-1, keepdims=True)
    d = a - mu
    var = jnp.mean(d * d, axis=-1, keepdims=True)
    y = d * lax.rsqrt(var + EPS) * g_ref[...] + beta_ref[...]
    o_ref[...] = _silu(y).astype(o_ref.dtype)


def _conv_branch(stream, a, conv_w, conv_b, norm_g, norm_b):
    n = stream.n_tok
    tiles_per_seq = stream.seq_len // CONV_TILE
    halo_per_tile = CONV_TILE // CONV_HALO
    n_halo = n // CONV_HALO
    fixed = lambda i: (0, 0)
    return pl.pallas_call(
        functools.partial(_conv_kernel, tiles_per_seq=tiles_per_seq),
        grid=(n // CONV_TILE,),
        in_specs=[
            pl.BlockSpec((CONV_HALO, CONV_CH), lambda i: (jnp.maximum(i * halo_per_tile - 1, 0), 0)),
            pl.BlockSpec((CONV_TILE, CONV_CH), lambda i: (i, 0)),
            pl.BlockSpec((CONV_HALO, CONV_CH), lambda i: (jnp.minimum((i + 1) * halo_per_tile, n_halo - 1), 0)),
            pl.BlockSpec((CONV_WIDTH, CONV_CH), fixed),
            pl.BlockSpec((1, CONV_CH), fixed),
            pl.BlockSpec((1, CONV_CH), fixed),
            pl.BlockSpec((1, CONV_CH), fixed),
        ],
        out_specs=pl.BlockSpec((CONV_TILE, CONV_CH), lambda i: (i, 0)),
        out_shape=jax.ShapeDtypeStruct((n, CONV_CH), BF16),
        scratch_shapes=[
            pltpu.VMEM((CONV_TILE + 2 * CONV_HALO, CONV_CH), F32),
            pltpu.VMEM((CONV_TILE, CONV_CH), F32),
        ],
        compiler_params=_params("parallel"),
        name="conv_branch",
    )(a, a, a, conv_w, conv_b.reshape(1, CONV_CH), norm_g.reshape(1, CONV_CH), norm_b.reshape(1, CONV_CH))


def _attn_kernel(*refs, tq, past):
    if past:
        q_ref, k_ref, v_ref, kc_ref, vc_ref, o_ref, ks_ref, vs_ref = refs
    else:
        q_ref, k_ref, v_ref, o_ref, ks_ref, vs_ref = refs
    kv_heads, n_keys, _ = ks_ref.shape
    group_w = Q_PER_KV * HEAD_DIM

    @pl.when(pl.program_id(2) == 0)
    def _():
        lane = lax.broadcasted_iota(jnp.int32, (n_keys, HEAD_DIM), 1)
        for hk in range(kv_heads):
            vs_ref[hk, :, HEAD_DIM:] = jnp.where(lane == 0, 1.0, 0.0).astype(BF16)
            if past:
                ks_ref[hk, 0:past, :] = kc_ref[0, hk].astype(BF16)
                vs_ref[hk, 0:past, 0:HEAD_DIM] = vc_ref[0, hk].astype(BF16)
            ks_ref[hk, past:, :] = k_ref[0, hk].astype(BF16)
            vs_ref[hk, past:, 0:HEAD_DIM] = v_ref[0, hk].astype(BF16)

    for hk in range(kv_heads):
        q = q_ref[:, hk * group_w:(hk + 1) * group_w]
        q4 = jnp.concatenate([q[:, g * HEAD_DIM:(g + 1) * HEAD_DIM] for g in range(Q_PER_KV)], axis=0)
        rows = q4.shape[0]
        m = jnp.full((rows, 1), NEG_BIG, F32)
        acc = jnp.zeros((rows, 2 * HEAD_DIM), F32)
        for c0 in range(0, n_keys, ATTN_KV_CHUNK):
            c1 = min(c0 + ATTN_KV_CHUNK, n_keys)
            s = _dot_nt(q4, ks_ref[hk, c0:c1, :])
            m_new = jnp.maximum(m, jnp.max(s, axis=-1, keepdims=True))
            p = jnp.exp2(s - m_new).astype(BF16)
            acc = jnp.exp2(m - m_new) * acc + _dot(p, vs_ref[hk, c0:c1, :])
            m = m_new
        o = acc[:, 0:HEAD_DIM] * (1.0 / acc[:, HEAD_DIM:HEAD_DIM + 1])
        for g in range(Q_PER_KV):
            col = hk * group_w + g * HEAD_DIM
            o_ref[:, col:col + HEAD_DIM] = o[g * tq:(g + 1) * tq].astype(o_ref.dtype)


def _attention(stream, q, k, v, cache, tq, kv_heads):
    n = stream.n_tok
    t = stream.seq_len
    q_tiles = t // tq
    width = kv_heads * Q_PER_KV * HEAD_DIM
    in_specs = [
        pl.BlockSpec((tq, width), lambda b, h, i: (b * q_tiles + i, h)),
        pl.BlockSpec((1, kv_heads, t, HEAD_DIM), lambda b, h, i: (b, h, 0, 0)),
        pl.BlockSpec((1, kv_heads, t, HEAD_DIM), lambda b, h, i: (b, h, 0, 0)),
    ]
    args = [q, k, v]
    past = 0
    if cache is not None:
        past = cache[0].shape[2]
        in_specs += [pl.BlockSpec((1, kv_heads, past, HEAD_DIM), lambda b, h, i: (b, h, 0, 0))] * 2
        args += list(cache)
    return pl.pallas_call(
        functools.partial(_attn_kernel, tq=tq, past=past),
        grid=(stream.n_seq, N_KV_HEADS // kv_heads, q_tiles),
        in_specs=in_specs,
        out_specs=pl.BlockSpec((tq, width), lambda b, h, i: (b * q_tiles + i, h)),
        out_shape=jax.ShapeDtypeStruct((n, ATTN_WIDTH), BF16),
        scratch_shapes=[pltpu.VMEM((kv_heads, past + t, HEAD_DIM), BF16),
                        pltpu.VMEM((kv_heads, past + t, 2 * HEAD_DIM), BF16)],
        compiler_params=_params("parallel", "parallel", "arbitrary"),
        name="attention",
    )(*args)


def _route(logits, running):
    rows = logits.shape[0]
    lane = lax.broadcasted_iota(jnp.int32, logits.shape, 1)
    lane_f = lane.astype(F32)
    far = float(ROUTE_LANES)
    is_group = lane < N_GROUPS
    gl = jnp.where(is_group, logits, NEG_BIG)
    gmax = jnp.max(gl, axis=-1, keepdims=True)
    gsum = jnp.sum(jnp.where(is_group, jnp.exp(gl - gmax), 0.0), axis=-1, keepdims=True)
    g_w = 1.0 / gsum
    gidx = jnp.min(jnp.where(gl == gmax, lane_f, far), axis=-1, keepdims=True)
    lo = EXPERT_LANE0 + EXPERTS_PER_GROUP * gidx
    in_group = (lane_f >= lo) & (lane_f < lo + EXPERTS_PER_GROUP)
    el = jnp.where(in_group, logits, NEG_BIG)
    v1 = jnp.max(el, axis=-1, keepdims=True)
    i1 = jnp.min(jnp.where(el == v1, lane_f, far), axis=-1, keepdims=True)
    el2 = jnp.where(lane_f == i1, NEG_BIG, el)
    v2 = jnp.max(el2, axis=-1, keepdims=True)
    i2 = jnp.min(jnp.where(el2 == v2, lane_f, far), axis=-1, keepdims=True)
    e2 = jnp.exp(v2 - v1)
    w1 = g_w / (1.0 + e2)
    w2 = w1 * e2
    in_my_group = lane_f == gidx
    chosen = in_my_group.astype(BF16)
    earlier = (lax.broadcasted_iota(jnp.int32, (rows, rows), 0)
               > lax.broadcasted_iota(jnp.int32, (rows, rows), 1)).astype(BF16)
    before = running + _dot(earlier, chosen)
    rank = jnp.sum(jnp.where(in_my_group, before, 0.0), axis=-1, keepdims=True)
    record = jnp.where(lane == ROUTE_GROUP, gidx, jnp.where(lane == ROUTE_RANK, rank, 0.0))
    weights = jnp.where(lane_f == i1 - lo, w1, 0.0) + jnp.where(lane_f == i2 - lo, w2, 0.0)
    return record, weights, running + jnp.sum(chosen.astype(F32), axis=0, keepdims=True)


def _outproj_kernel(*refs, widths):
    n_in = len(widths)
    in_refs = refs[:n_in]
    w_ref, x_ref, mod_ref, g_ref, rw_ref, rb_ref, x1_ref, pay_ref, route_t_ref, count_ref, run_ref = refs[n_in:]

    @pl.when(pl.program_id(0) == 0)
    def _():
        run_ref[...] = jnp.zeros_like(run_ref)

    acc = None
    off = 0
    for r, width in zip(in_refs, widths):
        part = _dot(r[...], w_ref[off:off + width, :])
        acc = part if acc is None else acc + part
        off += width
    x1 = x_ref[...] + mod_ref[0, 2:3, :] * acc
    x1_ref[...] = x1
    h = _modulated_norm(x1, g_ref[...], mod_ref[0, 3:4, :], mod_ref[0, 4:5, :])
    h_hi = h.astype(BF16)
    h_lo = (h - h_hi.astype(F32)).astype(BF16)
    both = _dot(h_hi, rw_ref[...])
    logits = (both[:, :ROUTE_LANES] + both[:, ROUTE_LANES:] + _dot(h_lo, rw_ref[:, :ROUTE_LANES]) + rb_ref[...])
    record, weights, running = _route(logits, run_ref[...])
    route_t_ref[...] = record.T[:SUBLANES]
    run_ref[...] = running
    count_ref[...] = jnp.broadcast_to(running, count_ref.shape)
    rows = h.shape[0]
    _store_row_major(pay_ref, h, PAYLOAD_CHUNKS)
    pay_ref[pl.ds(ROW_CHUNKS, rows, stride=PAYLOAD_CHUNKS), :] = weights
    for c in range(ROW_CHUNKS + 1, PAYLOAD_CHUNKS):
        pay_ref[pl.ds(c, rows, stride=PAYLOAD_CHUNKS), :] = jnp.zeros((rows, LANES), F32)


def _store_row_major(ref, x, pitch=ROW_CHUNKS):
    rows = x.shape[0]
    for c in range(ROW_CHUNKS):
        ref[pl.ds(c, rows, stride=pitch), :] = x[:, c * LANES:(c + 1) * LANES]


def _load_row_major(ref, rows, pitch=ROW_CHUNKS):
    return jnp.concatenate([ref[pl.ds(c, rows, stride=pitch), :] for c in range(ROW_CHUNKS)], axis=-1)


def _outproj_route(stream, inputs, w_bf16, x, mod, ffn_norm_g, route_w, route_b):
    n = stream.n_tok
    widths = tuple(a.shape[1] for a in inputs)
    slot = _slot_map(stream)
    row = lambda i: (i, 0)
    fixed = lambda i: (0, 0)
    return pl.pallas_call(
        functools.partial(_outproj_kernel, widths=widths),
        grid=(n // ROW_TILE,),
        in_specs=[pl.BlockSpec((ROW_TILE, width), row) for width in widths] + [
            pl.BlockSpec((sum(widths), D_MODEL), fixed),
            pl.BlockSpec((ROW_TILE, D_MODEL), row),
            pl.BlockSpec((1, 6, D_MODEL), lambda i: (slot(i), 0, 0)),
            pl.BlockSpec((1, D_MODEL), fixed),
            pl.BlockSpec((D_MODEL, 2 * ROUTE_LANES), fixed),
            pl.BlockSpec((1, ROUTE_LANES), fixed),
        ],
        out_specs=[
            pl.BlockSpec((ROW_TILE, D_MODEL), row),
            pl.BlockSpec((ROW_TILE * PAYLOAD_CHUNKS, LANES), row),
            pl.BlockSpec((SUBLANES, ROW_TILE), lambda i: (0, i)),
            pl.BlockSpec((SUBLANES, ROUTE_LANES), fixed),
        ],
        out_shape=[
            jax.ShapeDtypeStruct((n, D_MODEL), F32),
            jax.ShapeDtypeStruct((n * PAYLOAD_CHUNKS, LANES), F32),
            jax.ShapeDtypeStruct((SUBLANES, n), F32),
            jax.ShapeDtypeStruct((SUBLANES, ROUTE_LANES), F32),
        ],
        scratch_shapes=[pltpu.VMEM((1, ROUTE_LANES), F32)],
        compiler_params=_params("arbitrary"),
        name="outproj_route",
    )(*inputs, w_bf16, x, mod, ffn_norm_g.reshape(1, D_MODEL), route_w, route_b)


def _router_params(grp_w, grp_b, rtr_w, rtr_b):
    w = jnp.concatenate([grp_w, jnp.moveaxis(rtr_w, 0, 1).reshape(D_MODEL, N_EXPERTS)], axis=1)
    w = jnp.pad(w, ((0, 0), (0, ROUTE_LANES - w.shape[1])))
    hi = w.astype(BF16)
    lo = (w - hi.astype(F32)).astype(BF16)
    b = jnp.concatenate([grp_b, rtr_b.reshape(N_EXPERTS)])
    b = jnp.pad(b, (0, ROUTE_LANES - b.shape[0])).reshape(1, ROUTE_LANES)
    return jnp.concatenate([hi, lo], axis=1), b


def _sorted_tiles(n_tok):
    return n_tok // FFN_TILE + N_GROUPS


def _dispatch_plan(route_t, counts, n_tok):
    n_tiles = _sorted_tiles(n_tok)
    count = counts[0, :N_GROUPS].astype(jnp.int32)
    tiles = (count + (FFN_TILE - 1)) // FFN_TILE
    start = (jnp.cumsum(tiles) - tiles) * FFN_TILE
    tile_end = jnp.cumsum(tiles)
    j = jnp.arange(n_tiles, dtype=jnp.int32)
    tile_group = jnp.minimum(jnp.sum(j[:, None] >= tile_end[None, :], axis=1), N_GROUPS - 1).astype(jnp.int32)
    group_end = jnp.take(start + count, tile_group)
    valid = jnp.where(j < tile_end[-1], jnp.clip(group_end - j * FFN_TILE, 0, FFN_TILE), 0).astype(jnp.int32)
    ids = route_t[:2].astype(jnp.int32)
    where = jnp.take(start, ids[ROUTE_GROUP]) + ids[ROUTE_RANK]
    return where, tile_group, valid


def _token_rows(t, pitch=ROW_CHUNKS):
    return pl.ds(pl.multiple_of(t * pitch, pitch), pitch)


def _ffn_kernel(tg_ref, valid_ref, where_ref, pay_ref, wg_ref, wu_ref, wd_ref, ys_ref, x_buf, src_ref, sems, *, n_tok):
    j = pl.program_id(0)
    n_tiles = pl.num_programs(0)

    @pl.when(j == 0)
    def _():
        def place(t, carry):
            src_ref[where_ref[t]] = t
            return carry

        lax.fori_loop(0, n_tok, place, 0, unroll=8)

    def gather_copy(tile, r):
        tok = src_ref[tile * FFN_TILE + r]
        return pltpu.make_async_copy(pay_ref.at[_token_rows(tok, PAYLOAD_CHUNKS)],
                                     x_buf.at[tile % 2, _token_rows(r, PAYLOAD_CHUNKS)], sems.at[tile % 2])

    def start_gather(tile):
        def issue_pair(r2, carry):
            for k in range(2):
                gather_copy(tile, 2 * r2 + k).start(priority=k)
            return carry

        def issue(r, carry):
            gather_copy(tile, r).start()
            return carry

        @pl.when(valid_ref[tile] == FFN_TILE)
        def _():
            lax.fori_loop(0, FFN_TILE // 2, issue_pair, 0, unroll=4)

        @pl.when(valid_ref[tile] < FFN_TILE)
        def _():
            lax.fori_loop(0, valid_ref[tile], issue, 0)

    @pl.when(j == 0)
    def _():
        x_buf[...] = jnp.zeros_like(x_buf)
        start_gather(j)

    @pl.when(j + 1 < n_tiles)
    def _():
        @pl.when(valid_ref[j + 1] > 0)
        def _():
            start_gather(j + 1)

    @pl.when(valid_ref[j] == 0)
    def _():
        ys_ref[...] = jnp.zeros_like(ys_ref)

    @pl.when(valid_ref[j] > 0)
    def _():
        slot = j % 2
        n = pl.multiple_of(valid_ref[j] * PAYLOAD_CHUNKS, PAYLOAD_CHUNKS)
        pltpu.make_async_copy(pay_ref.at[pl.ds(0, n)], x_buf.at[slot, pl.ds(0, n)], sems.at[slot]).wait()
        x = _load_row_major(x_buf.at[slot], FFN_TILE, PAYLOAD_CHUNKS).astype(BF16)
        w = x_buf[slot, pl.ds(ROW_CHUNKS, FFN_TILE, stride=PAYLOAD_CHUNKS), :]
        y = None
        for e in range(EXPERTS_PER_GROUP):
            hidden = _silu(_dot(x, wg_ref[e])) * _dot(x, wu_ref[e]) * w[:, e:e + 1]
            part = _dot(hidden.astype(BF16), wd_ref[e])
            y = part if y is None else y + part
        _store_row_major(ys_ref, y)


def _expert_ffn(n_tok, tile_group, valid, where, payload, wg, wu, wd):
    group = lambda j, tg, va, sr: (tg[j], 0, 0)
    n_tiles = _sorted_tiles(n_tok)
    return pl.pallas_call(
        functools.partial(_ffn_kernel, n_tok=n_tok),
        grid_spec=pltpu.PrefetchScalarGridSpec(
            num_scalar_prefetch=3,
            grid=(n_tiles,),
            in_specs=[
                pl.BlockSpec(memory_space=pl.ANY),
                pl.BlockSpec((EXPERTS_PER_GROUP, D_MODEL, EXPERT_FF), group),
                pl.BlockSpec((EXPERTS_PER_GROUP, D_MODEL, EXPERT_FF), group),
                pl.BlockSpec((EXPERTS_PER_GROUP, EXPERT_FF, D_MODEL), group),
            ],
            out_specs=pl.BlockSpec((FFN_TILE * ROW_CHUNKS, LANES), lambda j, tg, va, sr: (j, 0)),
            scratch_shapes=[pltpu.VMEM((2, FFN_TILE * PAYLOAD_CHUNKS, LANES), F32),
                            pltpu.SMEM((n_tiles * FFN_TILE,), jnp.int32), pltpu.SemaphoreType.DMA((2,))],
        ),
        out_shape=jax.ShapeDtypeStruct((n_tiles * FFN_TILE * ROW_CHUNKS, LANES), F32),
        compiler_params=_params("arbitrary"),
        name="moe_ffn",
    )(tile_group, valid, where, payload, wg, wu, wd)


def _combine_kernel(where_ref, x_ref, mod_ref, ng_ref, nmod_ref, ys_ref, o_ref, *rest, last_layer):
    if last_layer:
        y_buf, sems = rest
    else:
        h_ref, y_buf, sems = rest
    i = pl.program_id(0)
    n_tiles = pl.num_programs(0)

    def row_copy(tile, t, k):
        slot = tile % 2
        d = where_ref[tile * COMBINE_TILE + t]
        return pltpu.make_async_copy(ys_ref.at[_token_rows(d)], y_buf.at[slot, _token_rows(t)], sems.at[slot, k])

    def start_tile(tile):
        def issue(t2, carry):
            for k in range(2):
                row_copy(tile, 2 * t2 + k, k).start(priority=k)
            return carry

        lax.fori_loop(0, COMBINE_TILE // 2, issue, 0, unroll=4)

    @pl.when(i == 0)
    def _():
        start_tile(i)

    @pl.when(i + 1 < n_tiles)
    def _():
        start_tile(i + 1)

    slot = i % 2
    half = COMBINE_TILE * ROW_CHUNKS // 2
    for k in range(2):
        pltpu.make_async_copy(ys_ref.at[pl.ds(0, half)], y_buf.at[slot, pl.ds(0, half)], sems.at[slot, k]).wait()
    x = x_ref[...] + mod_ref[0, 5:6, :] * _load_row_major(y_buf.at[slot], COMBINE_TILE)
    if last_layer:
        o_ref[...] = _rms(x) * ng_ref[...]
    else:
        o_ref[...] = x
        h_ref[...] = _modulated_norm(x, ng_ref[...], nmod_ref[0, 0:1, :], nmod_ref[0, 1:2, :]).astype(h_ref.dtype)


def _combine(stream, where, ys, x, mod, norm_g, next_mod, last_layer):
    n = stream.n_tok
    first_slot, seq_len = stream.first_slot, stream.seq_len
    slot = (lambda i: 0) if first_slot == 0 else (lambda i: first_slot + (i * COMBINE_TILE) // seq_len)
    row = lambda i, w: (i, 0)
    out_specs = [pl.BlockSpec((COMBINE_TILE, D_MODEL), row)]
    out_shape = [jax.ShapeDtypeStruct((n, D_MODEL), F32)]
    if not last_layer:
        out_specs.append(pl.BlockSpec((COMBINE_TILE, D_MODEL), row))
        out_shape.append(jax.ShapeDtypeStruct((n, D_MODEL), BF16))
    return pl.pallas_call(
        functools.partial(_combine_kernel, last_layer=last_layer),
        grid_spec=pltpu.PrefetchScalarGridSpec(
            num_scalar_prefetch=1,
            grid=(n // COMBINE_TILE,),
            in_specs=[
                pl.BlockSpec((COMBINE_TILE, D_MODEL), row),
                pl.BlockSpec((1, 6, D_MODEL), lambda i, w: (slot(i), 0, 0)),
                pl.BlockSpec((1, D_MODEL), lambda i, w: (0, 0)),
                pl.BlockSpec((1, 6, D_MODEL), lambda i, w: (slot(i), 0, 0)),
                pl.BlockSpec(memory_space=pl.ANY),
            ],
            out_specs=out_specs,
            scratch_shapes=[pltpu.VMEM((2, COMBINE_TILE * ROW_CHUNKS, LANES), F32),
                            pltpu.SemaphoreType.DMA((2, 2))],
        ),
        out_shape=out_shape,
        compiler_params=_params("arbitrary"),
        name="moe_combine",
    )(where, x, mod, norm_g.reshape(1, D_MODEL), next_mod, ys)


def _moe(stream, payload, route_t, counts, wg, wu, wd, x, mod, norm_g, next_mod, last_layer):
    n = stream.n_tok
    where, tile_group, valid = _dispatch_plan(route_t, counts, n)
    ys = _expert_ffn(n, tile_group, valid, where, payload, wg, wu, wd)
    return _combine(stream, where, ys, x, mod, norm_g, next_mod, last_layer)


IN1_COL = 1024
IN1_ROW_TILE = 1024
QK_BLOCKS = 2 * RET_QK_WIDTH // IN1_COL
Q_BLOCKS = RET_QK_WIDTH // IN1_COL
V_BLOCKS = RET_V_WIDTH // IN1_COL


def _inproj1_qk_kernel(h_ref, w_ref, cos_ref, sin_ref, o_ref, *, rope):
    p = _dot(h_ref[...], w_ref[...])
    scale = jnp.where(pl.program_id(0) < Q_BLOCKS, 1.0, RET_DK ** -0.5)
    cos = cos_ref[...]
    sin = sin_ref[...]
    for hh in range(IN1_COL // RET_DK):
        ph = p[:, hh * RET_DK:(hh + 1) * RET_DK]
        if rope:
            ph = _rope(ph, cos, sin)
        o_ref[:, hh * RET_DK:(hh + 1) * RET_DK] = (ph * scale).astype(o_ref.dtype)


def _inproj1_vg_kernel(h_ref, w_ref, o_ref):
    p = _dot(h_ref[...], w_ref[...])
    o_ref[...] = jnp.where(pl.program_id(0) < V_BLOCKS, p, _silu(p)).astype(o_ref.dtype)


def _inproj1(stream, h, w_bf16, cos, sin):
    n = stream.n_tok
    tm = IN1_ROW_TILE
    tiles_per_seq = max(stream.seq_len // tm, 1)
    rope_map = (lambda j, i: (i % tiles_per_seq, 0)) if stream.rope else (lambda j, i: (0, 0))
    h_spec = pl.BlockSpec((tm, D_MODEL), lambda j, i: (i, 0))
    out_spec = pl.BlockSpec((tm, IN1_COL), lambda j, i: (i, j))
    qk = pl.pallas_call(
        functools.partial(_inproj1_qk_kernel, rope=stream.rope),
        grid=(QK_BLOCKS, n // tm),
        in_specs=[
            h_spec,
            pl.BlockSpec((D_MODEL, IN1_COL), lambda j, i: (0, j)),
            pl.BlockSpec((tm, RET_DK), rope_map),
            pl.BlockSpec((tm, RET_DK), rope_map),
        ],
        out_specs=out_spec,
        out_shape=jax.ShapeDtypeStruct((n, QK_BLOCKS * IN1_COL), BF16),
        compiler_params=_params("parallel", "parallel"),
        name="inproj1_qk",
    )(h, w_bf16, cos, sin)
    vg_blocks = IN1_WIDTH // IN1_COL - QK_BLOCKS
    vg = pl.pallas_call(
        _inproj1_vg_kernel,
        grid=(vg_blocks, n // tm),
        in_specs=[h_spec, pl.BlockSpec((D_MODEL, IN1_COL), lambda j, i: (0, QK_BLOCKS + j))],
        out_specs=out_spec,
        out_shape=jax.ShapeDtypeStruct((n, vg_blocks * IN1_COL), BF16),
        compiler_params=_params("parallel", "parallel"),
        name="inproj1_vg",
    )(h, w_bf16)
    return qk, vg


def _retention_kernel(*refs, n_chunks, has_state, heads, unroll):
    if has_state:
        decay_ref, q_ref, k_ref, v_ref, gf_ref, gb_ref, sf0_ref, sb0_ref, o_ref, inc_ref, seen_ref = refs
    else:
        decay_ref, q_ref, k_ref, v_ref, gf_ref, gb_ref, o_ref, sf_ref, sb_ref, inc_ref, seen_ref = refs
    c = RET_CHUNK
    row = lax.broadcasted_iota(jnp.int32, (c, c), 0).astype(F32)
    col = lax.broadcasted_iota(jnp.int32, (c, c), 1).astype(F32)
    pos = lax.broadcasted_iota(jnp.int32, (c, 1), 0).astype(F32)

    def chunk(i):
        return pl.ds(pl.multiple_of(i * c, c), c)

    diff = [row - col, col - row]
    qk_cols = [slice(g * RET_DK, (g + 1) * RET_DK) for g in range(heads)]
    v_cols = [slice(g * RET_DV, (g + 1) * RET_DV) for g in range(heads)]
    intra, q_dec, k_dec, chunk_dec = [], [], [], []
    for g in range(heads):
        hd = pl.program_id(1) * heads + g
        lg = [-jnp.exp(jnp.full((1, 1), decay_ref[d, hd], F32)) for d in range(2)]
        intra.append([jnp.where(diff[d] >= 0, jnp.exp(lg[d] * jnp.maximum(diff[d], 0.0)), 0.0) for d in range(2)])
        q_dec.append([jnp.exp(lg[0] * (pos + 1.0)), jnp.exp(lg[1] * (c - pos))])
        k_dec.append([jnp.exp(lg[0] * ((c - 1.0) - pos)), jnp.exp(lg[1] * pos)])
        chunk_dec.append([jnp.exp(lg[d] * float(c)) for d in range(2)])

    def increments(i, carry):
        rows = chunk(i)
        for g in range(heads):
            ki = k_ref[rows, qk_cols[g]].astype(F32)
            vi = v_ref[rows, v_cols[g]]
            for d in range(2):
                inc_ref[g, d, i] = _dot_tn((ki * k_dec[g][d]).astype(BF16), vi)
        return carry

    lax.fori_loop(0, n_chunks, increments, 0, unroll=unroll)

    for g in range(heads):
        finals = []
        for d in range(2):
            def scan(t, state):
                i = t if d == 0 else n_chunks - 1 - t
                seen_ref[g, d, i] = state.astype(BF16)
                return state * chunk_dec[g][d] + inc_ref[g, d, i]

            if has_state:
                state0 = (sf0_ref if d == 0 else sb0_ref)[0, g]
            else:
                state0 = jnp.zeros((RET_DK, RET_DV), F32)
            finals.append(lax.fori_loop(0, n_chunks, scan, state0))
        if not has_state:
            sf_ref[0, g] = finals[0]
            sb_ref[0, g] = finals[1]

    def outputs(i, carry):
        rows = chunk(i)
        for g in range(heads):
            qi = q_ref[rows, qk_cols[g]]
            vi = v_ref[rows, v_cols[g]]
            s = _dot_nt(qi, k_ref[rows, qk_cols[g]])
            mixed = None
            for d, g_ref in enumerate((gf_ref, gb_ref)):
                y = _dot((s * intra[g][d]).astype(BF16), vi) + _dot(qi, seen_ref[g, d, i]) * q_dec[g][d]
                gated = g_ref[rows, v_cols[g]].astype(F32) * _rms(y)
                mixed = gated if mixed is None else mixed + gated
            o_ref[rows, v_cols[g]] = mixed.astype(o_ref.dtype)
        return carry

    lax.fori_loop(0, n_chunks, outputs, 0, unroll=unroll)


def _retention(stream, qk, vg, decays, states, heads):
    n = stream.n_tok
    t = stream.seq_len
    n_chunks = t // RET_CHUNK
    has_state = states is not None
    qk_w = heads * RET_DK
    v_w = heads * RET_DV
    k_blk = RET_QK_WIDTH // qk_w
    gf_blk = RET_V_WIDTH // v_w
    gb_blk = 2 * gf_blk
    in_specs = [
        pl.BlockSpec(memory_space=pltpu.SMEM),
        pl.BlockSpec((t, qk_w), lambda b, h: (b, h)),
        pl.BlockSpec((t, qk_w), lambda b, h: (b, k_blk + h)),
        pl.BlockSpec((t, v_w), lambda b, h: (b, h)),
        pl.BlockSpec((t, v_w), lambda b, h: (b, gf_blk + h)),
        pl.BlockSpec((t, v_w), lambda b, h: (b, gb_blk + h)),
    ]
    args = [decays, qk, qk, vg, vg, vg]
    state_spec = pl.BlockSpec((1, heads, RET_DK, RET_DV), lambda b, h: (b, h, 0, 0))
    out_specs = [pl.BlockSpec((t, v_w), lambda b, h: (b, h))]
    out_shape = [jax.ShapeDtypeStruct((n, RET_V_WIDTH), BF16)]
    if has_state:
        in_specs += [state_spec, state_spec]
        args += list(states)
    else:
        out_specs += [state_spec, state_spec]
        out_shape += [jax.ShapeDtypeStruct((stream.n_seq, RET_HEADS, RET_DK, RET_DV), F32)] * 2
    return pl.pallas_call(
        functools.partial(_retention_kernel, n_chunks=n_chunks, has_state=has_state, heads=heads,
                          unroll=min(n_chunks, 2)),
        grid=(stream.n_seq, RET_HEADS // heads),
        in_specs=in_specs,
        out_specs=out_specs,
        out_shape=out_shape,
        scratch_shapes=[pltpu.VMEM((heads, 2, n_chunks, RET_DK, RET_DV), F32),
                        pltpu.VMEM((heads, 2, n_chunks, RET_DK, RET_DV), BF16)],
        compiler_params=_params("parallel", "parallel"),
        name="retention",
    )(*args)


def _rope_tables(n_tokens, dim):
    t = np.arange(n_tokens)
    pos = np.stack([t // GRID_W, t % GRID_W]).astype(np.float64)
    n_freq = dim // 4
    freqs = ROPE_THETA ** (-np.arange(n_freq, dtype=np.float64) / n_freq)
    ang = pos[:, :, None] * freqs
    cos, sin = np.cos(ang), np.sin(ang)
    cos_t = np.concatenate([cos[0], cos[0], cos[1], cos[1]], axis=-1)
    sin_t = np.concatenate([-sin[0], sin[0], -sin[1], sin[1]], axis=-1)
    return jnp.asarray(cos_t, F32), jnp.asarray(sin_t, F32)


def kernel(x_prompt, x_sample, cache_k0, cache_v0, state_ret_fwd1, state_ret_bwd1, c, c_ctx,
           norm_mix_g0, mod_w0, mod_b0, in_w0, conv_w0, conv_b0, conv_norm_g0, conv_norm_b0,
           q_norm_g0, k_norm_g0, out_w0, norm_ffn_g0,
           moe_grp_w0, moe_grp_b0, moe_rtr_w0, moe_rtr_b0, moe_w_gate0, moe_w_up0, moe_w_down0,
           norm_mix_g1, mod_w1, mod_b1, in_w1, ret_decay_fwd1, ret_decay_bwd1, out_w1, norm_ffn_g1,
           moe_grp_w1, moe_grp_b1, moe_rtr_w1, moe_rtr_b1, moe_w_gate1, moe_w_up1, moe_w_down1,
           final_norm_g):
    batch, seq, d = x_prompt.shape
    dec_batch, dec_seq, _ = x_sample.shape
    assert d == D_MODEL and 1 + dec_batch <= MOD_SLOTS
    ctx = _Stream(batch, seq, first_slot=0, rope=False)
    lat = _Stream(dec_batch, dec_seq, first_slot=1, rope=True)
    assert ctx.n_tok % ROW_TILE == 0 and lat.n_tok % ROW_TILE == 0 and ROW_CHUNKS == SUBLANES
    assert seq % CONV_TILE == 0 and dec_seq % ROW_TILE == 0 and ROW_TILE % seq == 0

    cond = jnp.concatenate([c_ctx[None, :], c, jnp.zeros((MOD_SLOTS - 1 - dec_batch, d), F32)], axis=0)
    mod0 = _modulation(cond, mod_w0, mod_b0)
    mod1 = _modulation(cond, mod_w1, mod_b1)
    cos, sin = _rope_tables(dec_seq, HEAD_DIM)
    decays = jnp.stack([ret_decay_fwd1, ret_decay_bwd1]).astype(F32)

    in_w0_b = in_w0.astype(BF16)
    def partner_order(a):
        quarter = HEAD_DIM // 4
        blocks = a.reshape(a.shape[:-1] + (a.shape[-1] // (2 * quarter), 2, quarter))
        return blocks[..., ::-1, :].reshape(a.shape)

    qk0 = 2 * CONV_CH
    in_w0_partner = partner_order(in_w0_b[:, qk0:qk0 + ATTN_WIDTH + KV_WIDTH])
    q_gains = jnp.stack([q_norm_g0, partner_order(q_norm_g0)])
    k_gains = jnp.stack([k_norm_g0, partner_order(k_norm_g0)])
    out_w0_b = out_w0.astype(BF16)
    in_w1_b = in_w1.astype(BF16)
    out_w1_b = out_w1.astype(BF16)
    moe0 = (moe_w_gate0.astype(BF16), moe_w_up0.astype(BF16), moe_w_down0.astype(BF16))
    moe1 = (moe_w_gate1.astype(BF16), moe_w_up1.astype(BF16), moe_w_down1.astype(BF16))
    route0 = _router_params(moe_grp_w0, moe_grp_b0, moe_rtr_w0, moe_rtr_b0)
    route1 = _router_params(moe_grp_w1, moe_grp_b1, moe_rtr_w1, moe_rtr_b1)

    def run(stream, x, cache, states):
        x = x.reshape(stream.n_tok, d)
        kv_dtype = BF16 if stream.rope else F32
        a, q, k, v = _inproj0(stream, x, mod0, norm_mix_g0, in_w0_b, in_w0_partner, q_gains, k_gains, cos, sin,
                              kv_dtype)
        a = _conv_branch(stream, a, conv_w0, conv_b0, conv_norm_g0, conv_norm_b0)
        tq = min(stream.seq_len, 256)
        o = _attention(stream, q, k, v, cache, tq, kv_heads=1 if cache is not None else N_KV_HEADS)
        x, *routed = _outproj_route(stream, [a, o], out_w0_b, x, mod0, norm_ffn_g0, *route0)
        x, h = _moe(stream, *routed, *moe0, x, mod0, norm_mix_g1, mod1, last_layer=False)
        qk, vg = _inproj1(stream, h, in_w1_b, cos, sin)
        ret = _retention(stream, qk, vg, decays, states, heads=2 if states is not None else RET_HEADS)
        x, *routed = _outproj_route(stream, [ret[0]], out_w1_b, x, mod1, norm_ffn_g1, *route1)
        y, = _moe(stream, *routed, *moe1, x, mod1, final_norm_g, mod1, last_layer=True)
        return y.reshape(stream.n_seq, stream.seq_len, d), k, v, ret[1:]

    y_prompt, k_ctx, v_ctx, new_states = run(ctx, x_prompt, None, None)
    y_sample, _, _, _ = run(lat, x_sample, (cache_k0, cache_v0), (state_ret_fwd1, state_ret_bwd1))
    return (y_prompt, y_sample, k_ctx, v_ctx, new_states[0], new_states[1])
```

```python
import functools

import jax
import jax.numpy as jnp
import numpy as np
from jax import lax
from jax.experimental import pallas as pl
from jax.experimental.pallas import tpu as pltpu

D_MODEL = 1024
GRID_W = 64
EPS = 1e-6
CONV_CH = 512
CONV_WIDTH = 31
CONV_HALF = CONV_WIDTH // 2
N_Q_HEADS = 8
N_KV_HEADS = 2
Q_PER_KV = N_Q_HEADS // N_KV_HEADS
HEAD_DIM = 128
ROPE_THETA = 10000.0
ATTN_WIDTH = N_Q_HEADS * HEAD_DIM
KV_WIDTH = N_KV_HEADS * HEAD_DIM
IN0_WIDTH = 2 * CONV_CH + ATTN_WIDTH + 2 * KV_WIDTH
RET_HEADS = 8
RET_DK = 128
RET_DV = 256
RET_CHUNK = 128
RET_QK_WIDTH = RET_HEADS * RET_DK
RET_V_WIDTH = RET_HEADS * RET_DV
IN1_WIDTH = 2 * RET_QK_WIDTH + 3 * RET_V_WIDTH
N_GROUPS = 4
EXPERTS_PER_GROUP = 4
N_EXPERTS = N_GROUPS * EXPERTS_PER_GROUP
EXPERT_FF = 512

LANES = 128
SUBLANES = 8
VMEM_LIMIT_BYTES = 56 * 1024 * 1024

MOD_SLOTS = 8
ROW_TILE = 512
ROW_CHUNKS = D_MODEL // LANES
PAYLOAD_CHUNKS = 2 * ROW_CHUNKS
FFN_TILE = 256
COMBINE_TILE = 512
ROUTE_GROUP, ROUTE_RANK = 0, 1
CONV_TILE = 256
CONV_HALO = 16
CONV_ROWS = 64
ROUTE_LANES = LANES
EXPERT_LANE0 = N_GROUPS
NEG_BIG = -1e30
LOG2_E = 1.4426950408889634
ATTN_KV_CHUNK = 256

F32 = jnp.float32
BF16 = jnp.bfloat16


def _params(*semantics):
    return pltpu.CompilerParams(dimension_semantics=semantics, vmem_limit_bytes=VMEM_LIMIT_BYTES)


def _sigmoid(x):
    return 1.0 / (1.0 + jnp.exp(-x))


def _silu(x):
    return x * _sigmoid(x)


def _rms(x):
    return x * lax.rsqrt(jnp.mean(x * x, axis=-1, keepdims=True) + EPS)


def _rope(x, cos, sin):
    lane = lax.broadcasted_iota(jnp.int32, x.shape, 1)
    take_upper = (lane % (HEAD_DIM // 2)) < (HEAD_DIM // 4)
    partner = jnp.where(take_upper, pltpu.roll(x, HEAD_DIM - HEAD_DIM // 4, 1), pltpu.roll(x, HEAD_DIM // 4, 1))
    return x * cos + partner * sin


def _dot(a, b):
    return jnp.dot(a, b, preferred_element_type=F32)


def _dot_nt(a, b):
    return lax.dot_general(a, b, (((1,), (1,)), ((), ())), preferred_element_type=F32)


def _dot_tn(a, b):
    return lax.dot_general(a, b, (((0,), (0,)), ((), ())), preferred_element_type=F32)


def _mod_kernel(c_ref, w_ref, b_ref, o_ref):
    c = c_ref[...]
    o_ref[...] = _dot(_silu(c).astype(BF16), w_ref[...].astype(BF16)) + b_ref[...]


def _modulation(cond, mod_w, mod_b):
    n_out = mod_w.shape[1]
    col = D_MODEL
    out = pl.pallas_call(
        _mod_kernel,
        grid=(n_out // col,),
        in_specs=[
            pl.BlockSpec((MOD_SLOTS, D_MODEL), lambda j: (0, 0)),
            pl.BlockSpec((D_MODEL, col), lambda j: (0, j)),
            pl.BlockSpec((1, col), lambda j: (0, j)),
        ],
        out_specs=pl.BlockSpec((MOD_SLOTS, col), lambda j: (0, j)),
        out_shape=jax.ShapeDtypeStruct((MOD_SLOTS, n_out), F32),
        compiler_params=_params("parallel"),
        name="modulation",
    )(cond, mod_w, mod_b.reshape(1, n_out))
    return out.reshape(MOD_SLOTS, 6, D_MODEL)


def _slot_map(stream):
    if stream.first_slot == 0:
        return lambda i: 0
    return lambda i: stream.first_slot + (i * ROW_TILE) // stream.seq_len


class _Stream:
    def __init__(self, n_seq, seq_len, first_slot, rope):
        self.n_seq = n_seq
        self.seq_len = seq_len
        self.first_slot = first_slot
        self.rope = rope
        self.n_tok = n_seq * seq_len


def _modulated_norm(x, g, shift, scale):
    return _rms(x) * g * (1.0 + scale) + shift


def _inproj0_kernel(x_ref, mod_ref, g_ref, w_ref, wp_ref, qg_ref, kg_ref, cos_ref, sin_ref,
                    a_ref, q_ref, k_ref, v_ref, *, rope):
    h = _modulated_norm(x_ref[...], g_ref[...], mod_ref[0, 0:1, :], mod_ref[0, 1:2, :]).astype(BF16)
    p = _dot(h, w_ref[...])
    if rope:
        p_partner = _dot(h, wp_ref[...])
    a_ref[...] = p[:, :CONV_CH] * _sigmoid(p[:, CONV_CH:2 * CONV_CH])
    q0 = 2 * CONV_CH
    k0 = q0 + ATTN_WIDTH
    v0 = k0 + KV_WIDTH
    cos = cos_ref[...]
    sin = sin_ref[...]
    q_scale = HEAD_DIM ** -0.5 * LOG2_E

    def normed_head(col, partner_col, gain_ref):
        ph = p[:, col:col + HEAD_DIM]
        inv = lax.rsqrt(jnp.mean(ph * ph, axis=-1, keepdims=True) + EPS)
        out = ph * inv * gain_ref[0:1, :]
        if rope:
            out = out * cos + (p_partner[:, partner_col:partner_col + HEAD_DIM] * inv * gain_ref[1:2, :]) * sin
        return out

    for hh in range(N_Q_HEADS):
        qh = normed_head(q0 + hh * HEAD_DIM, hh * HEAD_DIM, qg_ref)
        q_ref[:, hh * HEAD_DIM:(hh + 1) * HEAD_DIM] = (qh * q_scale).astype(q_ref.dtype)
    seqs, _, rows, _ = k_ref.shape
    for hh in range(N_KV_HEADS):
        kh = normed_head(k0 + hh * HEAD_DIM, ATTN_WIDTH + hh * HEAD_DIM, kg_ref)
        vh = p[:, v0 + hh * HEAD_DIM:v0 + (hh + 1) * HEAD_DIM]
        for sq in range(seqs):
            k_ref[sq, hh] = kh[sq * rows:(sq + 1) * rows].astype(k_ref.dtype)
            v_ref[sq, hh] = vh[sq * rows:(sq + 1) * rows].astype(v_ref.dtype)


def _inproj0(stream, x, mod, norm_g, w_bf16, w_partner, q_gains, k_gains, cos, sin, kv_dtype):
    n = stream.n_tok
    tiles_per_seq = max(stream.seq_len // ROW_TILE, 1)
    rope_map = (lambda i: (i % tiles_per_seq, 0)) if stream.rope else (lambda i: (0, 0))
    slot = _slot_map(stream)
    row = lambda i: (i, 0)
    fixed = lambda i: (0, 0)
    seqs_per_tile = max(ROW_TILE // stream.seq_len, 1)
    kv_spec = pl.BlockSpec((seqs_per_tile, N_KV_HEADS, ROW_TILE // seqs_per_tile, HEAD_DIM),
                           lambda i: (i // tiles_per_seq, 0, i % tiles_per_seq, 0))
    kv_shape = jax.ShapeDtypeStruct((stream.n_seq, N_KV_HEADS, stream.seq_len, HEAD_DIM), kv_dtype)
    return pl.pallas_call(
        functools.partial(_inproj0_kernel, rope=stream.rope),
        grid=(n // ROW_TILE,),
        in_specs=[
            pl.BlockSpec((ROW_TILE, D_MODEL), row),
            pl.BlockSpec((1, 6, D_MODEL), lambda i: (slot(i), 0, 0)),
            pl.BlockSpec((1, D_MODEL), fixed),
            pl.BlockSpec((D_MODEL, IN0_WIDTH), fixed),
            pl.BlockSpec((D_MODEL, ATTN_WIDTH + KV_WIDTH), fixed),
            pl.BlockSpec((2, HEAD_DIM), fixed),
            pl.BlockSpec((2, HEAD_DIM), fixed),
            pl.BlockSpec((ROW_TILE, HEAD_DIM), rope_map),
            pl.BlockSpec((ROW_TILE, HEAD_DIM), rope_map),
        ],
        out_specs=[
            pl.BlockSpec((ROW_TILE, CONV_CH), row),
            pl.BlockSpec((ROW_TILE, ATTN_WIDTH), row),
            kv_spec,
            kv_spec,
        ],
        out_shape=[
            jax.ShapeDtypeStruct((n, CONV_CH), F32),
            jax.ShapeDtypeStruct((n, ATTN_WIDTH), BF16),
            kv_shape,
            kv_shape,
        ],
        compiler_params=_params("parallel"),
        name="inproj0",
    )(x, mod, norm_g.reshape(1, D_MODEL), w_bf16, w_partner, q_gains, k_gains, cos, sin)


def _conv_kernel(prev_ref, main_ref, next_ref, w_ref, b_ref, g_ref, beta_ref, o_ref, pad_ref, acc_ref, *,
                 tiles_per_seq):
    i = pl.program_id(0)
    has_prev = (i % tiles_per_seq) != 0
    has_next = (i % tiles_per_seq) != (tiles_per_seq - 1)
    pad_ref[0:CONV_HALO, :] = jnp.where(has_prev, prev_ref[...], 0.0)
    pad_ref[CONV_HALO:CONV_HALO + CONV_TILE, :] = main_ref[...]
    pad_ref[CONV_HALO + CONV_TILE:, :] = jnp.where(has_next, next_ref[...], 0.0)
    first = CONV_HALO - CONV_HALF
    window = CONV_ROWS + SUBLANES
    for c0 in range(0, CONV_CH, LANES):
        for r0 in range(0, CONV_TILE, CONV_ROWS):
            acc = jnp.zeros((CONV_ROWS, LANES), F32)
            for phase in range(SUBLANES):
                partial = jnp.zeros((window, LANES), F32)
                for j in range(phase, CONV_WIDTH, SUBLANES):
                    base = r0 + j - phase
                    partial = partial + w_ref[j:j + 1, c0:c0 + LANES] * pad_ref[base:base + window, c0:c0 + LANES]
                acc = acc + partial[first + phase:first + phase + CONV_ROWS]
            acc_ref[r0:r0 + CONV_ROWS, c0:c0 + LANES] = acc
    a = acc_ref[...] + b_ref[...]
    mu = jnp.mean(a, axis=-1, keepdims=True)
    d = a - mu
    var = jnp.mean(d * d, axis=-1, keepdims=True)
    y = d * lax.rsqrt(var + EPS) * g_ref[...] + beta_ref[...]
    o_ref[...] = _silu(y).astype(o_ref.dtype)


def _conv_branch(stream, a, conv_w, conv_b, norm_g, norm_b):
    n = stream.n_tok
    tiles_per_seq = stream.seq_len // CONV_TILE
    halo_per_tile = CONV_TILE // CONV_HALO
    n_halo = n // CONV_HALO
    fixed = lambda i: (0, 0)
    return pl.pallas_call(
        functools.partial(_conv_kernel, tiles_per_seq=tiles_per_seq),
        grid=(n // CONV_TILE,),
        in_specs=[
            pl.BlockSpec((CONV_HALO, CONV_CH), lambda i: (jnp.maximum(i * halo_per_tile - 1, 0), 0)),
            pl.BlockSpec((CONV_TILE, CONV_CH), lambda i: (i, 0)),
            pl.BlockSpec((CONV_HALO, CONV_CH), lambda i: (jnp.minimum((i + 1) * halo_per_tile, n_halo - 1), 0)),
            pl.BlockSpec((CONV_WIDTH, CONV_CH), fixed),
            pl.BlockSpec((1, CONV_CH), fixed),
            pl.BlockSpec((1, CONV_CH), fixed),
            pl.BlockSpec((1, CONV_CH), fixed),
        ],
        out_specs=pl.BlockSpec((CONV_TILE, CONV_CH), lambda i: (i, 0)),
        out_shape=jax.ShapeDtypeStruct((n, CONV_CH), BF16),
        scratch_shapes=[
            pltpu.VMEM((CONV_TILE + 2 * CONV_HALO, CONV_CH), F32),
            pltpu.VMEM((CONV_TILE, CONV_CH), F32),
        ],
        compiler_params=_params("parallel"),
        name="conv_branch",
    )(a, a, a, conv_w, conv_b.reshape(1, CONV_CH), norm_g.reshape(1, CONV_CH), norm_b.reshape(1, CONV_CH))


def _attn_kernel(*refs, tq, past):
    if past:
        q_ref, k_ref, v_ref, kc_ref, vc_ref, o_ref, ks_ref, vs_ref = refs
    else:
        q_ref, k_ref, v_ref, o_ref, ks_ref, vs_ref = refs
    kv_heads, n_keys, _ = ks_ref.shape
    group_w = Q_PER_KV * HEAD_DIM

    @pl.when(pl.program_id(2) == 0)
    def _():
        lane = lax.broadcasted_iota(jnp.int32, (n_keys, HEAD_DIM), 1)
        for hk in range(kv_heads):
            vs_ref[hk, :, HEAD_DIM:] = jnp.where(lane == 0, 1.0, 0.0).astype(BF16)
            if past:
                ks_ref[hk, 0:past, :] = kc_ref[0, hk].astype(BF16)
                vs_ref[hk, 0:past, 0:HEAD_DIM] = vc_ref[0, hk].astype(BF16)
            ks_ref[hk, past:, :] = k_ref[0, hk].astype(BF16)
            vs_ref[hk, past:, 0:HEAD_DIM] = v_ref[0, hk].astype(BF16)

    for hk in range(kv_heads):
        q = q_ref[:, hk * group_w:(hk + 1) * group_w]
        q4 = jnp.concatenate([q[:, g * HEAD_DIM:(g + 1) * HEAD_DIM] for g in range(Q_PER_KV)], axis=0)
        rows = q4.shape[0]
        m = jnp.full((rows, 1), NEG_BIG, F32)
        acc = jnp.zeros((rows, 2 * HEAD_DIM), F32)
        for c0 in range(0, n_keys, ATTN_KV_CHUNK):
            c1 = min(c0 + ATTN_KV_CHUNK, n_keys)
            s = _dot_nt(q4, ks_ref[hk, c0:c1, :])
            m_new = jnp.maximum(m, jnp.max(s, axis=-1, keepdims=True))
            p = jnp.exp2(s - m_new).astype(BF16)
            acc = jnp.exp2(m - m_new) * acc + _dot(p, vs_ref[hk, c0:c1, :])
            m = m_new
        o = acc[:, 0:HEAD_DIM] * (1.0 / acc[:, HEAD_DIM:HEAD_DIM + 1])
        for g in range(Q_PER_KV):
            col = hk * group_w + g * HEAD_DIM
            o_ref[:, col:col + HEAD_DIM] = o[g * tq:(g + 1) * tq].astype(o_ref.dtype)


def _attention(stream, q, k, v, cache, tq, kv_heads):
    n = stream.n_tok
    t = stream.seq_len
    q_tiles = t // tq
    width = kv_heads * Q_PER_KV * HEAD_DIM
    in_specs = [
        pl.BlockSpec((tq, width), lambda b, h, i: (b * q_tiles + i, h)),
        pl.BlockSpec((1, kv_heads, t, HEAD_DIM), lambda b, h, i: (b, h, 0, 0)),
        pl.BlockSpec((1, kv_heads, t, HEAD_DIM), lambda b, h, i: (b, h, 0, 0)),
    ]
    args = [q, k, v]
    past = 0
    if cache is not None:
        past = cache[0].shape[2]
        in_specs += [pl.BlockSpec((1, kv_heads, past, HEAD_DIM), lambda b, h, i: (b, h, 0, 0))] * 2
        args += list(cache)
    return pl.pallas_call(
        functools.partial(_attn_kernel, tq=tq, past=past),
        grid=(stream.n_seq, N_KV_HEADS // kv_heads, q_tiles),
        in_specs=in_specs,
        out_specs=pl.BlockSpec((tq, width), lambda b, h, i: (b * q_tiles + i, h)),
        out_shape=jax.ShapeDtypeStruct((n, ATTN_WIDTH), BF16),
        scratch_shapes=[pltpu.VMEM((kv_heads, past + t, HEAD_DIM), BF16),
                        pltpu.VMEM((kv_heads, past + t, 2 * HEAD_DIM), BF16)],
        compiler_params=_params("parallel", "parallel", "arbitrary"),
        name="attention",
    )(*args)


def _route(logits, running):
    rows = logits.shape[0]
    lane = lax.broadcasted_iota(jnp.int32, logits.shape, 1)
    lane_f = lane.astype(F32)
    far = float(ROUTE_LANES)
    is_group = lane < N_GROUPS
    gl = jnp.where(is_group, logits, NEG_BIG)
    gmax = jnp.max(gl, axis=-1, keepdims=True)
    gsum = jnp.sum(jnp.where(is_group, jnp.exp(gl - gmax), 0.0), axis=-1, keepdims=True)
    g_w = 1.0 / gsum
    gidx = jnp.min(jnp.where(gl == gmax, lane_f, far), axis=-1, keepdims=True)
    lo = EXPERT_LANE0 + EXPERTS_PER_GROUP * gidx
    in_group = (lane_f >= lo) & (lane_f < lo + EXPERTS_PER_GROUP)
    el = jnp.where(in_group, logits, NEG_BIG)
    v1 = jnp.max(el, axis=-1, keepdims=True)
    i1 = jnp.min(jnp.where(el == v1, lane_f, far), axis=-1, keepdims=True)
    el2 = jnp.where(lane_f == i1, NEG_BIG, el)
    v2 = jnp.max(el2, axis=-1, keepdims=True)
    i2 = jnp.min(jnp.where(el2 == v2, lane_f, far), axis=-1, keepdims=True)
    e2 = jnp.exp(v2 - v1)
    w1 = g_w / (1.0 + e2)
    w2 = w1 * e2
    in_my_group = lane_f == gidx
    chosen = in_my_group.astype(BF16)
    earlier = (lax.broadcasted_iota(jnp.int32, (rows, rows), 0)
               > lax.broadcasted_iota(jnp.int32, (rows, rows), 1)).astype(BF16)
    before = running + _dot(earlier, chosen)
    rank = jnp.sum(jnp.where(in_my_group, before, 0.0), axis=-1, keepdims=True)
    record = jnp.where(lane == ROUTE_GROUP, gidx, jnp.where(lane == ROUTE_RANK, rank, 0.0))
    weights = jnp.where(lane_f == i1 - lo, w1, 0.0) + jnp.where(lane_f == i2 - lo, w2, 0.0)
    return record, weights, running + jnp.sum(chosen.astype(F32), axis=0, keepdims=True)


def _outproj_kernel(*refs, widths):
    n_in = len(widths)
    in_refs = refs[:n_in]
    w_ref, x_ref, mod_ref, g_ref, rw_ref, rb_ref, x1_ref, pay_ref, route_t_ref, count_ref, run_ref = refs[n_in:]

    @pl.when(pl.program_id(0) == 0)
    def _():
        run_ref[...] = jnp.zeros_like(run_ref)

    acc = None
    off = 0
    for r, width in zip(in_refs, widths):
        part = _dot(r[...], w_ref[off:off + width, :])
        acc = part if acc is None else acc + part
        off += width
    x1 = x_ref[...] + mod_ref[0, 2:3, :] * acc
    x1_ref[...] = x1
    h = _modulated_norm(x1, g_ref[...], mod_ref[0, 3:4, :], mod_ref[0, 4:5, :])
    h_hi = h.astype(BF16)
    h_lo = (h - h_hi.astype(F32)).astype(BF16)
    both = _dot(h_hi, rw_ref[...])
    logits = (both[:, :ROUTE_LANES] + both[:, ROUTE_LANES:] + _dot(h_lo, rw_ref[:, :ROUTE_LANES]) + rb_ref[...])
    record, weights, running = _route(logits, run_ref[...])
    route_t_ref[...] = record.T[:SUBLANES]
    run_ref[...] = running
    count_ref[...] = jnp.broadcast_to(running, count_ref.shape)
    rows = h.shape[0]
    _store_row_major(pay_ref, h, PAYLOAD_CHUNKS)
    pay_ref[pl.ds(ROW_CHUNKS, rows, stride=PAYLOAD_CHUNKS), :] = weights
    for c in range(ROW_CHUNKS + 1, PAYLOAD_CHUNKS):
        pay_ref[pl.ds(c, rows, stride=PAYLOAD_CHUNKS), :] = jnp.zeros((rows, LANES), F32)


def _store_row_major(ref, x, pitch=ROW_CHUNKS):
    rows = x.shape[0]
    for c in range(ROW_CHUNKS):
        ref[pl.ds(c, rows, stride=pitch), :] = x[:, c * LANES:(c + 1) * LANES]


def _load_row_major(ref, rows, pitch=ROW_CHUNKS):
    return jnp.concatenate([ref[pl.ds(c, rows, stride=pitch), :] for c in range(ROW_CHUNKS)], axis=-1)


def _outproj_route(stream, inputs, w_bf16, x, mod, ffn_norm_g, route_w, route_b):
    n = stream.n_tok
    widths = tuple(a.shape[1] for a in inputs)
    slot = _slot_map(stream)
    row = lambda i: (i, 0)
    fixed = lambda i: (0, 0)
    return pl.pallas_call(
        functools.partial(_outproj_kernel, widths=widths),
        grid=(n // ROW_TILE,),
        in_specs=[pl.BlockSpec((ROW_TILE, width), row) for width in widths] + [
            pl.BlockSpec((sum(widths), D_MODEL), fixed),
            pl.BlockSpec((ROW_TILE, D_MODEL), row),
            pl.BlockSpec((1, 6, D_MODEL), lambda i: (slot(i), 0, 0)),
            pl.BlockSpec((1, D_MODEL), fixed),
            pl.BlockSpec((D_MODEL, 2 * ROUTE_LANES), fixed),
            pl.BlockSpec((1, ROUTE_LANES), fixed),
        ],
        out_specs=[
            pl.BlockSpec((ROW_TILE, D_MODEL), row),
            pl.BlockSpec((ROW_TILE * PAYLOAD_CHUNKS, LANES), row),
            pl.BlockSpec((SUBLANES, ROW_TILE), lambda i: (0, i)),
            pl.BlockSpec((SUBLANES, ROUTE_LANES), fixed),
        ],
        out_shape=[
            jax.ShapeDtypeStruct((n, D_MODEL), F32),
            jax.ShapeDtypeStruct((n * PAYLOAD_CHUNKS, LANES), F32),
            jax.ShapeDtypeStruct((SUBLANES, n), F32),
            jax.ShapeDtypeStruct((SUBLANES, ROUTE_LANES), F32),
        ],
        scratch_shapes=[pltpu.VMEM((1, ROUTE_LANES), F32)],
        compiler_params=_params("arbitrary"),
        name="outproj_route",
    )(*inputs, w_bf16, x, mod, ffn_norm_g.reshape(1, D_MODEL), route_w, route_b)


def _router_params(grp_w, grp_b, rtr_w, rtr_b):
    w = jnp.concatenate([grp_w, jnp.moveaxis(rtr_w, 0, 1).reshape(D_MODEL, N_EXPERTS)], axis=1)
    w = jnp.pad(w, ((0, 0), (0, ROUTE_LANES - w.shape[1])))
    hi = w.astype(BF16)
    lo = (w - hi.astype(F32)).astype(BF16)
    b = jnp.concatenate([grp_b, rtr_b.reshape(N_EXPERTS)])
    b = jnp.pad(b, (0, ROUTE_LANES - b.shape[0])).reshape(1, ROUTE_LANES)
    return jnp.concatenate([hi, lo], axis=1), b


def _sorted_tiles(n_tok):
    return n_tok // FFN_TILE + N_GROUPS


def _dispatch_plan(route_t, counts, n_tok):
    n_tiles = _sorted_tiles(n_tok)
    count = counts[0, :N_GROUPS].astype(jnp.int32)
    tiles = (count + (FFN_TILE - 1)) // FFN_TILE
    start = (jnp.cumsum(tiles) - tiles) * FFN_TILE
    tile_end = jnp.cumsum(tiles)
    j = jnp.arange(n_tiles, dtype=jnp.int32)
    tile_group = jnp.minimum(jnp.sum(j[:, None] >= tile_end[None, :], axis=1), N_GROUPS - 1).astype(jnp.int32)
    group_end = jnp.take(start + count, tile_group)
    valid = jnp.where(j < tile_end[-1], jnp.clip(group_end - j * FFN_TILE, 0, FFN_TILE), 0).astype(jnp.int32)
    ids = route_t[:2].astype(jnp.int32)
    where = jnp.take(start, ids[ROUTE_GROUP]) + ids[ROUTE_RANK]
    return where, tile_group, valid


def _token_rows(t, pitch=ROW_CHUNKS):
    return pl.ds(pl.multiple_of(t * pitch, pitch), pitch)


def _ffn_kernel(tg_ref, valid_ref, where_ref, pay_ref, wg_ref, wu_ref, wd_ref, ys_ref, x_buf, src_ref, sems, *, n_tok):
    j = pl.program_id(0)
    n_tiles = pl.num_programs(0)

    @pl.when(j == 0)
    def _():
        def place(t, carry):
            src_ref[where_ref[t]] = t
            return carry

        lax.fori_loop(0, n_tok, place, 0, unroll=8)

    def gather_copy(tile, r):
        tok = src_ref[tile * FFN_TILE + r]
        return pltpu.make_async_copy(pay_ref.at[_token_rows(tok, PAYLOAD_CHUNKS)],
                                     x_buf.at[tile % 2, _token_rows(r, PAYLOAD_CHUNKS)], sems.at[tile % 2])

    def start_gather(tile):
        def issue_pair(r2, carry):
            for k in range(2):
                gather_copy(tile, 2 * r2 + k).start(priority=k)
            return carry

        def issue(r, carry):
            gather_copy(tile, r).start()
            return carry

        @pl.when(valid_ref[tile] == FFN_TILE)
        def _():
            lax.fori_loop(0, FFN_TILE // 2, issue_pair, 0, unroll=4)

        @pl.when(valid_ref[tile] < FFN_TILE)
        def _():
            lax.fori_loop(0, valid_ref[tile], issue, 0)

    @pl.when(j == 0)
    def _():
        x_buf[...] = jnp.zeros_like(x_buf)
        start_gather(j)

    @pl.when(j + 1 < n_tiles)
    def _():
        @pl.when(valid_ref[j + 1] > 0)
        def _():
            start_gather(j + 1)

    @pl.when(valid_ref[j] == 0)
    def _():
        ys_ref[...] = jnp.zeros_like(ys_ref)

    @pl.when(valid_ref[j] > 0)
    def _():
        slot = j % 2
        n = pl.multiple_of(valid_ref[j] * PAYLOAD_CHUNKS, PAYLOAD_CHUNKS)
        pltpu.make_async_copy(pay_ref.at[pl.ds(0, n)], x_buf.at[slot, pl.ds(0, n)], sems.at[slot]).wait()
        x = _load_row_major(x_buf.at[slot], FFN_TILE, PAYLOAD_CHUNKS).astype(BF16)
        w = x_buf[slot, pl.ds(ROW_CHUNKS, FFN_TILE, stride=PAYLOAD_CHUNKS), :]
        y = None
        for e in range(EXPERTS_PER_GROUP):
            hidden = _silu(_dot(x, wg_ref[e])) * _dot(x, wu_ref[e]) * w[:, e:e + 1]
            part = _dot(hidden.astype(BF16), wd_ref[e])
            y = part if y is None else y + part
        _store_row_major(ys_ref, y)


def _expert_ffn(n_tok, tile_group, valid, where, payload, wg, wu, wd):
    group = lambda j, tg, va, sr: (tg[j], 0, 0)
    n_tiles = _sorted_tiles(n_tok)
    return pl.pallas_call(
        functools.partial(_ffn_kernel, n_tok=n_tok),
        grid_spec=pltpu.PrefetchScalarGridSpec(
            num_scalar_prefetch=3,
            grid=(n_tiles,),
            in_specs=[
                pl.BlockSpec(memory_space=pl.ANY),
                pl.BlockSpec((EXPERTS_PER_GROUP, D_MODEL, EXPERT_FF), group),
                pl.BlockSpec((EXPERTS_PER_GROUP, D_MODEL, EXPERT_FF), group),
                pl.BlockSpec((EXPERTS_PER_GROUP, EXPERT_FF, D_MODEL), group),
            ],
            out_specs=pl.BlockSpec((FFN_TILE * ROW_CHUNKS, LANES), lambda j, tg, va, sr: (j, 0)),
            scratch_shapes=[pltpu.VMEM((2, FFN_TILE * PAYLOAD_CHUNKS, LANES), F32),
                            pltpu.SMEM((n_tiles * FFN_TILE,), jnp.int32), pltpu.SemaphoreType.DMA((2,))],
        ),
        out_shape=jax.ShapeDtypeStruct((n_tiles * FFN_TILE * ROW_CHUNKS, LANES), F32),
        compiler_params=_params("arbitrary"),
        name="moe_ffn",
    )(tile_group, valid, where, payload, wg, wu, wd)


def _combine_kernel(where_ref, x_ref, mod_ref, ng_ref, nmod_ref, ys_ref, o_ref, *rest, last_layer):
    if last_layer:
        y_buf, sems = rest
    else:
        h_ref, y_buf, sems = rest
    i = pl.program_id(0)
    n_tiles = pl.num_programs(0)

    def row_copy(tile, t, k):
        slot = tile % 2
        d = where_ref[tile * COMBINE_TILE + t]
        return pltpu.make_async_copy(ys_ref.at[_token_rows(d)], y_buf.at[slot, _token_rows(t)], sems.at[slot, k])

    def start_tile(tile):
        def issue(t2, carry):
            for k in range(2):
                row_copy(tile, 2 * t2 + k, k).start(priority=k)
            return carry

        lax.fori_loop(0, COMBINE_TILE // 2, issue, 0, unroll=4)

    @pl.when(i == 0)
    def _():
        start_tile(i)

    @pl.when(i + 1 < n_tiles)
    def _():
        start_tile(i + 1)

    slot = i % 2
    half = COMBINE_TILE * ROW_CHUNKS // 2
    for k in range(2):
        pltpu.make_async_copy(ys_ref.at[pl.ds(0, half)], y_buf.at[slot, pl.ds(0, half)], sems.at[slot, k]).wait()
    x = x_ref[...] + mod_ref[0, 5:6, :] * _load_row_major(y_buf.at[slot], COMBINE_TILE)
    if last_layer:
        o_ref[...] = _rms(x) * ng_ref[...]
    else:
        o_ref[...] = x
        h_ref[...] = _modulated_norm(x, ng_ref[...], nmod_ref[0, 0:1, :], nmod_ref[0, 1:2, :]).astype(h_ref.dtype)


def _combine(stream, where, ys, x, mod, norm_g, next_mod, last_layer):
    n = stream.n_tok
    first_slot, seq_len = stream.first_slot, stream.seq_len
    slot = (lambda i: 0) if first_slot == 0 else (lambda i: first_slot + (i * COMBINE_TILE) // seq_len)
    row = lambda i, w: (i, 0)
    out_specs = [pl.BlockSpec((COMBINE_TILE, D_MODEL), row)]
    out_shape = [jax.ShapeDtypeStruct((n, D_MODEL), F32)]
    if not last_layer:
        out_specs.append(pl.BlockSpec((COMBINE_TILE, D_MODEL), row))
        out_shape.append(jax.ShapeDtypeStruct((n, D_MODEL), BF16))
    return pl.pallas_call(
        functools.partial(_combine_kernel, last_layer=last_layer),
        grid_spec=pltpu.PrefetchScalarGridSpec(
            num_scalar_prefetch=1,
            grid=(n // COMBINE_TILE,),
            in_specs=[
                pl.BlockSpec((COMBINE_TILE, D_MODEL), row),
                pl.BlockSpec((1, 6, D_MODEL), lambda i, w: (slot(i), 0, 0)),
                pl.BlockSpec((1, D_MODEL), lambda i, w: (0, 0)),
                pl.BlockSpec((1, 6, D_MODEL), lambda i, w: (slot(i), 0, 0)),
                pl.BlockSpec(memory_space=pl.ANY),
            ],
            out_specs=out_specs,
            scratch_shapes=[pltpu.VMEM((2, COMBINE_TILE * ROW_CHUNKS, LANES), F32),
                            pltpu.SemaphoreType.DMA((2, 2))],
        ),
        out_shape=out_shape,
        compiler_params=_params("arbitrary"),
        name="moe_combine",
    )(where, x, mod, norm_g.reshape(1, D_MODEL), next_mod, ys)


def _moe(stream, payload, route_t, counts, wg, wu, wd, x, mod, norm_g, next_mod, last_layer):
    n = stream.n_tok
    where, tile_group, valid = _dispatch_plan(route_t, counts, n)
    ys = _expert_ffn(n, tile_group, valid, where, payload, wg, wu, wd)
    return _combine(stream, where, ys, x, mod, norm_g, next_mod, last_layer)


IN1_COL = 1024
IN1_ROW_TILE = 1024
QK_BLOCKS = 2 * RET_QK_WIDTH // IN1_COL
Q_BLOCKS = RET_QK_WIDTH // IN1_COL
V_BLOCKS = RET_V_WIDTH // IN1_COL


def _inproj1_qk_kernel(h_ref, w_ref, cos_ref, sin_ref, o_ref, *, rope):
    p = _dot(h_ref[...], w_ref[...])
    scale = jnp.where(pl.program_id(0) < Q_BLOCKS, 1.0, RET_DK ** -0.5)
    cos = cos_ref[...]
    sin = sin_ref[...]
    for hh in range(IN1_COL // RET_DK):
        ph = p[:, hh * RET_DK:(hh + 1) * RET_DK]
        if rope:
            ph = _rope(ph, cos, sin)
        o_ref[:, hh * RET_DK:(hh + 1) * RET_DK] = (ph * scale).astype(o_ref.dtype)


def _inproj1_vg_kernel(h_ref, w_ref, o_ref):
    p = _dot(h_ref[...], w_ref[...])
    o_ref[...] = jnp.where(pl.program_id(0) < V_BLOCKS, p, _silu(p)).astype(o_ref.dtype)


def _inproj1(stream, h, w_bf16, cos, sin):
    n = stream.n_tok
    tm = IN1_ROW_TILE
    tiles_per_seq = max(stream.seq_len // tm, 1)
    rope_map = (lambda j, i: (i % tiles_per_seq, 0)) if stream.rope else (lambda j, i: (0, 0))
    h_spec = pl.BlockSpec((tm, D_MODEL), lambda j, i: (i, 0))
    out_spec = pl.BlockSpec((tm, IN1_COL), lambda j, i: (i, j))
    qk = pl.pallas_call(
        functools.partial(_inproj1_qk_kernel, rope=stream.rope),
        grid=(QK_BLOCKS, n // tm),
        in_specs=[
            h_spec,
            pl.BlockSpec((D_MODEL, IN1_COL), lambda j, i: (0, j)),
            pl.BlockSpec((tm, RET_DK), rope_map),
            pl.BlockSpec((tm, RET_DK), rope_map),
        ],
        out_specs=out_spec,
        out_shape=jax.ShapeDtypeStruct((n, QK_BLOCKS * IN1_COL), BF16),
        compiler_params=_params("parallel", "parallel"),
        name="inproj1_qk",
    )(h, w_bf16, cos, sin)
    vg_blocks = IN1_WIDTH // IN1_COL - QK_BLOCKS
    vg = pl.pallas_call(
        _inproj1_vg_kernel,
        grid=(vg_blocks, n // tm),
        in_specs=[h_spec, pl.BlockSpec((D_MODEL, IN1_COL), lambda j, i: (0, QK_BLOCKS + j))],
        out_specs=out_spec,
        out_shape=jax.ShapeDtypeStruct((n, vg_blocks * IN1_COL), BF16),
        compiler_params=_params("parallel", "parallel"),
        name="inproj1_vg",
    )(h, w_bf16)
    return qk, vg


def _retention_kernel(*refs, n_chunks, has_state, heads, unroll):
    if has_state:
        decay_ref, q_ref, k_ref, v_ref, gf_ref, gb_ref, sf0_ref, sb0_ref, o_ref, inc_ref, seen_ref = refs
    else:
        decay_ref, q_ref, k_ref, v_ref, gf_ref, gb_ref, o_ref, sf_ref, sb_ref, inc_ref, seen_ref = refs
    c = RET_CHUNK
    row = lax.broadcasted_iota(jnp.int32, (c, c), 0).astype(F32)
    col = lax.broadcasted_iota(jnp.int32, (c, c), 1).astype(F32)
    pos = lax.broadcasted_iota(jnp.int32, (c, 1), 0).astype(F32)

    def chunk(i):
        return pl.ds(pl.multiple_of(i * c, c), c)

    diff = [row - col, col - row]
    qk_cols = [slice(g * RET_DK, (g + 1) * RET_DK) for g in range(heads)]
    v_cols = [slice(g * RET_DV, (g + 1) * RET_DV) for g in range(heads)]
    intra, q_dec, k_dec, chunk_dec = [], [], [], []
    for g in range(heads):
        hd = pl.program_id(1) * heads + g
        lg = [-jnp.exp(jnp.full((1, 1), decay_ref[d, hd], F32)) for d in range(2)]
        intra.append([jnp.where(diff[d] >= 0, jnp.exp(lg[d] * jnp.maximum(diff[d], 0.0)), 0.0) for d in range(2)])
        q_dec.append([jnp.exp(lg[0] * (pos + 1.0)), jnp.exp(lg[1] * (c - pos))])
        k_dec.append([jnp.exp(lg[0] * ((c - 1.0) - pos)), jnp.exp(lg[1] * pos)])
        chunk_dec.append([jnp.exp(lg[d] * float(c)) for d in range(2)])

    def increments(i, carry):
        rows = chunk(i)
        for g in range(heads):
            ki = k_ref[rows, qk_cols[g]].astype(F32)
            vi = v_ref[rows, v_cols[g]]
            for d in range(2):
                inc_ref[g, d, i] = _dot_tn((ki * k_dec[g][d]).astype(BF16), vi)
        return carry

    lax.fori_loop(0, n_chunks, increments, 0, unroll=unroll)

    for g in range(heads):
        finals = []
        for d in range(2):
            def scan(t, state):
                i = t if d == 0 else n_chunks - 1 - t
                seen_ref[g, d, i] = state.astype(BF16)
                return state * chunk_dec[g][d] + inc_ref[g, d, i]

            if has_state:
                state0 = (sf0_ref if d == 0 else sb0_ref)[0, g]
            else:
                state0 = jnp.zeros((RET_DK, RET_DV), F32)
            finals.append(lax.fori_loop(0, n_chunks, scan, state0))
        if not has_state:
            sf_ref[0, g] = finals[0]
            sb_ref[0, g] = finals[1]

    def outputs(i, carry):
        rows = chunk(i)
        for g in range(heads):
            qi = q_ref[rows, qk_cols[g]]
            vi = v_ref[rows, v_cols[g]]
            s = _dot_nt(qi, k_ref[rows, qk_cols[g]])
            mixed = None
            for d, g_ref in enumerate((gf_ref, gb_ref)):
                y = _dot((s * intra[g][d]).astype(BF16), vi) + _dot(qi, seen_ref[g, d, i]) * q_dec[g][d]
                gated = g_ref[rows, v_cols[g]].astype(F32) * _rms(y)
                mixed = gated if mixed is None else mixed + gated
            o_ref[rows, v_cols[g]] = mixed.astype(o_ref.dtype)
        return carry

    lax.fori_loop(0, n_chunks, outputs, 0, unroll=unroll)


def _retention(stream, qk, vg, decays, states, heads):
    n = stream.n_tok
    t = stream.seq_len
    n_chunks = t // RET_CHUNK
    has_state = states is not None
    qk_w = heads * RET_DK
    v_w = heads * RET_DV
    k_blk = RET_QK_WIDTH // qk_w
    gf_blk = RET_V_WIDTH // v_w
    gb_blk = 2 * gf_blk
    in_specs = [
        pl.BlockSpec(memory_space=pltpu.SMEM),
        pl.BlockSpec((t, qk_w), lambda b, h: (b, h)),
        pl.BlockSpec((t, qk_w), lambda b, h: (b, k_blk + h)),
        pl.BlockSpec((t, v_w), lambda b, h: (b, h)),
        pl.BlockSpec((t, v_w), lambda b, h: (b, gf_blk + h)),
        pl.BlockSpec((t, v_w), lambda b, h: (b, gb_blk + h)),
    ]
    args = [decays, qk, qk, vg, vg, vg]
    state_spec = pl.BlockSpec((1, heads, RET_DK, RET_DV), lambda b, h: (b, h, 0, 0))
    out_specs = [pl.BlockSpec((t, v_w), lambda b, h: (b, h))]
    out_shape = [jax.ShapeDtypeStruct((n, RET_V_WIDTH), BF16)]
    if has_state:
        in_specs += [state_spec, state_spec]
        args += list(states)
    else:
        out_specs += [state_spec, state_spec]
        out_shape += [jax.ShapeDtypeStruct((stream.n_seq, RET_HEADS, RET_DK, RET_DV), F32)] * 2
    return pl.pallas_call(
        functools.partial(_retention_kernel, n_chunks=n_chunks, has_state=has_state, heads=heads,
                          unroll=min(n_chunks, 4)),
        grid=(stream.n_seq, RET_HEADS // heads),
        in_specs=in_specs,
        out_specs=out_specs,
        out_shape=out_shape,
        scratch_shapes=[pltpu.VMEM((heads, 2, n_chunks, RET_DK, RET_DV), F32),
                        pltpu.VMEM((heads, 2, n_chunks, RET_DK, RET_DV), BF16)],
        compiler_params=_params("parallel", "parallel"),
        name="retention",
    )(*args)


def _rope_tables(n_tokens, dim):
    t = np.arange(n_tokens)
    pos = np.stack([t // GRID_W, t % GRID_W]).astype(np.float64)
    n_freq = dim // 4
    freqs = ROPE_THETA ** (-np.arange(n_freq, dtype=np.float64) / n_freq)
    ang = pos[:, :, None] * freqs
    cos, sin = np.cos(ang), np.sin(ang)
    cos_t = np.concatenate([cos[0], cos[0], cos[1], cos[1]], axis=-1)
    sin_t = np.concatenate([-sin[0], sin[0], -sin[1], sin[1]], axis=-1)
    return jnp.asarray(cos_t, F32), jnp.asarray(sin_t, F32)


def kernel(x_prompt, x_sample, cache_k0, cache_v0, state_ret_fwd1, state_ret_bwd1, c, c_ctx,
           norm_mix_g0, mod_w0, mod_b0, in_w0, conv_w0, conv_b0, conv_norm_g0, conv_norm_b0,
           q_norm_g0, k_norm_g0, out_w0, norm_ffn_g0,
           moe_grp_w0, moe_grp_b0, moe_rtr_w0, moe_rtr_b0, moe_w_gate0, moe_w_up0, moe_w_down0,
           norm_mix_g1, mod_w1, mod_b1, in_w1, ret_decay_fwd1, ret_decay_bwd1, out_w1, norm_ffn_g1,
           moe_grp_w1, moe_grp_b1, moe_rtr_w1, moe_rtr_b1, moe_w_gate1, moe_w_up1, moe_w_down1,
           final_norm_g):
    batch, seq, d = x_prompt.shape
    dec_batch, dec_seq, _ = x_sample.shape
    assert d == D_MODEL and 1 + dec_batch <= MOD_SLOTS
    ctx = _Stream(batch, seq, first_slot=0, rope=False)
    lat = _Stream(dec_batch, dec_seq, first_slot=1, rope=True)
    assert ctx.n_tok % ROW_TILE == 0 and lat.n_tok % ROW_TILE == 0 and ROW_CHUNKS == SUBLANES
    assert seq % CONV_TILE == 0 and dec_seq % ROW_TILE == 0 and ROW_TILE % seq == 0

    cond = jnp.concatenate([c_ctx[None, :], c, jnp.zeros((MOD_SLOTS - 1 - dec_batch, d), F32)], axis=0)
    mod0 = _modulation(cond, mod_w0, mod_b0)
    mod1 = _modulation(cond, mod_w1, mod_b1)
    cos, sin = _rope_tables(dec_seq, HEAD_DIM)
    decays = jnp.stack([ret_decay_fwd1, ret_decay_bwd1]).astype(F32)

    in_w0_b = in_w0.astype(BF16)
    def partner_order(a):
        quarter = HEAD_DIM // 4
        blocks = a.reshape(a.shape[:-1] + (a.shape[-1] // (2 * quarter), 2, quarter))
        return blocks[..., ::-1, :].reshape(a.shape)

    qk0 = 2 * CONV_CH
    in_w0_partner = partner_order(in_w0_b[:, qk0:qk0 + ATTN_WIDTH + KV_WIDTH])
    q_gains = jnp.stack([q_norm_g0, partner_order(q_norm_g0)])
    k_gains = jnp.stack([k_norm_g0, partner_order(k_norm_g0)])
    out_w0_b = out_w0.astype(BF16)
    in_w1_b = in_w1.astype(BF16)
    out_w1_b = out_w1.astype(BF16)
    moe0 = (moe_w_gate0.astype(BF16), moe_w_up0.astype(BF16), moe_w_down0.astype(BF16))
    moe1 = (moe_w_gate1.astype(BF16), moe_w_up1.astype(BF16), moe_w_down1.astype(BF16))
    route0 = _router_params(moe_grp_w0, moe_grp_b0, moe_rtr_w0, moe_rtr_b0)
    route1 = _router_params(moe_grp_w1, moe_grp_b1, moe_rtr_w1, moe_rtr_b1)

    def run(stream, x, cache, states):
        x = x.reshape(stream.n_tok, d)
        kv_dtype = BF16 if stream.rope else F32
        a, q, k, v = _inproj0(stream, x, mod0, norm_mix_g0, in_w0_b, in_w0_partner, q_gains, k_gains, cos, sin,
                              kv_dtype)
        a = _conv_branch(stream, a, conv_w0, conv_b0, conv_norm_g0, conv_norm_b0)
        tq = min(stream.seq_len, 256)
        o = _attention(stream, q, k, v, cache, tq, kv_heads=1 if cache is not None else N_KV_HEADS)
        x, *routed = _outproj_route(stream, [a, o], out_w0_b, x, mod0, norm_ffn_g0, *route0)
        x, h = _moe(stream, *routed, *moe0, x, mod0, norm_mix_g1, mod1, last_layer=False)
        qk, vg = _inproj1(stream, h, in_w1_b, cos, sin)
        ret = _retention(stream, qk, vg, decays, states, heads=2 if states is not None else RET_HEADS)
        x, *routed = _outproj_route(stream, [ret[0]], out_w1_b, x, mod1, norm_ffn_g1, *route1)
        y, = _moe(stream, *routed, *moe1, x, mod1, final_norm_g, mod1, last_layer=True)
        return y.reshape(stream.n_seq, stream.seq_len, d), k, v, ret[1:]

    y_prompt, k_ctx, v_ctx, new_states = run(ctx, x_prompt, None, None)
    y_sample, _, _, _ = run(lat, x_sample, (cache_k0, cache_v0), (state_ret_fwd1, state_ret_bwd1))
    return (y_prompt, y_sample, k_ctx, v_ctx, new_states[0], new_states[1])
```
